```python
import math
import jax, jax.numpy as jnp
from jax import lax
import numpy as np

D_MODEL = 1024
BATCH = 4
SEQ = 4096
DEPTH = 2
DEC_BATCH = 128
DEC_SEQ = 1
PAST_LEN = 16384
PAGE_SIZE = 128

HEAD_DIM = 64
N_Q_HEADS = (D_MODEL // 2) // HEAD_DIM
N_KV_HEADS = 2
GQA_GROUP = N_Q_HEADS // N_KV_HEADS
ATT_W = N_Q_HEADS * HEAD_DIM
KV_W = N_KV_HEADS * HEAD_DIM
WINDOW = 128
ATT_SCALE = HEAD_DIM ** -0.5
NUM_BUCKETS = 32
MAX_EXACT = NUM_BUCKETS // 2
MAX_DISTANCE = 128
NEG_INF = -1e30
RWKV_HEAD = 64
RWKV_W = D_MODEL // 2
RWKV_HEADS = RWKV_W // RWKV_HEAD
D_DECAY_LORA = 64
D_AAA_LORA = 64
D_GATE_LORA = 128
RWKV_IN = 3 * RWKV_W + D_DECAY_LORA + D_AAA_LORA + D_GATE_LORA
GN_EPS = 64e-5
IN_W = ATT_W + 2 * KV_W + RWKV_IN + 2 * D_MODEL
IN_SPLITS = [ATT_W, ATT_W + KV_W, ATT_W + 2 * KV_W, ATT_W + 2 * KV_W + RWKV_IN]
RWKV_SPLITS = [RWKV_W, 2 * RWKV_W, 3 * RWKV_W, 3 * RWKV_W + D_DECAY_LORA,
               3 * RWKV_W + D_DECAY_LORA + D_AAA_LORA]
D_FF = 11 * D_MODEL // 4
N_EXPERTS = 8
TOP_K = 2
D_EXPERT = 7 * D_MODEL // 2
MOE_BLOCK = 128
N_DENSE = (DEPTH + 1) // 2
N_MOE = DEPTH // 2
NORM_EPS = 1e-6

kernel_name = 'hybrid_swa_rwkv7_moe_step'


def rmsnorm(x, g):
    xf = x.astype(jnp.float32)
    y = xf * lax.rsqrt(jnp.mean(xf * xf, axis=-1, keepdims=True) + NORM_EPS)
    return (y * g.astype(jnp.float32)).astype(x.dtype)


def t5_bucket(dist):
    n = jnp.maximum(dist, 0)
    nf = jnp.maximum(n, 1).astype(jnp.float32)
    large = MAX_EXACT + (jnp.log(nf / MAX_EXACT) / math.log(MAX_DISTANCE / MAX_EXACT)
                         * (NUM_BUCKETS - MAX_EXACT)).astype(jnp.int32)
    return jnp.where(n < MAX_EXACT, n, jnp.minimum(large, NUM_BUCKETS - 1))


def rel_bias_heads(dist, rel_bias):
    b = rel_bias[t5_bucket(dist)]
    return jnp.transpose(b, (2, 0, 1)).reshape(N_KV_HEADS, GQA_GROUP, dist.shape[0], dist.shape[1])


def attend_with_sinks(q, k, v, bias, valid, sinks):
    s = jnp.einsum('...qhgd,...khd->...hgqk', q, k).astype(jnp.float32) * ATT_SCALE + bias.astype(jnp.float32)
    s = jnp.where(valid, s, NEG_INF)
    sink = sinks.astype(jnp.float32).reshape(N_KV_HEADS, GQA_GROUP, 1, 1)
    m = jnp.maximum(jnp.max(s, axis=-1, keepdims=True), sink)
    p = jnp.exp(s - m)
    p = p / (jnp.sum(p, axis=-1, keepdims=True) + jnp.exp(sink - m))
    return jnp.einsum('...hgqk,...khd->...qhgd', p.astype(v.dtype), v)


def swa_prompt(q, k, v, sinks, rel_bias):
    b, s = q.shape[0], q.shape[1]
    nb = s // WINDOW
    qb = q.reshape(b, nb, WINDOW, N_KV_HEADS, GQA_GROUP, HEAD_DIM)
    pad = jnp.zeros((b, WINDOW, N_KV_HEADS, HEAD_DIM), k.dtype)

    def band(t):
        tp = jnp.concatenate([pad, t], axis=1)
        prev = tp[:, :s].reshape(b, nb, WINDOW, N_KV_HEADS, HEAD_DIM)
        cur = tp[:, WINDOW:].reshape(b, nb, WINDOW, N_KV_HEADS, HEAD_DIM)
        return jnp.concatenate([prev, cur], axis=2)

    q_off = jnp.arange(WINDOW)
    k_off = jnp.arange(2 * WINDOW) - WINDOW
    dist = q_off[:, None] - k_off[None, :]
    key_pos = jnp.arange(nb)[:, None] * WINDOW + k_off[None, :]
    valid = ((dist >= 0) & (dist <= WINDOW))[None] & (key_pos >= 0)[:, None, :]
    valid = valid[:, None, None]
    o = attend_with_sinks(qb, band(k), band(v), rel_bias_heads(dist, rel_bias), valid, sinks)
    return o.reshape(b, s, ATT_W), k[:, s - WINDOW:], v[:, s - WINDOW:]


def swa_sample(q, k, v, ck, cv, sinks, rel_bias):
    b, t = q.shape[0], q.shape[1]
    n_buf = ck.shape[1]
    qb = q.reshape(b, t, N_KV_HEADS, GQA_GROUP, HEAD_DIM)
    k_all = jnp.concatenate([ck.astype(k.dtype), k], axis=1)
    v_all = jnp.concatenate([cv.astype(v.dtype), v], axis=1)
    k_off = jnp.concatenate([jnp.arange(n_buf) - n_buf, jnp.arange(t)])
    dist = jnp.arange(t)[:, None] - k_off[None, :]
    valid = (dist >= 0) & (dist <= WINDOW)
    o = attend_with_sinks(qb, k_all, v_all, rel_bias_heads(dist, rel_bias), valid, sinks)
    return o.reshape(b, t, ATT_W), k_all[:, t:], v_all[:, t:]


def rwkv7_mix(z, prev_z, s0, mu, w0, w2, a0, a2, g2, k_k, k_a, r_k, lnx_g, lnx_b):
    b, t, _ = z.shape
    zs = jnp.concatenate([prev_z[:, None].astype(z.dtype), z[:, :-1]], axis=1)
    zz = z + (zs - z) * mu
    r, k, v, wl, al, gl = jnp.split(zz, RWKV_SPLITS, axis=-1)
    w = -jax.nn.softplus(-(w0 + jnp.tanh(wl) @ w2).astype(jnp.float32)) - 0.5
    decay = jnp.exp(-jnp.exp(w))
    a = jax.nn.sigmoid((a0 + al @ a2).astype(jnp.float32))
    g = (jax.nn.sigmoid(gl) @ g2).astype(jnp.float32)

    def heads(u):
        return u.reshape(b, t, RWKV_HEADS, RWKV_HEAD)

    kf = k.astype(jnp.float32)
    rf = r.astype(jnp.float32)
    vf = v.astype(jnp.float32)
    kk = heads(kf * k_k)
    kk = kk / jnp.maximum(jnp.sqrt(jnp.sum(kk * kk, axis=-1, keepdims=True)), 1e-12)
    k2 = kf * (1.0 + (a - 1.0) * k_a)

    def tm(u):
        return jnp.swapaxes(u, 0, 1)

    xs = (tm(heads(rf)), tm(heads(decay)), tm(heads(k2)), tm(heads(vf)), tm(-kk), tm(kk * heads(a)))

    def step(s, inp):
        rt, wt, kt, vt, at, bt = inp
        sa = jnp.einsum('bhij,bhj->bhi', s, at)
        s = s * wt[:, :, None, :] + sa[..., None] * bt[:, :, None, :] + vt[..., None] * kt[:, :, None, :]
        return s, jnp.einsum('bhij,bhj->bhi', s, rt)

    s_fin, y = lax.scan(step, s0.astype(jnp.float32), xs)
    y = jnp.swapaxes(y, 0, 1)
    mean = jnp.mean(y, axis=-1, keepdims=True)
    var = jnp.mean(jnp.square(y - mean), axis=-1, keepdims=True)
    yn = ((y - mean) * lax.rsqrt(var + GN_EPS)).reshape(b, t, RWKV_W) * lnx_g + lnx_b
    bonus = (jnp.sum(heads(rf) * heads(k2) * r_k, axis=-1, keepdims=True) * heads(vf)).reshape(b, t, RWKV_W)
    out = ((yn + bonus) * g).astype(z.dtype)
    return out, z[:, -1], s_fin.astype(s0.dtype)


def swiglu(h, wg, wu, wd):
    return (jax.nn.silu(h @ wg) * (h @ wu)) @ wd


def moe_swiglu(h, router_w, wg, wu, wd):
    n_tok = h.shape[0]
    logits = (h @ router_w).astype(jnp.float32)
    top_v, top_i = lax.top_k(logits, TOP_K)
    gates = jax.nn.softmax(top_v, axis=-1)
    n_as = n_tok * TOP_K
    e = top_i.reshape(-1).astype(jnp.int32)
    tok = jnp.repeat(jnp.arange(n_tok, dtype=jnp.int32), TOP_K)
    gw = gates.reshape(-1)
    order = jnp.argsort(e)
    se, stok, sg = e[order], tok[order], gw[order]
    counts = jnp.zeros((N_EXPERTS,), jnp.int32).at[e].add(1)
    starts = jnp.cumsum(counts) - counts
    padded = (counts + MOE_BLOCK - 1) // MOE_BLOCK * MOE_BLOCK
    pends = jnp.cumsum(padded)
    pstarts = pends - padded
    dest = pstarts[se] + jnp.arange(n_as, dtype=jnp.int32) - starts[se]
    n_blk = -(-n_as // MOE_BLOCK) + N_EXPERTS
    row_tok = jnp.zeros((n_blk * MOE_BLOCK,), jnp.int32).at[dest].set(stok)
    row_g = jnp.zeros((n_blk * MOE_BLOCK,), jnp.float32).at[dest].set(sg)
    blk_e = jnp.minimum(jnp.searchsorted(pends, jnp.arange(n_blk, dtype=jnp.int32) * MOE_BLOCK, side='right'),
                        N_EXPERTS - 1).astype(jnp.int32)

    def expert_block(args):
        e_b, t_b, g_b = args
        xb = h[t_b]
        y = swiglu(xb, wg[e_b], wu[e_b], wd[e_b])
        return y * g_b[:, None].astype(y.dtype)

    ys = lax.map(expert_block, (blk_e, row_tok.reshape(n_blk, MOE_BLOCK), row_g.reshape(n_blk, MOE_BLOCK)))
    return jax.ops.segment_sum(ys.reshape(-1, D_MODEL), row_tok, num_segments=n_tok)


def setup_inputs(seed: int = 0) -> dict:
    key = jax.random.key(seed)
    ks = jax.random.split(key, 40)
    cache_rows = min(WINDOW, PAST_LEN)

    def nrm(i, shape, scale):
        return jax.random.normal(ks[i], shape, jnp.float32) * scale

    def gain(i, shape):
        return 1.0 + 0.02 * jax.random.normal(ks[i], shape, jnp.float32)

    d = {}
    d['x_prompt'] = nrm(0, (BATCH, SEQ, D_MODEL), 1.0)
    d['x_sample'] = nrm(1, (DEC_BATCH, DEC_SEQ, D_MODEL), 1.0)
    d['cache_k'] = nrm(2, (DEPTH, DEC_BATCH, cache_rows, N_KV_HEADS, HEAD_DIM), 1.0)
    d['cache_v'] = nrm(3, (DEPTH, DEC_BATCH, cache_rows, N_KV_HEADS, HEAD_DIM), 1.0)
    d['state_wkv'] = nrm(4, (DEPTH, DEC_BATCH, RWKV_HEADS, RWKV_HEAD, RWKV_HEAD), 0.3)
    d['state_shift'] = nrm(5, (DEPTH, DEC_BATCH, RWKV_IN), 1.0)
    d['norm_mix_g'] = gain(6, (DEPTH, D_MODEL))
    d['w_in'] = nrm(7, (DEPTH, D_MODEL, IN_W), D_MODEL ** -0.5)
    d['attn_sinks'] = nrm(8, (DEPTH, N_Q_HEADS), 0.5)
    d['rel_bias'] = nrm(9, (NUM_BUCKETS, N_Q_HEADS), 0.5)
    d['mu_shift'] = jax.random.uniform(ks[10], (DEPTH, RWKV_IN), jnp.float32)
    d['w0'] = jax.random.uniform(ks[11], (DEPTH, RWKV_W), jnp.float32, -4.0, -0.5)
    d['w2'] = nrm(12, (DEPTH, D_DECAY_LORA, RWKV_W), 0.1 * D_DECAY_LORA ** -0.5)
    d['a0'] = nrm(13, (DEPTH, RWKV_W), 0.5)
    d['a2'] = nrm(14, (DEPTH, D_AAA_LORA, RWKV_W), 0.5 * D_AAA_LORA ** -0.5)
    d['g2'] = nrm(15, (DEPTH, D_GATE_LORA, RWKV_W), D_GATE_LORA ** -0.5)
    d['k_k'] = 1.0 + nrm(16, (DEPTH, RWKV_W), 0.1)
    d['k_a'] = 1.0 + nrm(17, (DEPTH, RWKV_W), 0.1)
    d['r_k'] = nrm(18, (DEPTH, RWKV_HEADS, RWKV_HEAD), 0.1)
    d['lnx_g'] = gain(19, (DEPTH, RWKV_W))
    d['lnx_b'] = nrm(20, (DEPTH, RWKV_W), 0.02)
    d['w_proj_attn'] = nrm(21, (DEPTH, ATT_W, D_MODEL), ATT_W ** -0.5)
    d['w_proj_rwkv'] = nrm(22, (DEPTH, RWKV_W, D_MODEL), RWKV_W ** -0.5)
    d['w_out'] = nrm(23, (DEPTH, D_MODEL, D_MODEL), D_MODEL ** -0.5)
    d['norm_ffn_g'] = gain(24, (DEPTH, D_MODEL))
    d['dense_w_gate'] = nrm(25, (N_DENSE, D_MODEL, D_FF), D_MODEL ** -0.5)
    d['dense_w_up'] = nrm(26, (N_DENSE, D_MODEL, D_FF), D_MODEL ** -0.5)
    d['dense_w_down'] = nrm(27, (N_DENSE, D_FF, D_MODEL), D_FF ** -0.5)
    d['router_w'] = nrm(28, (N_MOE, D_MODEL, N_EXPERTS), D_MODEL ** -0.5)
    d['moe_w_gate'] = nrm(29, (N_MOE, N_EXPERTS, D_MODEL, D_EXPERT), D_MODEL ** -0.5)
    d['moe_w_up'] = nrm(30, (N_MOE, N_EXPERTS, D_MODEL, D_EXPERT), D_MODEL ** -0.5)
    d['moe_w_down'] = nrm(31, (N_MOE, N_EXPERTS, D_EXPERT, D_MODEL), D_EXPERT ** -0.5)
    d['norm_final_g'] = gain(32, (D_MODEL,))
    return d


def reference(x_prompt, x_sample, cache_k, cache_v, state_wkv, state_shift, norm_mix_g, w_in,
              attn_sinks, rel_bias, mu_shift, w0, w2, a0, a2, g2, k_k, k_a, r_k, lnx_g, lnx_b,
              w_proj_attn, w_proj_rwkv, w_out, norm_ffn_g, dense_w_gate, dense_w_up, dense_w_down,
              router_w, moe_w_gate, moe_w_up, moe_w_down, norm_final_g):

    def run_group(x, is_prompt):
        nk, nv, ns, nz = [], [], [], []
        b, t, _ = x.shape
        for l in range(DEPTH):
            n = rmsnorm(x, norm_mix_g[l])
            proj = n @ w_in[l]
            q, k, v, z, gate = jnp.split(proj, IN_SPLITS, axis=-1)
            q = q.reshape(b, t, N_Q_HEADS, HEAD_DIM)
            k = k.reshape(b, t, N_KV_HEADS, HEAD_DIM)
            v = v.reshape(b, t, N_KV_HEADS, HEAD_DIM)
            if is_prompt:
                ya, ck, cv = swa_prompt(q, k, v, attn_sinks[l], rel_bias)
                prev_z = jnp.zeros((b, RWKV_IN), x.dtype)
                s0 = jnp.zeros((b, RWKV_HEADS, RWKV_HEAD, RWKV_HEAD), jnp.float32)
            else:
                ya, ck, cv = swa_sample(q, k, v, cache_k[l], cache_v[l], attn_sinks[l], rel_bias)
                prev_z = state_shift[l]
                s0 = state_wkv[l]
            yb, z_last, s_new = rwkv7_mix(z, prev_z, s0, mu_shift[l], w0[l], w2[l], a0[l], a2[l], g2[l],
                                          k_k[l], k_a[l], r_k[l], lnx_g[l], lnx_b[l])
            g_a, g_b = jnp.split(gate, 2, axis=-1)
            merged = jax.nn.sigmoid(g_a) * (ya @ w_proj_attn[l]) + jax.nn.sigmoid(g_b) * (yb @ w_proj_rwkv[l])
            x = x + merged @ w_out[l]
            h = rmsnorm(x, norm_ffn_g[l]).reshape(b * t, D_MODEL)
            if l % 2 == 0:
                f = swiglu(h, dense_w_gate[l // 2], dense_w_up[l // 2], dense_w_down[l // 2])
            else:
                f = moe_swiglu(h, router_w[l // 2], moe_w_gate[l // 2], moe_w_up[l // 2], moe_w_down[l // 2])
            x = x + f.reshape(b, t, D_MODEL)
            nk.append(ck)
            nv.append(cv)
            ns.append(s_new)
            nz.append(z_last)
        return rmsnorm(x, norm_final_g), jnp.stack(nk), jnp.stack(nv), jnp.stack(ns), jnp.stack(nz)

    y_prompt, nk_p, nv_p, ns_p, nz_p = run_group(x_prompt, True)
    y_sample, nk_s, nv_s, ns_s, nz_s = run_group(x_sample, False)
    return (y_prompt, y_sample, nk_p, nv_p, ns_p, nz_p, nk_s, nv_s, ns_s, nz_s)
```

```python
import functools
import math

import jax
import jax.numpy as jnp
from jax import lax
from jax.experimental import pallas as pl
from jax.experimental.pallas import tpu as pltpu

BF = jnp.bfloat16
F32 = jnp.float32

D_MODEL = 1024
DEPTH = 2
HEAD_DIM = 64
N_Q_HEADS = 8
N_KV_HEADS = 2
GQA_GROUP = 4
ATT_W = 512
KV_W = 128
WINDOW = 128
ATT_SCALE = HEAD_DIM ** -0.5
NUM_BUCKETS = 32
MAX_EXACT = 16
MAX_DISTANCE = 128
NEG_INF = -1e30
RWKV_HEAD = 64
RWKV_W = 512
RWKV_HEADS = 8
D_DECAY_LORA = 64
D_AAA_LORA = 64
D_GATE_LORA = 128
RWKV_IN = 3 * RWKV_W + D_DECAY_LORA + D_AAA_LORA + D_GATE_LORA
GN_EPS = 64e-5
IN_W = ATT_W + 2 * KV_W + RWKV_IN + 2 * D_MODEL
N_EXPERTS = 8
TOP_K = 2
NORM_EPS = 1e-6

VMEM_LIMIT_BYTES = 56 * 1024 * 1024
LANES = 128


def _params(*sem):
    return pltpu.CompilerParams(dimension_semantics=sem, vmem_limit_bytes=VMEM_LIMIT_BYTES)


def _const_spec(shape):
    zeros = (0,) * len(shape)
    return pl.BlockSpec(shape, lambda *_: zeros, pipeline_mode=pl.Buffered(1))


def _dot(a, b):
    return jnp.dot(a, b, preferred_element_type=F32)


def _dot_nt(a, b):
    return lax.dot_general(a, b, (((1,), (1,)), ((), ())), preferred_element_type=F32)


def _dot_tn(a, b):
    return lax.dot_general(a, b, (((0,), (0,)), ((), ())), preferred_element_type=F32)


def _split2(x):
    hi = x.astype(BF)
    lo = (x - hi.astype(F32)).astype(BF)
    return hi, lo


def _dot_exact_rhs(x, w):
    hi, lo = _split2(x)
    return _dot(hi, w) + _dot(lo, w)


def _rms(x, g):
    ms = jnp.mean(x * x, axis=-1, keepdims=True)
    return x * lax.rsqrt(ms + NORM_EPS) * g


def _sigmoid(x):
    return 1.0 / (1.0 + jnp.exp(-x))


def _inproj_kernel(x_ref, g_ref, w_ref, q_ref, k_ref, v_ref, z_ref, gate_ref):
    n = _rms(x_ref[...], g_ref[...]).astype(BF)
    q_ref[...] = _dot(n, w_ref[:, 0:ATT_W]).astype(BF)
    k_ref[...] = _dot(n, w_ref[:, ATT_W:ATT_W + KV_W])
    v_ref[...] = _dot(n, w_ref[:, ATT_W + KV_W:ATT_W + 2 * KV_W])
    z0 = ATT_W + 2 * KV_W
    z_ref[...] = _dot(n, w_ref[:, z0:z0 + RWKV_IN])
    gate_ref[...] = _dot(n, w_ref[:, z0 + RWKV_IN:IN_W])


def _inproj(x, g, w, tm):
    t = x.shape[0]
    row = lambda w_: pl.BlockSpec((tm, w_), lambda i: (i, 0))
    return pl.pallas_call(
        _inproj_kernel,
        grid=(t // tm,),
        in_specs=[row(D_MODEL), _const_spec((1, D_MODEL)), _const_spec((D_MODEL, IN_W))],
        out_specs=[row(ATT_W), row(KV_W), row(KV_W), row(RWKV_IN), row(2 * D_MODEL)],
        out_shape=[jax.ShapeDtypeStruct((t, ATT_W), BF),
                   jax.ShapeDtypeStruct((t, KV_W), F32),
                   jax.ShapeDtypeStruct((t, KV_W), F32),
                   jax.ShapeDtypeStruct((t, RWKV_IN), F32),
                   jax.ShapeDtypeStruct((t, 2 * D_MODEL), F32)],
        compiler_params=_params("parallel"),
        name="inproj",
    )(x, g, w)


def _t5_bucket(dist):
    n = jnp.maximum(dist, 0)
    nf = jnp.maximum(n, 1).astype(F32)
    large = MAX_EXACT + (jnp.log(nf / MAX_EXACT) / math.log(MAX_DISTANCE / MAX_EXACT)
                         * (NUM_BUCKETS - MAX_EXACT)).astype(jnp.int32)
    return jnp.where(n < MAX_EXACT, n, jnp.minimum(large, NUM_BUCKETS - 1))


def _prompt_bias_table(rel_bias):
    q_off = jnp.arange(WINDOW)
    k_off = jnp.arange(2 * WINDOW) - WINDOW
    dist = q_off[:, None] - k_off[None, :]
    b = rel_bias[_t5_bucket(dist)]
    valid = (dist >= 0) & (dist <= WINDOW)
    b = jnp.where(valid[:, :, None], b, NEG_INF)
    b = jnp.transpose(b, (2, 0, 1))
    return b.reshape(N_KV_HEADS, GQA_GROUP * WINDOW, 2 * WINDOW)


def _swa_prompt_kernel(q_ref, kp_ref, kc_ref, vp_ref, vc_ref, bias_ref, sink_ref, o_ref):
    first = pl.program_id(1) == 0
    q = q_ref[...]
    kp, kc, vp, vc = kp_ref[...], kc_ref[...], vp_ref[...], vc_ref[...]
    col = lax.broadcasted_iota(jnp.int32, (GQA_GROUP * WINDOW, 2 * WINDOW), 1)
    pad_mask = jnp.logical_and(first, col < WINDOW)
    outs = []
    for h in range(N_KV_HEADS):
        sl = slice(h * HEAD_DIM, (h + 1) * HEAD_DIM)
        kh = jnp.concatenate([kp[:, sl], kc[:, sl]], axis=0).astype(BF)
        vh = jnp.concatenate([vp[:, sl], vc[:, sl]], axis=0).astype(BF)
        qh = jnp.concatenate(
            [q[:, (GQA_GROUP * h + g) * HEAD_DIM:(GQA_GROUP * h + g + 1) * HEAD_DIM]
             for g in range(GQA_GROUP)], axis=0)
        s = _dot_nt(qh, kh) * ATT_SCALE + bias_ref[h]
        s = jnp.where(pad_mask, NEG_INF, s)
        sink = sink_ref[h]
        m = jnp.maximum(jnp.max(s, axis=-1, keepdims=True), sink)
        p = jnp.exp(s - m)
        denom = jnp.sum(p, axis=-1, keepdims=True) + jnp.exp(sink - m)
        o = _dot(p.astype(BF), vh) / denom
        outs.extend(o[g * WINDOW:(g + 1) * WINDOW] for g in range(GQA_GROUP))
    o_ref[...] = jnp.concatenate(outs, axis=-1).astype(o_ref.dtype)


def _swa_prompt(q, k, v, bias_tab, sink_col, batch, seq):
    nb = seq // WINDOW
    cur = lambda b, i: (b * nb + i, 0)
    prev = lambda b, i: (b * nb + jnp.maximum(i - 1, 0), 0)
    kv_c = pl.BlockSpec((WINDOW, KV_W), cur)
    kv_p = pl.BlockSpec((WINDOW, KV_W), prev)
    return pl.pallas_call(
        _swa_prompt_kernel,
        grid=(batch, nb),
        in_specs=[pl.BlockSpec((WINDOW, ATT_W), cur), kv_p, kv_c, kv_p, kv_c,
                  _const_spec(bias_tab.shape), _const_spec(sink_col.shape)],
        out_specs=pl.BlockSpec((WINDOW, ATT_W), cur),
        out_shape=jax.ShapeDtypeStruct((batch * seq, ATT_W), BF),
        compiler_params=_params("parallel", "parallel"),
        name="swa_prompt",
    )(q, k, k, v, v, bias_tab, sink_col)


def _swa_sample_kernel(q_ref, kn_ref, vn_ref, ck_ref, cv_ref, bc_ref, bn_ref, sink_ref, o_ref):
    q = q_ref[...]
    qf = q.astype(F32)
    kn = kn_ref[...].astype(BF).astype(F32)
    vn = vn_ref[...].astype(BF).astype(F32)
    ck, cv = ck_ref[...], cv_ref[...]
    head = lax.broadcasted_iota(jnp.int32, (1, N_Q_HEADS, 1), 1)
    low = head < GQA_GROUP
    s_h, sn_h = [], []
    for h in range(N_KV_HEADS):
        sl = slice(h * HEAD_DIM, (h + 1) * HEAD_DIM)
        s_h.append(jnp.einsum("bgd,bkd->bgk", q, ck[:, :, sl].astype(BF), preferred_element_type=F32))
        sn_h.append(jnp.sum(qf * kn[:, None, sl], axis=-1, keepdims=True))
    s = jnp.where(low, s_h[0], s_h[1]) * ATT_SCALE + bc_ref[...]
    sn = jnp.where(low, sn_h[0], sn_h[1]) * ATT_SCALE + bn_ref[...]
    sink = sink_ref[...]
    m = jnp.maximum(jnp.maximum(jnp.max(s, axis=-1, keepdims=True), sn), sink)
    p = jnp.exp(s - m)
    pn = jnp.exp(sn - m)
    denom = jnp.sum(p, axis=-1, keepdims=True) + pn + jnp.exp(sink - m)
    pb = p.astype(BF)
    pnb = pn.astype(BF).astype(F32)
    o_h = []
    for h in range(N_KV_HEADS):
        sl = slice(h * HEAD_DIM, (h + 1) * HEAD_DIM)
        o = jnp.einsum("bgk,bkd->bgd", pb, cv[:, :, sl].astype(BF), preferred_element_type=F32)
        o_h.append(o + pnb * vn[:, None, sl])
    o_ref[...] = (jnp.where(low, o_h[0], o_h[1]) / denom).astype(o_ref.dtype)


def _swa_sample(q3, kn, vn, ck, cv, bias_c, bias_n, sink3, bb):
    b = q3.shape[0]
    return pl.pallas_call(
        _swa_sample_kernel,
        grid=(b // bb,),
        in_specs=[pl.BlockSpec((bb, N_Q_HEADS, HEAD_DIM), lambda i: (i, 0, 0)),
                  pl.BlockSpec((bb, KV_W), lambda i: (i, 0)),
                  pl.BlockSpec((bb, KV_W), lambda i: (i, 0)),
                  pl.BlockSpec((bb, WINDOW, KV_W), lambda i: (i, 0, 0)),
                  pl.BlockSpec((bb, WINDOW, KV_W), lambda i: (i, 0, 0)),
                  _const_spec(bias_c.shape), _const_spec(bias_n.shape), _const_spec(sink3.shape)],
        out_specs=pl.BlockSpec((bb, N_Q_HEADS, HEAD_DIM), lambda i: (i, 0, 0)),
        out_shape=jax.ShapeDtypeStruct((b, N_Q_HEADS, HEAD_DIM), BF),
        compiler_params=_params("parallel"),
        name="swa_sample",
    )(q3, kn, vn, ck, cv, bias_c, bias_n, sink3)


def _rwkv_prep_kernel(t_real, z_ref, zp8_ref, pz_ref, mu_ref, w0_ref, w2_ref, a0_ref, a2_ref, g2_ref,
                      kk_ref, ka_ref, ones_ref,
                      r_ref, lw_ref, k2_ref, v_ref, a_ref, b_ref, g_ref):
    i = pl.program_id(1)
    z = z_ref[0]
    tm = z.shape[0]
    prev_row = jnp.where(i == 0, pz_ref[0], zp8_ref[0, 7:8, :])
    row = lax.broadcasted_iota(jnp.int32, (tm, 1), 0)
    zs = jnp.where(row == 0, prev_row, pltpu.roll(z, 1, 0))
    zz = z + (zs - z) * mu_ref[...]
    r = zz[:, 0:RWKV_W]
    k = zz[:, RWKV_W:2 * RWKV_W]
    v = zz[:, 2 * RWKV_W:3 * RWKV_W]
    o = 3 * RWKV_W
    wl = zz[:, o:o + D_DECAY_LORA]
    al = zz[:, o + D_DECAY_LORA:o + D_DECAY_LORA + D_AAA_LORA]
    gl = zz[:, o + D_DECAY_LORA + D_AAA_LORA:RWKV_IN]
    wpre = -(w0_ref[...] + _dot(jnp.tanh(wl).astype(BF), w2_ref[...]))
    softplus = jnp.maximum(wpre, 0.0) + jnp.log1p(jnp.exp(-jnp.abs(wpre)))
    lw = -jnp.exp(-softplus - 0.5)
    a = _sigmoid(a0_ref[...] + _dot(al.astype(BF), a2_ref[...]))
    g = _dot(_sigmoid(gl).astype(BF), g2_ref[...])
    kkv = k * kk_ref[...]
    ss = _dot_exact_rhs(kkv * kkv, ones_ref[...])
    kk = kkv / jnp.maximum(jnp.sqrt(ss), 1e-12)
    k2 = k * (1.0 + (a - 1.0) * ka_ref[...])
    na = -kk
    nb = kk * a
    if t_real is not None:
        keep = (row + i * tm) < t_real
        zero = jnp.zeros_like(r)
        r, lw, k2, v = (jnp.where(keep, u, zero) for u in (r, lw, k2, v))
        na, nb, g = (jnp.where(keep, u, zero) for u in (na, nb, g))
    r_ref[0], lw_ref[0], k2_ref[0], v_ref[0] = r, lw, k2, v
    a_ref[0], b_ref[0], g_ref[0] = na, nb, g


def _rwkv_prep(z3, prev_z, p, tm, t_real):
    b, t, _ = z3.shape
    blk8 = tm // 8
    vec = lambda w_: _const_spec((1, w_))
    out_spec = pl.BlockSpec((1, tm, RWKV_W), lambda bi, i: (bi, i, 0))
    return pl.pallas_call(
        functools.partial(_rwkv_prep_kernel, t_real),
        grid=(b, t // tm),
        in_specs=[pl.BlockSpec((1, tm, RWKV_IN), lambda bi, i: (bi, i, 0)),
                  pl.BlockSpec((1, 8, RWKV_IN), lambda bi, i: (bi, jnp.maximum(i * blk8 - 1, 0), 0)),
                  pl.BlockSpec((1, 1, RWKV_IN), lambda bi, i: (bi, 0, 0)),
                  vec(RWKV_IN), vec(RWKV_W), _const_spec((D_DECAY_LORA, RWKV_W)),
                  vec(RWKV_W), _const_spec((D_AAA_LORA, RWKV_W)), _const_spec((D_GATE_LORA, RWKV_W)),
                  vec(RWKV_W), vec(RWKV_W), _const_spec((RWKV_W, RWKV_W))],
        out_specs=[out_spec] * 7,
        out_shape=[jax.ShapeDtypeStruct((b, t, RWKV_W), F32)] * 7,
        compiler_params=_params("parallel", "parallel"),
        name="rwkv_prep",
    )(z3, z3, prev_z, p["mu"], p["w0"], p["w2"], p["a0"], p["a2"], p["g2"], p["k_k"], p["k_a"], p["ones_bd"])


def _rwkv_scan_kernel(chunk, r_ref, lw_ref, k_ref, v_ref, a_ref, b_ref, g_ref, s0_ref,
                      rk_ref, lg_ref, lb_ref, ones_ref, y_ref, sout_ref, s_scr):
    c = pl.program_id(1)

    @pl.when(c == 0)
    def _():
        s_scr[...] = s0_ref[0]

    r, lw, k, v = r_ref[0], lw_ref[0], k_ref[0], v_ref[0]
    a, b = a_ref[0], b_ref[0]
    ri = lax.broadcasted_iota(jnp.int32, (chunk, chunk), 0)
    ci = lax.broadcasted_iota(jnp.int32, (chunk, chunk), 1)
    incl = ri >= ci
    strict = ri > ci
    tri = incl.astype(BF)
    hi, lo = _split2(lw)
    lo2 = (lw - hi.astype(F32) - lo.astype(F32)).astype(BF)
    cum = _dot(tri, hi) + _dot(tri, lo) + _dot(tri, lo2)
    tail = cum[chunk - 1:chunk, :]
    g_in = jnp.exp(cum)
    g_inv = jnp.exp(-cum)
    g_tail = jnp.exp(tail - cum)
    g_all = jnp.exp(tail)
    ra = (r * g_in).astype(BF)
    aa = (a * jnp.exp(cum - lw)).astype(BF)
    bt = (b * g_inv).astype(BF)
    kt = (k * g_inv).astype(BF)
    bh = (b * g_tail).astype(BF)
    kh = (k * g_tail).astype(BF)
    vb = v.astype(BF)
    eye = (ri == ci).astype(F32)
    n_sq = int(math.log2(chunk)) - 1
    ys = []
    for h in range(RWKV_HEADS):
        sl = slice(h * RWKV_HEAD, (h + 1) * RWKV_HEAD)
        s_h = s_scr[h]
        ar = jnp.concatenate([aa[:, sl], ra[:, sl]], axis=0)
        gb = _dot_nt(ar, bt[:, sl])
        gk = _dot_nt(ar, kt[:, sl])
        l_ab = jnp.where(strict, gb[:chunk], 0.0)
        l_ak = jnp.where(strict, gk[:chunk], 0.0)
        m_rb = jnp.where(incl, gb[chunk:], 0.0)
        m_rk = jnp.where(incl, gk[chunk:], 0.0)
        p = _dot_nt(ar, s_h.astype(BF))
        v_h = vb[:, sl]
        rhs = p[:chunk] + _dot(l_ak.astype(BF), v_h)
        t_inv = eye + l_ab
        lp = l_ab
        for _ in range(n_sq):
            lpb = lp.astype(BF)
            lp = _dot(lpb, lpb)
            t_inv = t_inv + _dot(lp.astype(BF), t_inv.astype(BF))
        u = _dot(t_inv.astype(BF), rhs.astype(BF))
        ub = u.astype(BF)
        ys.append(p[chunk:] + _dot(m_rb.astype(BF), ub) + _dot(m_rk.astype(BF), v_h))
        s_scr[h] = s_h * g_all[:, sl] + _dot_tn(ub, bh[:, sl]) + _dot_tn(v_h, kh[:, sl])
    y = jnp.concatenate(ys, axis=-1)

    ones = ones_ref[...]
    inv_n = 1.0 / RWKV_HEAD
    mean = _dot_exact_rhs(y, ones) * inv_n
    d = y - mean
    var = _dot_exact_rhs(d * d, ones) * inv_n
    yn = d * lax.rsqrt(var + GN_EPS) * lg_ref[...] + lb_ref[...]
    bonus = _dot_exact_rhs(r * k * rk_ref[...], ones) * v
    y_ref[0] = ((yn + bonus) * g_ref[0]).astype(y_ref.dtype)

    @pl.when(c == pl.num_programs(1) - 1)
    def _():
        sout_ref[0] = s_scr[...]


def _rwkv_scan(coef, s0, p, chunk):
    r = coef[0]
    b, t, _ = r.shape
    seq_spec = pl.BlockSpec((1, chunk, RWKV_W), lambda bi, c: (bi, c, 0))
    st_spec = pl.BlockSpec((1, RWKV_HEADS, RWKV_HEAD, RWKV_HEAD), lambda bi, c: (bi, 0, 0, 0))
    vec = _const_spec((1, RWKV_W))
    return pl.pallas_call(
        functools.partial(_rwkv_scan_kernel, chunk),
        grid=(b, t // chunk),
        in_specs=[seq_spec] * 7 + [st_spec, vec, vec, vec, _const_spec((RWKV_W, RWKV_W))],
        out_specs=[seq_spec, st_spec],
        out_shape=[jax.ShapeDtypeStruct((b, t, RWKV_W), BF),
                   jax.ShapeDtypeStruct((b, RWKV_HEADS, RWKV_HEAD, RWKV_HEAD), F32)],
        scratch_shapes=[pltpu.VMEM((RWKV_HEADS, RWKV_HEAD, RWKV_HEAD), F32)],
        compiler_params=_params("parallel", "arbitrary"),
        name="rwkv_scan",
    )(*coef, s0, p["r_k"], p["lnx_g"], p["lnx_b"], p["ones_bd"])


def _merge_kernel(with_router, x_ref, ya_ref, yb_ref, gate_ref, wa_ref, wb_ref, wo_ref, g_ref, *rest):
    if with_router:
        wr_ref, x1_ref, h_ref, route_ref = rest
    else:
        x1_ref, h_ref = rest
    gate = gate_ref[...]
    pa = _dot(ya_ref[...], wa_ref[...])
    pb = _dot(yb_ref[...], wb_ref[...])
    merged = _sigmoid(gate[:, :D_MODEL]) * pa + _sigmoid(gate[:, D_MODEL:]) * pb
    x1 = x_ref[...] + _dot(merged.astype(BF), wo_ref[...])
    x1_ref[...] = x1
    h = _rms(x1, g_ref[...])
    h_ref[...] = h.astype(h_ref.dtype)
    if with_router:
        hh, hl = _split2(h)
        wh, wl = wr_ref[0], wr_ref[1]
        logits = _dot(hh, wh) + _dot(hl, wh) + _dot(hh, wl)
        lane = lax.broadcasted_iota(jnp.int32, logits.shape, 1)
        logits = jnp.where(lane < N_EXPERTS, logits, -jnp.inf)
        m1 = jnp.max(logits, axis=-1, keepdims=True)
        i1 = jnp.min(jnp.where(logits == m1, lane, LANES), axis=-1, keepdims=True)
        rest_l = jnp.where(lane == i1, -jnp.inf, logits)
        m2 = jnp.max(rest_l, axis=-1, keepdims=True)
        i2 = jnp.min(jnp.where(rest_l == m2, lane, LANES), axis=-1, keepdims=True)
        e2 = jnp.exp(m2 - m1)
        g1 = 1.0 / (1.0 + e2)
        g2 = e2 / (1.0 + e2)
        route = jnp.where(lane == 0, i1.astype(F32), 0.0)
        route = jnp.where(lane == 1, i2.astype(F32), route)
        route = jnp.where(lane == 2, g1, route)
        route = jnp.where(lane == 3, g2, route)
        route_ref[...] = route


def _merge_out(x, ya, yb, gate, wa, wb, wo, g, tm, router=None):
    t = x.shape[0]
    row = lambda w_: pl.BlockSpec((tm, w_), lambda i: (i, 0))
    in_specs = [row(D_MODEL), row(ATT_W), row(RWKV_W), row(2 * D_MODEL),
                _const_spec((ATT_W, D_MODEL)), _const_spec((RWKV_W, D_MODEL)),
                _const_spec((D_MODEL, D_MODEL)), _const_spec((1, D_MODEL))]
    out_specs = [row(D_MODEL), row(D_MODEL)]
    out_shape = [jax.ShapeDtypeStruct((t, D_MODEL), F32),
                 jax.ShapeDtypeStruct((t, D_MODEL), F32 if router is not None else BF)]
    args = [x, ya, yb, gate, wa, wb, wo, g]
    if router is not None:
        in_specs.append(_const_spec(router.shape))
        out_specs.append(row(LANES))
        out_shape.append(jax.ShapeDtypeStruct((t, LANES), F32))
        args.append(router)
    return pl.pallas_call(
        functools.partial(_merge_kernel, router is not None),
        grid=(t // tm,),
        in_specs=in_specs, out_specs=out_specs, out_shape=out_shape,
        compiler_params=_params("parallel"),
        name="merge_out",
    )(*args)


def _swiglu(x, wg, wu, wd):
    s = _dot(x, wg)
    u = _dot(x, wu)
    act = (s * _sigmoid(s) * u).astype(BF)
    return _dot(act, wd)


def _dense_ffn_kernel(x1_ref, h_ref, wg_ref, wu_ref, wd_ref, o_ref):
    o_ref[...] = x1_ref[...] + _swiglu(h_ref[...], wg_ref[...], wu_ref[...], wd_ref[...])


def _dense_ffn(x1, h, wg, wu, wd, tm):
    t = x1.shape[0]
    d_ff = wg.shape[1]
    row = lambda: pl.BlockSpec((tm, D_MODEL), lambda i: (i, 0))
    return pl.pallas_call(
        _dense_ffn_kernel,
        grid=(t // tm,),
        in_specs=[row(), row(), _const_spec((D_MODEL, d_ff)), _const_spec((D_MODEL, d_ff)),
                  _const_spec((d_ff, D_MODEL))],
        out_specs=row(),
        out_shape=jax.ShapeDtypeStruct((t, D_MODEL), F32),
        compiler_params=_params("parallel"),
        name="dense_ffn",
    )(x1, h, wg, wu, wd)


def _row_copy(src_hbm, dst_vmem, src_row, dst_row, sem):
    return pltpu.make_async_copy(src_hbm.at[pl.ds(src_row, 1), :], dst_vmem.at[pl.ds(dst_row, 1), :], sem)


def _moe_kernel(bm, n_half, blk_e_ref, n_used_ref, row_tok_ref, h_hbm, rg_ref, wg_ref, wu_ref, wd_ref,
                y_ref, xbuf, sem):
    i = pl.program_id(0)
    used = i < n_used_ref[0]

    @pl.when(used)
    def _():
        base = i * bm

        def issue(r, carry):
            _row_copy(h_hbm, xbuf, row_tok_ref[base + r], r, sem).start()
            return carry

        lax.fori_loop(0, bm, issue, 0, unroll=8)
        pltpu.make_async_copy(h_hbm.at[pl.ds(0, bm), :], xbuf, sem).wait()
        x = xbuf[...].astype(BF)
        d_e = wg_ref.shape[2]
        step = d_e // n_half
        acc = None
        for j in range(n_half):
            cs = slice(j * step, (j + 1) * step)
            part = _swiglu(x, wg_ref[0, :, cs], wu_ref[0, :, cs], wd_ref[0, cs, :])
            acc = part if acc is None else acc + part
        y_ref[...] = acc * rg_ref[...]

    @pl.when(jnp.logical_not(used))
    def _():
        y_ref[...] = jnp.zeros_like(y_ref)


def _moe_ffn(h, blk_e, n_used, row_tok, row_g, wg, wu, wd, bm):
    n_blk = blk_e.shape[0]
    d_e = wg.shape[2]
    wspec = lambda shape: pl.BlockSpec(shape, lambda i, be, nu, rt: (be[i], 0, 0), pipeline_mode=pl.Buffered(1))
    grid_spec = pltpu.PrefetchScalarGridSpec(
        num_scalar_prefetch=3,
        grid=(n_blk,),
        in_specs=[pl.BlockSpec(memory_space=pl.ANY),
                  pl.BlockSpec((bm, 1), lambda i, be, nu, rt: (i, 0)),
                  wspec((1, D_MODEL, d_e)), wspec((1, D_MODEL, d_e)), wspec((1, d_e, D_MODEL))],
        out_specs=pl.BlockSpec((bm, D_MODEL), lambda i, be, nu, rt: (i, 0)),
        scratch_shapes=[pltpu.VMEM((bm, D_MODEL), F32), pltpu.SemaphoreType.DMA(())],
    )
    return pl.pallas_call(
        functools.partial(_moe_kernel, bm, 2),
        grid_spec=grid_spec,
        out_shape=jax.ShapeDtypeStruct((n_blk * bm, D_MODEL), F32),
        compiler_params=_params("arbitrary"),
        name="moe_ffn",
    )(blk_e, n_used, row_tok, h, row_g, wg, wu, wd)


def _combine_kernel(tc, pos_ref, x1_ref, y_hbm, g_ref, o_ref, ybuf, sem):
    base = pl.program_id(0) * tc

    def issue(r, carry):
        for j in range(TOP_K):
            _row_copy(y_hbm, ybuf.at[j], pos_ref[TOP_K * (base + r) + j], r, sem).start()
        return carry

    lax.fori_loop(0, tc, issue, 0, unroll=4)
    for j in range(TOP_K):
        pltpu.make_async_copy(y_hbm.at[pl.ds(0, tc), :], ybuf.at[j], sem).wait()
    x2 = x1_ref[...] + ybuf[0] + ybuf[1]
    o_ref[...] = _rms(x2, g_ref[...])


def _moe_combine(pos, x1, y, g, tc):
    t = x1.shape[0]
    grid_spec = pltpu.PrefetchScalarGridSpec(
        num_scalar_prefetch=1,
        grid=(t // tc,),
        in_specs=[pl.BlockSpec((tc, D_MODEL), lambda i, ps: (i, 0)),
                  pl.BlockSpec(memory_space=pl.ANY),
                  pl.BlockSpec((1, D_MODEL), lambda i, ps: (0, 0))],
        out_specs=pl.BlockSpec((tc, D_MODEL), lambda i, ps: (i, 0)),
        scratch_shapes=[pltpu.VMEM((TOP_K, tc, D_MODEL), F32), pltpu.SemaphoreType.DMA(())],
    )
    return pl.pallas_call(
        functools.partial(_combine_kernel, tc),
        grid_spec=grid_spec,
        out_shape=jax.ShapeDtypeStruct((t, D_MODEL), F32),
        compiler_params=_params("arbitrary"),
        name="moe_combine",
    )(pos, x1, y, g)


def _route_plan(route, bm):
    n_tok = route.shape[0]
    e = route[:, 0:TOP_K].astype(jnp.int32).reshape(-1)
    gw = route[:, TOP_K:2 * TOP_K].reshape(-1)
    n_as = n_tok * TOP_K
    tok = jnp.repeat(jnp.arange(n_tok, dtype=jnp.int32), TOP_K)
    onehot = (e[:, None] == jnp.arange(N_EXPERTS, dtype=jnp.int32)[None, :]).astype(jnp.int32)
    csum = jnp.cumsum(onehot, axis=0)
    rank = jnp.sum((csum - onehot) * onehot, axis=1)
    counts = csum[-1]
    padded = (counts + bm - 1) // bm * bm
    pends = jnp.cumsum(padded)
    pstarts = pends - padded
    dest = (jnp.sum(pstarts[None, :] * onehot, axis=1) + rank).astype(jnp.int32)
    n_blk = -(-n_as // bm) + N_EXPERTS
    row_tok = jnp.zeros((n_blk * bm,), jnp.int32).at[dest].set(tok)
    row_g = jnp.zeros((n_blk * bm,), F32).at[dest].set(gw)
    blk_start = jnp.arange(n_blk, dtype=jnp.int32) * bm
    blk_e = jnp.minimum(jnp.sum((blk_start[:, None] >= pends[None, :]).astype(jnp.int32), axis=1),
                        N_EXPERTS - 1).astype(jnp.int32)
    n_used = (pends[-1] // bm).astype(jnp.int32).reshape(1)
    return blk_e, n_used, row_tok, row_g.reshape(-1, 1), dest


def _layer_params(l, w_in, mu_shift, w0, w2, a0, a2, g2, k_k, k_a, r_k, lnx_g, lnx_b,
                  w_proj_attn, w_proj_rwkv, w_out, ones_bd):
    row = lambda u: u.reshape(1, -1)
    return dict(
        w_in=w_in[l].astype(BF), mu=row(mu_shift[l]), w0=row(w0[l]), w2=w2[l].astype(BF),
        a0=row(a0[l]), a2=a2[l].astype(BF), g2=g2[l].astype(BF), k_k=row(k_k[l]), k_a=row(k_a[l]),
        r_k=row(r_k[l]), lnx_g=row(lnx_g[l]), lnx_b=row(lnx_b[l]),
        wa=w_proj_attn[l].astype(BF), wb=w_proj_rwkv[l].astype(BF), wo=w_out[l].astype(BF),
        ones_bd=ones_bd)


def kernel(x_prompt, x_sample, cache_k, cache_v, state_wkv, state_shift, norm_mix_g, w_in, attn_sinks, rel_bias, mu_shift, w0, w2, a0, a2, g2, k_k, k_a, r_k, lnx_g, lnx_b, w_proj_attn, w_proj_rwkv, w_out, norm_ffn_g, dense_w_gate, dense_w_up, dense_w_down, router_w, moe_w_gate, moe_w_up, moe_w_down, norm_final_g):
    batch, seq, _ = x_prompt.shape
    dec_batch, dec_seq, _ = x_sample.shape
    assert dec_seq == 1 and seq % WINDOW == 0

    head_id = jnp.arange(RWKV_W, dtype=jnp.int32) // RWKV_HEAD
    ones_bd = (head_id[:, None] == head_id[None, :]).astype(BF)
    bias_tab = _prompt_bias_table(rel_bias)
    bias_by_dist = rel_bias[_t5_bucket(jnp.arange(WINDOW + 1))]
    bias_c = jnp.transpose(bias_by_dist[WINDOW:0:-1], (1, 0))[None]
    bias_n = bias_by_dist[0][None, :, None]

    layers = [_layer_params(l, w_in, mu_shift, w0, w2, a0, a2, g2, k_k, k_a, r_k, lnx_g, lnx_b,
                            w_proj_attn, w_proj_rwkv, w_out, ones_bd) for l in range(DEPTH)]

    def router_pieces(l):
        wr = jnp.zeros((D_MODEL, LANES), F32).at[:, :N_EXPERTS].set(router_w[l // 2])
        hi = wr.astype(BF)
        lo = (wr - hi.astype(F32)).astype(BF)
        return jnp.stack([hi, lo])

    def run_group(x3, is_prompt):
        b, t, _ = x3.shape
        n_tok = b * t
        x = x3.reshape(n_tok, D_MODEL)
        tm = 256 if is_prompt else n_tok
        nk, nv, ns, nz = [], [], [], []
        for l in range(DEPTH):
            p = layers[l]
            q, k, v, z, gate = _inproj(x, norm_mix_g[l].reshape(1, -1), p["w_in"], tm)
            sinks = attn_sinks[l]
            if is_prompt:
                sink_col = jnp.repeat(sinks, WINDOW).reshape(N_KV_HEADS, GQA_GROUP * WINDOW, 1)
                ya = _swa_prompt(q, k, v, bias_tab, sink_col, b, t)
                k4 = k.reshape(b, t, N_KV_HEADS, HEAD_DIM)
                v4 = v.reshape(b, t, N_KV_HEADS, HEAD_DIM)
                nk.append(k4[:, t - WINDOW:])
                nv.append(v4[:, t - WINDOW:])
                z3 = z.reshape(b, t, RWKV_IN)
                prev_z = jnp.zeros((b, 1, RWKV_IN), F32)
                s0 = jnp.zeros((b, RWKV_HEADS, RWKV_HEAD, RWKV_HEAD), F32)
                coef = _rwkv_prep(z3, prev_z, p, 256, None)
                yb3, s_new = _rwkv_scan(coef, s0, p, 64)
                yb = yb3.reshape(n_tok, RWKV_W)
                nz.append(z3[:, -1])
            else:
                ck = cache_k[l].reshape(b, WINDOW, KV_W)
                cv = cache_v[l].reshape(b, WINDOW, KV_W)
                ya3 = _swa_sample(q.reshape(b, N_Q_HEADS, HEAD_DIM), k, v, ck, cv, bias_c, bias_n,
                                  sinks.reshape(1, N_Q_HEADS, 1), 16)
                ya = ya3.reshape(b, ATT_W)
                nk.append(jnp.concatenate([ck[:, 1:], k[:, None]], axis=1)
                          .reshape(b, WINDOW, N_KV_HEADS, HEAD_DIM))
                nv.append(jnp.concatenate([cv[:, 1:], v[:, None]], axis=1)
                          .reshape(b, WINDOW, N_KV_HEADS, HEAD_DIM))
                t_pad = 16
                z3 = jnp.pad(z.reshape(b, 1, RWKV_IN), ((0, 0), (0, t_pad - 1), (0, 0)))
                coef = _rwkv_prep(z3, state_shift[l].reshape(b, 1, RWKV_IN), p, t_pad, 1)
                yb3, s_new = _rwkv_scan(coef, state_wkv[l], p, t_pad)
                yb = yb3[:, 0]
                nz.append(z)
            ns.append(s_new)
            gn = norm_ffn_g[l].reshape(1, -1)
            if l % 2 == 0:
                x1, h = _merge_out(x, ya, yb, gate, p["wa"], p["wb"], p["wo"], gn, tm)
                j = l // 2
                x = _dense_ffn(x1, h, dense_w_gate[j].astype(BF), dense_w_up[j].astype(BF),
                               dense_w_down[j].astype(BF), tm)
            else:
                x1, h, route = _merge_out(x, ya, yb, gate, p["wa"], p["wb"], p["wo"], gn, tm,
                                          router=router_pieces(l))
                j = l // 2
                bm = 512 if is_prompt else 128
                blk_e, n_used, row_tok, row_g, dest = _route_plan(route, bm)
                y = _moe_ffn(h, blk_e, n_used, row_tok, row_g, moe_w_gate[j].astype(BF),
                             moe_w_up[j].astype(BF), moe_w_down[j].astype(BF), bm)
                assert l == DEPTH - 1
                x = _moe_combine(dest, x1, y, norm_final_g.reshape(1, -1), 128)
        return x.reshape(b, t, D_MODEL), jnp.stack(nk), jnp.stack(nv), jnp.stack(ns), jnp.stack(nz)

    y_p, nk_p, nv_p, ns_p, nz_p = run_group(x_prompt, True)
    y_s, nk_s, nv_s, ns_s, nz_s = run_group(x_sample, False)
    return (y_p, y_s, nk_p, nv_p, ns_p, nz_p, nk_s, nv_s, ns_s, nz_s)
```

```python
import functools
import math

import jax
import jax.numpy as jnp
from jax import lax
from jax.experimental import pallas as pl
from jax.experimental.pallas import tpu as pltpu

BF = jnp.bfloat16
F32 = jnp.float32

D_MODEL = 1024
DEPTH = 2
HEAD_DIM = 64
N_Q_HEADS = 8
N_KV_HEADS = 2
GQA_GROUP = 4
ATT_W = 512
KV_W = 128
WINDOW = 128
ATT_SCALE = HEAD_DIM ** -0.5
NUM_BUCKETS = 32
MAX_EXACT = 16
MAX_DISTANCE = 128
NEG_INF = -1e30
RWKV_HEAD = 64
RWKV_W = 512
RWKV_HEADS = 8
D_DECAY_LORA = 64
D_AAA_LORA = 64
D_GATE_LORA = 128
RWKV_IN = 3 * RWKV_W + D_DECAY_LORA + D_AAA_LORA + D_GATE_LORA
GN_EPS = 64e-5
IN_W = ATT_W + 2 * KV_W + RWKV_IN + 2 * D_MODEL
N_EXPERTS = 8
TOP_K = 2
NORM_EPS = 1e-6

VMEM_LIMIT_BYTES = 56 * 1024 * 1024
LANES = 128


def _params(*sem):
    return pltpu.CompilerParams(dimension_semantics=sem, vmem_limit_bytes=VMEM_LIMIT_BYTES)


def _const_spec(shape):
    zeros = (0,) * len(shape)
    return pl.BlockSpec(shape, lambda *_: zeros, pipeline_mode=pl.Buffered(1))


def _dot(a, b):
    return jnp.dot(a, b, preferred_element_type=F32)


def _dot_nt(a, b):
    return lax.dot_general(a, b, (((1,), (1,)), ((), ())), preferred_element_type=F32)


def _dot_tn(a, b):
    return lax.dot_general(a, b, (((0,), (0,)), ((), ())), preferred_element_type=F32)


def _split2(x):
    hi = x.astype(BF)
    lo = (x - hi.astype(F32)).astype(BF)
    return hi, lo


def _dot_exact_rhs(x, w):
    hi, lo = _split2(x)
    return _dot(hi, w) + _dot(lo, w)


def _rms(x, g):
    ms = jnp.mean(x * x, axis=-1, keepdims=True)
    return x * lax.rsqrt(ms + NORM_EPS) * g


def _sigmoid(x):
    return 1.0 / (1.0 + jnp.exp(-x))


def _inproj_kernel(x_ref, g_ref, w_ref, q_ref, k_ref, v_ref, z_ref, gate_ref):
    n = _rms(x_ref[...], g_ref[...]).astype(BF)
    q_ref[...] = _dot(n, w_ref[:, 0:ATT_W]).astype(BF)
    k_ref[...] = _dot(n, w_ref[:, ATT_W:ATT_W + KV_W])
    v_ref[...] = _dot(n, w_ref[:, ATT_W + KV_W:ATT_W + 2 * KV_W])
    z0 = ATT_W + 2 * KV_W
    z_ref[...] = _dot(n, w_ref[:, z0:z0 + RWKV_IN])
    gate_ref[...] = _dot(n, w_ref[:, z0 + RWKV_IN:IN_W])


def _inproj(x, g, w, tm):
    t = x.shape[0]
    row = lambda w_: pl.BlockSpec((tm, w_), lambda i: (i, 0))
    return pl.pallas_call(
        _inproj_kernel,
        grid=(t // tm,),
        in_specs=[row(D_MODEL), _const_spec((1, D_MODEL)), _const_spec((D_MODEL, IN_W))],
        out_specs=[row(ATT_W), row(KV_W), row(KV_W), row(RWKV_IN), row(2 * D_MODEL)],
        out_shape=[jax.ShapeDtypeStruct((t, ATT_W), BF),
                   jax.ShapeDtypeStruct((t, KV_W), F32),
                   jax.ShapeDtypeStruct((t, KV_W), F32),
                   jax.ShapeDtypeStruct((t, RWKV_IN), F32),
                   jax.ShapeDtypeStruct((t, 2 * D_MODEL), F32)],
        compiler_params=_params("parallel"),
        name="inproj",
    )(x, g, w)


def _t5_bucket(dist):
    n = jnp.maximum(dist, 0)
    nf = jnp.maximum(n, 1).astype(F32)
    large = MAX_EXACT + (jnp.log(nf / MAX_EXACT) / math.log(MAX_DISTANCE / MAX_EXACT)
                         * (NUM_BUCKETS - MAX_EXACT)).astype(jnp.int32)
    return jnp.where(n < MAX_EXACT, n, jnp.minimum(large, NUM_BUCKETS - 1))


def _prompt_bias_table(rel_bias):
    q_off = jnp.arange(WINDOW)
    k_off = jnp.arange(2 * WINDOW) - WINDOW
    dist = q_off[:, None] - k_off[None, :]
    b = rel_bias[_t5_bucket(dist)]
    valid = (dist >= 0) & (dist <= WINDOW)
    b = jnp.where(valid[:, :, None], b, NEG_INF)
    b = jnp.transpose(b, (2, 0, 1))
    return b.reshape(N_KV_HEADS, GQA_GROUP * WINDOW, 2 * WINDOW)


def _swa_prompt_kernel(q_ref, kp_ref, kc_ref, vp_ref, vc_ref, bias_ref, sink_ref, o_ref):
    first = pl.program_id(1) == 0
    q = q_ref[...]
    kp, kc, vp, vc = kp_ref[...], kc_ref[...], vp_ref[...], vc_ref[...]
    col = lax.broadcasted_iota(jnp.int32, (GQA_GROUP * WINDOW, 2 * WINDOW), 1)
    pad_mask = jnp.logical_and(first, col < WINDOW)
    outs = []
    for h in range(N_KV_HEADS):
        sl = slice(h * HEAD_DIM, (h + 1) * HEAD_DIM)
        kh = jnp.concatenate([kp[:, sl], kc[:, sl]], axis=0).astype(BF)
        vh = jnp.concatenate([vp[:, sl], vc[:, sl]], axis=0).astype(BF)
        qh = jnp.concatenate(
            [q[:, (GQA_GROUP * h + g) * HEAD_DIM:(GQA_GROUP * h + g + 1) * HEAD_DIM]
             for g in range(GQA_GROUP)], axis=0)
        s = _dot_nt(qh, kh) * ATT_SCALE + bias_ref[h]
        s = jnp.where(pad_mask, NEG_INF, s)
        sink = sink_ref[h]
        m = jnp.maximum(jnp.max(s, axis=-1, keepdims=True), sink)
        p = jnp.exp(s - m)
        denom = jnp.sum(p, axis=-1, keepdims=True) + jnp.exp(sink - m)
        o = _dot(p.astype(BF), vh) / denom
        outs.extend(o[g * WINDOW:(g + 1) * WINDOW] for g in range(GQA_GROUP))
    o_ref[...] = jnp.concatenate(outs, axis=-1).astype(o_ref.dtype)


def _swa_prompt(q, k, v, bias_tab, sink_col, batch, seq):
    nb = seq // WINDOW
    cur = lambda b, i: (b * nb + i, 0)
    prev = lambda b, i: (b * nb + jnp.maximum(i - 1, 0), 0)
    kv_c = pl.BlockSpec((WINDOW, KV_W), cur)
    kv_p = pl.BlockSpec((WINDOW, KV_W), prev)
    return pl.pallas_call(
        _swa_prompt_kernel,
        grid=(batch, nb),
        in_specs=[pl.BlockSpec((WINDOW, ATT_W), cur), kv_p, kv_c, kv_p, kv_c,
                  _const_spec(bias_tab.shape), _const_spec(sink_col.shape)],
        out_specs=pl.BlockSpec((WINDOW, ATT_W), cur),
        out_shape=jax.ShapeDtypeStruct((batch * seq, ATT_W), BF),
        compiler_params=_params("parallel", "parallel"),
        name="swa_prompt",
    )(q, k, k, v, v, bias_tab, sink_col)


def _swa_sample_kernel(q_ref, kn_ref, vn_ref, ck_ref, cv_ref, bc_ref, bn_ref, sink_ref, o_ref):
    q = q_ref[...]
    qf = q.astype(F32)
    kn = kn_ref[...].astype(BF).astype(F32)
    vn = vn_ref[...].astype(BF).astype(F32)
    ck, cv = ck_ref[...], cv_ref[...]
    head = lax.broadcasted_iota(jnp.int32, (1, N_Q_HEADS, 1), 1)
    low = head < GQA_GROUP
    s_h, sn_h = [], []
    for h in range(N_KV_HEADS):
        sl = slice(h * HEAD_DIM, (h + 1) * HEAD_DIM)
        s_h.append(jnp.einsum("bgd,bkd->bgk", q, ck[:, :, sl].astype(BF), preferred_element_type=F32))
        sn_h.append(jnp.sum(qf * kn[:, None, sl], axis=-1, keepdims=True))
    s = jnp.where(low, s_h[0], s_h[1]) * ATT_SCALE + bc_ref[...]
    sn = jnp.where(low, sn_h[0], sn_h[1]) * ATT_SCALE + bn_ref[...]
    sink = sink_ref[...]
    m = jnp.maximum(jnp.maximum(jnp.max(s, axis=-1, keepdims=True), sn), sink)
    p = jnp.exp(s - m)
    pn = jnp.exp(sn - m)
    denom = jnp.sum(p, axis=-1, keepdims=True) + pn + jnp.exp(sink - m)
    pb = p.astype(BF)
    pnb = pn.astype(BF).astype(F32)
    o_h = []
    for h in range(N_KV_HEADS):
        sl = slice(h * HEAD_DIM, (h + 1) * HEAD_DIM)
        o = jnp.einsum("bgk,bkd->bgd", pb, cv[:, :, sl].astype(BF), preferred_element_type=F32)
        o_h.append(o + pnb * vn[:, None, sl])
    o_ref[...] = (jnp.where(low, o_h[0], o_h[1]) / denom).astype(o_ref.dtype)


def _swa_sample(q3, kn, vn, ck, cv, bias_c, bias_n, sink3, bb):
    b = q3.shape[0]
    return pl.pallas_call(
        _swa_sample_kernel,
        grid=(b // bb,),
        in_specs=[pl.BlockSpec((bb, N_Q_HEADS, HEAD_DIM), lambda i: (i, 0, 0)),
                  pl.BlockSpec((bb, KV_W), lambda i: (i, 0)),
                  pl.BlockSpec((bb, KV_W), lambda i: (i, 0)),
                  pl.BlockSpec((bb, WINDOW, KV_W), lambda i: (i, 0, 0)),
                  pl.BlockSpec((bb, WINDOW, KV_W), lambda i: (i, 0, 0)),
                  _const_spec(bias_c.shape), _const_spec(bias_n.shape), _const_spec(sink3.shape)],
        out_specs=pl.BlockSpec((bb, N_Q_HEADS, HEAD_DIM), lambda i: (i, 0, 0)),
        out_shape=jax.ShapeDtypeStruct((b, N_Q_HEADS, HEAD_DIM), BF),
        compiler_params=_params("parallel"),
        name="swa_sample",
    )(q3, kn, vn, ck, cv, bias_c, bias_n, sink3)


def _rwkv_prep_kernel(t_real, z_ref, zp8_ref, pz_ref, mu_ref, w0_ref, w2_ref, a0_ref, a2_ref, g2_ref,
                      kk_ref, ka_ref, ones_ref,
                      r_ref, lw_ref, k2_ref, v_ref, a_ref, b_ref, g_ref):
    i = pl.program_id(1)
    z = z_ref[0]
    tm = z.shape[0]
    prev_row = jnp.where(i == 0, pz_ref[0], zp8_ref[0, 7:8, :])
    row = lax.broadcasted_iota(jnp.int32, (tm, 1), 0)
    zs = jnp.where(row == 0, prev_row, pltpu.roll(z, 1, 0))
    zz = z + (zs - z) * mu_ref[...]
    r = zz[:, 0:RWKV_W]
    k = zz[:, RWKV_W:2 * RWKV_W]
    v = zz[:, 2 * RWKV_W:3 * RWKV_W]
    o = 3 * RWKV_W
    wl = zz[:, o:o + D_DECAY_LORA]
    al = zz[:, o + D_DECAY_LORA:o + D_DECAY_LORA + D_AAA_LORA]
    gl = zz[:, o + D_DECAY_LORA + D_AAA_LORA:RWKV_IN]
    wpre = -(w0_ref[...] + _dot(jnp.tanh(wl).astype(BF), w2_ref[...]))
    softplus = jnp.maximum(wpre, 0.0) + jnp.log1p(jnp.exp(-jnp.abs(wpre)))
    lw = -jnp.exp(-softplus - 0.5)
    a = _sigmoid(a0_ref[...] + _dot(al.astype(BF), a2_ref[...]))
    g = _dot(_sigmoid(gl).astype(BF), g2_ref[...])
    kkv = k * kk_ref[...]
    ss = _dot_exact_rhs(kkv * kkv, ones_ref[...])
    kk = kkv / jnp.maximum(jnp.sqrt(ss), 1e-12)
    k2 = k * (1.0 + (a - 1.0) * ka_ref[...])
    na = -kk
    nb = kk * a
    if t_real is not None:
        keep = (row + i * tm) < t_real
        zero = jnp.zeros_like(r)
        r, lw, k2, v = (jnp.where(keep, u, zero) for u in (r, lw, k2, v))
        na, nb, g = (jnp.where(keep, u, zero) for u in (na, nb, g))
    r_ref[0], lw_ref[0], k2_ref[0], v_ref[0] = r, lw, k2, v
    a_ref[0], b_ref[0], g_ref[0] = na, nb, g


def _rwkv_prep(z3, prev_z, p, tm, t_real):
    b, t, _ = z3.shape
    blk8 = tm // 8
    vec = lambda w_: _const_spec((1, w_))
    out_spec = pl.BlockSpec((1, tm, RWKV_W), lambda bi, i: (bi, i, 0))
    return pl.pallas_call(
        functools.partial(_rwkv_prep_kernel, t_real),
        grid=(b, t // tm),
        in_specs=[pl.BlockSpec((1, tm, RWKV_IN), lambda bi, i: (bi, i, 0)),
                  pl.BlockSpec((1, 8, RWKV_IN), lambda bi, i: (bi, jnp.maximum(i * blk8 - 1, 0), 0)),
                  pl.BlockSpec((1, 1, RWKV_IN), lambda bi, i: (bi, 0, 0)),
                  vec(RWKV_IN), vec(RWKV_W), _const_spec((D_DECAY_LORA, RWKV_W)),
                  vec(RWKV_W), _const_spec((D_AAA_LORA, RWKV_W)), _const_spec((D_GATE_LORA, RWKV_W)),
                  vec(RWKV_W), vec(RWKV_W), _const_spec((RWKV_W, RWKV_W))],
        out_specs=[out_spec] * 7,
        out_shape=[jax.ShapeDtypeStruct((b, t, RWKV_W), F32)] * 7,
        compiler_params=_params("parallel", "parallel"),
        name="rwkv_prep",
    )(z3, z3, prev_z, p["mu"], p["w0"], p["w2"], p["a0"], p["a2"], p["g2"], p["k_k"], p["k_a"], p["ones_bd"])


def _rwkv_scan_kernel(nb, chunk, r_ref, lw_ref, k_ref, v_ref, a_ref, b_ref, g_ref, s0_ref,
                      rk_ref, lg_ref, lb_ref, ones_ref, y_ref, sout_ref, s_scr):
    c = pl.program_id(1)

    @pl.when(c == 0)
    def _():
        s_scr[...] = s0_ref[...]

    ri = lax.broadcasted_iota(jnp.int32, (chunk, chunk), 0)
    ci = lax.broadcasted_iota(jnp.int32, (chunk, chunk), 1)
    incl = ri >= ci
    strict = ri > ci
    tri = incl.astype(BF)
    eye = (ri == ci).astype(F32)
    n_sq = int(math.log2(chunk)) - 1

    chains = [(bi, h) for bi in range(nb) for h in range(RWKV_HEADS)]
    hs = lambda h: slice(h * RWKV_HEAD, (h + 1) * RWKV_HEAD)

    ra, aa, bt, kt, bh, kh, vb, g_all = [], [], [], [], [], [], [], []
    for bi in range(nb):
        r, lw, k, v = r_ref[bi], lw_ref[bi], k_ref[bi], v_ref[bi]
        a, b = a_ref[bi], b_ref[bi]
        hi, lo = _split2(lw)
        lo2 = (lw - hi.astype(F32) - lo.astype(F32)).astype(BF)
        cum = _dot(tri, hi) + _dot(tri, lo) + _dot(tri, lo2)
        tail = cum[chunk - 1:chunk, :]
        g_inv = jnp.exp(-cum)
        g_tail = jnp.exp(tail - cum)
        g_all.append(jnp.exp(tail))
        ra.append((r * jnp.exp(cum)).astype(BF))
        aa.append((a * jnp.exp(cum - lw)).astype(BF))
        bt.append((b * g_inv).astype(BF))
        kt.append((k * g_inv).astype(BF))
        bh.append((b * g_tail).astype(BF))
        kh.append((k * g_tail).astype(BF))
        vb.append(v.astype(BF))

    s_old = [s_scr[bi, h] for bi, h in chains]
    ar = [jnp.concatenate([aa[bi][:, hs(h)], ra[bi][:, hs(h)]], axis=0) for bi, h in chains]
    v_h = [vb[bi][:, hs(h)] for bi, h in chains]
    gb = [_dot_nt(ar[n], bt[bi][:, hs(h)]) for n, (bi, h) in enumerate(chains)]
    gk = [_dot_nt(ar[n], kt[bi][:, hs(h)]) for n, (bi, h) in enumerate(chains)]
    p = [_dot_nt(ar[n], s_old[n].astype(BF)) for n in range(len(chains))]
    l_ab = [jnp.where(strict, x[:chunk], 0.0) for x in gb]
    l_ak = [jnp.where(strict, x[:chunk], 0.0).astype(BF) for x in gk]
    m_rb = [jnp.where(incl, x[chunk:], 0.0).astype(BF) for x in gb]
    m_rk = [jnp.where(incl, x[chunk:], 0.0).astype(BF) for x in gk]
    rhs = [p[n][:chunk] + _dot(l_ak[n], v_h[n]) for n in range(len(chains))]
    t_inv = [eye + x for x in l_ab]
    lp = l_ab
    for _ in range(n_sq):
        lpb = [x.astype(BF) for x in lp]
        lp = [_dot(x, x) for x in lpb]
        t_inv = [t + _dot(x.astype(BF), t.astype(BF)) for x, t in zip(lp, t_inv)]
    ub = [_dot(t.astype(BF), x.astype(BF)).astype(BF) for t, x in zip(t_inv, rhs)]
    y_h = [p[n][chunk:] + _dot(m_rb[n], ub[n]) + _dot(m_rk[n], v_h[n]) for n in range(len(chains))]
    s_new = [s_old[n] * g_all[bi][:, hs(h)] + _dot_tn(ub[n], bh[bi][:, hs(h)]) + _dot_tn(v_h[n], kh[bi][:, hs(h)])
             for n, (bi, h) in enumerate(chains)]
    for n, (bi, h) in enumerate(chains):
        s_scr[bi, h] = s_new[n]

    ones = ones_ref[...]
    inv_n = 1.0 / RWKV_HEAD
    for bi in range(nb):
        y = jnp.concatenate(y_h[bi * RWKV_HEADS:(bi + 1) * RWKV_HEADS], axis=-1)
        r, k, v = r_ref[bi], k_ref[bi], v_ref[bi]
        mean = _dot_exact_rhs(y, ones) * inv_n
        d = y - mean
        var = _dot_exact_rhs(d * d, ones) * inv_n
        yn = d * lax.rsqrt(var + GN_EPS) * lg_ref[...] + lb_ref[...]
        bonus = _dot_exact_rhs(r * k * rk_ref[...], ones) * v
        y_ref[bi] = ((yn + bonus) * g_ref[bi]).astype(y_ref.dtype)

    @pl.when(c == pl.num_programs(1) - 1)
    def _():
        sout_ref[...] = s_scr[...]


def _rwkv_scan(coef, s0, p, nb, chunk):
    r = coef[0]
    b, t, _ = r.shape
    seq_spec = pl.BlockSpec((nb, chunk, RWKV_W), lambda bi, c: (bi, c, 0))
    st_spec = pl.BlockSpec((nb, RWKV_HEADS, RWKV_HEAD, RWKV_HEAD), lambda bi, c: (bi, 0, 0, 0))
    vec = _const_spec((1, RWKV_W))
    return pl.pallas_call(
        functools.partial(_rwkv_scan_kernel, nb, chunk),
        grid=(b // nb, t // chunk),
        in_specs=[seq_spec] * 7 + [st_spec, vec, vec, vec, _const_spec((RWKV_W, RWKV_W))],
        out_specs=[seq_spec, st_spec],
        out_shape=[jax.ShapeDtypeStruct((b, t, RWKV_W), BF),
                   jax.ShapeDtypeStruct((b, RWKV_HEADS, RWKV_HEAD, RWKV_HEAD), F32)],
        scratch_shapes=[pltpu.VMEM((nb, RWKV_HEADS, RWKV_HEAD, RWKV_HEAD), F32)],
        compiler_params=_params("parallel", "arbitrary"),
        name="rwkv_scan",
    )(*coef, s0, p["r_k"], p["lnx_g"], p["lnx_b"], p["ones_bd"])


def _merge_kernel(with_router, x_ref, ya_ref, yb_ref, gate_ref, wa_ref, wb_ref, wo_ref, g_ref, *rest):
    if with_router:
        wr_ref, x1_ref, h_ref, route_ref = rest
    else:
        x1_ref, h_ref = rest
    gate = gate_ref[...]
    pa = _dot(ya_ref[...], wa_ref[...])
    pb = _dot(yb_ref[...], wb_ref[...])
    merged = _sigmoid(gate[:, :D_MODEL]) * pa + _sigmoid(gate[:, D_MODEL:]) * pb
    x1 = x_ref[...] + _dot(merged.astype(BF), wo_ref[...])
    x1_ref[...] = x1
    h = _rms(x1, g_ref[...])
    h_ref[...] = h.astype(h_ref.dtype)
    if with_router:
        hh, hl = _split2(h)
        wh, wl = wr_ref[0], wr_ref[1]
        logits = _dot(hh, wh) + _dot(hl, wh) + _dot(hh, wl)
        lane = lax.broadcasted_iota(jnp.int32, logits.shape, 1)
        logits = jnp.where(lane < N_EXPERTS, logits, -jnp.inf)
        m1 = jnp.max(logits, axis=-1, keepdims=True)
        i1 = jnp.min(jnp.where(logits == m1, lane, LANES), axis=-1, keepdims=True)
        rest_l = jnp.where(lane == i1, -jnp.inf, logits)
        m2 = jnp.max(rest_l, axis=-1, keepdims=True)
        i2 = jnp.min(jnp.where(rest_l == m2, lane, LANES), axis=-1, keepdims=True)
        e2 = jnp.exp(m2 - m1)
        g1 = 1.0 / (1.0 + e2)
        g2 = e2 / (1.0 + e2)
        route = jnp.where(lane == 0, i1.astype(F32), 0.0)
        route = jnp.where(lane == 1, i2.astype(F32), route)
        route = jnp.where(lane == 2, g1, route)
        route = jnp.where(lane == 3, g2, route)
        route_ref[...] = route


def _merge_out(x, ya, yb, gate, wa, wb, wo, g, tm, router=None):
    t = x.shape[0]
    row = lambda w_: pl.BlockSpec((tm, w_), lambda i: (i, 0))
    in_specs = [row(D_MODEL), row(ATT_W), row(RWKV_W), row(2 * D_MODEL),
                _const_spec((ATT_W, D_MODEL)), _const_spec((RWKV_W, D_MODEL)),
                _const_spec((D_MODEL, D_MODEL)), _const_spec((1, D_MODEL))]
    out_specs = [row(D_MODEL), row(D_MODEL)]
    out_shape = [jax.ShapeDtypeStruct((t, D_MODEL), F32),
                 jax.ShapeDtypeStruct((t, D_MODEL), F32 if router is not None else BF)]
    args = [x, ya, yb, gate, wa, wb, wo, g]
    if router is not None:
        in_specs.append(_const_spec(router.shape))
        out_specs.append(row(LANES))
        out_shape.append(jax.ShapeDtypeStruct((t, LANES), F32))
        args.append(router)
    return pl.pallas_call(
        functools.partial(_merge_kernel, router is not None),
        grid=(t // tm,),
        in_specs=in_specs, out_specs=out_specs, out_shape=out_shape,
        compiler_params=_params("parallel"),
        name="merge_out",
    )(*args)


def _swiglu(x, wg, wu, wd):
    s = _dot(x, wg)
    u = _dot(x, wu)
    act = (s * _sigmoid(s) * u).astype(BF)
    return _dot(act, wd)


def _dense_ffn_kernel(x1_ref, h_ref, wg_ref, wu_ref, wd_ref, o_ref):
    o_ref[...] = x1_ref[...] + _swiglu(h_ref[...], wg_ref[...], wu_ref[...], wd_ref[...])


def _dense_ffn(x1, h, wg, wu, wd, tm):
    t = x1.shape[0]
    d_ff = wg.shape[1]
    row = lambda: pl.BlockSpec((tm, D_MODEL), lambda i: (i, 0))
    return pl.pallas_call(
        _dense_ffn_kernel,
        grid=(t // tm,),
        in_specs=[row(), row(), _const_spec((D_MODEL, d_ff)), _const_spec((D_MODEL, d_ff)),
                  _const_spec((d_ff, D_MODEL))],
        out_specs=row(),
        out_shape=jax.ShapeDtypeStruct((t, D_MODEL), F32),
        compiler_params=_params("parallel"),
        name="dense_ffn",
    )(x1, h, wg, wu, wd)


def _row_copy(src_hbm, dst_vmem, src_row, dst_row, sem):
    return pltpu.make_async_copy(src_hbm.at[pl.ds(src_row, 1), :], dst_vmem.at[pl.ds(dst_row, 1), :], sem)


def _moe_kernel(bm, n_half, blk_e_ref, n_used_ref, row_tok_ref, h_hbm, rg_ref, wg_ref, wu_ref, wd_ref,
                y_ref, xbuf, sem):
    i = pl.program_id(0)
    used = i < n_used_ref[0]

    @pl.when(used)
    def _():
        base = i * bm

        def issue(r, carry):
            _row_copy(h_hbm, xbuf, row_tok_ref[base + r], r, sem).start()
            return carry

        lax.fori_loop(0, bm, issue, 0, unroll=8)
        pltpu.make_async_copy(h_hbm.at[pl.ds(0, bm), :], xbuf, sem).wait()
        x = xbuf[...].astype(BF)
        d_e = wg_ref.shape[2]
        step = d_e // n_half
        acc = None
        for j in range(n_half):
            cs = slice(j * step, (j + 1) * step)
            part = _swiglu(x, wg_ref[0, :, cs], wu_ref[0, :, cs], wd_ref[0, cs, :])
            acc = part if acc is None else acc + part
        y_ref[...] = acc * rg_ref[...]

    @pl.when(jnp.logical_not(used))
    def _():
        y_ref[...] = jnp.zeros_like(y_ref)


def _moe_ffn(h, blk_e, n_used, row_tok, row_g, wg, wu, wd, bm):
    n_blk = blk_e.shape[0]
    d_e = wg.shape[2]
    wspec = lambda shape: pl.BlockSpec(shape, lambda i, be, nu, rt: (be[i], 0, 0), pipeline_mode=pl.Buffered(1))
    grid_spec = pltpu.PrefetchScalarGridSpec(
        num_scalar_prefetch=3,
        grid=(n_blk,),
        in_specs=[pl.BlockSpec(memory_space=pl.ANY),
                  pl.BlockSpec((bm, 1), lambda i, be, nu, rt: (i, 0)),
                  wspec((1, D_MODEL, d_e)), wspec((1, D_MODEL, d_e)), wspec((1, d_e, D_MODEL))],
        out_specs=pl.BlockSpec((bm, D_MODEL), lambda i, be, nu, rt: (i, 0)),
        scratch_shapes=[pltpu.VMEM((bm, D_MODEL), F32), pltpu.SemaphoreType.DMA(())],
    )
    return pl.pallas_call(
        functools.partial(_moe_kernel, bm, 2),
        grid_spec=grid_spec,
        out_shape=jax.ShapeDtypeStruct((n_blk * bm, D_MODEL), F32),
        compiler_params=_params("arbitrary"),
        name="moe_ffn",
    )(blk_e, n_used, row_tok, h, row_g, wg, wu, wd)


def _combine_kernel(tc, pos_ref, x1_ref, y_hbm, g_ref, o_ref, ybuf, sem):
    base = pl.program_id(0) * tc

    def issue(r, carry):
        for j in range(TOP_K):
            _row_copy(y_hbm, ybuf.at[j], pos_ref[TOP_K * (base + r) + j], r, sem).start()
        return carry

    lax.fori_loop(0, tc, issue, 0, unroll=4)
    for j in range(TOP_K):
        pltpu.make_async_copy(y_hbm.at[pl.ds(0, tc), :], ybuf.at[j], sem).wait()
    x2 = x1_ref[...] + ybuf[0] + ybuf[1]
    o_ref[...] = _rms(x2, g_ref[...])


def _moe_combine(pos, x1, y, g, tc):
    t = x1.shape[0]
    grid_spec = pltpu.PrefetchScalarGridSpec(
        num_scalar_prefetch=1,
        grid=(t // tc,),
        in_specs=[pl.BlockSpec((tc, D_MODEL), lambda i, ps: (i, 0)),
                  pl.BlockSpec(memory_space=pl.ANY),
                  pl.BlockSpec((1, D_MODEL), lambda i, ps: (0, 0))],
        out_specs=pl.BlockSpec((tc, D_MODEL), lambda i, ps: (i, 0)),
        scratch_shapes=[pltpu.VMEM((TOP_K, tc, D_MODEL), F32), pltpu.SemaphoreType.DMA(())],
    )
    return pl.pallas_call(
        functools.partial(_combine_kernel, tc),
        grid_spec=grid_spec,
        out_shape=jax.ShapeDtypeStruct((t, D_MODEL), F32),
        compiler_params=_params("arbitrary"),
        name="moe_combine",
    )(pos, x1, y, g)


def _route_plan(route, bm):
    n_tok = route.shape[0]
    e = route[:, 0:TOP_K].astype(jnp.int32).reshape(-1)
    gw = route[:, TOP_K:2 * TOP_K].reshape(-1)
    n_as = n_tok * TOP_K
    tok = jnp.repeat(jnp.arange(n_tok, dtype=jnp.int32), TOP_K)
    onehot = (e[:, None] == jnp.arange(N_EXPERTS, dtype=jnp.int32)[None, :]).astype(jnp.int32)
    csum = jnp.cumsum(onehot, axis=0)
    rank = jnp.sum((csum - onehot) * onehot, axis=1)
    counts = csum[-1]
    padded = (counts + bm - 1) // bm * bm
    pends = jnp.cumsum(padded)
    pstarts = pends - padded
    dest = (jnp.sum(pstarts[None, :] * onehot, axis=1) + rank).astype(jnp.int32)
    n_blk = -(-n_as // bm) + N_EXPERTS
    row_tok = jnp.zeros((n_blk * bm,), jnp.int32).at[dest].set(tok)
    row_g = jnp.zeros((n_blk * bm,), F32).at[dest].set(gw)
    blk_start = jnp.arange(n_blk, dtype=jnp.int32) * bm
    blk_e = jnp.minimum(jnp.sum((blk_start[:, None] >= pends[None, :]).astype(jnp.int32), axis=1),
                        N_EXPERTS - 1).astype(jnp.int32)
    n_used = (pends[-1] // bm).astype(jnp.int32).reshape(1)
    return blk_e, n_used, row_tok, row_g.reshape(-1, 1), dest


def _layer_params(l, w_in, mu_shift, w0, w2, a0, a2, g2, k_k, k_a, r_k, lnx_g, lnx_b,
                  w_proj_attn, w_proj_rwkv, w_out, ones_bd):
    row = lambda u: u.reshape(1, -1)
    return dict(
        w_in=w_in[l].astype(BF), mu=row(mu_shift[l]), w0=row(w0[l]), w2=w2[l].astype(BF),
        a0=row(a0[l]), a2=a2[l].astype(BF), g2=g2[l].astype(BF), k_k=row(k_k[l]), k_a=row(k_a[l]),
        r_k=row(r_k[l]), lnx_g=row(lnx_g[l]), lnx_b=row(lnx_b[l]),
        wa=w_proj_attn[l].astype(BF), wb=w_proj_rwkv[l].astype(BF), wo=w_out[l].astype(BF),
        ones_bd=ones_bd)


def kernel(x_prompt, x_sample, cache_k, cache_v, state_wkv, state_shift, norm_mix_g, w_in, attn_sinks, rel_bias, mu_shift, w0, w2, a0, a2, g2, k_k, k_a, r_k, lnx_g, lnx_b, w_proj_attn, w_proj_rwkv, w_out, norm_ffn_g, dense_w_gate, dense_w_up, dense_w_down, router_w, moe_w_gate, moe_w_up, moe_w_down, norm_final_g):
    batch, seq, _ = x_prompt.shape
    dec_batch, dec_seq, _ = x_sample.shape
    assert dec_seq == 1 and seq % WINDOW == 0

    head_id = jnp.arange(RWKV_W, dtype=jnp.int32) // RWKV_HEAD
    ones_bd = (head_id[:, None] == head_id[None, :]).astype(BF)
    bias_tab = _prompt_bias_table(rel_bias)
    bias_by_dist = rel_bias[_t5_bucket(jnp.arange(WINDOW + 1))]
    bias_c = jnp.transpose(bias_by_dist[WINDOW:0:-1], (1, 0))[None]
    bias_n = bias_by_dist[0][None, :, None]

    layers = [_layer_params(l, w_in, mu_shift, w0, w2, a0, a2, g2, k_k, k_a, r_k, lnx_g, lnx_b,
                            w_proj_attn, w_proj_rwkv, w_out, ones_bd) for l in range(DEPTH)]

    def router_pieces(l):
        wr = jnp.zeros((D_MODEL, LANES), F32).at[:, :N_EXPERTS].set(router_w[l // 2])
        hi = wr.astype(BF)
        lo = (wr - hi.astype(F32)).astype(BF)
        return jnp.stack([hi, lo])

    def run_group(x3, is_prompt):
        b, t, _ = x3.shape
        n_tok = b * t
        x = x3.reshape(n_tok, D_MODEL)
        tm = 256 if is_prompt else n_tok
        nk, nv, ns, nz = [], [], [], []
        for l in range(DEPTH):
            p = layers[l]
            q, k, v, z, gate = _inproj(x, norm_mix_g[l].reshape(1, -1), p["w_in"], tm)
            sinks = attn_sinks[l]
            if is_prompt:
                sink_col = jnp.repeat(sinks, WINDOW).reshape(N_KV_HEADS, GQA_GROUP * WINDOW, 1)
                ya = _swa_prompt(q, k, v, bias_tab, sink_col, b, t)
                k4 = k.reshape(b, t, N_KV_HEADS, HEAD_DIM)
                v4 = v.reshape(b, t, N_KV_HEADS, HEAD_DIM)
                nk.append(k4[:, t - WINDOW:])
                nv.append(v4[:, t - WINDOW:])
                z3 = z.reshape(b, t, RWKV_IN)
                prev_z = jnp.zeros((b, 1, RWKV_IN), F32)
                s0 = jnp.zeros((b, RWKV_HEADS, RWKV_HEAD, RWKV_HEAD), F32)
                coef = _rwkv_prep(z3, prev_z, p, 256, None)
                yb3, s_new = _rwkv_scan(coef, s0, p, b, 64)
                yb = yb3.reshape(n_tok, RWKV_W)
                nz.append(z3[:, -1])
            else:
                ck = cache_k[l].reshape(b, WINDOW, KV_W)
                cv = cache_v[l].reshape(b, WINDOW, KV_W)
                ya3 = _swa_sample(q.reshape(b, N_Q_HEADS, HEAD_DIM), k, v, ck, cv, bias_c, bias_n,
                                  sinks.reshape(1, N_Q_HEADS, 1), 16)
                ya = ya3.reshape(b, ATT_W)
                nk.append(jnp.concatenate([ck[:, 1:], k[:, None]], axis=1)
                          .reshape(b, WINDOW, N_KV_HEADS, HEAD_DIM))
                nv.append(jnp.concatenate([cv[:, 1:], v[:, None]], axis=1)
                          .reshape(b, WINDOW, N_KV_HEADS, HEAD_DIM))
                t_pad = 16
                z3 = jnp.pad(z.reshape(b, 1, RWKV_IN), ((0, 0), (0, t_pad - 1), (0, 0)))
                coef = _rwkv_prep(z3, state_shift[l].reshape(b, 1, RWKV_IN), p, t_pad, 1)
                yb3, s_new = _rwkv_scan(coef, state_wkv[l], p, 4, t_pad)
                yb = yb3[:, 0]
                nz.append(z)
            ns.append(s_new)
            gn = norm_ffn_g[l].reshape(1, -1)
            if l % 2 == 0:
                x1, h = _merge_out(x, ya, yb, gate, p["wa"], p["wb"], p["wo"], gn, tm)
                j = l // 2
                x = _dense_ffn(x1, h, dense_w_gate[j].astype(BF), dense_w_up[j].astype(BF),
                               dense_w_down[j].astype(BF), tm)
            else:
                x1, h, route = _merge_out(x, ya, yb, gate, p["wa"], p["wb"], p["wo"], gn, tm,
                                          router=router_pieces(l))
                j = l // 2
                bm = 512 if is_prompt else 128
                blk_e, n_used, row_tok, row_g, dest = _route_plan(route, bm)
                y = _moe_ffn(h, blk_e, n_used, row_tok, row_g, moe_w_gate[j].astype(BF),
                             moe_w_up[j].astype(BF), moe_w_down[j].astype(BF), bm)
                assert l == DEPTH - 1
                x = _moe_combine(dest, x1, y, norm_final_g.reshape(1, -1), 128)
        return x.reshape(b, t, D_MODEL), jnp.stack(nk), jnp.stack(nv), jnp.stack(ns), jnp.stack(nz)

    y_p, nk_p, nv_p, ns_p, nz_p = run_group(x_prompt, True)
    y_s, nk_s, nv_s, ns_s, nz_s = run_group(x_sample, False)
    return (y_p, y_s, nk_p, nv_p, ns_p, nz_p, nk_s, nv_s, ns_s, nz_s)
```

```python
import functools
import math

import jax
import jax.numpy as jnp
from jax import lax
from jax.experimental import pallas as pl
from jax.experimental.pallas import tpu as pltpu

BF = jnp.bfloat16
F32 = jnp.float32

D_MODEL = 1024
DEPTH = 2
HEAD_DIM = 64
N_Q_HEADS = 8
N_KV_HEADS = 2
GQA_GROUP = 4
ATT_W = 512
KV_W = 128
WINDOW = 128
ATT_SCALE = HEAD_DIM ** -0.5
NUM_BUCKETS = 32
MAX_EXACT = 16
MAX_DISTANCE = 128
NEG_INF = -1e30
RWKV_HEAD = 64
RWKV_W = 512
RWKV_HEADS = 8
D_DECAY_LORA = 64
D_AAA_LORA = 64
D_GATE_LORA = 128
RWKV_IN = 3 * RWKV_W + D_DECAY_LORA + D_AAA_LORA + D_GATE_LORA
GN_EPS = 64e-5
IN_W = ATT_W + 2 * KV_W + RWKV_IN + 2 * D_MODEL
N_EXPERTS = 8
TOP_K = 2
NORM_EPS = 1e-6

VMEM_LIMIT_BYTES = 56 * 1024 * 1024
LANES = 128


def _params(*sem):
    return pltpu.CompilerParams(dimension_semantics=sem, vmem_limit_bytes=VMEM_LIMIT_BYTES)


def _const_spec(shape):
    zeros = (0,) * len(shape)
    return pl.BlockSpec(shape, lambda *_: zeros, pipeline_mode=pl.Buffered(1))


def _dot(a, b):
    return jnp.dot(a, b, preferred_element_type=F32)


def _dot_nt(a, b):
    return lax.dot_general(a, b, (((1,), (1,)), ((), ())), preferred_element_type=F32)


def _dot_tn(a, b):
    return lax.dot_general(a, b, (((0,), (0,)), ((), ())), preferred_element_type=F32)


def _split2(x):
    hi = x.astype(BF)
    lo = (x - hi.astype(F32)).astype(BF)
    return hi, lo


def _dot_exact_rhs(x, w):
    hi, lo = _split2(x)
    return _dot(hi, w) + _dot(lo, w)


def _rms(x, g):
    ms = jnp.mean(x * x, axis=-1, keepdims=True)
    return x * lax.rsqrt(ms + NORM_EPS) * g


def _sigmoid(x):
    return 1.0 / (1.0 + jnp.exp(-x))


def _inproj_kernel(x_ref, g_ref, w_ref, q_ref, k_ref, v_ref, z_ref, gate_ref):
    n = _rms(x_ref[...], g_ref[...]).astype(BF)
    q_ref[...] = _dot(n, w_ref[:, 0:ATT_W]).astype(BF)
    k_ref[...] = _dot(n, w_ref[:, ATT_W:ATT_W + KV_W])
    v_ref[...] = _dot(n, w_ref[:, ATT_W + KV_W:ATT_W + 2 * KV_W])
    z0 = ATT_W + 2 * KV_W
    z_ref[...] = _dot(n, w_ref[:, z0:z0 + RWKV_IN])
    gate_ref[...] = _dot(n, w_ref[:, z0 + RWKV_IN:IN_W])


def _inproj(x, g, w, tm):
    t = x.shape[0]
    row = lambda w_: pl.BlockSpec((tm, w_), lambda i: (i, 0))
    return pl.pallas_call(
        _inproj_kernel,
        grid=(t // tm,),
        in_specs=[row(D_MODEL), _const_spec((1, D_MODEL)), _const_spec((D_MODEL, IN_W))],
        out_specs=[row(ATT_W), row(KV_W), row(KV_W), row(RWKV_IN), row(2 * D_MODEL)],
        out_shape=[jax.ShapeDtypeStruct((t, ATT_W), BF),
                   jax.ShapeDtypeStruct((t, KV_W), F32),
                   jax.ShapeDtypeStruct((t, KV_W), F32),
                   jax.ShapeDtypeStruct((t, RWKV_IN), F32),
                   jax.ShapeDtypeStruct((t, 2 * D_MODEL), F32)],
        compiler_params=_params("parallel"),
        name="inproj",
    )(x, g, w)


def _t5_bucket(dist):
    n = jnp.maximum(dist, 0)
    nf = jnp.maximum(n, 1).astype(F32)
    large = MAX_EXACT + (jnp.log(nf / MAX_EXACT) / math.log(MAX_DISTANCE / MAX_EXACT)
                         * (NUM_BUCKETS - MAX_EXACT)).astype(jnp.int32)
    return jnp.where(n < MAX_EXACT, n, jnp.minimum(large, NUM_BUCKETS - 1))


def _prompt_bias_table(rel_bias):
    q_off = jnp.arange(WINDOW)
    k_off = jnp.arange(2 * WINDOW) - WINDOW
    dist = q_off[:, None] - k_off[None, :]
    b = rel_bias[_t5_bucket(dist)]
    valid = (dist >= 0) & (dist <= WINDOW)
    b = jnp.where(valid[:, :, None], b, NEG_INF)
    b = jnp.transpose(b, (2, 0, 1))
    return b.reshape(N_KV_HEADS, GQA_GROUP * WINDOW, 2 * WINDOW)


def _swa_prompt_kernel(nb, q_ref, kp_ref, kc_ref, vp_ref, vc_ref, bias_ref, sink_ref, o_ref):
    first = pl.program_id(0) == 0
    col = lax.broadcasted_iota(jnp.int32, (GQA_GROUP * WINDOW, 2 * WINDOW), 1)
    pad_mask = jnp.logical_and(first, col < WINDOW)
    chains = [(b, h) for b in range(nb) for h in range(N_KV_HEADS)]
    hs = lambda h: slice(h * HEAD_DIM, (h + 1) * HEAD_DIM)
    kh = [jnp.concatenate([kp_ref[b][:, hs(h)], kc_ref[b][:, hs(h)]], axis=0).astype(BF) for b, h in chains]
    vh = [jnp.concatenate([vp_ref[b][:, hs(h)], vc_ref[b][:, hs(h)]], axis=0).astype(BF) for b, h in chains]
    qh = [jnp.concatenate([q_ref[b][:, hs(GQA_GROUP * h + g)] for g in range(GQA_GROUP)], axis=0)
          for b, h in chains]
    s = [_dot_nt(qh[n], kh[n]) * ATT_SCALE + bias_ref[h] for n, (b, h) in enumerate(chains)]
    s = [jnp.where(pad_mask, NEG_INF, x) for x in s]
    m = [jnp.maximum(jnp.max(x, axis=-1, keepdims=True), sink_ref[h]) for x, (b, h) in zip(s, chains)]
    p = [jnp.exp(x - mm) for x, mm in zip(s, m)]
    denom = [jnp.sum(x, axis=-1, keepdims=True) + jnp.exp(sink_ref[h] - mm)
             for x, mm, (b, h) in zip(p, m, chains)]
    o = [_dot(x.astype(BF), vv) / d for x, vv, d in zip(p, vh, denom)]
    for b in range(nb):
        pieces = [o[b * N_KV_HEADS + h][g * WINDOW:(g + 1) * WINDOW]
                  for h in range(N_KV_HEADS) for g in range(GQA_GROUP)]
        o_ref[b] = jnp.concatenate(pieces, axis=-1).astype(o_ref.dtype)


def _swa_prompt(q3, k3, v3, bias_tab, sink_col):
    batch, seq, _ = q3.shape
    cur = lambda i: (0, i, 0)
    prev = lambda i: (0, jnp.maximum(i - 1, 0), 0)
    kv_c = pl.BlockSpec((batch, WINDOW, KV_W), cur)
    kv_p = pl.BlockSpec((batch, WINDOW, KV_W), prev)
    return pl.pallas_call(
        functools.partial(_swa_prompt_kernel, batch),
        grid=(seq // WINDOW,),
        in_specs=[pl.BlockSpec((batch, WINDOW, ATT_W), cur), kv_p, kv_c, kv_p, kv_c,
                  _const_spec(bias_tab.shape), _const_spec(sink_col.shape)],
        out_specs=pl.BlockSpec((batch, WINDOW, ATT_W), cur),
        out_shape=jax.ShapeDtypeStruct((batch, seq, ATT_W), BF),
        compiler_params=_params("parallel"),
        name="swa_prompt",
    )(q3, k3, k3, v3, v3, bias_tab, sink_col)


def _swa_sample_kernel(q_ref, kn_ref, vn_ref, ck_ref, cv_ref, bc_ref, bn_ref, sink_ref, o_ref):
    q = q_ref[...]
    qf = q.astype(F32)
    kn = kn_ref[...].astype(BF).astype(F32)
    vn = vn_ref[...].astype(BF).astype(F32)
    ck, cv = ck_ref[...], cv_ref[...]
    head = lax.broadcasted_iota(jnp.int32, (1, N_Q_HEADS, 1), 1)
    low = head < GQA_GROUP
    s_h, sn_h = [], []
    for h in range(N_KV_HEADS):
        sl = slice(h * HEAD_DIM, (h + 1) * HEAD_DIM)
        s_h.append(jnp.einsum("bgd,bkd->bgk", q, ck[:, :, sl].astype(BF), preferred_element_type=F32))
        sn_h.append(jnp.sum(qf * kn[:, None, sl], axis=-1, keepdims=True))
    s = jnp.where(low, s_h[0], s_h[1]) * ATT_SCALE + bc_ref[...]
    sn = jnp.where(low, sn_h[0], sn_h[1]) * ATT_SCALE + bn_ref[...]
    sink = sink_ref[...]
    m = jnp.maximum(jnp.maximum(jnp.max(s, axis=-1, keepdims=True), sn), sink)
    p = jnp.exp(s - m)
    pn = jnp.exp(sn - m)
    denom = jnp.sum(p, axis=-1, keepdims=True) + pn + jnp.exp(sink - m)
    pb = p.astype(BF)
    pnb = pn.astype(BF).astype(F32)
    o_h = []
    for h in range(N_KV_HEADS):
        sl = slice(h * HEAD_DIM, (h + 1) * HEAD_DIM)
        o = jnp.einsum("bgk,bkd->bgd", pb, cv[:, :, sl].astype(BF), preferred_element_type=F32)
        o_h.append(o + pnb * vn[:, None, sl])
    o_ref[...] = (jnp.where(low, o_h[0], o_h[1]) / denom).astype(o_ref.dtype)


def _swa_sample(q3, kn, vn, ck, cv, bias_c, bias_n, sink3, bb):
    b = q3.shape[0]
    return pl.pallas_call(
        _swa_sample_kernel,
        grid=(b // bb,),
        in_specs=[pl.BlockSpec((bb, N_Q_HEADS, HEAD_DIM), lambda i: (i, 0, 0)),
                  pl.BlockSpec((bb, KV_W), lambda i: (i, 0)),
                  pl.BlockSpec((bb, KV_W), lambda i: (i, 0)),
                  pl.BlockSpec((bb, WINDOW, KV_W), lambda i: (i, 0, 0)),
                  pl.BlockSpec((bb, WINDOW, KV_W), lambda i: (i, 0, 0)),
                  _const_spec(bias_c.shape), _const_spec(bias_n.shape), _const_spec(sink3.shape)],
        out_specs=pl.BlockSpec((bb, N_Q_HEADS, HEAD_DIM), lambda i: (i, 0, 0)),
        out_shape=jax.ShapeDtypeStruct((b, N_Q_HEADS, HEAD_DIM), BF),
        compiler_params=_params("parallel"),
        name="swa_sample",
    )(q3, kn, vn, ck, cv, bias_c, bias_n, sink3)


def _rwkv_prep_kernel(z_ref, zp8_ref, pz_ref, mu_ref, w0_ref, w2_ref, a0_ref, a2_ref, g2_ref,
                      kk_ref, ka_ref, ones_ref,
                      r_ref, lw_ref, k2_ref, v_ref, a_ref, b_ref, g_ref):
    z = z_ref[0]
    tm = z.shape[0]
    if pz_ref.shape[1] == tm:
        zs = pz_ref[0]
    else:
        prev_row = jnp.where(pl.program_id(1) == 0, pz_ref[0], zp8_ref[0, 7:8, :])
        row = lax.broadcasted_iota(jnp.int32, (tm, 1), 0)
        zs = jnp.where(row == 0, prev_row, pltpu.roll(z, 1, 0))
    zz = z + (zs - z) * mu_ref[...]
    r = zz[:, 0:RWKV_W]
    k = zz[:, RWKV_W:2 * RWKV_W]
    v = zz[:, 2 * RWKV_W:3 * RWKV_W]
    o = 3 * RWKV_W
    wl = zz[:, o:o + D_DECAY_LORA]
    al = zz[:, o + D_DECAY_LORA:o + D_DECAY_LORA + D_AAA_LORA]
    gl = zz[:, o + D_DECAY_LORA + D_AAA_LORA:RWKV_IN]
    wpre = -(w0_ref[...] + _dot(jnp.tanh(wl).astype(BF), w2_ref[...]))
    softplus = jnp.maximum(wpre, 0.0) + jnp.log1p(jnp.exp(-jnp.abs(wpre)))
    lw = -jnp.exp(-softplus - 0.5)
    a = _sigmoid(a0_ref[...] + _dot(al.astype(BF), a2_ref[...]))
    g = _dot(_sigmoid(gl).astype(BF), g2_ref[...])
    kkv = k * kk_ref[...]
    ss = _dot_exact_rhs(kkv * kkv, ones_ref[...])
    kk = kkv / jnp.maximum(jnp.sqrt(ss), 1e-12)
    k2 = k * (1.0 + (a - 1.0) * ka_ref[...])
    r_ref[0], lw_ref[0], k2_ref[0], v_ref[0] = r, lw, k2, v
    a_ref[0], b_ref[0], g_ref[0] = -kk, kk * a, g


def _rwkv_prep(z3, prev_z, p, tm):
    b, t, _ = z3.shape
    blk8 = tm // 8
    vec = lambda w_: _const_spec((1, w_))
    out_spec = pl.BlockSpec((1, tm, RWKV_W), lambda bi, i: (bi, i, 0))
    return pl.pallas_call(
        _rwkv_prep_kernel,
        grid=(b, t // tm),
        in_specs=[pl.BlockSpec((1, tm, RWKV_IN), lambda bi, i: (bi, i, 0)),
                  pl.BlockSpec((1, 8, RWKV_IN), lambda bi, i: (bi, jnp.maximum(i * blk8 - 1, 0), 0)),
                  pl.BlockSpec((1, prev_z.shape[1], RWKV_IN), lambda bi, i: (bi, 0, 0)),
                  vec(RWKV_IN), vec(RWKV_W), _const_spec((D_DECAY_LORA, RWKV_W)),
                  vec(RWKV_W), _const_spec((D_AAA_LORA, RWKV_W)), _const_spec((D_GATE_LORA, RWKV_W)),
                  vec(RWKV_W), vec(RWKV_W), _const_spec((RWKV_W, RWKV_W))],
        out_specs=[out_spec] * 7,
        out_shape=[jax.ShapeDtypeStruct((b, t, RWKV_W), F32)] * 7,
        compiler_params=_params("parallel", "parallel"),
        name="rwkv_prep",
    )(z3, z3, prev_z, p["mu"], p["w0"], p["w2"], p["a0"], p["a2"], p["g2"], p["k_k"], p["k_a"], p["ones_bd"])


def _rwkv_scan_kernel(nb, chunk, r_ref, lw_ref, k_ref, v_ref, a_ref, b_ref, g_ref, s0_ref,
                      rk_ref, lg_ref, lb_ref, ones_ref, y_ref, sout_ref, s_scr):
    c = pl.program_id(1)

    @pl.when(c == 0)
    def _():
        s_scr[...] = s0_ref[...]

    ri = lax.broadcasted_iota(jnp.int32, (chunk, chunk), 0)
    ci = lax.broadcasted_iota(jnp.int32, (chunk, chunk), 1)
    incl = ri >= ci
    strict = ri > ci
    tri = incl.astype(BF)
    eye = (ri == ci).astype(F32)
    n_sq = int(math.log2(chunk)) - 1

    chains = [(bi, h) for bi in range(nb) for h in range(RWKV_HEADS)]
    hs = lambda h: slice(h * RWKV_HEAD, (h + 1) * RWKV_HEAD)

    ra, aa, bt, kt, bh, kh, vb, g_all = [], [], [], [], [], [], [], []
    for bi in range(nb):
        r, lw, k, v = r_ref[bi], lw_ref[bi], k_ref[bi], v_ref[bi]
        a, b = a_ref[bi], b_ref[bi]
        hi, lo = _split2(lw)
        lo2 = (lw - hi.astype(F32) - lo.astype(F32)).astype(BF)
        cum = _dot(tri, hi) + _dot(tri, lo) + _dot(tri, lo2)
        tail = cum[chunk - 1:chunk, :]
        g_inv = jnp.exp(-cum)
        g_tail = jnp.exp(tail - cum)
        g_all.append(jnp.exp(tail))
        ra.append((r * jnp.exp(cum)).astype(BF))
        aa.append((a * jnp.exp(cum - lw)).astype(BF))
        bt.append((b * g_inv).astype(BF))
        kt.append((k * g_inv).astype(BF))
        bh.append((b * g_tail).astype(BF))
        kh.append((k * g_tail).astype(BF))
        vb.append(v.astype(BF))

    s_old = [s_scr[bi, h] for bi, h in chains]
    ar = [jnp.concatenate([aa[bi][:, hs(h)], ra[bi][:, hs(h)]], axis=0) for bi, h in chains]
    v_h = [vb[bi][:, hs(h)] for bi, h in chains]
    gb = [_dot_nt(ar[n], bt[bi][:, hs(h)]) for n, (bi, h) in enumerate(chains)]
    gk = [_dot_nt(ar[n], kt[bi][:, hs(h)]) for n, (bi, h) in enumerate(chains)]
    p = [_dot_nt(ar[n], s_old[n].astype(BF)) for n in range(len(chains))]
    l_ab = [jnp.where(strict, x[:chunk], 0.0) for x in gb]
    l_ak = [jnp.where(strict, x[:chunk], 0.0).astype(BF) for x in gk]
    m_rb = [jnp.where(incl, x[chunk:], 0.0).astype(BF) for x in gb]
    m_rk = [jnp.where(incl, x[chunk:], 0.0).astype(BF) for x in gk]
    rhs = [p[n][:chunk] + _dot(l_ak[n], v_h[n]) for n in range(len(chains))]
    t_inv = [eye + x for x in l_ab]
    lp = l_ab
    for _ in range(n_sq):
        lpb = [x.astype(BF) for x in lp]
        lp = [_dot(x, x) for x in lpb]
        t_inv = [t + _dot(x.astype(BF), t.astype(BF)) for x, t in zip(lp, t_inv)]
    ub = [_dot(t.astype(BF), x.astype(BF)).astype(BF) for t, x in zip(t_inv, rhs)]
    y_h = [p[n][chunk:] + _dot(m_rb[n], ub[n]) + _dot(m_rk[n], v_h[n]) for n in range(len(chains))]
    s_new = [s_old[n] * g_all[bi][:, hs(h)] + _dot_tn(ub[n], bh[bi][:, hs(h)]) + _dot_tn(v_h[n], kh[bi][:, hs(h)])
             for n, (bi, h) in enumerate(chains)]
    for n, (bi, h) in enumerate(chains):
        s_scr[bi, h] = s_new[n]

    for bi in range(nb):
        y = jnp.concatenate(y_h[bi * RWKV_HEADS:(bi + 1) * RWKV_HEADS], axis=-1)
        y_ref[bi] = _rwkv_epilogue(y, r_ref[bi], k_ref[bi], v_ref[bi], g_ref[bi],
                                   rk_ref, lg_ref, lb_ref, ones_ref).astype(y_ref.dtype)

    @pl.when(c == pl.num_programs(1) - 1)
    def _():
        sout_ref[...] = s_scr[...]


def _rwkv_epilogue(y, r, k, v, g, rk_ref, lg_ref, lb_ref, ones_ref):
    ones = ones_ref[...]
    inv_n = 1.0 / RWKV_HEAD
    mean = _dot_exact_rhs(y, ones) * inv_n
    d = y - mean
    var = _dot_exact_rhs(d * d, ones) * inv_n
    yn = d * lax.rsqrt(var + GN_EPS) * lg_ref[...] + lb_ref[...]
    bonus = _dot_exact_rhs(r * k * rk_ref[...], ones) * v
    return (yn + bonus) * g


def _rwkv_scan(coef, s0, p, nb, chunk):
    r = coef[0]
    b, t, _ = r.shape
    seq_spec = pl.BlockSpec((nb, chunk, RWKV_W), lambda bi, c: (bi, c, 0))
    st_spec = pl.BlockSpec((nb, RWKV_HEADS, RWKV_HEAD, RWKV_HEAD), lambda bi, c: (bi, 0, 0, 0))
    vec = _const_spec((1, RWKV_W))
    return pl.pallas_call(
        functools.partial(_rwkv_scan_kernel, nb, chunk),
        grid=(b // nb, t // chunk),
        in_specs=[seq_spec] * 7 + [st_spec, vec, vec, vec, _const_spec((RWKV_W, RWKV_W))],
        out_specs=[seq_spec, st_spec],
        out_shape=[jax.ShapeDtypeStruct((b, t, RWKV_W), BF),
                   jax.ShapeDtypeStruct((b, RWKV_HEADS, RWKV_HEAD, RWKV_HEAD), F32)],
        scratch_shapes=[pltpu.VMEM((nb, RWKV_HEADS, RWKV_HEAD, RWKV_HEAD), F32)],
        compiler_params=_params("parallel", "arbitrary"),
        name="rwkv_scan",
    )(*coef, s0, p["r_k"], p["lnx_g"], p["lnx_b"], p["ones_bd"])


def _rwkv_step_kernel(r_ref, lw_ref, k_ref, v_ref, a_ref, b_ref, g_ref, s_ref,
                      rk_ref, lg_ref, lb_ref, ones_ref, y_ref, sout_ref, y_scr):
    n_pair = LANES // RWKV_HEAD
    r, k, v = r_ref[...], k_ref[...], v_ref[...]
    rT, kT, vT = r.T, k.T, v.T
    aT, bT, wT = a_ref[...].T, b_ref[...].T, jnp.exp(lw_ref[...]).T
    for hl in range(LANES // RWKV_HEAD):
        hsl = slice(hl * RWKV_HEAD, (hl + 1) * RWKV_HEAD)
        a_h, b_h, k_h, w_h, r_h = aT[hsl], bT[hsl], kT[hsl], wT[hsl], rT[hsl]
        tiles = [(hl * RWKV_HEAD + n_pair * t) * RWKV_HEAD for t in range(RWKV_HEAD // n_pair)]
        st = [s_ref[:, c0:c0 + LANES].T for c0 in tiles]
        new = []
        for t, x in enumerate(st):
            halves = []
            for il in range(n_pair):
                i = hl * RWKV_HEAD + n_pair * t + il
                slab = x[il * RWKV_HEAD:(il + 1) * RWKV_HEAD]
                sa = jnp.sum(slab * a_h, axis=0, keepdims=True)
                slab = slab * w_h + sa * b_h + vT[i:i + 1] * k_h
                y_scr[i:i + 1, :] = jnp.sum(slab * r_h, axis=0, keepdims=True)
                halves.append(slab)
            new.append(jnp.concatenate(halves, axis=0))
        for c0, x in zip(tiles, new):
            sout_ref[:, c0:c0 + LANES] = x.T
    y_ref[...] = _rwkv_epilogue(y_scr[...].T, r, k, v, g_ref[...],
                                rk_ref, lg_ref, lb_ref, ones_ref).astype(y_ref.dtype)


def _rwkv_step(coef, s0, p):
    b = s0.shape[0]
    per_pair = (LANES // RWKV_HEAD) * RWKV_HEAD * RWKV_HEAD
    n_steps = RWKV_W // LANES
    col = pl.BlockSpec((b, LANES), lambda i: (0, i))
    vec = pl.BlockSpec((1, LANES), lambda i: (0, i))
    st_spec = pl.BlockSpec((b, per_pair), lambda i: (0, i))
    y, s_new = pl.pallas_call(
        _rwkv_step_kernel,
        grid=(n_steps,),
        in_specs=[col] * 7 + [st_spec, vec, vec, vec, _const_spec((LANES, LANES))],
        out_specs=[col, st_spec],
        out_shape=[jax.ShapeDtypeStruct((b, RWKV_W), BF),
                   jax.ShapeDtypeStruct((b, n_steps * per_pair), F32)],
        scratch_shapes=[pltpu.VMEM((LANES, b), F32)],
        compiler_params=_params("parallel"),
        name="rwkv_step",
    )(*coef, s0.reshape(b, n_steps * per_pair), p["r_k"], p["lnx_g"], p["lnx_b"],
      p["ones_bd"][:LANES, :LANES])
    return y, s_new.reshape(s0.shape)


def _merge_kernel(with_router, x_ref, ya_ref, yb_ref, gate_ref, wa_ref, wb_ref, wo_ref, g_ref, *rest):
    if with_router:
        wr_ref, cnt_in_ref, x1_ref, h_ref, route_ref, cnt_out_ref, cnt_scr = rest
    else:
        x1_ref, h_ref = rest
    gate = gate_ref[...]
    pa = _dot(ya_ref[...], wa_ref[...])
    pb = _dot(yb_ref[...], wb_ref[...])
    merged = _sigmoid(gate[:, :D_MODEL]) * pa + _sigmoid(gate[:, D_MODEL:]) * pb
    x1 = x_ref[...] + _dot(merged.astype(BF), wo_ref[...])
    x1_ref[...] = x1
    h = _rms(x1, g_ref[...])
    if not with_router:
        h_ref[...] = h.astype(h_ref.dtype)
        return
    for s in range(D_MODEL // LANES):
        h_ref[:, s, :] = h[:, s * LANES:(s + 1) * LANES]

    @pl.when(pl.program_id(0) == 0)
    def _():
        cnt_scr[...] = cnt_in_ref[...]

    hh, hl = _split2(h)
    wh, wl = wr_ref[0], wr_ref[1]
    logits = _dot(hh, wh) + _dot(hl, wh) + _dot(hh, wl)
    tm = logits.shape[0]
    lane = lax.broadcasted_iota(jnp.int32, logits.shape, 1).astype(F32)
    logits = jnp.where(lane < N_EXPERTS, logits, -jnp.inf)
    m1 = jnp.max(logits, axis=-1, keepdims=True)
    i1 = jnp.min(jnp.where(logits == m1, lane, float(LANES)), axis=-1, keepdims=True)
    rest_l = jnp.where(lane == i1, -jnp.inf, logits)
    m2 = jnp.max(rest_l, axis=-1, keepdims=True)
    i2 = jnp.min(jnp.where(rest_l == m2, lane, float(LANES)), axis=-1, keepdims=True)
    e2 = jnp.exp(m2 - m1)
    g1 = 1.0 / (1.0 + e2)
    g2 = e2 / (1.0 + e2)
    oh1 = (lane == i1).astype(F32)
    oh2 = (lane == i2).astype(F32)
    both = oh1 + oh2
    ri = lax.broadcasted_iota(jnp.int32, (tm, tm), 0)
    ci = lax.broadcasted_iota(jnp.int32, (tm, tm), 1)
    before = _dot((ri > ci).astype(BF), both.astype(BF)) + cnt_scr[...]
    rank1 = jnp.sum(oh1 * before, axis=-1, keepdims=True)
    rank2 = jnp.sum(oh2 * before, axis=-1, keepdims=True)
    cnt_scr[...] += jnp.sum(both, axis=0, keepdims=True)
    cnt_out_ref[...] = cnt_scr[...]
    route = jnp.zeros_like(logits)
    for n, val in enumerate((i1, i2, g1, g2, rank1, rank2)):
        route = jnp.where(lane == n, val, route)
    route_ref[...] = route


def _merge_out(x, ya, yb, gate, wa, wb, wo, g, tm, router=None, cnt_in=None):
    t = x.shape[0]
    row = lambda w_: pl.BlockSpec((tm, w_), lambda i: (i, 0))
    in_specs = [row(D_MODEL), row(ATT_W), row(RWKV_W), row(2 * D_MODEL),
                _const_spec((ATT_W, D_MODEL)), _const_spec((RWKV_W, D_MODEL)),
                _const_spec((D_MODEL, D_MODEL)), _const_spec((1, D_MODEL))]
    args = [x, ya, yb, gate, wa, wb, wo, g]
    if router is None:
        out_specs = [row(D_MODEL), row(D_MODEL)]
        out_shape = [jax.ShapeDtypeStruct((t, D_MODEL), F32), jax.ShapeDtypeStruct((t, D_MODEL), BF)]
        scratch = []
    else:
        n_sub = D_MODEL // LANES
        in_specs += [_const_spec(router.shape), _const_spec((1, LANES))]
        args += [router, cnt_in]
        out_specs = [row(D_MODEL), pl.BlockSpec((tm, n_sub, LANES), lambda i: (i, 0, 0)), row(LANES),
                     pl.BlockSpec((1, LANES), lambda i: (0, 0))]
        out_shape = [jax.ShapeDtypeStruct((t, D_MODEL), F32), jax.ShapeDtypeStruct((t, n_sub, LANES), F32),
                     jax.ShapeDtypeStruct((t, LANES), F32), jax.ShapeDtypeStruct((1, LANES), F32)]
        scratch = [pltpu.VMEM((1, LANES), F32)]
    return pl.pallas_call(
        functools.partial(_merge_kernel, router is not None),
        grid=(t // tm,),
        in_specs=in_specs, out_specs=out_specs, out_shape=out_shape, scratch_shapes=scratch,
        compiler_params=_params("parallel" if router is None else "arbitrary"),
        name="merge_out",
    )(*args)


def _swiglu(x, wg, wu, wd):
    s = _dot(x, wg)
    u = _dot(x, wu)
    act = (s * _sigmoid(s) * u).astype(BF)
    return _dot(act, wd)


def _dense_ffn_kernel(x1_ref, h_ref, wg_ref, wu_ref, wd_ref, o_ref):
    o_ref[...] = x1_ref[...] + _swiglu(h_ref[...], wg_ref[...], wu_ref[...], wd_ref[...])


def _dense_ffn(x1, h, wg, wu, wd, tm):
    t = x1.shape[0]
    d_ff = wg.shape[1]
    row = lambda: pl.BlockSpec((tm, D_MODEL), lambda i: (i, 0))
    return pl.pallas_call(
        _dense_ffn_kernel,
        grid=(t // tm,),
        in_specs=[row(), row(), _const_spec((D_MODEL, d_ff)), _const_spec((D_MODEL, d_ff)),
                  _const_spec((d_ff, D_MODEL))],
        out_specs=row(),
        out_shape=jax.ShapeDtypeStruct((t, D_MODEL), F32),
        compiler_params=_params("parallel"),
        name="dense_ffn",
    )(x1, h, wg, wu, wd)


def _rows_as_matrix(ref3):
    return jnp.concatenate([ref3[:, s, :] for s in range(ref3.shape[1])], axis=-1)


def _dispatch_kernel(tt, first, dest_ref, fill_ref, h_hbm, *rest):
    if first:
        xs_hbm, zbuf, sem, zsem = rest
    else:
        _, xs_hbm, sem = rest
    i = pl.program_id(0)
    base = i * tt

    if first:
        @pl.when(i == 0)
        def _():
            zbuf[...] = jnp.zeros_like(zbuf)
            bm = zbuf.shape[0]
            fills = [pltpu.make_async_copy(zbuf, xs_hbm.at[pl.ds(fill_ref[n] * bm, bm)], zsem)
                     for n in range(fill_ref.shape[0])]
            for f in fills:
                f.start()
            for f in fills:
                f.wait()

    def issue(r, carry):
        for j in range(TOP_K):
            pltpu.make_async_copy(h_hbm.at[base + r], xs_hbm.at[dest_ref[TOP_K * (base + r) + j]], sem).start()
        return carry

    lax.fori_loop(0, tt, issue, 0, unroll=4)
    for j in range(TOP_K):
        pltpu.make_async_copy(h_hbm.at[pl.ds(0, tt)], xs_hbm.at[pl.ds(0, tt)], sem).wait()


def _dispatch(dest, fill_blocks, h3, xs, tt, bm, n_rows):
    t, n_sub, _ = h3.shape
    first = xs is None
    in_specs = [pl.BlockSpec(memory_space=pl.ANY)]
    args = [dest, fill_blocks, h3]
    if first:
        scratch = [pltpu.VMEM((bm, n_sub, LANES), F32), pltpu.SemaphoreType.DMA(()), pltpu.SemaphoreType.DMA(())]
        aliases = {}
    else:
        in_specs.append(pl.BlockSpec(memory_space=pl.ANY))
        args.append(xs)
        scratch = [pltpu.SemaphoreType.DMA(())]
        aliases = {3: 0}
    grid_spec = pltpu.PrefetchScalarGridSpec(
        num_scalar_prefetch=2, grid=(t // tt,), in_specs=in_specs,
        out_specs=pl.BlockSpec(memory_space=pl.ANY), scratch_shapes=scratch)
    return pl.pallas_call(
        functools.partial(_dispatch_kernel, tt, first),
        grid_spec=grid_spec,
        out_shape=jax.ShapeDtypeStruct((n_rows, n_sub, LANES), F32),
        input_output_aliases=aliases,
        compiler_params=_params("arbitrary"),
        name="moe_dispatch",
    )(*args)


def _moe_kernel(n_half, blk_e_ref, n_used_ref, xs_ref, wg_ref, wu_ref, wd_ref, y_ref):
    @pl.when(pl.program_id(0) < n_used_ref[0])
    def _():
        x = _rows_as_matrix(xs_ref).astype(BF)
        d_e = wg_ref.shape[2]
        step = d_e // n_half
        acc = None
        for j in range(n_half):
            cs = slice(j * step, (j + 1) * step)
            part = _swiglu(x, wg_ref[0, :, cs], wu_ref[0, :, cs], wd_ref[0, cs, :])
            acc = part if acc is None else acc + part
        for s in range(y_ref.shape[1]):
            y_ref[:, s, :] = acc[:, s * LANES:(s + 1) * LANES]

    @pl.when(pl.program_id(0) >= n_used_ref[0])
    def _():
        y_ref[...] = jnp.zeros_like(y_ref)


def _moe_ffn(xs, blk_e, n_used, wg, wu, wd, bm):
    n_blk = blk_e.shape[0]
    n_rows, n_sub, _ = xs.shape
    d_e = wg.shape[2]
    wspec = lambda shape: pl.BlockSpec(shape, lambda i, be, nu: (be[i], 0, 0), pipeline_mode=pl.Buffered(1))
    rows = pl.BlockSpec((bm, n_sub, LANES), lambda i, be, nu: (i, 0, 0))
    grid_spec = pltpu.PrefetchScalarGridSpec(
        num_scalar_prefetch=2,
        grid=(n_blk,),
        in_specs=[rows, wspec((1, D_MODEL, d_e)), wspec((1, D_MODEL, d_e)), wspec((1, d_e, D_MODEL))],
        out_specs=rows,
    )
    return pl.pallas_call(
        functools.partial(_moe_kernel, 2),
        grid_spec=grid_spec,
        out_shape=jax.ShapeDtypeStruct((n_rows, n_sub, LANES), F32),
        compiler_params=_params("arbitrary"),
        name="moe_ffn",
    )(blk_e, n_used, xs, wg, wu, wd)


def _row_layout(counts, bm, n_blk):
    per_e = (counts + bm - 1) // bm
    ends = jnp.cumsum(per_e)
    n_used = ends[-1]
    first_row = (ends - per_e) * bm
    i = jnp.arange(n_blk, dtype=jnp.int32)
    blk_e = jnp.sum((jnp.minimum(i, jnp.maximum(n_used - 1, 0))[:, None] >= ends[None, :]).astype(jnp.int32), axis=1)
    need = jnp.any(i[:, None] == (ends - 1)[None, :], axis=1) | (i >= n_blk - N_EXPERTS)
    fill_blocks = jnp.argsort(jnp.logical_not(need), stable=True)[:2 * N_EXPERTS].astype(jnp.int32)
    return first_row.astype(jnp.int32), blk_e.astype(jnp.int32), n_used.astype(jnp.int32).reshape(1), fill_blocks


def _combine_kernel(tc, dest_ref, x1_ref, route_ref, y_hbm, g_ref, o_ref, ybuf, sem):
    base = pl.program_id(0) * tc

    def issue(r, carry):
        for j in range(TOP_K):
            pltpu.make_async_copy(y_hbm.at[dest_ref[TOP_K * (base + r) + j]], ybuf.at[j, r], sem).start()
        return carry

    lax.fori_loop(0, tc, issue, 0, unroll=4)
    for j in range(TOP_K):
        pltpu.make_async_copy(y_hbm.at[pl.ds(0, tc)], ybuf.at[j], sem).wait()
    route = route_ref[...]
    x2 = (x1_ref[...] + route[:, 2:3] * _rows_as_matrix(ybuf.at[0])
          + route[:, 3:4] * _rows_as_matrix(ybuf.at[1]))
    o_ref[...] = _rms(x2, g_ref[...])


def _moe_combine(dest, x1, route, y, g, tc):
    t = x1.shape[0]
    n_sub = y.shape[1]
    grid_spec = pltpu.PrefetchScalarGridSpec(
        num_scalar_prefetch=1,
        grid=(t // tc,),
        in_specs=[pl.BlockSpec((tc, D_MODEL), lambda i, ds: (i, 0)),
                  pl.BlockSpec((tc, LANES), lambda i, ds: (i, 0)),
                  pl.BlockSpec(memory_space=pl.ANY),
                  pl.BlockSpec((1, D_MODEL), lambda i, ds: (0, 0))],
        out_specs=pl.BlockSpec((tc, D_MODEL), lambda i, ds: (i, 0)),
        scratch_shapes=[pltpu.VMEM((TOP_K, tc, n_sub, LANES), F32), pltpu.SemaphoreType.DMA(())],
    )
    return pl.pallas_call(
        functools.partial(_combine_kernel, tc),
        grid_spec=grid_spec,
        out_shape=jax.ShapeDtypeStruct((t, D_MODEL), F32),
        compiler_params=_params("arbitrary"),
        name="moe_combine",
    )(dest, x1, route, y, g)


def _layer_params(l, w_in, mu_shift, w0, w2, a0, a2, g2, k_k, k_a, r_k, lnx_g, lnx_b,
                  w_proj_attn, w_proj_rwkv, w_out, ones_bd):
    row = lambda u: u.reshape(1, -1)
    return dict(
        w_in=w_in[l].astype(BF), mu=row(mu_shift[l]), w0=row(w0[l]), w2=w2[l].astype(BF),
        a0=row(a0[l]), a2=a2[l].astype(BF), g2=g2[l].astype(BF), k_k=row(k_k[l]), k_a=row(k_a[l]),
        r_k=row(r_k[l]), lnx_g=row(lnx_g[l]), lnx_b=row(lnx_b[l]),
        wa=w_proj_attn[l].astype(BF), wb=w_proj_rwkv[l].astype(BF), wo=w_out[l].astype(BF),
        ones_bd=ones_bd)


def kernel(x_prompt, x_sample, cache_k, cache_v, state_wkv, state_shift, norm_mix_g, w_in, attn_sinks, rel_bias, mu_shift, w0, w2, a0, a2, g2, k_k, k_a, r_k, lnx_g, lnx_b, w_proj_attn, w_proj_rwkv, w_out, norm_ffn_g, dense_w_gate, dense_w_up, dense_w_down, router_w, moe_w_gate, moe_w_up, moe_w_down, norm_final_g):
    batch, seq, _ = x_prompt.shape
    dec_batch, dec_seq, _ = x_sample.shape
    assert dec_seq == 1 and seq % WINDOW == 0

    head_id = jnp.arange(RWKV_W, dtype=jnp.int32) // RWKV_HEAD
    ones_bd = (head_id[:, None] == head_id[None, :]).astype(BF)
    bias_tab = _prompt_bias_table(rel_bias)
    bias_by_dist = rel_bias[_t5_bucket(jnp.arange(WINDOW + 1))]
    bias_c = jnp.transpose(bias_by_dist[WINDOW:0:-1], (1, 0))[None]
    bias_n = bias_by_dist[0][None, :, None]

    layers = [_layer_params(l, w_in, mu_shift, w0, w2, a0, a2, g2, k_k, k_a, r_k, lnx_g, lnx_b,
                            w_proj_attn, w_proj_rwkv, w_out, ones_bd) for l in range(DEPTH)]

    def router_pieces(l):
        wr = jnp.zeros((D_MODEL, LANES), F32).at[:, :N_EXPERTS].set(router_w[l // 2])
        hi = wr.astype(BF)
        lo = (wr - hi.astype(F32)).astype(BF)
        return jnp.stack([hi, lo])

    n_prompt = batch * seq
    tm_p = 256
    bm = 512
    n_all = n_prompt + dec_batch
    n_blk = -(-(n_all * TOP_K) // bm) + N_EXPERTS

    def mix(x, is_prompt, l, state):
        p = layers[l]
        n_tok = x.shape[0]
        tm = tm_p if is_prompt else n_tok
        q, k, v, z, gate = _inproj(x, norm_mix_g[l].reshape(1, -1), p["w_in"], tm)
        sinks = attn_sinks[l]
        if is_prompt:
            b, t = batch, seq
            sink_col = jnp.repeat(sinks, WINDOW).reshape(N_KV_HEADS, GQA_GROUP * WINDOW, 1)
            ya = _swa_prompt(q.reshape(b, t, ATT_W), k.reshape(b, t, KV_W), v.reshape(b, t, KV_W),
                             bias_tab, sink_col).reshape(n_tok, ATT_W)
            k4 = k.reshape(b, t, N_KV_HEADS, HEAD_DIM)
            v4 = v.reshape(b, t, N_KV_HEADS, HEAD_DIM)
            state["k"].append(k4[:, t - WINDOW:])
            state["v"].append(v4[:, t - WINDOW:])
            z3 = z.reshape(b, t, RWKV_IN)
            prev_z = jnp.zeros((b, 1, RWKV_IN), F32)
            s0 = jnp.zeros((b, RWKV_HEADS, RWKV_HEAD, RWKV_HEAD), F32)
            coef = _rwkv_prep(z3, prev_z, p, 256)
            yb3, s_new = _rwkv_scan(coef, s0, p, b, 64)
            yb = yb3.reshape(n_tok, RWKV_W)
            state["z"].append(z3[:, -1])
        else:
            b = dec_batch
            ck = cache_k[l].reshape(b, WINDOW, KV_W)
            cv = cache_v[l].reshape(b, WINDOW, KV_W)
            ya3 = _swa_sample(q.reshape(b, N_Q_HEADS, HEAD_DIM), k, v, ck, cv, bias_c, bias_n,
                              sinks.reshape(1, N_Q_HEADS, 1), 16)
            ya = ya3.reshape(b, ATT_W)
            state["k"].append(jnp.concatenate([ck[:, 1:], k[:, None]], axis=1)
                              .reshape(b, WINDOW, N_KV_HEADS, HEAD_DIM))
            state["v"].append(jnp.concatenate([cv[:, 1:], v[:, None]], axis=1)
                              .reshape(b, WINDOW, N_KV_HEADS, HEAD_DIM))
            coef = _rwkv_prep(z.reshape(1, b, RWKV_IN), state_shift[l].reshape(1, b, RWKV_IN), p, b)
            yb, s_new = _rwkv_step([u.reshape(b, RWKV_W) for u in coef], state_wkv[l], p)
            state["z"].append(z)
        state["s"].append(s_new)
        return ya, yb, gate

    groups = [dict(x=x_prompt.reshape(n_prompt, D_MODEL), prompt=True, tm=tm_p, k=[], v=[], s=[], z=[]),
              dict(x=x_sample.reshape(dec_batch, D_MODEL), prompt=False, tm=dec_batch, k=[], v=[], s=[], z=[])]
    for l in range(DEPTH):
        p = layers[l]
        gn = norm_ffn_g[l].reshape(1, -1)
        j = l // 2
        if l % 2 == 0:
            wg, wu, wd = (w[j].astype(BF) for w in (dense_w_gate, dense_w_up, dense_w_down))
            for grp in groups:
                ya, yb, gate = mix(grp["x"], grp["prompt"], l, grp)
                x1, h = _merge_out(grp["x"], ya, yb, gate, p["wa"], p["wb"], p["wo"], gn, grp["tm"])
                grp["x"] = _dense_ffn(x1, h, wg, wu, wd, grp["tm"])
        else:
            assert l == DEPTH - 1
            wg, wu, wd = (w[j].astype(BF) for w in (moe_w_gate, moe_w_up, moe_w_down))
            router = router_pieces(l)
            cnt = jnp.zeros((1, LANES), F32)
            xs = None
            for grp in groups:
                ya, yb, gate = mix(grp["x"], grp["prompt"], l, grp)
                x1, h3, route, cnt = _merge_out(grp["x"], ya, yb, gate, p["wa"], p["wb"], p["wo"], gn,
                                                grp["tm"], router=router, cnt_in=cnt)
                grp["x1"], grp["route"], grp["h3"] = x1, route, h3
            counts = cnt[0, :N_EXPERTS].astype(jnp.int32)
            first_row, blk_e, n_used, fill_blocks = _row_layout(counts, bm, n_blk)
            for grp in groups:
                route = grp["route"]
                expert = route[:, 0:TOP_K].astype(jnp.int32)
                rank = route[:, 2 * TOP_K:3 * TOP_K].astype(jnp.int32)
                start = jnp.sum(jnp.where(expert[..., None] == jnp.arange(N_EXPERTS, dtype=jnp.int32),
                                          first_row, 0), axis=-1)
                grp["dest"] = (start + rank).reshape(-1)
                xs = _dispatch(grp["dest"], fill_blocks, grp["h3"], xs, min(512, grp["h3"].shape[0]), bm,
                               n_blk * bm)
            y = _moe_ffn(xs, blk_e, n_used, wg, wu, wd, bm)
            for grp in groups:
                grp["x"] = _moe_combine(grp["dest"], grp["x1"], grp["route"], y,
                                        norm_final_g.reshape(1, -1), 128)

    gp, gs = groups
    outs = []
    for grp, shape in ((gp, (batch, seq, D_MODEL)), (gs, (dec_batch, dec_seq, D_MODEL))):
        outs.append((grp["x"].reshape(shape), jnp.stack(grp["k"]), jnp.stack(grp["v"]),
                     jnp.stack(grp["s"]), jnp.stack(grp["z"])))
    (y_p, nk_p, nv_p, ns_p, nz_p), (y_s, nk_s, nv_s, ns_s, nz_s) = outs
    return (y_p, y_s, nk_p, nv_p, ns_p, nz_p, nk_s, nv_s, ns_s, nz_s)
```

```python
import functools
import math

import jax
import jax.numpy as jnp
from jax import lax
from jax.experimental import pallas as pl
from jax.experimental.pallas import tpu as pltpu

BF = jnp.bfloat16
F32 = jnp.float32

D_MODEL = 1024
DEPTH = 2
HEAD_DIM = 64
N_Q_HEADS = 8
N_KV_HEADS = 2
GQA_GROUP = 4
ATT_W = 512
KV_W = 128
WINDOW = 128
ATT_SCALE = HEAD_DIM ** -0.5
NUM_BUCKETS = 32
MAX_EXACT = 16
MAX_DISTANCE = 128
NEG_INF = -1e30
RWKV_HEAD = 64
RWKV_W = 512
RWKV_HEADS = 8
D_DECAY_LORA = 64
D_AAA_LORA = 64
D_GATE_LORA = 128
RWKV_IN = 3 * RWKV_W + D_DECAY_LORA + D_AAA_LORA + D_GATE_LORA
GN_EPS = 64e-5
IN_W = ATT_W + 2 * KV_W + RWKV_IN + 2 * D_MODEL
N_EXPERTS = 8
TOP_K = 2
NORM_EPS = 1e-6

VMEM_LIMIT_BYTES = 56 * 1024 * 1024
LANES = 128


def _params(*sem):
    return pltpu.CompilerParams(dimension_semantics=sem, vmem_limit_bytes=VMEM_LIMIT_BYTES)


def _const_spec(shape):
    zeros = (0,) * len(shape)
    return pl.BlockSpec(shape, lambda *_: zeros, pipeline_mode=pl.Buffered(1))


def _dot(a, b):
    return jnp.dot(a, b, preferred_element_type=F32)


def _dot_nt(a, b):
    return lax.dot_general(a, b, (((1,), (1,)), ((), ())), preferred_element_type=F32)


def _dot_tn(a, b):
    return lax.dot_general(a, b, (((0,), (0,)), ((), ())), preferred_element_type=F32)


def _split2(x):
    hi = x.astype(BF)
    lo = (x - hi.astype(F32)).astype(BF)
    return hi, lo


def _dot_exact_rhs(x, w):
    hi, lo = _split2(x)
    return _dot(hi, w) + _dot(lo, w)


def _rms(x, g):
    ms = jnp.mean(x * x, axis=-1, keepdims=True)
    return x * lax.rsqrt(ms + NORM_EPS) * g


def _sigmoid(x):
    return 1.0 / (1.0 + jnp.exp(-x))


def _inproj_kernel(x_ref, g_ref, w_ref, q_ref, k_ref, v_ref, z_ref, gate_ref):
    n = _rms(x_ref[...], g_ref[...]).astype(BF)
    q_ref[...] = _dot(n, w_ref[:, 0:ATT_W]).astype(BF)
    k_ref[...] = _dot(n, w_ref[:, ATT_W:ATT_W + KV_W])
    v_ref[...] = _dot(n, w_ref[:, ATT_W + KV_W:ATT_W + 2 * KV_W])
    z0 = ATT_W + 2 * KV_W
    z_ref[...] = _dot(n, w_ref[:, z0:z0 + RWKV_IN])
    gate_ref[...] = _dot(n, w_ref[:, z0 + RWKV_IN:IN_W])


def _inproj(x, g, w, tm):
    t = x.shape[0]
    row = lambda w_: pl.BlockSpec((tm, w_), lambda i: (i, 0))
    return pl.pallas_call(
        _inproj_kernel,
        grid=(t // tm,),
        in_specs=[row(D_MODEL), _const_spec((1, D_MODEL)), _const_spec((D_MODEL, IN_W))],
        out_specs=[row(ATT_W), row(KV_W), row(KV_W), row(RWKV_IN), row(2 * D_MODEL)],
        out_shape=[jax.ShapeDtypeStruct((t, ATT_W), BF),
                   jax.ShapeDtypeStruct((t, KV_W), F32),
                   jax.ShapeDtypeStruct((t, KV_W), F32),
                   jax.ShapeDtypeStruct((t, RWKV_IN), F32),
                   jax.ShapeDtypeStruct((t, 2 * D_MODEL), F32)],
        compiler_params=_params("parallel"),
        name="inproj",
    )(x, g, w)


def _t5_bucket(dist):
    n = jnp.maximum(dist, 0)
    nf = jnp.maximum(n, 1).astype(F32)
    large = MAX_EXACT + (jnp.log(nf / MAX_EXACT) / math.log(MAX_DISTANCE / MAX_EXACT)
                         * (NUM_BUCKETS - MAX_EXACT)).astype(jnp.int32)
    return jnp.where(n < MAX_EXACT, n, jnp.minimum(large, NUM_BUCKETS - 1))


def _bias_lookup(rel_bias, bucket):
    hit = bucket[..., None, None] == jnp.arange(NUM_BUCKETS, dtype=bucket.dtype)[:, None]
    return jnp.sum(jnp.where(hit, rel_bias, 0.0), axis=-2)


def _prompt_bias_table(rel_bias):
    q_off = jnp.arange(WINDOW)
    k_off = jnp.arange(2 * WINDOW) - WINDOW
    dist = q_off[:, None] - k_off[None, :]
    b = _bias_lookup(rel_bias, _t5_bucket(dist))
    valid = (dist >= 0) & (dist <= WINDOW)
    b = jnp.where(valid[:, :, None], b, NEG_INF)
    b = jnp.transpose(b, (2, 0, 1))
    return b.reshape(N_KV_HEADS, GQA_GROUP * WINDOW, 2 * WINDOW)


def _swa_prompt_kernel(nb, q_ref, kp_ref, kc_ref, vp_ref, vc_ref, bias_ref, sink_ref, o_ref):
    first = pl.program_id(0) == 0
    col = lax.broadcasted_iota(jnp.int32, (GQA_GROUP * WINDOW, 2 * WINDOW), 1)
    pad_mask = jnp.logical_and(first, col < WINDOW)
    chains = [(b, h) for b in range(nb) for h in range(N_KV_HEADS)]
    hs = lambda h: slice(h * HEAD_DIM, (h + 1) * HEAD_DIM)
    kh = [jnp.concatenate([kp_ref[b][:, hs(h)], kc_ref[b][:, hs(h)]], axis=0).astype(BF) for b, h in chains]
    vh = [jnp.concatenate([vp_ref[b][:, hs(h)], vc_ref[b][:, hs(h)]], axis=0).astype(BF) for b, h in chains]
    qh = [jnp.concatenate([q_ref[b][:, hs(GQA_GROUP * h + g)] for g in range(GQA_GROUP)], axis=0)
          for b, h in chains]
    s = [_dot_nt(qh[n], kh[n]) * ATT_SCALE + bias_ref[h] for n, (b, h) in enumerate(chains)]
    s = [jnp.where(pad_mask, NEG_INF, x) for x in s]
    m = [jnp.maximum(jnp.max(x, axis=-1, keepdims=True), sink_ref[h]) for x, (b, h) in zip(s, chains)]
    p = [jnp.exp(x - mm) for x, mm in zip(s, m)]
    denom = [jnp.sum(x, axis=-1, keepdims=True) + jnp.exp(sink_ref[h] - mm)
             for x, mm, (b, h) in zip(p, m, chains)]
    o = [_dot(x.astype(BF), vv) / d for x, vv, d in zip(p, vh, denom)]
    for b in range(nb):
        pieces = [o[b * N_KV_HEADS + h][g * WINDOW:(g + 1) * WINDOW]
                  for h in range(N_KV_HEADS) for g in range(GQA_GROUP)]
        o_ref[b] = jnp.concatenate(pieces, axis=-1).astype(o_ref.dtype)


def _swa_prompt(q3, k3, v3, bias_tab, sink_col):
    batch, seq, _ = q3.shape
    cur = lambda i: (0, i, 0)
    prev = lambda i: (0, jnp.maximum(i - 1, 0), 0)
    kv_c = pl.BlockSpec((batch, WINDOW, KV_W), cur)
    kv_p = pl.BlockSpec((batch, WINDOW, KV_W), prev)
    return pl.pallas_call(
        functools.partial(_swa_prompt_kernel, batch),
        grid=(seq // WINDOW,),
        in_specs=[pl.BlockSpec((batch, WINDOW, ATT_W), cur), kv_p, kv_c, kv_p, kv_c,
                  _const_spec(bias_tab.shape), _const_spec(sink_col.shape)],
        out_specs=pl.BlockSpec((batch, WINDOW, ATT_W), cur),
        out_shape=jax.ShapeDtypeStruct((batch, seq, ATT_W), BF),
        compiler_params=_params("parallel"),
        name="swa_prompt",
    )(q3, k3, k3, v3, v3, bias_tab, sink_col)


def _swa_sample_kernel(q_ref, kn_ref, vn_ref, ck_ref, cv_ref, bc_ref, bn_ref, sink_ref, o_ref):
    q = q_ref[...]
    qf = q.astype(F32)
    kn = kn_ref[...].astype(BF).astype(F32)
    vn = vn_ref[...].astype(BF).astype(F32)
    ck, cv = ck_ref[...], cv_ref[...]
    head = lax.broadcasted_iota(jnp.int32, (1, N_Q_HEADS, 1), 1)
    low = head < GQA_GROUP
    s_h, sn_h = [], []
    for h in range(N_KV_HEADS):
        sl = slice(h * HEAD_DIM, (h + 1) * HEAD_DIM)
        s_h.append(jnp.einsum("bgd,bkd->bgk", q, ck[:, :, sl].astype(BF), preferred_element_type=F32))
        sn_h.append(jnp.sum(qf * kn[:, None, sl], axis=-1, keepdims=True))
    s = jnp.where(low, s_h[0], s_h[1]) * ATT_SCALE + bc_ref[...]
    sn = jnp.where(low, sn_h[0], sn_h[1]) * ATT_SCALE + bn_ref[...]
    sink = sink_ref[...]
    m = jnp.maximum(jnp.maximum(jnp.max(s, axis=-1, keepdims=True), sn), sink)
    p = jnp.exp(s - m)
    pn = jnp.exp(sn - m)
    denom = jnp.sum(p, axis=-1, keepdims=True) + pn + jnp.exp(sink - m)
    pb = p.astype(BF)
    pnb = pn.astype(BF).astype(F32)
    o_h = []
    for h in range(N_KV_HEADS):
        sl = slice(h * HEAD_DIM, (h + 1) * HEAD_DIM)
        o = jnp.einsum("bgk,bkd->bgd", pb, cv[:, :, sl].astype(BF), preferred_element_type=F32)
        o_h.append(o + pnb * vn[:, None, sl])
    o_ref[...] = (jnp.where(low, o_h[0], o_h[1]) / denom).astype(o_ref.dtype)


def _swa_sample(q3, kn, vn, ck, cv, bias_c, bias_n, sink3, bb):
    b = q3.shape[0]
    return pl.pallas_call(
        _swa_sample_kernel,
        grid=(b // bb,),
        in_specs=[pl.BlockSpec((bb, N_Q_HEADS, HEAD_DIM), lambda i: (i, 0, 0)),
                  pl.BlockSpec((bb, KV_W), lambda i: (i, 0)),
                  pl.BlockSpec((bb, KV_W), lambda i: (i, 0)),
                  pl.BlockSpec((bb, WINDOW, KV_W), lambda i: (i, 0, 0)),
                  pl.BlockSpec((bb, WINDOW, KV_W), lambda i: (i, 0, 0)),
                  _const_spec(bias_c.shape), _const_spec(bias_n.shape), _const_spec(sink3.shape)],
        out_specs=pl.BlockSpec((bb, N_Q_HEADS, HEAD_DIM), lambda i: (i, 0, 0)),
        out_shape=jax.ShapeDtypeStruct((b, N_Q_HEADS, HEAD_DIM), BF),
        compiler_params=_params("parallel"),
        name="swa_sample",
    )(q3, kn, vn, ck, cv, bias_c, bias_n, sink3)


def _rwkv_prep_kernel(z_ref, zp8_ref, pz_ref, mu_ref, w0_ref, w2_ref, a0_ref, a2_ref, g2_ref,
                      kk_ref, ka_ref, ones_ref,
                      r_ref, lw_ref, k2_ref, v_ref, a_ref, b_ref, g_ref):
    z = z_ref[0]
    tm = z.shape[0]
    if pz_ref.shape[1] == tm:
        zs = pz_ref[0]
    else:
        prev_row = jnp.where(pl.program_id(1) == 0, pz_ref[0], zp8_ref[0, 7:8, :])
        row = lax.broadcasted_iota(jnp.int32, (tm, 1), 0)
        zs = jnp.where(row == 0, prev_row, pltpu.roll(z, 1, 0))
    zz = z + (zs - z) * mu_ref[...]
    r = zz[:, 0:RWKV_W]
    k = zz[:, RWKV_W:2 * RWKV_W]
    v = zz[:, 2 * RWKV_W:3 * RWKV_W]
    o = 3 * RWKV_W
    wl = zz[:, o:o + D_DECAY_LORA]
    al = zz[:, o + D_DECAY_LORA:o + D_DECAY_LORA + D_AAA_LORA]
    gl = zz[:, o + D_DECAY_LORA + D_AAA_LORA:RWKV_IN]
    wpre = -(w0_ref[...] + _dot(jnp.tanh(wl).astype(BF), w2_ref[...]))
    softplus = jnp.maximum(wpre, 0.0) + jnp.log1p(jnp.exp(-jnp.abs(wpre)))
    lw = -jnp.exp(-softplus - 0.5)
    a = _sigmoid(a0_ref[...] + _dot(al.astype(BF), a2_ref[...]))
    g = _dot(_sigmoid(gl).astype(BF), g2_ref[...])
    kkv = k * kk_ref[...]
    ss = _dot_exact_rhs(kkv * kkv, ones_ref[...])
    kk = kkv / jnp.maximum(jnp.sqrt(ss), 1e-12)
    k2 = k * (1.0 + (a - 1.0) * ka_ref[...])
    r_ref[0], lw_ref[0], k2_ref[0], v_ref[0] = r, lw, k2, v
    a_ref[0], b_ref[0], g_ref[0] = -kk, kk * a, g


def _rwkv_prep(z3, prev_z, p, tm):
    b, t, _ = z3.shape
    blk8 = tm // 8
    vec = lambda w_: _const_spec((1, w_))
    out_spec = pl.BlockSpec((1, tm, RWKV_W), lambda bi, i: (bi, i, 0))
    return pl.pallas_call(
        _rwkv_prep_kernel,
        grid=(b, t // tm),
        in_specs=[pl.BlockSpec((1, tm, RWKV_IN), lambda bi, i: (bi, i, 0)),
                  pl.BlockSpec((1, 8, RWKV_IN), lambda bi, i: (bi, jnp.maximum(i * blk8 - 1, 0), 0)),
                  pl.BlockSpec((1, prev_z.shape[1], RWKV_IN), lambda bi, i: (bi, 0, 0)),
                  vec(RWKV_IN), vec(RWKV_W), _const_spec((D_DECAY_LORA, RWKV_W)),
                  vec(RWKV_W), _const_spec((D_AAA_LORA, RWKV_W)), _const_spec((D_GATE_LORA, RWKV_W)),
                  vec(RWKV_W), vec(RWKV_W), _const_spec((RWKV_W, RWKV_W))],
        out_specs=[out_spec] * 7,
        out_shape=[jax.ShapeDtypeStruct((b, t, RWKV_W), F32)] * 7,
        compiler_params=_params("parallel", "parallel"),
        name="rwkv_prep",
    )(z3, z3, prev_z, p["mu"], p["w0"], p["w2"], p["a0"], p["a2"], p["g2"], p["k_k"], p["k_a"], p["ones_bd"])


def _rwkv_scan_kernel(nb, chunk, r_ref, lw_ref, k_ref, v_ref, a_ref, b_ref, g_ref, s0_ref,
                      rk_ref, lg_ref, lb_ref, ones_ref, y_ref, sout_ref, s_scr):
    c = pl.program_id(1)

    @pl.when(c == 0)
    def _():
        s_scr[...] = s0_ref[...]

    ri = lax.broadcasted_iota(jnp.int32, (chunk, chunk), 0)
    ci = lax.broadcasted_iota(jnp.int32, (chunk, chunk), 1)
    incl = ri >= ci
    strict = ri > ci
    tri = incl.astype(BF)
    eye = (ri == ci).astype(F32)
    n_sq = int(math.log2(chunk)) - 1

    chains = [(bi, h) for bi in range(nb) for h in range(RWKV_HEADS)]
    hs = lambda h: slice(h * RWKV_HEAD, (h + 1) * RWKV_HEAD)

    ra, aa, bt, kt, bh, kh, vb, g_all = [], [], [], [], [], [], [], []
    for bi in range(nb):
        r, lw, k, v = r_ref[bi], lw_ref[bi], k_ref[bi], v_ref[bi]
        a, b = a_ref[bi], b_ref[bi]
        hi, lo = _split2(lw)
        lo2 = (lw - hi.astype(F32) - lo.astype(F32)).astype(BF)
        cum = _dot(tri, hi) + _dot(tri, lo) + _dot(tri, lo2)
        tail = cum[chunk - 1:chunk, :]
        g_inv = jnp.exp(-cum)
        g_tail = jnp.exp(tail - cum)
        g_all.append(jnp.exp(tail))
        ra.append((r * jnp.exp(cum)).astype(BF))
        aa.append((a * jnp.exp(cum - lw)).astype(BF))
        bt.append((b * g_inv).astype(BF))
        kt.append((k * g_inv).astype(BF))
        bh.append((b * g_tail).astype(BF))
        kh.append((k * g_tail).astype(BF))
        vb.append(v.astype(BF))

    s_old = [s_scr[bi, h] for bi, h in chains]
    ar = [jnp.concatenate([aa[bi][:, hs(h)], ra[bi][:, hs(h)]], axis=0) for bi, h in chains]
    v_h = [vb[bi][:, hs(h)] for bi, h in chains]
    gb = [_dot_nt(ar[n], bt[bi][:, hs(h)]) for n, (bi, h) in enumerate(chains)]
    gk = [_dot_nt(ar[n], kt[bi][:, hs(h)]) for n, (bi, h) in enumerate(chains)]
    p = [_dot_nt(ar[n], s_old[n].astype(BF)) for n in range(len(chains))]
    l_ab = [jnp.where(strict, x[:chunk], 0.0) for x in gb]
    l_ak = [jnp.where(strict, x[:chunk], 0.0).astype(BF) for x in gk]
    m_rb = [jnp.where(incl, x[chunk:], 0.0).astype(BF) for x in gb]
    m_rk = [jnp.where(incl, x[chunk:], 0.0).astype(BF) for x in gk]
    rhs = [p[n][:chunk] + _dot(l_ak[n], v_h[n]) for n in range(len(chains))]
    t_inv = [eye + x for x in l_ab]
    lp = l_ab
    for _ in range(n_sq):
        lpb = [x.astype(BF) for x in lp]
        lp = [_dot(x, x) for x in lpb]
        t_inv = [t + _dot(x.astype(BF), t.astype(BF)) for x, t in zip(lp, t_inv)]
    ub = [_dot(t.astype(BF), x.astype(BF)).astype(BF) for t, x in zip(t_inv, rhs)]
    y_h = [p[n][chunk:] + _dot(m_rb[n], ub[n]) + _dot(m_rk[n], v_h[n]) for n in range(len(chains))]
    s_new = [s_old[n] * g_all[bi][:, hs(h)] + _dot_tn(ub[n], bh[bi][:, hs(h)]) + _dot_tn(v_h[n], kh[bi][:, hs(h)])
             for n, (bi, h) in enumerate(chains)]
    for n, (bi, h) in enumerate(chains):
        s_scr[bi, h] = s_new[n]

    for bi in range(nb):
        y = jnp.concatenate(y_h[bi * RWKV_HEADS:(bi + 1) * RWKV_HEADS], axis=-1)
        y_ref[bi] = _rwkv_epilogue(y, r_ref[bi], k_ref[bi], v_ref[bi], g_ref[bi],
                                   rk_ref, lg_ref, lb_ref, ones_ref).astype(y_ref.dtype)

    @pl.when(c == pl.num_programs(1) - 1)
    def _():
        sout_ref[...] = s_scr[...]


def _rwkv_epilogue(y, r, k, v, g, rk_ref, lg_ref, lb_ref, ones_ref):
    ones = ones_ref[...]
    inv_n = 1.0 / RWKV_HEAD
    mean = _dot_exact_rhs(y, ones) * inv_n
    d = y - mean
    var = _dot_exact_rhs(d * d, ones) * inv_n
    yn = d * lax.rsqrt(var + GN_EPS) * lg_ref[...] + lb_ref[...]
    bonus = _dot_exact_rhs(r * k * rk_ref[...], ones) * v
    return (yn + bonus) * g


def _rwkv_scan(coef, s0, p, nb, chunk):
    r = coef[0]
    b, t, _ = r.shape
    seq_spec = pl.BlockSpec((nb, chunk, RWKV_W), lambda bi, c: (bi, c, 0))
    st_spec = pl.BlockSpec((nb, RWKV_HEADS, RWKV_HEAD, RWKV_HEAD), lambda bi, c: (bi, 0, 0, 0))
    vec = _const_spec((1, RWKV_W))
    return pl.pallas_call(
        functools.partial(_rwkv_scan_kernel, nb, chunk),
        grid=(b // nb, t // chunk),
        in_specs=[seq_spec] * 7 + [st_spec, vec, vec, vec, _const_spec((RWKV_W, RWKV_W))],
        out_specs=[seq_spec, st_spec],
        out_shape=[jax.ShapeDtypeStruct((b, t, RWKV_W), BF),
                   jax.ShapeDtypeStruct((b, RWKV_HEADS, RWKV_HEAD, RWKV_HEAD), F32)],
        scratch_shapes=[pltpu.VMEM((nb, RWKV_HEADS, RWKV_HEAD, RWKV_HEAD), F32)],
        compiler_params=_params("parallel", "arbitrary"),
        name="rwkv_scan",
    )(*coef, s0, p["r_k"], p["lnx_g"], p["lnx_b"], p["ones_bd"])


def _rwkv_step_kernel(r_ref, lw_ref, k_ref, v_ref, a_ref, b_ref, g_ref, s_ref,
                      rk_ref, lg_ref, lb_ref, ones_ref, y_ref, sout_ref, y_scr):
    n_pair = LANES // RWKV_HEAD
    r, k, v = r_ref[...], k_ref[...], v_ref[...]
    rT, kT, vT = r.T, k.T, v.T
    aT, bT, wT = a_ref[...].T, b_ref[...].T, jnp.exp(lw_ref[...]).T
    for hl in range(LANES // RWKV_HEAD):
        hsl = slice(hl * RWKV_HEAD, (hl + 1) * RWKV_HEAD)
        a_h, b_h, k_h, w_h, r_h = aT[hsl], bT[hsl], kT[hsl], wT[hsl], rT[hsl]
        tiles = [(hl * RWKV_HEAD + n_pair * t) * RWKV_HEAD for t in range(RWKV_HEAD // n_pair)]
        st = [s_ref[:, c0:c0 + LANES].T for c0 in tiles]
        new = []
        for t, x in enumerate(st):
            halves = []
            for il in range(n_pair):
                i = hl * RWKV_HEAD + n_pair * t + il
                slab = x[il * RWKV_HEAD:(il + 1) * RWKV_HEAD]
                sa = jnp.sum(slab * a_h, axis=0, keepdims=True)
                slab = slab * w_h + sa * b_h + vT[i:i + 1] * k_h
                y_scr[i:i + 1, :] = jnp.sum(slab * r_h, axis=0, keepdims=True)
                halves.append(slab)
            new.append(jnp.concatenate(halves, axis=0))
        for c0, x in zip(tiles, new):
            sout_ref[:, c0:c0 + LANES] = x.T
    y_ref[...] = _rwkv_epilogue(y_scr[...].T, r, k, v, g_ref[...],
                                rk_ref, lg_ref, lb_ref, ones_ref).astype(y_ref.dtype)


def _rwkv_step(coef, s0, p):
    b = s0.shape[0]
    per_pair = (LANES // RWKV_HEAD) * RWKV_HEAD * RWKV_HEAD
    n_steps = RWKV_W // LANES
    col = pl.BlockSpec((b, LANES), lambda i: (0, i))
    vec = pl.BlockSpec((1, LANES), lambda i: (0, i))
    st_spec = pl.BlockSpec((b, per_pair), lambda i: (0, i))
    y, s_new = pl.pallas_call(
        _rwkv_step_kernel,
        grid=(n_steps,),
        in_specs=[col] * 7 + [st_spec, vec, vec, vec, _const_spec((LANES, LANES))],
        out_specs=[col, st_spec],
        out_shape=[jax.ShapeDtypeStruct((b, RWKV_W), BF),
                   jax.ShapeDtypeStruct((b, n_steps * per_pair), F32)],
        scratch_shapes=[pltpu.VMEM((LANES, b), F32)],
        compiler_params=_params("parallel"),
        name="rwkv_step",
    )(*coef, s0.reshape(b, n_steps * per_pair), p["r_k"], p["lnx_g"], p["lnx_b"],
      p["ones_bd"][:LANES, :LANES])
    return y, s_new.reshape(s0.shape)


def _merge_kernel(with_router, x_ref, ya_ref, yb_ref, gate_ref, wa_ref, wb_ref, wo_ref, g_ref, *rest):
    if with_router:
        wr_ref, cnt_in_ref, x1_ref, h_ref, route_ref, cnt_out_ref, cnt_scr = rest
    else:
        x1_ref, h_ref = rest
    gate = gate_ref[...]
    pa = _dot(ya_ref[...], wa_ref[...])
    pb = _dot(yb_ref[...], wb_ref[...])
    merged = _sigmoid(gate[:, :D_MODEL]) * pa + _sigmoid(gate[:, D_MODEL:]) * pb
    x1 = x_ref[...] + _dot(merged.astype(BF), wo_ref[...])
    x1_ref[...] = x1
    h = _rms(x1, g_ref[...])
    if not with_router:
        h_ref[...] = h.astype(h_ref.dtype)
        return
    for s in range(D_MODEL // LANES):
        h_ref[:, s, :] = h[:, s * LANES:(s + 1) * LANES]

    @pl.when(pl.program_id(0) == 0)
    def _():
        cnt_scr[...] = cnt_in_ref[...]

    hh, hl = _split2(h)
    wh, wl = wr_ref[0], wr_ref[1]
    logits = _dot(hh, wh) + _dot(hl, wh) + _dot(hh, wl)
    tm = logits.shape[0]
    lane = lax.broadcasted_iota(jnp.int32, logits.shape, 1).astype(F32)
    logits = jnp.where(lane < N_EXPERTS, logits, -jnp.inf)
    m1 = jnp.max(logits, axis=-1, keepdims=True)
    i1 = jnp.min(jnp.where(logits == m1, lane, float(LANES)), axis=-1, keepdims=True)
    rest_l = jnp.where(lane == i1, -jnp.inf, logits)
    m2 = jnp.max(rest_l, axis=-1, keepdims=True)
    i2 = jnp.min(jnp.where(rest_l == m2, lane, float(LANES)), axis=-1, keepdims=True)
    e2 = jnp.exp(m2 - m1)
    g1 = 1.0 / (1.0 + e2)
    g2 = e2 / (1.0 + e2)
    oh1 = (lane == i1).astype(F32)
    oh2 = (lane == i2).astype(F32)
    both = oh1 + oh2
    ri = lax.broadcasted_iota(jnp.int32, (tm, tm), 0)
    ci = lax.broadcasted_iota(jnp.int32, (tm, tm), 1)
    before = _dot((ri > ci).astype(BF), both.astype(BF)) + cnt_scr[...]
    rank1 = jnp.sum(oh1 * before, axis=-1, keepdims=True)
    rank2 = jnp.sum(oh2 * before, axis=-1, keepdims=True)
    cnt_scr[...] += jnp.sum(both, axis=0, keepdims=True)
    cnt_out_ref[...] = cnt_scr[...]
    route = jnp.zeros_like(logits)
    for n, val in enumerate((i1, i2, g1, g2, rank1, rank2)):
        route = jnp.where(lane == n, val, route)
    route_ref[...] = route


def _merge_out(x, ya, yb, gate, wa, wb, wo, g, tm, router=None, cnt_in=None):
    t = x.shape[0]
    row = lambda w_: pl.BlockSpec((tm, w_), lambda i: (i, 0))
    in_specs = [row(D_MODEL), row(ATT_W), row(RWKV_W), row(2 * D_MODEL),
                _const_spec((ATT_W, D_MODEL)), _const_spec((RWKV_W, D_MODEL)),
                _const_spec((D_MODEL, D_MODEL)), _const_spec((1, D_MODEL))]
    args = [x, ya, yb, gate, wa, wb, wo, g]
    if router is None:
        out_specs = [row(D_MODEL), row(D_MODEL)]
        out_shape = [jax.ShapeDtypeStruct((t, D_MODEL), F32), jax.ShapeDtypeStruct((t, D_MODEL), BF)]
        scratch = []
    else:
        n_sub = D_MODEL // LANES
        in_specs += [_const_spec(router.shape), _const_spec((1, LANES))]
        args += [router, cnt_in]
        out_specs = [row(D_MODEL), pl.BlockSpec((tm, n_sub, LANES), lambda i: (i, 0, 0)), row(LANES),
                     pl.BlockSpec((1, LANES), lambda i: (0, 0))]
        out_shape = [jax.ShapeDtypeStruct((t, D_MODEL), F32), jax.ShapeDtypeStruct((t, n_sub, LANES), F32),
                     jax.ShapeDtypeStruct((t, LANES), F32), jax.ShapeDtypeStruct((1, LANES), F32)]
        scratch = [pltpu.VMEM((1, LANES), F32)]
    return pl.pallas_call(
        functools.partial(_merge_kernel, router is not None),
        grid=(t // tm,),
        in_specs=in_specs, out_specs=out_specs, out_shape=out_shape, scratch_shapes=scratch,
        compiler_params=_params("parallel" if router is None else "arbitrary"),
        name="merge_out",
    )(*args)


def _swiglu(x, wg, wu, wd):
    s = _dot(x, wg)
    u = _dot(x, wu)
    act = (s * _sigmoid(s) * u).astype(BF)
    return _dot(act, wd)


def _dense_ffn_kernel(x1_ref, h_ref, wg_ref, wu_ref, wd_ref, o_ref):
    o_ref[...] = x1_ref[...] + _swiglu(h_ref[...], wg_ref[...], wu_ref[...], wd_ref[...])


def _dense_ffn(x1, h, wg, wu, wd, tm):
    t = x1.shape[0]
    d_ff = wg.shape[1]
    row = lambda: pl.BlockSpec((tm, D_MODEL), lambda i: (i, 0))
    return pl.pallas_call(
        _dense_ffn_kernel,
        grid=(t // tm,),
        in_specs=[row(), row(), _const_spec((D_MODEL, d_ff)), _const_spec((D_MODEL, d_ff)),
                  _const_spec((d_ff, D_MODEL))],
        out_specs=row(),
        out_shape=jax.ShapeDtypeStruct((t, D_MODEL), F32),
        compiler_params=_params("parallel"),
        name="dense_ffn",
    )(x1, h, wg, wu, wd)


def _rows_as_matrix(ref3):
    return jnp.concatenate([ref3[:, s, :] for s in range(ref3.shape[1])], axis=-1)


def _dispatch_kernel(tt, first, dest_ref, fill_ref, h_ref, *rest):
    if first:
        xs_hbm, zbuf, sem, zsem = rest
    else:
        _, xs_hbm, sem = rest
    i = pl.program_id(0)
    base = i * tt

    if first:
        @pl.when(i == 0)
        def _():
            zbuf[...] = jnp.zeros_like(zbuf)
            bm = zbuf.shape[0]
            fills = [pltpu.make_async_copy(zbuf, xs_hbm.at[pl.ds(fill_ref[n] * bm, bm)], zsem)
                     for n in range(fill_ref.shape[0])]
            for f in fills:
                f.start()
            for f in fills:
                f.wait()

    def issue(r, carry):
        for j in range(TOP_K):
            pltpu.make_async_copy(h_ref.at[r], xs_hbm.at[dest_ref[TOP_K * (base + r) + j]], sem).start()
        return carry

    lax.fori_loop(0, tt, issue, 0, unroll=8)
    for j in range(TOP_K):
        pltpu.make_async_copy(h_ref, xs_hbm.at[pl.ds(0, tt)], sem).wait()


def _dispatch(dest, fill_blocks, h3, xs, tt, bm, n_rows):
    t, n_sub, _ = h3.shape
    first = xs is None
    in_specs = [pl.BlockSpec((tt, n_sub, LANES), lambda i, ds, fb: (i, 0, 0))]
    args = [dest, fill_blocks, h3]
    if first:
        scratch = [pltpu.VMEM((bm, n_sub, LANES), F32), pltpu.SemaphoreType.DMA(()), pltpu.SemaphoreType.DMA(())]
        aliases = {}
    else:
        in_specs.append(pl.BlockSpec(memory_space=pl.ANY))
        args.append(xs)
        scratch = [pltpu.SemaphoreType.DMA(())]
        aliases = {3: 0}
    grid_spec = pltpu.PrefetchScalarGridSpec(
        num_scalar_prefetch=2, grid=(t // tt,), in_specs=in_specs,
        out_specs=pl.BlockSpec(memory_space=pl.ANY), scratch_shapes=scratch)
    return pl.pallas_call(
        functools.partial(_dispatch_kernel, tt, first),
        grid_spec=grid_spec,
        out_shape=jax.ShapeDtypeStruct((n_rows, n_sub, LANES), F32),
        input_output_aliases=aliases,
        compiler_params=_params("arbitrary"),
        name="moe_dispatch",
    )(*args)


def _moe_kernel(n_half, blk_e_ref, n_used_ref, xs_ref, wg_ref, wu_ref, wd_ref, y_ref):
    @pl.when(pl.program_id(0) < n_used_ref[0])
    def _():
        x = _rows_as_matrix(xs_ref).astype(BF)
        d_e = wg_ref.shape[2]
        step = d_e // n_half
        acc = None
        for j in range(n_half):
            cs = slice(j * step, (j + 1) * step)
            part = _swiglu(x, wg_ref[0, :, cs], wu_ref[0, :, cs], wd_ref[0, cs, :])
            acc = part if acc is None else acc + part
        for s in range(y_ref.shape[1]):
            y_ref[:, s, :] = acc[:, s * LANES:(s + 1) * LANES]

    @pl.when(pl.program_id(0) >= n_used_ref[0])
    def _():
        y_ref[...] = jnp.zeros_like(y_ref)


def _moe_ffn(xs, blk_e, n_used, wg, wu, wd, bm):
    n_blk = blk_e.shape[0]
    n_rows, n_sub, _ = xs.shape
    d_e = wg.shape[2]
    wspec = lambda shape: pl.BlockSpec(shape, lambda i, be, nu: (be[i], 0, 0), pipeline_mode=pl.Buffered(1))
    rows = pl.BlockSpec((bm, n_sub, LANES), lambda i, be, nu: (i, 0, 0))
    grid_spec = pltpu.PrefetchScalarGridSpec(
        num_scalar_prefetch=2,
        grid=(n_blk,),
        in_specs=[rows, wspec((1, D_MODEL, d_e)), wspec((1, D_MODEL, d_e)), wspec((1, d_e, D_MODEL))],
        out_specs=rows,
    )
    return pl.pallas_call(
        functools.partial(_moe_kernel, 2),
        grid_spec=grid_spec,
        out_shape=jax.ShapeDtypeStruct((n_rows, n_sub, LANES), F32),
        compiler_params=_params("arbitrary"),
        name="moe_ffn",
    )(blk_e, n_used, xs, wg, wu, wd)


def _row_layout(counts, bm, n_blk):
    per_e = (counts + bm - 1) // bm
    ends = jnp.cumsum(per_e)
    n_used = ends[-1]
    first_row = (ends - per_e) * bm
    i = jnp.arange(n_blk, dtype=jnp.int32)
    blk_e = jnp.sum((jnp.minimum(i, jnp.maximum(n_used - 1, 0))[:, None] >= ends[None, :]).astype(jnp.int32), axis=1)
    need = jnp.any(i[:, None] == (ends - 1)[None, :], axis=1) | (i >= n_blk - N_EXPERTS)
    fill_blocks = jnp.argsort(jnp.logical_not(need), stable=True)[:2 * N_EXPERTS].astype(jnp.int32)
    return first_row.astype(jnp.int32), blk_e.astype(jnp.int32), n_used.astype(jnp.int32).reshape(1), fill_blocks


def _combine_kernel(tc, dest_ref, x1_ref, route_ref, y_hbm, g_ref, o_ref, ybuf, sem):
    base = pl.program_id(0) * tc

    def issue(r, carry):
        for j in range(TOP_K):
            pltpu.make_async_copy(y_hbm.at[dest_ref[TOP_K * (base + r) + j]], ybuf.at[j, r], sem).start()
        return carry

    lax.fori_loop(0, tc, issue, 0, unroll=4)
    for j in range(TOP_K):
        pltpu.make_async_copy(y_hbm.at[pl.ds(0, tc)], ybuf.at[j], sem).wait()
    route = route_ref[...]
    x2 = (x1_ref[...] + route[:, 2:3] * _rows_as_matrix(ybuf.at[0])
          + route[:, 3:4] * _rows_as_matrix(ybuf.at[1]))
    o_ref[...] = _rms(x2, g_ref[...])


def _moe_combine(dest, x1, route, y, g, tc):
    t = x1.shape[0]
    n_sub = y.shape[1]
    grid_spec = pltpu.PrefetchScalarGridSpec(
        num_scalar_prefetch=1,
        grid=(t // tc,),
        in_specs=[pl.BlockSpec((tc, D_MODEL), lambda i, ds: (i, 0)),
                  pl.BlockSpec((tc, LANES), lambda i, ds: (i, 0)),
                  pl.BlockSpec(memory_space=pl.ANY),
                  pl.BlockSpec((1, D_MODEL), lambda i, ds: (0, 0))],
        out_specs=pl.BlockSpec((tc, D_MODEL), lambda i, ds: (i, 0)),
        scratch_shapes=[pltpu.VMEM((TOP_K, tc, n_sub, LANES), F32), pltpu.SemaphoreType.DMA(())],
    )
    return pl.pallas_call(
        functools.partial(_combine_kernel, tc),
        grid_spec=grid_spec,
        out_shape=jax.ShapeDtypeStruct((t, D_MODEL), F32),
        compiler_params=_params("arbitrary"),
        name="moe_combine",
    )(dest, x1, route, y, g)


def _layer_params(l, w_in, mu_shift, w0, w2, a0, a2, g2, k_k, k_a, r_k, lnx_g, lnx_b,
                  w_proj_attn, w_proj_rwkv, w_out, ones_bd):
    row = lambda u: u.reshape(1, -1)
    return dict(
        w_in=w_in[l].astype(BF), mu=row(mu_shift[l]), w0=row(w0[l]), w2=w2[l].astype(BF),
        a0=row(a0[l]), a2=a2[l].astype(BF), g2=g2[l].astype(BF), k_k=row(k_k[l]), k_a=row(k_a[l]),
        r_k=row(r_k[l]), lnx_g=row(lnx_g[l]), lnx_b=row(lnx_b[l]),
        wa=w_proj_attn[l].astype(BF), wb=w_proj_rwkv[l].astype(BF), wo=w_out[l].astype(BF),
        ones_bd=ones_bd)


def kernel(x_prompt, x_sample, cache_k, cache_v, state_wkv, state_shift, norm_mix_g, w_in, attn_sinks, rel_bias, mu_shift, w0, w2, a0, a2, g2, k_k, k_a, r_k, lnx_g, lnx_b, w_proj_attn, w_proj_rwkv, w_out, norm_ffn_g, dense_w_gate, dense_w_up, dense_w_down, router_w, moe_w_gate, moe_w_up, moe_w_down, norm_final_g):
    batch, seq, _ = x_prompt.shape
    dec_batch, dec_seq, _ = x_sample.shape
    assert dec_seq == 1 and seq % WINDOW == 0

    head_id = jnp.arange(RWKV_W, dtype=jnp.int32) // RWKV_HEAD
    ones_bd = (head_id[:, None] == head_id[None, :]).astype(BF)
    bias_tab = _prompt_bias_table(rel_bias)
    bias_by_dist = _bias_lookup(rel_bias, _t5_bucket(jnp.arange(WINDOW + 1)))
    bias_c = jnp.transpose(bias_by_dist[WINDOW:0:-1], (1, 0))[None]
    bias_n = bias_by_dist[0][None, :, None]

    layers = [_layer_params(l, w_in, mu_shift, w0, w2, a0, a2, g2, k_k, k_a, r_k, lnx_g, lnx_b,
                            w_proj_attn, w_proj_rwkv, w_out, ones_bd) for l in range(DEPTH)]

    def router_pieces(l):
        wr = jnp.zeros((D_MODEL, LANES), F32).at[:, :N_EXPERTS].set(router_w[l // 2])
        hi = wr.astype(BF)
        lo = (wr - hi.astype(F32)).astype(BF)
        return jnp.stack([hi, lo])

    n_prompt = batch * seq
    tm_p = 256
    bm = 512
    n_all = n_prompt + dec_batch
    n_blk = -(-(n_all * TOP_K) // bm) + N_EXPERTS

    def mix(x, is_prompt, l, state):
        p = layers[l]
        n_tok = x.shape[0]
        tm = tm_p if is_prompt else n_tok
        q, k, v, z, gate = _inproj(x, norm_mix_g[l].reshape(1, -1), p["w_in"], tm)
        sinks = attn_sinks[l]
        if is_prompt:
            b, t = batch, seq
            sink_col = jnp.repeat(sinks, WINDOW).reshape(N_KV_HEADS, GQA_GROUP * WINDOW, 1)
            ya = _swa_prompt(q.reshape(b, t, ATT_W), k.reshape(b, t, KV_W), v.reshape(b, t, KV_W),
                             bias_tab, sink_col).reshape(n_tok, ATT_W)
            k4 = k.reshape(b, t, N_KV_HEADS, HEAD_DIM)
            v4 = v.reshape(b, t, N_KV_HEADS, HEAD_DIM)
            state["k"].append(k4[:, t - WINDOW:])
            state["v"].append(v4[:, t - WINDOW:])
            z3 = z.reshape(b, t, RWKV_IN)
            prev_z = jnp.zeros((b, 1, RWKV_IN), F32)
            s0 = jnp.zeros((b, RWKV_HEADS, RWKV_HEAD, RWKV_HEAD), F32)
            coef = _rwkv_prep(z3, prev_z, p, 256)
            yb3, s_new = _rwkv_scan(coef, s0, p, b, 64)
            yb = yb3.reshape(n_tok, RWKV_W)
            state["z"].append(z3[:, -1])
        else:
            b = dec_batch
            ck = cache_k[l].reshape(b, WINDOW, KV_W)
            cv = cache_v[l].reshape(b, WINDOW, KV_W)
            ya3 = _swa_sample(q.reshape(b, N_Q_HEADS, HEAD_DIM), k, v, ck, cv, bias_c, bias_n,
                              sinks.reshape(1, N_Q_HEADS, 1), 16)
            ya = ya3.reshape(b, ATT_W)
            state["k"].append(jnp.concatenate([ck[:, 1:], k[:, None]], axis=1)
                              .reshape(b, WINDOW, N_KV_HEADS, HEAD_DIM))
            state["v"].append(jnp.concatenate([cv[:, 1:], v[:, None]], axis=1)
                              .reshape(b, WINDOW, N_KV_HEADS, HEAD_DIM))
            coef = _rwkv_prep(z.reshape(1, b, RWKV_IN), state_shift[l].reshape(1, b, RWKV_IN), p, b)
            yb, s_new = _rwkv_step([u.reshape(b, RWKV_W) for u in coef], state_wkv[l], p)
            state["z"].append(z)
        state["s"].append(s_new)
        return ya, yb, gate

    groups = [dict(x=x_prompt.reshape(n_prompt, D_MODEL), prompt=True, tm=tm_p, k=[], v=[], s=[], z=[]),
              dict(x=x_sample.reshape(dec_batch, D_MODEL), prompt=False, tm=dec_batch, k=[], v=[], s=[], z=[])]
    for l in range(DEPTH):
        p = layers[l]
        gn = norm_ffn_g[l].reshape(1, -1)
        j = l // 2
        if l % 2 == 0:
            wg, wu, wd = (w[j].astype(BF) for w in (dense_w_gate, dense_w_up, dense_w_down))
            for grp in groups:
                ya, yb, gate = mix(grp["x"], grp["prompt"], l, grp)
                x1, h = _merge_out(grp["x"], ya, yb, gate, p["wa"], p["wb"], p["wo"], gn, grp["tm"])
                grp["x"] = _dense_ffn(x1, h, wg, wu, wd, grp["tm"])
        else:
            assert l == DEPTH - 1
            wg, wu, wd = (w[j].astype(BF) for w in (moe_w_gate, moe_w_up, moe_w_down))
            router = router_pieces(l)
            cnt = jnp.zeros((1, LANES), F32)
            xs = None
            for grp in groups:
                ya, yb, gate = mix(grp["x"], grp["prompt"], l, grp)
                x1, h3, route, cnt = _merge_out(grp["x"], ya, yb, gate, p["wa"], p["wb"], p["wo"], gn,
                                                grp["tm"], router=router, cnt_in=cnt)
                grp["x1"], grp["route"], grp["h3"] = x1, route, h3
            counts = cnt[0, :N_EXPERTS].astype(jnp.int32)
            first_row, blk_e, n_used, fill_blocks = _row_layout(counts, bm, n_blk)
            for grp in groups:
                route = grp["route"]
                expert = route[:, 0:TOP_K].astype(jnp.int32)
                rank = route[:, 2 * TOP_K:3 * TOP_K].astype(jnp.int32)
                start = jnp.sum(jnp.where(expert[..., None] == jnp.arange(N_EXPERTS, dtype=jnp.int32),
                                          first_row, 0), axis=-1)
                grp["dest"] = (start + rank).reshape(-1)
                xs = _dispatch(grp["dest"], fill_blocks, grp["h3"], xs, min(512, grp["h3"].shape[0]), bm,
                               n_blk * bm)
            y = _moe_ffn(xs, blk_e, n_used, wg, wu, wd, bm)
            for grp in groups:
                grp["x"] = _moe_combine(grp["dest"], grp["x1"], grp["route"], y,
                                        norm_final_g.reshape(1, -1), min(512, grp["x1"].shape[0]))

    gp, gs = groups
    outs = []
    for grp, shape in ((gp, (batch, seq, D_MODEL)), (gs, (dec_batch, dec_seq, D_MODEL))):
        outs.append((grp["x"].reshape(shape), jnp.stack(grp["k"]), jnp.stack(grp["v"]),
                     jnp.stack(grp["s"]), jnp.stack(grp["z"])))
    (y_p, nk_p, nv_p, ns_p, nz_p), (y_s, nk_s, nv_s, ns_s, nz_s) = outs
    return (y_p, y_s, nk_p, nv_p, ns_p, nz_p, nk_s, nv_s, ns_s, nz_s)
```

```python
import functools
import math

import jax
import jax.numpy as jnp
import numpy as np
from jax import lax
from jax.experimental import pallas as pl
from jax.experimental.pallas import tpu as pltpu

BF = jnp.bfloat16
F32 = jnp.float32

D_MODEL = 1024
DEPTH = 2
HEAD_DIM = 64
N_Q_HEADS = 8
N_KV_HEADS = 2
GQA_GROUP = 4
ATT_W = 512
KV_W = 128
WINDOW = 128
ATT_SCALE = HEAD_DIM ** -0.5
NUM_BUCKETS = 32
MAX_EXACT = 16
MAX_DISTANCE = 128
NEG_INF = -1e30
RWKV_HEAD = 64
RWKV_W = 512
RWKV_HEADS = 8
D_DECAY_LORA = 64
D_AAA_LORA = 64
D_GATE_LORA = 128
RWKV_IN = 3 * RWKV_W + D_DECAY_LORA + D_AAA_LORA + D_GATE_LORA
GN_EPS = 64e-5
IN_W = ATT_W + 2 * KV_W + RWKV_IN + 2 * D_MODEL
N_EXPERTS = 8
TOP_K = 2
NORM_EPS = 1e-6

VMEM_LIMIT_BYTES = 56 * 1024 * 1024
LANES = 128


def _params(*sem):
    return pltpu.CompilerParams(dimension_semantics=sem, vmem_limit_bytes=VMEM_LIMIT_BYTES)


def _const_spec(shape):
    zeros = (0,) * len(shape)
    return pl.BlockSpec(shape, lambda *_: zeros, pipeline_mode=pl.Buffered(1))


def _dot(a, b):
    return jnp.dot(a, b, preferred_element_type=F32)


def _dot_nt(a, b):
    return lax.dot_general(a, b, (((1,), (1,)), ((), ())), preferred_element_type=F32)


def _dot_tn(a, b):
    return lax.dot_general(a, b, (((0,), (0,)), ((), ())), preferred_element_type=F32)


def _split2(x):
    hi = x.astype(BF)
    lo = (x - hi.astype(F32)).astype(BF)
    return hi, lo


def _dot_exact_rhs(x, w):
    hi, lo = _split2(x)
    return _dot(hi, w) + _dot(lo, w)


def _rms(x, g):
    ms = jnp.mean(x * x, axis=-1, keepdims=True)
    return x * lax.rsqrt(ms + NORM_EPS) * g


def _sigmoid(x):
    return 1.0 / (1.0 + jnp.exp(-x))


def _inproj_kernel(x_ref, g_ref, w_ref, q_ref, k_ref, v_ref, z_ref, gate_ref):
    n = _rms(x_ref[...], g_ref[...]).astype(BF)
    q_ref[...] = _dot(n, w_ref[:, 0:ATT_W]).astype(BF)
    k_ref[...] = _dot(n, w_ref[:, ATT_W:ATT_W + KV_W])
    v_ref[...] = _dot(n, w_ref[:, ATT_W + KV_W:ATT_W + 2 * KV_W])
    z0 = ATT_W + 2 * KV_W
    z_ref[...] = _dot(n, w_ref[:, z0:z0 + RWKV_IN])
    gate_ref[...] = _dot(n, w_ref[:, z0 + RWKV_IN:IN_W])


def _inproj(x, g, w, tm):
    t = x.shape[0]
    row = lambda w_: pl.BlockSpec((tm, w_), lambda i: (i, 0))
    return pl.pallas_call(
        _inproj_kernel,
        grid=(t // tm,),
        in_specs=[row(D_MODEL), _const_spec((1, D_MODEL)), _const_spec((D_MODEL, IN_W))],
        out_specs=[row(ATT_W), row(KV_W), row(KV_W), row(RWKV_IN), row(2 * D_MODEL)],
        out_shape=[jax.ShapeDtypeStruct((t, ATT_W), BF),
                   jax.ShapeDtypeStruct((t, KV_W), F32),
                   jax.ShapeDtypeStruct((t, KV_W), F32),
                   jax.ShapeDtypeStruct((t, RWKV_IN), F32),
                   jax.ShapeDtypeStruct((t, 2 * D_MODEL), F32)],
        compiler_params=_params("parallel"),
        name="inproj",
    )(x, g, w)


def _t5_bucket(dist):
    n = np.maximum(dist, 0)
    nf = np.maximum(n, 1).astype(np.float32)
    large = MAX_EXACT + (np.log(nf / MAX_EXACT) / math.log(MAX_DISTANCE / MAX_EXACT)
                         * (NUM_BUCKETS - MAX_EXACT)).astype(np.int32)
    return np.where(n < MAX_EXACT, n, np.minimum(large, NUM_BUCKETS - 1)).astype(np.int32)


def _bias_lookup(rel_bias, bucket):
    hit = bucket[..., None, None] == np.arange(NUM_BUCKETS, dtype=np.int32)[:, None]
    return jnp.sum(jnp.where(hit, rel_bias, 0.0), axis=-2)


def _prompt_bucket_table():
    dist = np.arange(WINDOW)[:, None] - (np.arange(2 * WINDOW) - WINDOW)[None, :]
    return np.where((dist >= 0) & (dist <= WINDOW), _t5_bucket(dist), -1).astype(np.int32)


def _swa_prompt_kernel(nb, bucket_ref, relb_ref, q_ref, kp_ref, kc_ref, vp_ref, vc_ref, sink_ref, o_ref, bias_ref):
    first = pl.program_id(0) == 0

    @pl.when(first)
    def _():
        bucket = bucket_ref[...]
        for head in range(N_Q_HEADS):
            tab = jnp.full(bucket.shape, NEG_INF, F32)
            for n in range(NUM_BUCKETS):
                tab = jnp.where(bucket == n, relb_ref[n, head], tab)
            g = head % GQA_GROUP
            bias_ref[head // GQA_GROUP, g * WINDOW:(g + 1) * WINDOW, :] = tab

    col = lax.broadcasted_iota(jnp.int32, (GQA_GROUP * WINDOW, 2 * WINDOW), 1)
    pad_mask = jnp.logical_and(first, col < WINDOW)
    chains = [(b, h) for b in range(nb) for h in range(N_KV_HEADS)]
    hs = lambda h: slice(h * HEAD_DIM, (h + 1) * HEAD_DIM)
    kh = [jnp.concatenate([kp_ref[b][:, hs(h)], kc_ref[b][:, hs(h)]], axis=0).astype(BF) for b, h in chains]
    vh = [jnp.concatenate([vp_ref[b][:, hs(h)], vc_ref[b][:, hs(h)]], axis=0).astype(BF) for b, h in chains]
    qh = [jnp.concatenate([q_ref[b][:, hs(GQA_GROUP * h + g)] for g in range(GQA_GROUP)], axis=0)
          for b, h in chains]
    s = [_dot_nt(qh[n], kh[n]) * ATT_SCALE + bias_ref[h] for n, (b, h) in enumerate(chains)]
    s = [jnp.where(pad_mask, NEG_INF, x) for x in s]
    m = [jnp.maximum(jnp.max(x, axis=-1, keepdims=True), sink_ref[h]) for x, (b, h) in zip(s, chains)]
    p = [jnp.exp(x - mm) for x, mm in zip(s, m)]
    denom = [jnp.sum(x, axis=-1, keepdims=True) + jnp.exp(sink_ref[h] - mm)
             for x, mm, (b, h) in zip(p, m, chains)]
    o = [_dot(x.astype(BF), vv) / d for x, vv, d in zip(p, vh, denom)]
    for b in range(nb):
        pieces = [o[b * N_KV_HEADS + h][g * WINDOW:(g + 1) * WINDOW]
                  for h in range(N_KV_HEADS) for g in range(GQA_GROUP)]
        o_ref[b] = jnp.concatenate(pieces, axis=-1).astype(o_ref.dtype)


def _swa_prompt(q3, k3, v3, rel_bias, sink_col):
    batch, seq, _ = q3.shape
    cur = lambda i: (0, i, 0)
    prev = lambda i: (0, jnp.maximum(i - 1, 0), 0)
    kv_c = pl.BlockSpec((batch, WINDOW, KV_W), cur)
    kv_p = pl.BlockSpec((batch, WINDOW, KV_W), prev)
    bucket = jnp.asarray(_prompt_bucket_table())
    return pl.pallas_call(
        functools.partial(_swa_prompt_kernel, batch),
        grid=(seq // WINDOW,),
        in_specs=[_const_spec(bucket.shape), pl.BlockSpec(memory_space=pltpu.SMEM),
                  pl.BlockSpec((batch, WINDOW, ATT_W), cur), kv_p, kv_c, kv_p, kv_c,
                  _const_spec(sink_col.shape)],
        out_specs=pl.BlockSpec((batch, WINDOW, ATT_W), cur),
        out_shape=jax.ShapeDtypeStruct((batch, seq, ATT_W), BF),
        scratch_shapes=[pltpu.VMEM((N_KV_HEADS, GQA_GROUP * WINDOW, 2 * WINDOW), F32)],
        compiler_params=_params("arbitrary"),
        name="swa_prompt",
    )(bucket, rel_bias, q3, k3, k3, v3, v3, sink_col)


def _swa_sample_kernel(q_ref, kn_ref, vn_ref, ck_ref, cv_ref, bc_ref, bn_ref, sink_ref, o_ref):
    q = q_ref[...]
    qf = q.astype(F32)
    kn = kn_ref[...].astype(BF).astype(F32)
    vn = vn_ref[...].astype(BF).astype(F32)
    ck, cv = ck_ref[...], cv_ref[...]
    head = lax.broadcasted_iota(jnp.int32, (1, N_Q_HEADS, 1), 1)
    low = head < GQA_GROUP
    s_h, sn_h = [], []
    for h in range(N_KV_HEADS):
        sl = slice(h * HEAD_DIM, (h + 1) * HEAD_DIM)
        s_h.append(jnp.einsum("bgd,bkd->bgk", q, ck[:, :, sl].astype(BF), preferred_element_type=F32))
        sn_h.append(jnp.sum(qf * kn[:, None, sl], axis=-1, keepdims=True))
    s = jnp.where(low, s_h[0], s_h[1]) * ATT_SCALE + bc_ref[...]
    sn = jnp.where(low, sn_h[0], sn_h[1]) * ATT_SCALE + bn_ref[...]
    sink = sink_ref[...]
    m = jnp.maximum(jnp.maximum(jnp.max(s, axis=-1, keepdims=True), sn), sink)
    p = jnp.exp(s - m)
    pn = jnp.exp(sn - m)
    denom = jnp.sum(p, axis=-1, keepdims=True) + pn + jnp.exp(sink - m)
    pb = p.astype(BF)
    pnb = pn.astype(BF).astype(F32)
    o_h = []
    for h in range(N_KV_HEADS):
        sl = slice(h * HEAD_DIM, (h + 1) * HEAD_DIM)
        o = jnp.einsum("bgk,bkd->bgd", pb, cv[:, :, sl].astype(BF), preferred_element_type=F32)
        o_h.append(o + pnb * vn[:, None, sl])
    o_ref[...] = (jnp.where(low, o_h[0], o_h[1]) / denom).astype(o_ref.dtype)


def _swa_sample(q3, kn, vn, ck, cv, bias_c, bias_n, sink3, bb):
    b = q3.shape[0]
    return pl.pallas_call(
        _swa_sample_kernel,
        grid=(b // bb,),
        in_specs=[pl.BlockSpec((bb, N_Q_HEADS, HEAD_DIM), lambda i: (i, 0, 0)),
                  pl.BlockSpec((bb, KV_W), lambda i: (i, 0)),
                  pl.BlockSpec((bb, KV_W), lambda i: (i, 0)),
                  pl.BlockSpec((bb, WINDOW, KV_W), lambda i: (i, 0, 0)),
                  pl.BlockSpec((bb, WINDOW, KV_W), lambda i: (i, 0, 0)),
                  _const_spec(bias_c.shape), _const_spec(bias_n.shape), _const_spec(sink3.shape)],
        out_specs=pl.BlockSpec((bb, N_Q_HEADS, HEAD_DIM), lambda i: (i, 0, 0)),
        out_shape=jax.ShapeDtypeStruct((b, N_Q_HEADS, HEAD_DIM), BF),
        compiler_params=_params("parallel"),
        name="swa_sample",
    )(q3, kn, vn, ck, cv, bias_c, bias_n, sink3)


def _rwkv_prep_kernel(z_ref, zp8_ref, pz_ref, mu_ref, w0_ref, w2_ref, a0_ref, a2_ref, g2_ref,
                      kk_ref, ka_ref, ones_ref,
                      r_ref, lw_ref, k2_ref, v_ref, a_ref, b_ref, g_ref):
    z = z_ref[0]
    tm = z.shape[0]
    if pz_ref.shape[1] == tm:
        zs = pz_ref[0]
    else:
        prev_row = jnp.where(pl.program_id(1) == 0, pz_ref[0], zp8_ref[0, 7:8, :])
        row = lax.broadcasted_iota(jnp.int32, (tm, 1), 0)
        zs = jnp.where(row == 0, prev_row, pltpu.roll(z, 1, 0))
    zz = z + (zs - z) * mu_ref[...]
    r = zz[:, 0:RWKV_W]
    k = zz[:, RWKV_W:2 * RWKV_W]
    v = zz[:, 2 * RWKV_W:3 * RWKV_W]
    o = 3 * RWKV_W
    wl = zz[:, o:o + D_DECAY_LORA]
    al = zz[:, o + D_DECAY_LORA:o + D_DECAY_LORA + D_AAA_LORA]
    gl = zz[:, o + D_DECAY_LORA + D_AAA_LORA:RWKV_IN]
    wpre = -(w0_ref[...] + _dot(jnp.tanh(wl).astype(BF), w2_ref[...]))
    softplus = jnp.maximum(wpre, 0.0) + jnp.log1p(jnp.exp(-jnp.abs(wpre)))
    lw = -jnp.exp(-softplus - 0.5)
    a = _sigmoid(a0_ref[...] + _dot(al.astype(BF), a2_ref[...]))
    g = _dot(_sigmoid(gl).astype(BF), g2_ref[...])
    kkv = k * kk_ref[...]
    ss = _dot_exact_rhs(kkv * kkv, ones_ref[...])
    kk = kkv / jnp.maximum(jnp.sqrt(ss), 1e-12)
    k2 = k * (1.0 + (a - 1.0) * ka_ref[...])
    r_ref[0], lw_ref[0], k2_ref[0], v_ref[0] = r, lw, k2, v
    a_ref[0], b_ref[0], g_ref[0] = -kk, kk * a, g


def _rwkv_prep(z3, prev_z, p, tm):
    b, t, _ = z3.shape
    blk8 = tm // 8
    vec = lambda w_: _const_spec((1, w_))
    out_spec = pl.BlockSpec((1, tm, RWKV_W), lambda bi, i: (bi, i, 0))
    return pl.pallas_call(
        _rwkv_prep_kernel,
        grid=(b, t // tm),
        in_specs=[pl.BlockSpec((1, tm, RWKV_IN), lambda bi, i: (bi, i, 0)),
                  pl.BlockSpec((1, 8, RWKV_IN), lambda bi, i: (bi, jnp.maximum(i * blk8 - 1, 0), 0)),
                  pl.BlockSpec((1, prev_z.shape[1], RWKV_IN), lambda bi, i: (bi, 0, 0)),
                  vec(RWKV_IN), vec(RWKV_W), _const_spec((D_DECAY_LORA, RWKV_W)),
                  vec(RWKV_W), _const_spec((D_AAA_LORA, RWKV_W)), _const_spec((D_GATE_LORA, RWKV_W)),
                  vec(RWKV_W), vec(RWKV_W), _const_spec((RWKV_W, RWKV_W))],
        out_specs=[out_spec] * 7,
        out_shape=[jax.ShapeDtypeStruct((b, t, RWKV_W), F32)] * 7,
        compiler_params=_params("parallel", "parallel"),
        name="rwkv_prep",
    )(z3, z3, prev_z, p["mu"], p["w0"], p["w2"], p["a0"], p["a2"], p["g2"], p["k_k"], p["k_a"], p["ones_bd"])


def _rwkv_scan_kernel(nb, chunk, r_ref, lw_ref, k_ref, v_ref, a_ref, b_ref, g_ref, s0_ref,
                      rk_ref, lg_ref, lb_ref, ones_ref, y_ref, sout_ref, s_scr):
    c = pl.program_id(1)

    @pl.when(c == 0)
    def _():
        s_scr[...] = s0_ref[...]

    ri = lax.broadcasted_iota(jnp.int32, (chunk, chunk), 0)
    ci = lax.broadcasted_iota(jnp.int32, (chunk, chunk), 1)
    incl = ri >= ci
    strict = ri > ci
    tri = incl.astype(BF)
    eye = (ri == ci).astype(F32)
    n_sq = int(math.log2(chunk)) - 1

    chains = [(bi, h) for bi in range(nb) for h in range(RWKV_HEADS)]
    hs = lambda h: slice(h * RWKV_HEAD, (h + 1) * RWKV_HEAD)

    ra, aa, bt, kt, bh, kh, vb, g_all = [], [], [], [], [], [], [], []
    for bi in range(nb):
        r, lw, k, v = r_ref[bi], lw_ref[bi], k_ref[bi], v_ref[bi]
        a, b = a_ref[bi], b_ref[bi]
        hi, lo = _split2(lw)
        lo2 = (lw - hi.astype(F32) - lo.astype(F32)).astype(BF)
        cum = _dot(tri, hi) + _dot(tri, lo) + _dot(tri, lo2)
        tail = cum[chunk - 1:chunk, :]
        g_inv = jnp.exp(-cum)
        g_tail = jnp.exp(tail - cum)
        g_all.append(jnp.exp(tail))
        ra.append((r * jnp.exp(cum)).astype(BF))
        aa.append((a * jnp.exp(cum - lw)).astype(BF))
        bt.append((b * g_inv).astype(BF))
        kt.append((k * g_inv).astype(BF))
        bh.append((b * g_tail).astype(BF))
        kh.append((k * g_tail).astype(BF))
        vb.append(v.astype(BF))

    s_old = [s_scr[bi, h] for bi, h in chains]
    ar = [jnp.concatenate([aa[bi][:, hs(h)], ra[bi][:, hs(h)]], axis=0) for bi, h in chains]
    v_h = [vb[bi][:, hs(h)] for bi, h in chains]
    gb = [_dot_nt(ar[n], bt[bi][:, hs(h)]) for n, (bi, h) in enumerate(chains)]
    gk = [_dot_nt(ar[n], kt[bi][:, hs(h)]) for n, (bi, h) in enumerate(chains)]
    p = [_dot_nt(ar[n], s_old[n].astype(BF)) for n in range(len(chains))]
    l_ab = [jnp.where(strict, x[:chunk], 0.0) for x in gb]
    l_ak = [jnp.where(strict, x[:chunk], 0.0).astype(BF) for x in gk]
    m_rb = [jnp.where(incl, x[chunk:], 0.0).astype(BF) for x in gb]
    m_rk = [jnp.where(incl, x[chunk:], 0.0).astype(BF) for x in gk]
    rhs = [p[n][:chunk] + _dot(l_ak[n], v_h[n]) for n in range(len(chains))]
    t_inv = [eye + x for x in l_ab]
    lp = l_ab
    for _ in range(n_sq):
        lpb = [x.astype(BF) for x in lp]
        lp = [_dot(x, x) for x in lpb]
        t_inv = [t + _dot(x.astype(BF), t.astype(BF)) for x, t in zip(lp, t_inv)]
    ub = [_dot(t.astype(BF), x.astype(BF)).astype(BF) for t, x in zip(t_inv, rhs)]
    y_h = [p[n][chunk:] + _dot(m_rb[n], ub[n]) + _dot(m_rk[n], v_h[n]) for n in range(len(chains))]
    s_new = [s_old[n] * g_all[bi][:, hs(h)] + _dot_tn(ub[n], bh[bi][:, hs(h)]) + _dot_tn(v_h[n], kh[bi][:, hs(h)])
             for n, (bi, h) in enumerate(chains)]
    for n, (bi, h) in enumerate(chains):
        s_scr[bi, h] = s_new[n]

    for bi in range(nb):
        y = jnp.concatenate(y_h[bi * RWKV_HEADS:(bi + 1) * RWKV_HEADS], axis=-1)
        y_ref[bi] = _rwkv_epilogue(y, r_ref[bi], k_ref[bi], v_ref[bi], g_ref[bi],
                                   rk_ref, lg_ref, lb_ref, ones_ref).astype(y_ref.dtype)

    @pl.when(c == pl.num_programs(1) - 1)
    def _():
        sout_ref[...] = s_scr[...]


def _rwkv_epilogue(y, r, k, v, g, rk_ref, lg_ref, lb_ref, ones_ref):
    ones = ones_ref[...]
    inv_n = 1.0 / RWKV_HEAD
    mean = _dot_exact_rhs(y, ones) * inv_n
    d = y - mean
    var = _dot_exact_rhs(d * d, ones) * inv_n
    yn = d * lax.rsqrt(var + GN_EPS) * lg_ref[...] + lb_ref[...]
    bonus = _dot_exact_rhs(r * k * rk_ref[...], ones) * v
    return (yn + bonus) * g


def _rwkv_scan(coef, s0, p, nb, chunk):
    r = coef[0]
    b, t, _ = r.shape
    seq_spec = pl.BlockSpec((nb, chunk, RWKV_W), lambda bi, c: (bi, c, 0))
    st_spec = pl.BlockSpec((nb, RWKV_HEADS, RWKV_HEAD, RWKV_HEAD), lambda bi, c: (bi, 0, 0, 0))
    vec = _const_spec((1, RWKV_W))
    return pl.pallas_call(
        functools.partial(_rwkv_scan_kernel, nb, chunk),
        grid=(b // nb, t // chunk),
        in_specs=[seq_spec] * 7 + [st_spec, vec, vec, vec, _const_spec((RWKV_W, RWKV_W))],
        out_specs=[seq_spec, st_spec],
        out_shape=[jax.ShapeDtypeStruct((b, t, RWKV_W), BF),
                   jax.ShapeDtypeStruct((b, RWKV_HEADS, RWKV_HEAD, RWKV_HEAD), F32)],
        scratch_shapes=[pltpu.VMEM((nb, RWKV_HEADS, RWKV_HEAD, RWKV_HEAD), F32)],
        compiler_params=_params("parallel", "arbitrary"),
        name="rwkv_scan",
    )(*coef, s0, p["r_k"], p["lnx_g"], p["lnx_b"], p["ones_bd"])


def _rwkv_step_kernel(r_ref, lw_ref, k_ref, v_ref, a_ref, b_ref, g_ref, s_ref,
                      rk_ref, lg_ref, lb_ref, ones_ref, y_ref, sout_ref, y_scr):
    n_pair = LANES // RWKV_HEAD
    r, k, v = r_ref[...], k_ref[...], v_ref[...]
    rT, kT, vT = r.T, k.T, v.T
    aT, bT, wT = a_ref[...].T, b_ref[...].T, jnp.exp(lw_ref[...]).T
    for hl in range(LANES // RWKV_HEAD):
        hsl = slice(hl * RWKV_HEAD, (hl + 1) * RWKV_HEAD)
        a_h, b_h, k_h, w_h, r_h = aT[hsl], bT[hsl], kT[hsl], wT[hsl], rT[hsl]
        tiles = [(hl * RWKV_HEAD + n_pair * t) * RWKV_HEAD for t in range(RWKV_HEAD // n_pair)]
        st = [s_ref[:, c0:c0 + LANES].T for c0 in tiles]
        new = []
        for t, x in enumerate(st):
            halves = []
            for il in range(n_pair):
                i = hl * RWKV_HEAD + n_pair * t + il
                slab = x[il * RWKV_HEAD:(il + 1) * RWKV_HEAD]
                sa = jnp.sum(slab * a_h, axis=0, keepdims=True)
                slab = slab * w_h + sa * b_h + vT[i:i + 1] * k_h
                y_scr[i:i + 1, :] = jnp.sum(slab * r_h, axis=0, keepdims=True)
                halves.append(slab)
            new.append(jnp.concatenate(halves, axis=0))
        for c0, x in zip(tiles, new):
            sout_ref[:, c0:c0 + LANES] = x.T
    y_ref[...] = _rwkv_epilogue(y_scr[...].T, r, k, v, g_ref[...],
                                rk_ref, lg_ref, lb_ref, ones_ref).astype(y_ref.dtype)


def _rwkv_step(coef, s0, p):
    b = s0.shape[0]
    per_pair = (LANES // RWKV_HEAD) * RWKV_HEAD * RWKV_HEAD
    n_steps = RWKV_W // LANES
    col = pl.BlockSpec((b, LANES), lambda i: (0, i))
    vec = pl.BlockSpec((1, LANES), lambda i: (0, i))
    st_spec = pl.BlockSpec((b, per_pair), lambda i: (0, i))
    y, s_new = pl.pallas_call(
        _rwkv_step_kernel,
        grid=(n_steps,),
        in_specs=[col] * 7 + [st_spec, vec, vec, vec, _const_spec((LANES, LANES))],
        out_specs=[col, st_spec],
        out_shape=[jax.ShapeDtypeStruct((b, RWKV_W), BF),
                   jax.ShapeDtypeStruct((b, n_steps * per_pair), F32)],
        scratch_shapes=[pltpu.VMEM((LANES, b), F32)],
        compiler_params=_params("parallel"),
        name="rwkv_step",
    )(*coef, s0.reshape(b, n_steps * per_pair), p["r_k"], p["lnx_g"], p["lnx_b"],
      p["ones_bd"][:LANES, :LANES])
    return y, s_new.reshape(s0.shape)


def _merge_kernel(with_router, x_ref, ya_ref, yb_ref, gate_ref, wa_ref, wb_ref, wo_ref, g_ref, *rest):
    if with_router:
        wr_ref, cnt_in_ref, x1_ref, h_ref, route_ref, cnt_out_ref, cnt_scr = rest
    else:
        x1_ref, h_ref = rest
    gate = gate_ref[...]
    pa = _dot(ya_ref[...], wa_ref[...])
    pb = _dot(yb_ref[...], wb_ref[...])
    merged = _sigmoid(gate[:, :D_MODEL]) * pa + _sigmoid(gate[:, D_MODEL:]) * pb
    x1 = x_ref[...] + _dot(merged.astype(BF), wo_ref[...])
    x1_ref[...] = x1
    h = _rms(x1, g_ref[...])
    if not with_router:
        h_ref[...] = h.astype(h_ref.dtype)
        return
    for s in range(D_MODEL // LANES):
        h_ref[:, s, :] = h[:, s * LANES:(s + 1) * LANES]

    @pl.when(pl.program_id(0) == 0)
    def _():
        cnt_scr[...] = cnt_in_ref[...]

    hh, hl = _split2(h)
    wh, wl = wr_ref[0], wr_ref[1]
    logits = _dot(hh, wh) + _dot(hl, wh) + _dot(hh, wl)
    tm = logits.shape[0]
    lane = lax.broadcasted_iota(jnp.int32, logits.shape, 1).astype(F32)
    logits = jnp.where(lane < N_EXPERTS, logits, -jnp.inf)
    m1 = jnp.max(logits, axis=-1, keepdims=True)
    i1 = jnp.min(jnp.where(logits == m1, lane, float(LANES)), axis=-1, keepdims=True)
    rest_l = jnp.where(lane == i1, -jnp.inf, logits)
    m2 = jnp.max(rest_l, axis=-1, keepdims=True)
    i2 = jnp.min(jnp.where(rest_l == m2, lane, float(LANES)), axis=-1, keepdims=True)
    e2 = jnp.exp(m2 - m1)
    g1 = 1.0 / (1.0 + e2)
    g2 = e2 / (1.0 + e2)
    oh1 = (lane == i1).astype(F32)
    oh2 = (lane == i2).astype(F32)
    both = oh1 + oh2
    ri = lax.broadcasted_iota(jnp.int32, (tm, tm), 0)
    ci = lax.broadcasted_iota(jnp.int32, (tm, tm), 1)
    before = _dot((ri > ci).astype(BF), both.astype(BF)) + cnt_scr[...]
    rank1 = jnp.sum(oh1 * before, axis=-1, keepdims=True)
    rank2 = jnp.sum(oh2 * before, axis=-1, keepdims=True)
    cnt_scr[...] += jnp.sum(both, axis=0, keepdims=True)
    cnt_out_ref[...] = cnt_scr[...]
    route = jnp.zeros_like(logits)
    for n, val in enumerate((i1, i2, g1, g2, rank1, rank2)):
        route = jnp.where(lane == n, val, route)
    route_ref[...] = route


def _merge_out(x, ya, yb, gate, wa, wb, wo, g, tm, router=None, cnt_in=None):
    t = x.shape[0]
    row = lambda w_: pl.BlockSpec((tm, w_), lambda i: (i, 0))
    in_specs = [row(D_MODEL), row(ATT_W), row(RWKV_W), row(2 * D_MODEL),
                _const_spec((ATT_W, D_MODEL)), _const_spec((RWKV_W, D_MODEL)),
                _const_spec((D_MODEL, D_MODEL)), _const_spec((1, D_MODEL))]
    args = [x, ya, yb, gate, wa, wb, wo, g]
    if router is None:
        out_specs = [row(D_MODEL), row(D_MODEL)]
        out_shape = [jax.ShapeDtypeStruct((t, D_MODEL), F32), jax.ShapeDtypeStruct((t, D_MODEL), BF)]
        scratch = []
    else:
        n_sub = D_MODEL // LANES
        in_specs += [_const_spec(router.shape), _const_spec((1, LANES))]
        args += [router, cnt_in]
        out_specs = [row(D_MODEL), pl.BlockSpec((tm, n_sub, LANES), lambda i: (i, 0, 0)), row(LANES),
                     pl.BlockSpec((1, LANES), lambda i: (0, 0))]
        out_shape = [jax.ShapeDtypeStruct((t, D_MODEL), F32), jax.ShapeDtypeStruct((t, n_sub, LANES), F32),
                     jax.ShapeDtypeStruct((t, LANES), F32), jax.ShapeDtypeStruct((1, LANES), F32)]
        scratch = [pltpu.VMEM((1, LANES), F32)]
    return pl.pallas_call(
        functools.partial(_merge_kernel, router is not None),
        grid=(t // tm,),
        in_specs=in_specs, out_specs=out_specs, out_shape=out_shape, scratch_shapes=scratch,
        compiler_params=_params("parallel" if router is None else "arbitrary"),
        name="merge_out",
    )(*args)


def _swiglu(x, wg, wu, wd):
    s = _dot(x, wg)
    u = _dot(x, wu)
    act = (s * _sigmoid(s) * u).astype(BF)
    return _dot(act, wd)


def _dense_ffn_kernel(x1_ref, h_ref, wg_ref, wu_ref, wd_ref, o_ref):
    o_ref[...] = x1_ref[...] + _swiglu(h_ref[...], wg_ref[...], wu_ref[...], wd_ref[...])


def _dense_ffn(x1, h, wg, wu, wd, tm):
    t = x1.shape[0]
    d_ff = wg.shape[1]
    row = lambda: pl.BlockSpec((tm, D_MODEL), lambda i: (i, 0))
    return pl.pallas_call(
        _dense_ffn_kernel,
        grid=(t // tm,),
        in_specs=[row(), row(), _const_spec((D_MODEL, d_ff)), _const_spec((D_MODEL, d_ff)),
                  _const_spec((d_ff, D_MODEL))],
        out_specs=row(),
        out_shape=jax.ShapeDtypeStruct((t, D_MODEL), F32),
        compiler_params=_params("parallel"),
        name="dense_ffn",
    )(x1, h, wg, wu, wd)


def _rows_as_matrix(ref3):
    return jnp.concatenate([ref3[:, s, :] for s in range(ref3.shape[1])], axis=-1)


def _dispatch_kernel(tt, first, dest_ref, fill_ref, h_ref, *rest):
    if first:
        xs_hbm, zbuf, sem, zsem = rest
    else:
        _, xs_hbm, sem = rest
    i = pl.program_id(0)
    base = i * tt

    if first:
        @pl.when(i == 0)
        def _():
            zbuf[...] = jnp.zeros_like(zbuf)
            bm = zbuf.shape[0]
            fills = [pltpu.make_async_copy(zbuf, xs_hbm.at[pl.ds(fill_ref[n] * bm, bm)], zsem)
                     for n in range(fill_ref.shape[0])]
            for f in fills:
                f.start()
            for f in fills:
                f.wait()

    def issue(r, carry):
        for j in range(TOP_K):
            pltpu.make_async_copy(h_ref.at[r], xs_hbm.at[dest_ref[TOP_K * (base + r) + j]], sem).start()
        return carry

    lax.fori_loop(0, tt, issue, 0, unroll=8)
    for j in range(TOP_K):
        pltpu.make_async_copy(h_ref, xs_hbm.at[pl.ds(0, tt)], sem).wait()


def _dispatch(dest, fill_blocks, h3, xs, tt, bm, n_rows):
    t, n_sub, _ = h3.shape
    first = xs is None
    in_specs = [pl.BlockSpec((tt, n_sub, LANES), lambda i, ds, fb: (i, 0, 0))]
    args = [dest, fill_blocks, h3]
    if first:
        scratch = [pltpu.VMEM((bm, n_sub, LANES), F32), pltpu.SemaphoreType.DMA(()), pltpu.SemaphoreType.DMA(())]
        aliases = {}
    else:
        in_specs.append(pl.BlockSpec(memory_space=pl.ANY))
        args.append(xs)
        scratch = [pltpu.SemaphoreType.DMA(())]
        aliases = {3: 0}
    grid_spec = pltpu.PrefetchScalarGridSpec(
        num_scalar_prefetch=2, grid=(t // tt,), in_specs=in_specs,
        out_specs=pl.BlockSpec(memory_space=pl.ANY), scratch_shapes=scratch)
    return pl.pallas_call(
        functools.partial(_dispatch_kernel, tt, first),
        grid_spec=grid_spec,
        out_shape=jax.ShapeDtypeStruct((n_rows, n_sub, LANES), F32),
        input_output_aliases=aliases,
        compiler_params=_params("arbitrary"),
        name="moe_dispatch",
    )(*args)


def _moe_kernel(n_half, blk_e_ref, n_used_ref, xs_ref, wg_ref, wu_ref, wd_ref, y_ref):
    @pl.when(pl.program_id(0) < n_used_ref[0])
    def _():
        x = _rows_as_matrix(xs_ref).astype(BF)
        d_e = wg_ref.shape[2]
        step = d_e // n_half
        acc = None
        for j in range(n_half):
            cs = slice(j * step, (j + 1) * step)
            part = _swiglu(x, wg_ref[0, :, cs], wu_ref[0, :, cs], wd_ref[0, cs, :])
            acc = part if acc is None else acc + part
        for s in range(y_ref.shape[1]):
            y_ref[:, s, :] = acc[:, s * LANES:(s + 1) * LANES]

    @pl.when(pl.program_id(0) >= n_used_ref[0])
    def _():
        y_ref[...] = jnp.zeros_like(y_ref)


def _moe_ffn(xs, blk_e, n_used, wg, wu, wd, bm):
    n_blk = blk_e.shape[0]
    n_rows, n_sub, _ = xs.shape
    d_e = wg.shape[2]
    wspec = lambda shape: pl.BlockSpec(shape, lambda i, be, nu: (be[i], 0, 0), pipeline_mode=pl.Buffered(1))
    rows = pl.BlockSpec((bm, n_sub, LANES), lambda i, be, nu: (i, 0, 0))
    grid_spec = pltpu.PrefetchScalarGridSpec(
        num_scalar_prefetch=2,
        grid=(n_blk,),
        in_specs=[rows, wspec((1, D_MODEL, d_e)), wspec((1, D_MODEL, d_e)), wspec((1, d_e, D_MODEL))],
        out_specs=rows,
    )
    return pl.pallas_call(
        functools.partial(_moe_kernel, 2),
        grid_spec=grid_spec,
        out_shape=jax.ShapeDtypeStruct((n_rows, n_sub, LANES), F32),
        compiler_params=_params("arbitrary"),
        name="moe_ffn",
    )(blk_e, n_used, xs, wg, wu, wd)


def _row_layout(counts, bm, n_blk):
    per_e = (counts + bm - 1) // bm
    ends = jnp.cumsum(per_e)
    n_used = ends[-1]
    first_row = (ends - per_e) * bm
    i = jnp.arange(n_blk, dtype=jnp.int32)
    blk_e = jnp.sum((jnp.minimum(i, jnp.maximum(n_used - 1, 0))[:, None] >= ends[None, :]).astype(jnp.int32), axis=1)
    need = jnp.any(i[:, None] == (ends - 1)[None, :], axis=1) | (i >= n_blk - N_EXPERTS)
    fill_blocks = jnp.argsort(jnp.logical_not(need), stable=True)[:2 * N_EXPERTS].astype(jnp.int32)
    return first_row.astype(jnp.int32), blk_e.astype(jnp.int32), n_used.astype(jnp.int32).reshape(1), fill_blocks


def _combine_kernel(tc, dest_ref, x1_ref, route_ref, y_hbm, g_ref, o_ref, ybuf, sem):
    base = pl.program_id(0) * tc

    def issue(r, carry):
        for j in range(TOP_K):
            pltpu.make_async_copy(y_hbm.at[dest_ref[TOP_K * (base + r) + j]], ybuf.at[j, r], sem).start()
        return carry

    lax.fori_loop(0, tc, issue, 0, unroll=4)
    for j in range(TOP_K):
        pltpu.make_async_copy(y_hbm.at[pl.ds(0, tc)], ybuf.at[j], sem).wait()
    route = route_ref[...]
    x2 = (x1_ref[...] + route[:, 2:3] * _rows_as_matrix(ybuf.at[0])
          + route[:, 3:4] * _rows_as_matrix(ybuf.at[1]))
    o_ref[...] = _rms(x2, g_ref[...])


def _moe_combine(dest, x1, route, y, g, tc):
    t = x1.shape[0]
    n_sub = y.shape[1]
    grid_spec = pltpu.PrefetchScalarGridSpec(
        num_scalar_prefetch=1,
        grid=(t // tc,),
        in_specs=[pl.BlockSpec((tc, D_MODEL), lambda i, ds: (i, 0)),
                  pl.BlockSpec((tc, LANES), lambda i, ds: (i, 0)),
                  pl.BlockSpec(memory_space=pl.ANY),
                  pl.BlockSpec((1, D_MODEL), lambda i, ds: (0, 0))],
        out_specs=pl.BlockSpec((tc, D_MODEL), lambda i, ds: (i, 0)),
        scratch_shapes=[pltpu.VMEM((TOP_K, tc, n_sub, LANES), F32), pltpu.SemaphoreType.DMA(())],
    )
    return pl.pallas_call(
        functools.partial(_combine_kernel, tc),
        grid_spec=grid_spec,
        out_shape=jax.ShapeDtypeStruct((t, D_MODEL), F32),
        compiler_params=_params("arbitrary"),
        name="moe_combine",
    )(dest, x1, route, y, g)


def _layer_params(l, w_in, mu_shift, w0, w2, a0, a2, g2, k_k, k_a, r_k, lnx_g, lnx_b,
                  w_proj_attn, w_proj_rwkv, w_out, ones_bd):
    row = lambda u: u.reshape(1, -1)
    return dict(
        w_in=w_in[l].astype(BF), mu=row(mu_shift[l]), w0=row(w0[l]), w2=w2[l].astype(BF),
        a0=row(a0[l]), a2=a2[l].astype(BF), g2=g2[l].astype(BF), k_k=row(k_k[l]), k_a=row(k_a[l]),
        r_k=row(r_k[l]), lnx_g=row(lnx_g[l]), lnx_b=row(lnx_b[l]),
        wa=w_proj_attn[l].astype(BF), wb=w_proj_rwkv[l].astype(BF), wo=w_out[l].astype(BF),
        ones_bd=ones_bd)


def kernel(x_prompt, x_sample, cache_k, cache_v, state_wkv, state_shift, norm_mix_g, w_in, attn_sinks, rel_bias, mu_shift, w0, w2, a0, a2, g2, k_k, k_a, r_k, lnx_g, lnx_b, w_proj_attn, w_proj_rwkv, w_out, norm_ffn_g, dense_w_gate, dense_w_up, dense_w_down, router_w, moe_w_gate, moe_w_up, moe_w_down, norm_final_g):
    batch, seq, _ = x_prompt.shape
    dec_batch, dec_seq, _ = x_sample.shape
    assert dec_seq == 1 and seq % WINDOW == 0

    head_id = jnp.arange(RWKV_W, dtype=jnp.int32) // RWKV_HEAD
    ones_bd = (head_id[:, None] == head_id[None, :]).astype(BF)
    bias_by_dist = _bias_lookup(rel_bias, _t5_bucket(np.arange(WINDOW + 1)))
    bias_c = jnp.transpose(bias_by_dist[WINDOW:0:-1], (1, 0))[None]
    bias_n = bias_by_dist[0][None, :, None]

    layers = [_layer_params(l, w_in, mu_shift, w0, w2, a0, a2, g2, k_k, k_a, r_k, lnx_g, lnx_b,
                            w_proj_attn, w_proj_rwkv, w_out, ones_bd) for l in range(DEPTH)]

    def router_pieces(l):
        wr = jnp.zeros((D_MODEL, LANES), F32).at[:, :N_EXPERTS].set(router_w[l // 2])
        hi = wr.astype(BF)
        lo = (wr - hi.astype(F32)).astype(BF)
        return jnp.stack([hi, lo])

    n_prompt = batch * seq
    tm_p = 512
    bm = 512
    n_all = n_prompt + dec_batch
    n_blk = -(-(n_all * TOP_K) // bm) + N_EXPERTS

    def mix(x, is_prompt, l, state):
        p = layers[l]
        n_tok = x.shape[0]
        tm = tm_p if is_prompt else n_tok
        q, k, v, z, gate = _inproj(x, norm_mix_g[l].reshape(1, -1), p["w_in"], tm)
        sinks = attn_sinks[l]
        if is_prompt:
            b, t = batch, seq
            sink_col = jnp.repeat(sinks, WINDOW).reshape(N_KV_HEADS, GQA_GROUP * WINDOW, 1)
            ya = _swa_prompt(q.reshape(b, t, ATT_W), k.reshape(b, t, KV_W), v.reshape(b, t, KV_W),
                             rel_bias, sink_col).reshape(n_tok, ATT_W)
            k4 = k.reshape(b, t, N_KV_HEADS, HEAD_DIM)
            v4 = v.reshape(b, t, N_KV_HEADS, HEAD_DIM)
            state["k"].append(k4[:, t - WINDOW:])
            state["v"].append(v4[:, t - WINDOW:])
            z3 = z.reshape(b, t, RWKV_IN)
            prev_z = jnp.zeros((b, 1, RWKV_IN), F32)
            s0 = jnp.zeros((b, RWKV_HEADS, RWKV_HEAD, RWKV_HEAD), F32)
            coef = _rwkv_prep(z3, prev_z, p, 256)
            yb3, s_new = _rwkv_scan(coef, s0, p, b, 64)
            yb = yb3.reshape(n_tok, RWKV_W)
            state["z"].append(z3[:, -1])
        else:
            b = dec_batch
            ck = cache_k[l].reshape(b, WINDOW, KV_W)
            cv = cache_v[l].reshape(b, WINDOW, KV_W)
            ya3 = _swa_sample(q.reshape(b, N_Q_HEADS, HEAD_DIM), k, v, ck, cv, bias_c, bias_n,
                              sinks.reshape(1, N_Q_HEADS, 1), 16)
            ya = ya3.reshape(b, ATT_W)
            state["k"].append(jnp.concatenate([ck[:, 1:], k[:, None]], axis=1)
                              .reshape(b, WINDOW, N_KV_HEADS, HEAD_DIM))
            state["v"].append(jnp.concatenate([cv[:, 1:], v[:, None]], axis=1)
                              .reshape(b, WINDOW, N_KV_HEADS, HEAD_DIM))
            coef = _rwkv_prep(z.reshape(1, b, RWKV_IN), state_shift[l].reshape(1, b, RWKV_IN), p, b)
            yb, s_new = _rwkv_step([u.reshape(b, RWKV_W) for u in coef], state_wkv[l], p)
            state["z"].append(z)
        state["s"].append(s_new)
        return ya, yb, gate

    groups = [dict(x=x_prompt.reshape(n_prompt, D_MODEL), prompt=True, tm=tm_p, k=[], v=[], s=[], z=[]),
              dict(x=x_sample.reshape(dec_batch, D_MODEL), prompt=False, tm=dec_batch, k=[], v=[], s=[], z=[])]
    for l in range(DEPTH):
        p = layers[l]
        gn = norm_ffn_g[l].reshape(1, -1)
        j = l // 2
        if l % 2 == 0:
            wg, wu, wd = (w[j].astype(BF) for w in (dense_w_gate, dense_w_up, dense_w_down))
            for grp in groups:
                ya, yb, gate = mix(grp["x"], grp["prompt"], l, grp)
                x1, h = _merge_out(grp["x"], ya, yb, gate, p["wa"], p["wb"], p["wo"], gn, grp["tm"])
                grp["x"] = _dense_ffn(x1, h, wg, wu, wd, grp["tm"])
        else:
            assert l == DEPTH - 1
            wg, wu, wd = (w[j].astype(BF) for w in (moe_w_gate, moe_w_up, moe_w_down))
            router = router_pieces(l)
            cnt = jnp.zeros((1, LANES), F32)
            xs = None
            for grp in groups:
                ya, yb, gate = mix(grp["x"], grp["prompt"], l, grp)
                x1, h3, route, cnt = _merge_out(grp["x"], ya, yb, gate, p["wa"], p["wb"], p["wo"], gn,
                                                min(256, grp["tm"]), router=router, cnt_in=cnt)
                grp["x1"], grp["route"], grp["h3"] = x1, route, h3
            counts = cnt[0, :N_EXPERTS].astype(jnp.int32)
            first_row, blk_e, n_used, fill_blocks = _row_layout(counts, bm, n_blk)
            for grp in groups:
                route = grp["route"]
                expert = route[:, 0:TOP_K].astype(jnp.int32)
                rank = route[:, 2 * TOP_K:3 * TOP_K].astype(jnp.int32)
                start = jnp.sum(jnp.where(expert[..., None] == jnp.arange(N_EXPERTS, dtype=jnp.int32),
                                          first_row, 0), axis=-1)
                grp["dest"] = (start + rank).reshape(-1)
                xs = _dispatch(grp["dest"], fill_blocks, grp["h3"], xs, min(2048, grp["h3"].shape[0]), bm,
                               n_blk * bm)
            y = _moe_ffn(xs, blk_e, n_used, wg, wu, wd, bm)
            for grp in groups:
                grp["x"] = _moe_combine(grp["dest"], grp["x1"], grp["route"], y,
                                        norm_final_g.reshape(1, -1), min(1024, grp["x1"].shape[0]))

    gp, gs = groups
    outs = []
    for grp, shape in ((gp, (batch, seq, D_MODEL)), (gs, (dec_batch, dec_seq, D_MODEL))):
        outs.append((grp["x"].reshape(shape), jnp.stack(grp["k"]), jnp.stack(grp["v"]),
                     jnp.stack(grp["s"]), jnp.stack(grp["z"])))
    (y_p, nk_p, nv_p, ns_p, nz_p), (y_s, nk_s, nv_s, ns_s, nz_s) = outs
    return (y_p, y_s, nk_p, nv_p, ns_p, nz_p, nk_s, nv_s, ns_s, nz_s)
```

```python
import functools
import math

import jax
import jax.numpy as jnp
import numpy as np
from jax import lax
from jax.experimental import pallas as pl
from jax.experimental.pallas import tpu as pltpu

BF = jnp.bfloat16
F32 = jnp.float32

D_MODEL = 1024
DEPTH = 2
HEAD_DIM = 64
N_Q_HEADS = 8
N_KV_HEADS = 2
GQA_GROUP = 4
ATT_W = 512
KV_W = 128
WINDOW = 128
ATT_SCALE = HEAD_DIM ** -0.5
NUM_BUCKETS = 32
MAX_EXACT = 16
MAX_DISTANCE = 128
NEG_INF = -1e30
RWKV_HEAD = 64
RWKV_W = 512
RWKV_HEADS = 8
D_DECAY_LORA = 64
D_AAA_LORA = 64
D_GATE_LORA = 128
RWKV_IN = 3 * RWKV_W + D_DECAY_LORA + D_AAA_LORA + D_GATE_LORA
GN_EPS = 64e-5
IN_W = ATT_W + 2 * KV_W + RWKV_IN + 2 * D_MODEL
N_EXPERTS = 8
TOP_K = 2
NORM_EPS = 1e-6

VMEM_LIMIT_BYTES = 56 * 1024 * 1024
LANES = 128


def _params(*sem):
    return pltpu.CompilerParams(dimension_semantics=sem, vmem_limit_bytes=VMEM_LIMIT_BYTES)


def _const_spec(shape):
    zeros = (0,) * len(shape)
    return pl.BlockSpec(shape, lambda *_: zeros, pipeline_mode=pl.Buffered(1))


def _dot(a, b):
    return jnp.dot(a, b, preferred_element_type=F32)


def _dot_nt(a, b):
    return lax.dot_general(a, b, (((1,), (1,)), ((), ())), preferred_element_type=F32)


def _dot_tn(a, b):
    return lax.dot_general(a, b, (((0,), (0,)), ((), ())), preferred_element_type=F32)


def _split2(x):
    hi = x.astype(BF)
    lo = (x - hi.astype(F32)).astype(BF)
    return hi, lo


def _dot_exact_rhs(x, w):
    hi, lo = _split2(x)
    return _dot(hi, w) + _dot(lo, w)


def _rms(x, g):
    ms = jnp.mean(x * x, axis=-1, keepdims=True)
    return x * lax.rsqrt(ms + NORM_EPS) * g


def _sigmoid(x):
    return 1.0 / (1.0 + jnp.exp(-x))


def _inproj_kernel(rows_per_seq, x_ref, g_ref, w_ref, pz_ref, mu_ref, w0_ref, w2_ref, a0_ref, a2_ref, g2_ref,
                   kk_ref, ka_ref, ones_ref,
                   q_ref, k_ref, v_ref, gate_ref, zl_ref,
                   r_ref, lw_ref, k2_ref, vr_ref, a_ref, b_ref, gr_ref, *carry):
    n = _rms(x_ref[...], g_ref[...]).astype(BF)
    z0 = ATT_W + 2 * KV_W
    z = _dot(n, w_ref[:, z0:z0 + RWKV_IN])
    q_ref[...] = _dot(n, w_ref[:, 0:ATT_W]).astype(BF)
    k_ref[...] = _dot(n, w_ref[:, ATT_W:ATT_W + KV_W])
    v_ref[...] = _dot(n, w_ref[:, ATT_W + KV_W:ATT_W + 2 * KV_W])
    gate_ref[...] = _dot(n, w_ref[:, z0 + RWKV_IN:IN_W])
    tm = z.shape[0]
    if rows_per_seq is None:
        zs = pz_ref[...]
        zl_ref[...] = z
    else:
        prev_ref, = carry
        tiles_per_seq = rows_per_seq // tm

        @pl.when(pl.program_id(0) % tiles_per_seq == 0)
        def _():
            prev_ref[...] = pz_ref[0]

        row = lax.broadcasted_iota(jnp.int32, (tm, 1), 0)
        zs = jnp.where(row == 0, prev_ref[...], pltpu.roll(z, 1, 0))
        prev_ref[...] = z[tm - 1:tm, :]
        zl_ref[0] = z[tm - 1:tm, :]
    zz = z + (zs - z) * mu_ref[...]
    r = zz[:, 0:RWKV_W]
    k = zz[:, RWKV_W:2 * RWKV_W]
    v = zz[:, 2 * RWKV_W:3 * RWKV_W]
    o = 3 * RWKV_W
    wl = zz[:, o:o + D_DECAY_LORA]
    al = zz[:, o + D_DECAY_LORA:o + D_DECAY_LORA + D_AAA_LORA]
    gl = zz[:, o + D_DECAY_LORA + D_AAA_LORA:RWKV_IN]
    wpre = -(w0_ref[...] + _dot(jnp.tanh(wl).astype(BF), w2_ref[...]))
    softplus = jnp.maximum(wpre, 0.0) + jnp.log1p(jnp.exp(-jnp.abs(wpre)))
    lw = -jnp.exp(-softplus - 0.5)
    a = _sigmoid(a0_ref[...] + _dot(al.astype(BF), a2_ref[...]))
    g = _dot(_sigmoid(gl).astype(BF), g2_ref[...])
    kkv = k * kk_ref[...]
    ss = _dot_exact_rhs(kkv * kkv, ones_ref[...])
    kk = kkv / jnp.maximum(jnp.sqrt(ss), 1e-12)
    k2 = k * (1.0 + (a - 1.0) * ka_ref[...])
    r_ref[...], lw_ref[...], k2_ref[...], vr_ref[...] = r, lw, k2, v
    a_ref[...], b_ref[...], gr_ref[...] = -kk, kk * a, g


def _inproj(x, g, p, prev_z, tm, rows_per_seq):
    t = x.shape[0]
    row = lambda w_: pl.BlockSpec((tm, w_), lambda i: (i, 0))
    vec = lambda w_: _const_spec((1, w_))
    if rows_per_seq is None:
        pz_spec = row(RWKV_IN)
        zl_spec, zl_shape = row(RWKV_IN), (t, RWKV_IN)
        scratch = []
    else:
        tiles_per_seq = rows_per_seq // tm
        pz_spec = pl.BlockSpec((1, 1, RWKV_IN), lambda i: (i // tiles_per_seq, 0, 0))
        zl_spec, zl_shape = pz_spec, (t // rows_per_seq, 1, RWKV_IN)
        scratch = [pltpu.VMEM((1, RWKV_IN), F32)]
    return pl.pallas_call(
        functools.partial(_inproj_kernel, rows_per_seq),
        grid=(t // tm,),
        in_specs=[row(D_MODEL), vec(D_MODEL), _const_spec((D_MODEL, IN_W)), pz_spec,
                  vec(RWKV_IN), vec(RWKV_W), _const_spec((D_DECAY_LORA, RWKV_W)),
                  vec(RWKV_W), _const_spec((D_AAA_LORA, RWKV_W)), _const_spec((D_GATE_LORA, RWKV_W)),
                  vec(RWKV_W), vec(RWKV_W), _const_spec((RWKV_W, RWKV_W))],
        out_specs=[row(ATT_W), row(KV_W), row(KV_W), row(2 * D_MODEL), zl_spec] + [row(RWKV_W)] * 7,
        out_shape=[jax.ShapeDtypeStruct((t, ATT_W), BF),
                   jax.ShapeDtypeStruct((t, KV_W), F32),
                   jax.ShapeDtypeStruct((t, KV_W), F32),
                   jax.ShapeDtypeStruct((t, 2 * D_MODEL), F32),
                   jax.ShapeDtypeStruct(zl_shape, F32)] + [jax.ShapeDtypeStruct((t, RWKV_W), F32)] * 7,
        scratch_shapes=scratch,
        compiler_params=_params("arbitrary"),
        name="inproj",
    )(x, g, p["w_in"], prev_z, p["mu"], p["w0"], p["w2"], p["a0"], p["a2"], p["g2"], p["k_k"], p["k_a"],
      p["ones_bd"])


def _t5_bucket(dist):
    n = np.maximum(dist, 0)
    nf = np.maximum(n, 1).astype(np.float32)
    large = MAX_EXACT + (np.log(nf / MAX_EXACT) / math.log(MAX_DISTANCE / MAX_EXACT)
                         * (NUM_BUCKETS - MAX_EXACT)).astype(np.int32)
    return np.where(n < MAX_EXACT, n, np.minimum(large, NUM_BUCKETS - 1)).astype(np.int32)


def _bias_lookup(rel_bias, bucket):
    hit = bucket[..., None, None] == np.arange(NUM_BUCKETS, dtype=np.int32)[:, None]
    return jnp.sum(jnp.where(hit, rel_bias, 0.0), axis=-2)


def _prompt_bucket_table():
    dist = np.arange(WINDOW)[:, None] - (np.arange(2 * WINDOW) - WINDOW)[None, :]
    return np.where((dist >= 0) & (dist <= WINDOW), _t5_bucket(dist), -1).astype(np.int32)


def _swa_prompt_kernel(nb, bucket_ref, relb_ref, q_ref, kp_ref, kc_ref, vp_ref, vc_ref, sink_ref, o_ref, bias_ref):
    first = pl.program_id(0) == 0

    @pl.when(first)
    def _():
        bucket = bucket_ref[...]
        for head in range(N_Q_HEADS):
            tab = jnp.full(bucket.shape, NEG_INF, F32)
            for n in range(NUM_BUCKETS):
                tab = jnp.where(bucket == n, relb_ref[n, head], tab)
            g = head % GQA_GROUP
            bias_ref[head // GQA_GROUP, g * WINDOW:(g + 1) * WINDOW, :] = tab

    col = lax.broadcasted_iota(jnp.int32, (GQA_GROUP * WINDOW, 2 * WINDOW), 1)
    pad_mask = jnp.logical_and(first, col < WINDOW)
    chains = [(b, h) for b in range(nb) for h in range(N_KV_HEADS)]
    hs = lambda h: slice(h * HEAD_DIM, (h + 1) * HEAD_DIM)
    kh = [jnp.concatenate([kp_ref[b][:, hs(h)], kc_ref[b][:, hs(h)]], axis=0).astype(BF) for b, h in chains]
    vh = [jnp.concatenate([vp_ref[b][:, hs(h)], vc_ref[b][:, hs(h)]], axis=0).astype(BF) for b, h in chains]
    qh = [jnp.concatenate([q_ref[b][:, hs(GQA_GROUP * h + g)] for g in range(GQA_GROUP)], axis=0)
          for b, h in chains]
    s = [_dot_nt(qh[n], kh[n]) * ATT_SCALE + bias_ref[h] for n, (b, h) in enumerate(chains)]
    s = [jnp.where(pad_mask, NEG_INF, x) for x in s]
    m = [jnp.maximum(jnp.max(x, axis=-1, keepdims=True), sink_ref[h]) for x, (b, h) in zip(s, chains)]
    p = [jnp.exp(x - mm) for x, mm in zip(s, m)]
    denom = [jnp.sum(x, axis=-1, keepdims=True) + jnp.exp(sink_ref[h] - mm)
             for x, mm, (b, h) in zip(p, m, chains)]
    o = [_dot(x.astype(BF), vv) / d for x, vv, d in zip(p, vh, denom)]
    for b in range(nb):
        pieces = [o[b * N_KV_HEADS + h][g * WINDOW:(g + 1) * WINDOW]
                  for h in range(N_KV_HEADS) for g in range(GQA_GROUP)]
        o_ref[b] = jnp.concatenate(pieces, axis=-1).astype(o_ref.dtype)


def _swa_prompt(q3, k3, v3, rel_bias, sink_col):
    batch, seq, _ = q3.shape
    cur = lambda i: (0, i, 0)
    prev = lambda i: (0, jnp.maximum(i - 1, 0), 0)
    kv_c = pl.BlockSpec((batch, WINDOW, KV_W), cur)
    kv_p = pl.BlockSpec((batch, WINDOW, KV_W), prev)
    bucket = jnp.asarray(_prompt_bucket_table())
    return pl.pallas_call(
        functools.partial(_swa_prompt_kernel, batch),
        grid=(seq // WINDOW,),
        in_specs=[_const_spec(bucket.shape), pl.BlockSpec(memory_space=pltpu.SMEM),
                  pl.BlockSpec((batch, WINDOW, ATT_W), cur), kv_p, kv_c, kv_p, kv_c,
                  _const_spec(sink_col.shape)],
        out_specs=pl.BlockSpec((batch, WINDOW, ATT_W), cur),
        out_shape=jax.ShapeDtypeStruct((batch, seq, ATT_W), BF),
        scratch_shapes=[pltpu.VMEM((N_KV_HEADS, GQA_GROUP * WINDOW, 2 * WINDOW), F32)],
        compiler_params=_params("arbitrary"),
        name="swa_prompt",
    )(bucket, rel_bias, q3, k3, k3, v3, v3, sink_col)


def _swa_sample_kernel(q_ref, kn_ref, vn_ref, ck_ref, cv_ref, bc_ref, bn_ref, sink_ref, o_ref):
    q = q_ref[...]
    qf = q.astype(F32)
    kn = kn_ref[...].astype(BF).astype(F32)
    vn = vn_ref[...].astype(BF).astype(F32)
    ck, cv = ck_ref[...], cv_ref[...]
    head = lax.broadcasted_iota(jnp.int32, (1, N_Q_HEADS, 1), 1)
    low = head < GQA_GROUP
    s_h, sn_h = [], []
    for h in range(N_KV_HEADS):
        sl = slice(h * HEAD_DIM, (h + 1) * HEAD_DIM)
        s_h.append(jnp.einsum("bgd,bkd->bgk", q, ck[:, :, sl].astype(BF), preferred_element_type=F32))
        sn_h.append(jnp.sum(qf * kn[:, None, sl], axis=-1, keepdims=True))
    s = jnp.where(low, s_h[0], s_h[1]) * ATT_SCALE + bc_ref[...]
    sn = jnp.where(low, sn_h[0], sn_h[1]) * ATT_SCALE + bn_ref[...]
    sink = sink_ref[...]
    m = jnp.maximum(jnp.maximum(jnp.max(s, axis=-1, keepdims=True), sn), sink)
    p = jnp.exp(s - m)
    pn = jnp.exp(sn - m)
    denom = jnp.sum(p, axis=-1, keepdims=True) + pn + jnp.exp(sink - m)
    pb = p.astype(BF)
    pnb = pn.astype(BF).astype(F32)
    o_h = []
    for h in range(N_KV_HEADS):
        sl = slice(h * HEAD_DIM, (h + 1) * HEAD_DIM)
        o = jnp.einsum("bgk,bkd->bgd", pb, cv[:, :, sl].astype(BF), preferred_element_type=F32)
        o_h.append(o + pnb * vn[:, None, sl])
    o_ref[...] = (jnp.where(low, o_h[0], o_h[1]) / denom).astype(o_ref.dtype)


def _swa_sample(q3, kn, vn, ck, cv, bias_c, bias_n, sink3, bb):
    b = q3.shape[0]
    return pl.pallas_call(
        _swa_sample_kernel,
        grid=(b // bb,),
        in_specs=[pl.BlockSpec((bb, N_Q_HEADS, HEAD_DIM), lambda i: (i, 0, 0)),
                  pl.BlockSpec((bb, KV_W), lambda i: (i, 0)),
                  pl.BlockSpec((bb, KV_W), lambda i: (i, 0)),
                  pl.BlockSpec((bb, WINDOW, KV_W), lambda i: (i, 0, 0)),
                  pl.BlockSpec((bb, WINDOW, KV_W), lambda i: (i, 0, 0)),
                  _const_spec(bias_c.shape), _const_spec(bias_n.shape), _const_spec(sink3.shape)],
        out_specs=pl.BlockSpec((bb, N_Q_HEADS, HEAD_DIM), lambda i: (i, 0, 0)),
        out_shape=jax.ShapeDtypeStruct((b, N_Q_HEADS, HEAD_DIM), BF),
        compiler_params=_params("parallel"),
        name="swa_sample",
    )(q3, kn, vn, ck, cv, bias_c, bias_n, sink3)


def _rwkv_scan_kernel(nb, chunk, r_ref, lw_ref, k_ref, v_ref, a_ref, b_ref, g_ref, s0_ref,
                      rk_ref, lg_ref, lb_ref, ones_ref, y_ref, sout_ref, s_scr):
    c = pl.program_id(1)

    @pl.when(c == 0)
    def _():
        s_scr[...] = s0_ref[...]

    ri = lax.broadcasted_iota(jnp.int32, (chunk, chunk), 0)
    ci = lax.broadcasted_iota(jnp.int32, (chunk, chunk), 1)
    incl = ri >= ci
    strict = ri > ci
    tri = incl.astype(BF)
    eye = (ri == ci).astype(F32)
    n_sq = int(math.log2(chunk)) - 1

    chains = [(bi, h) for bi in range(nb) for h in range(RWKV_HEADS)]
    hs = lambda h: slice(h * RWKV_HEAD, (h + 1) * RWKV_HEAD)

    ra, aa, bt, kt, bh, kh, vb, g_all = [], [], [], [], [], [], [], []
    for bi in range(nb):
        r, lw, k, v = r_ref[bi], lw_ref[bi], k_ref[bi], v_ref[bi]
        a, b = a_ref[bi], b_ref[bi]
        hi, lo = _split2(lw)
        lo2 = (lw - hi.astype(F32) - lo.astype(F32)).astype(BF)
        cum = _dot(tri, hi) + _dot(tri, lo) + _dot(tri, lo2)
        tail = cum[chunk - 1:chunk, :]
        g_inv = jnp.exp(-cum)
        g_tail = jnp.exp(tail - cum)
        g_all.append(jnp.exp(tail))
        ra.append((r * jnp.exp(cum)).astype(BF))
        aa.append((a * jnp.exp(cum - lw)).astype(BF))
        bt.append((b * g_inv).astype(BF))
        kt.append((k * g_inv).astype(BF))
        bh.append((b * g_tail).astype(BF))
        kh.append((k * g_tail).astype(BF))
        vb.append(v.astype(BF))

    s_old = [s_scr[bi, h] for bi, h in chains]
    ar = [jnp.concatenate([aa[bi][:, hs(h)], ra[bi][:, hs(h)]], axis=0) for bi, h in chains]
    v_h = [vb[bi][:, hs(h)] for bi, h in chains]
    gb = [_dot_nt(ar[n], bt[bi][:, hs(h)]) for n, (bi, h) in enumerate(chains)]
    gk = [_dot_nt(ar[n], kt[bi][:, hs(h)]) for n, (bi, h) in enumerate(chains)]
    p = [_dot_nt(ar[n], s_old[n].astype(BF)) for n in range(len(chains))]
    l_ab = [jnp.where(strict, x[:chunk], 0.0) for x in gb]
    l_ak = [jnp.where(strict, x[:chunk], 0.0).astype(BF) for x in gk]
    m_rb = [jnp.where(incl, x[chunk:], 0.0).astype(BF) for x in gb]
    m_rk = [jnp.where(incl, x[chunk:], 0.0).astype(BF) for x in gk]
    rhs = [p[n][:chunk] + _dot(l_ak[n], v_h[n]) for n in range(len(chains))]
    t_inv = [eye + x for x in l_ab]
    lp = l_ab
    for _ in range(n_sq):
        lpb = [x.astype(BF) for x in lp]
        lp = [_dot(x, x) for x in lpb]
        t_inv = [t + _dot(x.astype(BF), t.astype(BF)) for x, t in zip(lp, t_inv)]
    ub = [_dot(t.astype(BF), x.astype(BF)).astype(BF) for t, x in zip(t_inv, rhs)]
    y_h = [p[n][chunk:] + _dot(m_rb[n], ub[n]) + _dot(m_rk[n], v_h[n]) for n in range(len(chains))]
    s_new = [s_old[n] * g_all[bi][:, hs(h)] + _dot_tn(ub[n], bh[bi][:, hs(h)]) + _dot_tn(v_h[n], kh[bi][:, hs(h)])
             for n, (bi, h) in enumerate(chains)]
    for n, (bi, h) in enumerate(chains):
        s_scr[bi, h] = s_new[n]

    for bi in range(nb):
        y = jnp.concatenate(y_h[bi * RWKV_HEADS:(bi + 1) * RWKV_HEADS], axis=-1)
        y_ref[bi] = _rwkv_epilogue(y, r_ref[bi], k_ref[bi], v_ref[bi], g_ref[bi],
                                   rk_ref, lg_ref, lb_ref, ones_ref).astype(y_ref.dtype)

    @pl.when(c == pl.num_programs(1) - 1)
    def _():
        sout_ref[...] = s_scr[...]


def _rwkv_epilogue(y, r, k, v, g, rk_ref, lg_ref, lb_ref, ones_ref):
    ones = ones_ref[...]
    inv_n = 1.0 / RWKV_HEAD
    mean = _dot_exact_rhs(y, ones) * inv_n
    d = y - mean
    var = _dot_exact_rhs(d * d, ones) * inv_n
    yn = d * lax.rsqrt(var + GN_EPS) * lg_ref[...] + lb_ref[...]
    bonus = _dot_exact_rhs(r * k * rk_ref[...], ones) * v
    return (yn + bonus) * g


def _rwkv_scan(coef, s0, p, nb, chunk):
    r = coef[0]
    b, t, _ = r.shape
    seq_spec = pl.BlockSpec((nb, chunk, RWKV_W), lambda bi, c: (bi, c, 0))
    st_spec = pl.BlockSpec((nb, RWKV_HEADS, RWKV_HEAD, RWKV_HEAD), lambda bi, c: (bi, 0, 0, 0))
    vec = _const_spec((1, RWKV_W))
    return pl.pallas_call(
        functools.partial(_rwkv_scan_kernel, nb, chunk),
        grid=(b // nb, t // chunk),
        in_specs=[seq_spec] * 7 + [st_spec, vec, vec, vec, _const_spec((RWKV_W, RWKV_W))],
        out_specs=[seq_spec, st_spec],
        out_shape=[jax.ShapeDtypeStruct((b, t, RWKV_W), BF),
                   jax.ShapeDtypeStruct((b, RWKV_HEADS, RWKV_HEAD, RWKV_HEAD), F32)],
        scratch_shapes=[pltpu.VMEM((nb, RWKV_HEADS, RWKV_HEAD, RWKV_HEAD), F32)],
        compiler_params=_params("parallel", "arbitrary"),
        name="rwkv_scan",
    )(*coef, s0, p["r_k"], p["lnx_g"], p["lnx_b"], p["ones_bd"])


def _rwkv_step_kernel(r_ref, lw_ref, k_ref, v_ref, a_ref, b_ref, g_ref, s_ref,
                      rk_ref, lg_ref, lb_ref, ones_ref, y_ref, sout_ref, y_scr):
    n_pair = LANES // RWKV_HEAD
    r, k, v = r_ref[...], k_ref[...], v_ref[...]
    rT, kT, vT = r.T, k.T, v.T
    aT, bT, wT = a_ref[...].T, b_ref[...].T, jnp.exp(lw_ref[...]).T
    for hl in range(LANES // RWKV_HEAD):
        hsl = slice(hl * RWKV_HEAD, (hl + 1) * RWKV_HEAD)
        a_h, b_h, k_h, w_h, r_h = aT[hsl], bT[hsl], kT[hsl], wT[hsl], rT[hsl]
        tiles = [(hl * RWKV_HEAD + n_pair * t) * RWKV_HEAD for t in range(RWKV_HEAD // n_pair)]
        st = [s_ref[:, c0:c0 + LANES].T for c0 in tiles]
        new = []
        for t, x in enumerate(st):
            halves = []
            for il in range(n_pair):
                i = hl * RWKV_HEAD + n_pair * t + il
                slab = x[il * RWKV_HEAD:(il + 1) * RWKV_HEAD]
                sa = jnp.sum(slab * a_h, axis=0, keepdims=True)
                slab = slab * w_h + sa * b_h + vT[i:i + 1] * k_h
                y_scr[i:i + 1, :] = jnp.sum(slab * r_h, axis=0, keepdims=True)
                halves.append(slab)
            new.append(jnp.concatenate(halves, axis=0))
        for c0, x in zip(tiles, new):
            sout_ref[:, c0:c0 + LANES] = x.T
    y_ref[...] = _rwkv_epilogue(y_scr[...].T, r, k, v, g_ref[...],
                                rk_ref, lg_ref, lb_ref, ones_ref).astype(y_ref.dtype)


def _rwkv_step(coef, s0, p):
    b = s0.shape[0]
    per_pair = (LANES // RWKV_HEAD) * RWKV_HEAD * RWKV_HEAD
    n_steps = RWKV_W // LANES
    col = pl.BlockSpec((b, LANES), lambda i: (0, i))
    vec = pl.BlockSpec((1, LANES), lambda i: (0, i))
    st_spec = pl.BlockSpec((b, per_pair), lambda i: (0, i))
    y, s_new = pl.pallas_call(
        _rwkv_step_kernel,
        grid=(n_steps,),
        in_specs=[col] * 7 + [st_spec, vec, vec, vec, _const_spec((LANES, LANES))],
        out_specs=[col, st_spec],
        out_shape=[jax.ShapeDtypeStruct((b, RWKV_W), BF),
                   jax.ShapeDtypeStruct((b, n_steps * per_pair), F32)],
        scratch_shapes=[pltpu.VMEM((LANES, b), F32)],
        compiler_params=_params("parallel"),
        name="rwkv_step",
    )(*coef, s0.reshape(b, n_steps * per_pair), p["r_k"], p["lnx_g"], p["lnx_b"],
      p["ones_bd"][:LANES, :LANES])
    return y, s_new.reshape(s0.shape)


def _merge_kernel(with_router, x_ref, ya_ref, yb_ref, gate_ref, wa_ref, wb_ref, wo_ref, g_ref, *rest):
    if with_router:
        wr_ref, cnt_in_ref, x1_ref, h_ref, route_ref, cnt_out_ref, cnt_scr = rest
    else:
        x1_ref, h_ref = rest
    gate = gate_ref[...]
    pa = _dot(ya_ref[...], wa_ref[...])
    pb = _dot(yb_ref[...], wb_ref[...])
    merged = _sigmoid(gate[:, :D_MODEL]) * pa + _sigmoid(gate[:, D_MODEL:]) * pb
    x1 = x_ref[...] + _dot(merged.astype(BF), wo_ref[...])
    x1_ref[...] = x1
    h = _rms(x1, g_ref[...])
    if not with_router:
        h_ref[...] = h.astype(h_ref.dtype)
        return
    for s in range(D_MODEL // LANES):
        h_ref[:, s, :] = h[:, s * LANES:(s + 1) * LANES]

    @pl.when(pl.program_id(0) == 0)
    def _():
        cnt_scr[...] = cnt_in_ref[...]

    logits = _dot(h.astype(BF), wr_ref[...])
    tm = logits.shape[0]
    lane = lax.broadcasted_iota(jnp.int32, logits.shape, 1).astype(F32)
    logits = jnp.where(lane < N_EXPERTS, logits, -jnp.inf)
    m1 = jnp.max(logits, axis=-1, keepdims=True)
    i1 = jnp.min(jnp.where(logits == m1, lane, float(LANES)), axis=-1, keepdims=True)
    rest_l = jnp.where(lane == i1, -jnp.inf, logits)
    m2 = jnp.max(rest_l, axis=-1, keepdims=True)
    i2 = jnp.min(jnp.where(rest_l == m2, lane, float(LANES)), axis=-1, keepdims=True)
    e2 = jnp.exp(m2 - m1)
    g1 = 1.0 / (1.0 + e2)
    g2 = e2 / (1.0 + e2)
    oh1 = (lane == i1).astype(F32)
    oh2 = (lane == i2).astype(F32)
    both = oh1 + oh2
    ri = lax.broadcasted_iota(jnp.int32, (tm, tm), 0)
    ci = lax.broadcasted_iota(jnp.int32, (tm, tm), 1)
    before = _dot((ri > ci).astype(BF), both.astype(BF)) + cnt_scr[...]
    rank1 = jnp.sum(oh1 * before, axis=-1, keepdims=True)
    rank2 = jnp.sum(oh2 * before, axis=-1, keepdims=True)
    cnt_scr[...] += jnp.sum(both, axis=0, keepdims=True)
    cnt_out_ref[...] = cnt_scr[...]
    route = jnp.zeros_like(logits)
    for n, val in enumerate((i1, i2, g1, g2, rank1, rank2)):
        route = jnp.where(lane == n, val, route)
    route_ref[...] = route


def _merge_out(x, ya, yb, gate, wa, wb, wo, g, tm, router=None, cnt_in=None):
    t = x.shape[0]
    row = lambda w_: pl.BlockSpec((tm, w_), lambda i: (i, 0))
    in_specs = [row(D_MODEL), row(ATT_W), row(RWKV_W), row(2 * D_MODEL),
                _const_spec((ATT_W, D_MODEL)), _const_spec((RWKV_W, D_MODEL)),
                _const_spec((D_MODEL, D_MODEL)), _const_spec((1, D_MODEL))]
    args = [x, ya, yb, gate, wa, wb, wo, g]
    if router is None:
        out_specs = [row(D_MODEL), row(D_MODEL)]
        out_shape = [jax.ShapeDtypeStruct((t, D_MODEL), F32), jax.ShapeDtypeStruct((t, D_MODEL), BF)]
        scratch = []
    else:
        n_sub = D_MODEL // LANES
        in_specs += [_const_spec(router.shape), _const_spec((1, LANES))]
        args += [router, cnt_in]
        out_specs = [row(D_MODEL), pl.BlockSpec((tm, n_sub, LANES), lambda i: (i, 0, 0)), row(LANES),
                     pl.BlockSpec((1, LANES), lambda i: (0, 0))]
        out_shape = [jax.ShapeDtypeStruct((t, D_MODEL), F32), jax.ShapeDtypeStruct((t, n_sub, LANES), F32),
                     jax.ShapeDtypeStruct((t, LANES), F32), jax.ShapeDtypeStruct((1, LANES), F32)]
        scratch = [pltpu.VMEM((1, LANES), F32)]
    return pl.pallas_call(
        functools.partial(_merge_kernel, router is not None),
        grid=(t // tm,),
        in_specs=in_specs, out_specs=out_specs, out_shape=out_shape, scratch_shapes=scratch,
        compiler_params=_params("parallel" if router is None else "arbitrary"),
        name="merge_out",
    )(*args)


def _swiglu(x, wg, wu, wd):
    s = _dot(x, wg)
    u = _dot(x, wu)
    act = (s * _sigmoid(s) * u).astype(BF)
    return _dot(act, wd)


def _dense_ffn_kernel(x1_ref, h_ref, wg_ref, wu_ref, wd_ref, o_ref):
    o_ref[...] = x1_ref[...] + _swiglu(h_ref[...], wg_ref[...], wu_ref[...], wd_ref[...])


def _dense_ffn(x1, h, wg, wu, wd, tm):
    t = x1.shape[0]
    d_ff = wg.shape[1]
    row = lambda: pl.BlockSpec((tm, D_MODEL), lambda i: (i, 0))
    return pl.pallas_call(
        _dense_ffn_kernel,
        grid=(t // tm,),
        in_specs=[row(), row(), _const_spec((D_MODEL, d_ff)), _const_spec((D_MODEL, d_ff)),
                  _const_spec((d_ff, D_MODEL))],
        out_specs=row(),
        out_shape=jax.ShapeDtypeStruct((t, D_MODEL), F32),
        compiler_params=_params("parallel"),
        name="dense_ffn",
    )(x1, h, wg, wu, wd)


def _rows_as_matrix(ref3):
    return jnp.concatenate([ref3[:, s, :] for s in range(ref3.shape[1])], axis=-1)


def _dispatch_kernel(tt, first, dest_ref, fill_ref, h_ref, *rest):
    if first:
        xs_hbm, zbuf, sem, zsem = rest
    else:
        _, xs_hbm, sem = rest
    i = pl.program_id(0)
    base = i * tt

    if first:
        @pl.when(i == 0)
        def _():
            zbuf[...] = jnp.zeros_like(zbuf)
            bm = zbuf.shape[0]
            fills = [pltpu.make_async_copy(zbuf, xs_hbm.at[pl.ds(fill_ref[n] * bm, bm)], zsem)
                     for n in range(fill_ref.shape[0])]
            for f in fills:
                f.start()
            for f in fills:
                f.wait()

    def issue(r, carry):
        for j in range(TOP_K):
            pltpu.make_async_copy(h_ref.at[r], xs_hbm.at[dest_ref[TOP_K * (base + r) + j]], sem).start()
        return carry

    lax.fori_loop(0, tt, issue, 0, unroll=8)
    for j in range(TOP_K):
        pltpu.make_async_copy(h_ref, xs_hbm.at[pl.ds(0, tt)], sem).wait()


def _dispatch(dest, fill_blocks, h3, xs, tt, bm, n_rows):
    t, n_sub, _ = h3.shape
    first = xs is None
    in_specs = [pl.BlockSpec((tt, n_sub, LANES), lambda i, ds, fb: (i, 0, 0))]
    args = [dest, fill_blocks, h3]
    if first:
        scratch = [pltpu.VMEM((bm, n_sub, LANES), F32), pltpu.SemaphoreType.DMA(()), pltpu.SemaphoreType.DMA(())]
        aliases = {}
    else:
        in_specs.append(pl.BlockSpec(memory_space=pl.ANY))
        args.append(xs)
        scratch = [pltpu.SemaphoreType.DMA(())]
        aliases = {3: 0}
    grid_spec = pltpu.PrefetchScalarGridSpec(
        num_scalar_prefetch=2, grid=(t // tt,), in_specs=in_specs,
        out_specs=pl.BlockSpec(memory_space=pl.ANY), scratch_shapes=scratch)
    return pl.pallas_call(
        functools.partial(_dispatch_kernel, tt, first),
        grid_spec=grid_spec,
        out_shape=jax.ShapeDtypeStruct((n_rows, n_sub, LANES), F32),
        input_output_aliases=aliases,
        compiler_params=_params("arbitrary"),
        name="moe_dispatch",
    )(*args)


def _moe_kernel(n_half, blk_e_ref, n_used_ref, xs_ref, wg_ref, wu_ref, wd_ref, y_ref):
    @pl.when(pl.program_id(0) < n_used_ref[0])
    def _():
        x = _rows_as_matrix(xs_ref).astype(BF)
        d_e = wg_ref.shape[2]
        step = d_e // n_half
        acc = None
        for j in range(n_half):
            cs = slice(j * step, (j + 1) * step)
            part = _swiglu(x, wg_ref[0, :, cs], wu_ref[0, :, cs], wd_ref[0, cs, :])
            acc = part if acc is None else acc + part
        for s in range(y_ref.shape[1]):
            y_ref[:, s, :] = acc[:, s * LANES:(s + 1) * LANES]

    @pl.when(pl.program_id(0) >= n_used_ref[0])
    def _():
        y_ref[...] = jnp.zeros_like(y_ref)


def _moe_ffn(xs, blk_e, n_used, wg, wu, wd, bm):
    n_blk = blk_e.shape[0]
    n_rows, n_sub, _ = xs.shape
    d_e = wg.shape[2]
    wspec = lambda shape: pl.BlockSpec(shape, lambda i, be, nu: (be[i], 0, 0), pipeline_mode=pl.Buffered(1))
    rows = pl.BlockSpec((bm, n_sub, LANES), lambda i, be, nu: (i, 0, 0))
    grid_spec = pltpu.PrefetchScalarGridSpec(
        num_scalar_prefetch=2,
        grid=(n_blk,),
        in_specs=[rows, wspec((1, D_MODEL, d_e)), wspec((1, D_MODEL, d_e)), wspec((1, d_e, D_MODEL))],
        out_specs=rows,
    )
    return pl.pallas_call(
        functools.partial(_moe_kernel, 2),
        grid_spec=grid_spec,
        out_shape=jax.ShapeDtypeStruct((n_rows, n_sub, LANES), F32),
        compiler_params=_params("arbitrary"),
        name="moe_ffn",
    )(blk_e, n_used, xs, wg, wu, wd)


def _row_layout(counts, bm, n_blk):
    per_e = (counts + bm - 1) // bm
    ends = jnp.cumsum(per_e)
    n_used = ends[-1]
    first_row = (ends - per_e) * bm
    i = jnp.arange(n_blk, dtype=jnp.int32)
    blk_e = jnp.sum((jnp.minimum(i, jnp.maximum(n_used - 1, 0))[:, None] >= ends[None, :]).astype(jnp.int32), axis=1)
    need = jnp.any(i[:, None] == (ends - 1)[None, :], axis=1) | (i >= n_blk - N_EXPERTS)
    fill_blocks = jnp.argsort(jnp.logical_not(need), stable=True)[:2 * N_EXPERTS].astype(jnp.int32)
    return first_row.astype(jnp.int32), blk_e.astype(jnp.int32), n_used.astype(jnp.int32).reshape(1), fill_blocks


def _combine_kernel(tc, dest_ref, x1_ref, route_ref, y_hbm, g_ref, o_ref, ybuf, sem):
    base = pl.program_id(0) * tc

    def issue(r, carry):
        for j in range(TOP_K):
            pltpu.make_async_copy(y_hbm.at[dest_ref[TOP_K * (base + r) + j]], ybuf.at[j, r], sem).start()
        return carry

    lax.fori_loop(0, tc, issue, 0, unroll=4)
    for j in range(TOP_K):
        pltpu.make_async_copy(y_hbm.at[pl.ds(0, tc)], ybuf.at[j], sem).wait()
    route = route_ref[...]
    x2 = (x1_ref[...] + route[:, 2:3] * _rows_as_matrix(ybuf.at[0])
          + route[:, 3:4] * _rows_as_matrix(ybuf.at[1]))
    o_ref[...] = _rms(x2, g_ref[...])


def _moe_combine(dest, x1, route, y, g, tc):
    t = x1.shape[0]
    n_sub = y.shape[1]
    grid_spec = pltpu.PrefetchScalarGridSpec(
        num_scalar_prefetch=1,
        grid=(t // tc,),
        in_specs=[pl.BlockSpec((tc, D_MODEL), lambda i, ds: (i, 0)),
                  pl.BlockSpec((tc, LANES), lambda i, ds: (i, 0)),
                  pl.BlockSpec(memory_space=pl.ANY),
                  pl.BlockSpec((1, D_MODEL), lambda i, ds: (0, 0))],
        out_specs=pl.BlockSpec((tc, D_MODEL), lambda i, ds: (i, 0)),
        scratch_shapes=[pltpu.VMEM((TOP_K, tc, n_sub, LANES), F32), pltpu.SemaphoreType.DMA(())],
    )
    return pl.pallas_call(
        functools.partial(_combine_kernel, tc),
        grid_spec=grid_spec,
        out_shape=jax.ShapeDtypeStruct((t, D_MODEL), F32),
        compiler_params=_params("arbitrary"),
        name="moe_combine",
    )(dest, x1, route, y, g)


def _layer_params(l, w_in, mu_shift, w0, w2, a0, a2, g2, k_k, k_a, r_k, lnx_g, lnx_b,
                  w_proj_attn, w_proj_rwkv, w_out, ones_bd):
    row = lambda u: u.reshape(1, -1)
    return dict(
        w_in=w_in[l].astype(BF), mu=row(mu_shift[l]), w0=row(w0[l]), w2=w2[l].astype(BF),
        a0=row(a0[l]), a2=a2[l].astype(BF), g2=g2[l].astype(BF), k_k=row(k_k[l]), k_a=row(k_a[l]),
        r_k=row(r_k[l]), lnx_g=row(lnx_g[l]), lnx_b=row(lnx_b[l]),
        wa=w_proj_attn[l].astype(BF), wb=w_proj_rwkv[l].astype(BF), wo=w_out[l].astype(BF),
        ones_bd=ones_bd)


def kernel(x_prompt, x_sample, cache_k, cache_v, state_wkv, state_shift, norm_mix_g, w_in, attn_sinks, rel_bias, mu_shift, w0, w2, a0, a2, g2, k_k, k_a, r_k, lnx_g, lnx_b, w_proj_attn, w_proj_rwkv, w_out, norm_ffn_g, dense_w_gate, dense_w_up, dense_w_down, router_w, moe_w_gate, moe_w_up, moe_w_down, norm_final_g):
    batch, seq, _ = x_prompt.shape
    dec_batch, dec_seq, _ = x_sample.shape
    assert dec_seq == 1 and seq % WINDOW == 0

    head_id = jnp.arange(RWKV_W, dtype=jnp.int32) // RWKV_HEAD
    ones_bd = (head_id[:, None] == head_id[None, :]).astype(BF)
    bias_by_dist = _bias_lookup(rel_bias, _t5_bucket(np.arange(WINDOW + 1)))
    bias_c = jnp.transpose(bias_by_dist[WINDOW:0:-1], (1, 0))[None]
    bias_n = bias_by_dist[0][None, :, None]

    layers = [_layer_params(l, w_in, mu_shift, w0, w2, a0, a2, g2, k_k, k_a, r_k, lnx_g, lnx_b,
                            w_proj_attn, w_proj_rwkv, w_out, ones_bd) for l in range(DEPTH)]

    def router_pieces(l):
        return jnp.pad(router_w[l // 2], ((0, 0), (0, LANES - N_EXPERTS))).astype(BF)

    n_prompt = batch * seq
    tm_p = 512
    bm = 512
    n_all = n_prompt + dec_batch
    n_blk = -(-(n_all * TOP_K) // bm) + N_EXPERTS

    def mix(x, is_prompt, l, state):
        p = layers[l]
        n_tok = x.shape[0]
        gm = norm_mix_g[l].reshape(1, -1)
        sinks = attn_sinks[l]
        if is_prompt:
            b, t = batch, seq
            prev_z = jnp.zeros((b, 1, RWKV_IN), F32)
            q, k, v, gate, z_last, *coef = _inproj(x, gm, p, prev_z, 256, t)
            sink_col = jnp.repeat(sinks, WINDOW).reshape(N_KV_HEADS, GQA_GROUP * WINDOW, 1)
            ya = _swa_prompt(q.reshape(b, t, ATT_W), k.reshape(b, t, KV_W), v.reshape(b, t, KV_W),
                             rel_bias, sink_col).reshape(n_tok, ATT_W)
            k4 = k.reshape(b, t, N_KV_HEADS, HEAD_DIM)
            v4 = v.reshape(b, t, N_KV_HEADS, HEAD_DIM)
            state["k"].append(k4[:, t - WINDOW:])
            state["v"].append(v4[:, t - WINDOW:])
            s0 = jnp.zeros((b, RWKV_HEADS, RWKV_HEAD, RWKV_HEAD), F32)
            yb3, s_new = _rwkv_scan([u.reshape(b, t, RWKV_W) for u in coef], s0, p, b, 64)
            yb = yb3.reshape(n_tok, RWKV_W)
            state["z"].append(z_last.reshape(b, RWKV_IN))
        else:
            b = dec_batch
            q, k, v, gate, z, *coef = _inproj(x, gm, p, state_shift[l], n_tok, None)
            ck = cache_k[l].reshape(b, WINDOW, KV_W)
            cv = cache_v[l].reshape(b, WINDOW, KV_W)
            ya3 = _swa_sample(q.reshape(b, N_Q_HEADS, HEAD_DIM), k, v, ck, cv, bias_c, bias_n,
                              sinks.reshape(1, N_Q_HEADS, 1), 16)
            ya = ya3.reshape(b, ATT_W)
            state["k"].append(jnp.concatenate([ck[:, 1:], k[:, None]], axis=1)
                              .reshape(b, WINDOW, N_KV_HEADS, HEAD_DIM))
            state["v"].append(jnp.concatenate([cv[:, 1:], v[:, None]], axis=1)
                              .reshape(b, WINDOW, N_KV_HEADS, HEAD_DIM))
            yb, s_new = _rwkv_step(coef, state_wkv[l], p)
            state["z"].append(z)
        state["s"].append(s_new)
        return ya, yb, gate

    groups = [dict(x=x_prompt.reshape(n_prompt, D_MODEL), prompt=True, tm=tm_p, k=[], v=[], s=[], z=[]),
              dict(x=x_sample.reshape(dec_batch, D_MODEL), prompt=False, tm=dec_batch, k=[], v=[], s=[], z=[])]
    for l in range(DEPTH):
        p = layers[l]
        gn = norm_ffn_g[l].reshape(1, -1)
        j = l // 2
        if l % 2 == 0:
            wg, wu, wd = (w[j].astype(BF) for w in (dense_w_gate, dense_w_up, dense_w_down))
            for grp in groups:
                ya, yb, gate = mix(grp["x"], grp["prompt"], l, grp)
                x1, h = _merge_out(grp["x"], ya, yb, gate, p["wa"], p["wb"], p["wo"], gn, grp["tm"])
                grp["x"] = _dense_ffn(x1, h, wg, wu, wd, grp["tm"])
        else:
            assert l == DEPTH - 1
            wg, wu, wd = (w[j].astype(BF) for w in (moe_w_gate, moe_w_up, moe_w_down))
            router = router_pieces(l)
            cnt = jnp.zeros((1, LANES), F32)
            xs = None
            for grp in groups:
                ya, yb, gate = mix(grp["x"], grp["prompt"], l, grp)
                x1, h3, route, cnt = _merge_out(grp["x"], ya, yb, gate, p["wa"], p["wb"], p["wo"], gn,
                                                min(256, grp["tm"]), router=router, cnt_in=cnt)
                grp["x1"], grp["route"], grp["h3"] = x1, route, h3
            counts = cnt[0, :N_EXPERTS].astype(jnp.int32)
            first_row, blk_e, n_used, fill_blocks = _row_layout(counts, bm, n_blk)
            for grp in groups:
                route = grp["route"]
                expert = route[:, 0:TOP_K].astype(jnp.int32)
                rank = route[:, 2 * TOP_K:3 * TOP_K].astype(jnp.int32)
                start = jnp.sum(jnp.where(expert[..., None] == jnp.arange(N_EXPERTS, dtype=jnp.int32),
                                          first_row, 0), axis=-1)
                grp["dest"] = (start + rank).reshape(-1)
                xs = _dispatch(grp["dest"], fill_blocks, grp["h3"], xs, min(2048, grp["h3"].shape[0]), bm,
                               n_blk * bm)
            y = _moe_ffn(xs, blk_e, n_used, wg, wu, wd, bm)
            for grp in groups:
                grp["x"] = _moe_combine(grp["dest"], grp["x1"], grp["route"], y,
                                        norm_final_g.reshape(1, -1), min(1024, grp["x1"].shape[0]))

    gp, gs = groups
    outs = []
    for grp, shape in ((gp, (batch, seq, D_MODEL)), (gs, (dec_batch, dec_seq, D_MODEL))):
        outs.append((grp["x"].reshape(shape), jnp.stack(grp["k"]), jnp.stack(grp["v"]),
                     jnp.stack(grp["s"]), jnp.stack(grp["z"])))
    (y_p, nk_p, nv_p, ns_p, nz_p), (y_s, nk_s, nv_s, ns_s, nz_s) = outs
    return (y_p, y_s, nk_p, nv_p, ns_p, nz_p, nk_s, nv_s, ns_s, nz_s)
```

```python
import functools
import math

import jax
import jax.numpy as jnp
import numpy as np
from jax import lax
from jax.experimental import pallas as pl
from jax.experimental.pallas import tpu as pltpu

BF = jnp.bfloat16
F32 = jnp.float32

D_MODEL = 1024
DEPTH = 2
HEAD_DIM = 64
N_Q_HEADS = 8
N_KV_HEADS = 2
GQA_GROUP = 4
ATT_W = 512
KV_W = 128
WINDOW = 128
ATT_SCALE = HEAD_DIM ** -0.5
NUM_BUCKETS = 32
MAX_EXACT = 16
MAX_DISTANCE = 128
NEG_INF = -1e30
RWKV_HEAD = 64
RWKV_W = 512
RWKV_HEADS = 8
D_DECAY_LORA = 64
D_AAA_LORA = 64
D_GATE_LORA = 128
RWKV_IN = 3 * RWKV_W + D_DECAY_LORA + D_AAA_LORA + D_GATE_LORA
GN_EPS = 64e-5
IN_W = ATT_W + 2 * KV_W + RWKV_IN + 2 * D_MODEL
N_EXPERTS = 8
TOP_K = 2
NORM_EPS = 1e-6

VMEM_LIMIT_BYTES = 56 * 1024 * 1024
LANES = 128


def _params(*sem):
    return pltpu.CompilerParams(dimension_semantics=sem, vmem_limit_bytes=VMEM_LIMIT_BYTES)


def _const_spec(shape):
    zeros = (0,) * len(shape)
    return pl.BlockSpec(shape, lambda *_: zeros, pipeline_mode=pl.Buffered(1))


def _dot(a, b):
    return jnp.dot(a, b, preferred_element_type=F32)


def _dot_nt(a, b):
    return lax.dot_general(a, b, (((1,), (1,)), ((), ())), preferred_element_type=F32)


def _dot_tn(a, b):
    return lax.dot_general(a, b, (((0,), (0,)), ((), ())), preferred_element_type=F32)


def _split2(x):
    hi = x.astype(BF)
    lo = (x - hi.astype(F32)).astype(BF)
    return hi, lo


def _dot_exact_rhs(x, w):
    hi, lo = _split2(x)
    return _dot(hi, w) + _dot(lo, w)


def _rms(x, g):
    ms = jnp.mean(x * x, axis=-1, keepdims=True)
    return x * lax.rsqrt(ms + NORM_EPS) * g


def _sigmoid(x):
    return 1.0 / (1.0 + jnp.exp(-x))


def _cast_kernel(x_ref, o_ref):
    o_ref[...] = x_ref[...].astype(o_ref.dtype)


def _to_bf16(w, row_tile):
    rows, cols = w.shape
    spec = pl.BlockSpec((row_tile, cols), lambda i: (i, 0))
    return pl.pallas_call(
        _cast_kernel, grid=(rows // row_tile,), in_specs=[spec], out_specs=spec,
        out_shape=jax.ShapeDtypeStruct((rows, cols), BF),
        compiler_params=_params("parallel"), name="to_bf16",
    )(w)


def _inproj_kernel(rows_per_seq, x_ref, g_ref, w_ref, pz_ref, mu_ref, w0_ref, w2_ref, a0_ref, a2_ref, g2_ref,
                   kk_ref, ka_ref, ones_ref,
                   q_ref, k_ref, v_ref, gate_ref, zl_ref,
                   r_ref, lw_ref, k2_ref, vr_ref, a_ref, b_ref, gr_ref, *carry):
    n = _rms(x_ref[...], g_ref[...]).astype(BF)
    z0 = ATT_W + 2 * KV_W
    z = _dot(n, w_ref[:, z0:z0 + RWKV_IN])
    q_ref[...] = _dot(n, w_ref[:, 0:ATT_W]).astype(BF)
    k_ref[...] = _dot(n, w_ref[:, ATT_W:ATT_W + KV_W])
    v_ref[...] = _dot(n, w_ref[:, ATT_W + KV_W:ATT_W + 2 * KV_W])
    gate_ref[...] = _dot(n, w_ref[:, z0 + RWKV_IN:IN_W])
    tm = z.shape[0]
    if rows_per_seq is None:
        zs = pz_ref[...]
        zl_ref[...] = z
    else:
        prev_ref, = carry
        tiles_per_seq = rows_per_seq // tm

        @pl.when(pl.program_id(0) % tiles_per_seq == 0)
        def _():
            prev_ref[...] = pz_ref[0]

        row = lax.broadcasted_iota(jnp.int32, (tm, 1), 0)
        zs = jnp.where(row == 0, prev_ref[...], pltpu.roll(z, 1, 0))
        prev_ref[...] = z[tm - 1:tm, :]
        zl_ref[0] = z[tm - 1:tm, :]
    zz = z + (zs - z) * mu_ref[...]
    r = zz[:, 0:RWKV_W]
    k = zz[:, RWKV_W:2 * RWKV_W]
    v = zz[:, 2 * RWKV_W:3 * RWKV_W]
    o = 3 * RWKV_W
    wl = zz[:, o:o + D_DECAY_LORA]
    al = zz[:, o + D_DECAY_LORA:o + D_DECAY_LORA + D_AAA_LORA]
    gl = zz[:, o + D_DECAY_LORA + D_AAA_LORA:RWKV_IN]
    wpre = -(w0_ref[...] + _dot(jnp.tanh(wl).astype(BF), w2_ref[...]))
    softplus = jnp.maximum(wpre, 0.0) + jnp.log1p(jnp.exp(-jnp.abs(wpre)))
    lw = -jnp.exp(-softplus - 0.5)
    a = _sigmoid(a0_ref[...] + _dot(al.astype(BF), a2_ref[...]))
    g = _dot(_sigmoid(gl).astype(BF), g2_ref[...])
    kkv = k * kk_ref[...]
    ss = _dot_exact_rhs(kkv * kkv, ones_ref[...])
    kk = kkv / jnp.maximum(jnp.sqrt(ss), 1e-12)
    k2 = k * (1.0 + (a - 1.0) * ka_ref[...])
    r_ref[...], lw_ref[...], k2_ref[...], vr_ref[...] = r, lw, k2, v
    a_ref[...], b_ref[...], gr_ref[...] = -kk, kk * a, g


def _inproj(x, g, p, prev_z, tm, rows_per_seq):
    t = x.shape[0]
    row = lambda w_: pl.BlockSpec((tm, w_), lambda i: (i, 0))
    vec = lambda w_: _const_spec((1, w_))
    if rows_per_seq is None:
        pz_spec = row(RWKV_IN)
        zl_spec, zl_shape = row(RWKV_IN), (t, RWKV_IN)
        scratch = []
    else:
        tiles_per_seq = rows_per_seq // tm
        pz_spec = pl.BlockSpec((1, 1, RWKV_IN), lambda i: (i // tiles_per_seq, 0, 0))
        zl_spec, zl_shape = pz_spec, (t // rows_per_seq, 1, RWKV_IN)
        scratch = [pltpu.VMEM((1, RWKV_IN), F32)]
    return pl.pallas_call(
        functools.partial(_inproj_kernel, rows_per_seq),
        grid=(t // tm,),
        in_specs=[row(D_MODEL), vec(D_MODEL), _const_spec((D_MODEL, IN_W)), pz_spec,
                  vec(RWKV_IN), vec(RWKV_W), _const_spec((D_DECAY_LORA, RWKV_W)),
                  vec(RWKV_W), _const_spec((D_AAA_LORA, RWKV_W)), _const_spec((D_GATE_LORA, RWKV_W)),
                  vec(RWKV_W), vec(RWKV_W), _const_spec((RWKV_W, RWKV_W))],
        out_specs=[row(ATT_W), row(KV_W), row(KV_W), row(2 * D_MODEL), zl_spec] + [row(RWKV_W)] * 7,
        out_shape=[jax.ShapeDtypeStruct((t, ATT_W), BF),
                   jax.ShapeDtypeStruct((t, KV_W), F32),
                   jax.ShapeDtypeStruct((t, KV_W), F32),
                   jax.ShapeDtypeStruct((t, 2 * D_MODEL), F32),
                   jax.ShapeDtypeStruct(zl_shape, F32)] + [jax.ShapeDtypeStruct((t, RWKV_W), F32)] * 7,
        scratch_shapes=scratch,
        compiler_params=_params("arbitrary"),
        name="inproj",
    )(x, g, p["w_in"], prev_z, p["mu"], p["w0"], p["w2"], p["a0"], p["a2"], p["g2"], p["k_k"], p["k_a"],
      p["ones_bd"])


def _t5_bucket(dist):
    n = np.maximum(dist, 0)
    nf = np.maximum(n, 1).astype(np.float32)
    large = MAX_EXACT + (np.log(nf / MAX_EXACT) / math.log(MAX_DISTANCE / MAX_EXACT)
                         * (NUM_BUCKETS - MAX_EXACT)).astype(np.int32)
    return np.where(n < MAX_EXACT, n, np.minimum(large, NUM_BUCKETS - 1)).astype(np.int32)


def _bias_lookup(rel_bias, bucket):
    hit = bucket[..., None, None] == np.arange(NUM_BUCKETS, dtype=np.int32)[:, None]
    return jnp.sum(jnp.where(hit, rel_bias, 0.0), axis=-2)


def _prompt_bucket_table():
    dist = np.arange(WINDOW)[:, None] - (np.arange(2 * WINDOW) - WINDOW)[None, :]
    return np.where((dist >= 0) & (dist <= WINDOW), _t5_bucket(dist), -1).astype(np.int32)


def _swa_prompt_kernel(nb, bucket_ref, relb_ref, q_ref, kp_ref, kc_ref, vp_ref, vc_ref, sink_ref, o_ref, bias_ref):
    first = pl.program_id(0) == 0

    @pl.when(first)
    def _():
        bucket = bucket_ref[...]
        for head in range(N_Q_HEADS):
            tab = jnp.full(bucket.shape, NEG_INF, F32)
            for n in range(NUM_BUCKETS):
                tab = jnp.where(bucket == n, relb_ref[n, head], tab)
            g = head % GQA_GROUP
            bias_ref[head // GQA_GROUP, g * WINDOW:(g + 1) * WINDOW, :] = tab

    col = lax.broadcasted_iota(jnp.int32, (GQA_GROUP * WINDOW, 2 * WINDOW), 1)
    pad_mask = jnp.logical_and(first, col < WINDOW)
    chains = [(b, h) for b in range(nb) for h in range(N_KV_HEADS)]
    hs = lambda h: slice(h * HEAD_DIM, (h + 1) * HEAD_DIM)
    kh = [jnp.concatenate([kp_ref[b][:, hs(h)], kc_ref[b][:, hs(h)]], axis=0).astype(BF) for b, h in chains]
    vh = [jnp.concatenate([vp_ref[b][:, hs(h)], vc_ref[b][:, hs(h)]], axis=0).astype(BF) for b, h in chains]
    qh = [jnp.concatenate([q_ref[b][:, hs(GQA_GROUP * h + g)] for g in range(GQA_GROUP)], axis=0)
          for b, h in chains]
    s = [_dot_nt(qh[n], kh[n]) * ATT_SCALE + bias_ref[h] for n, (b, h) in enumerate(chains)]
    s = [jnp.where(pad_mask, NEG_INF, x) for x in s]
    m = [jnp.maximum(jnp.max(x, axis=-1, keepdims=True), sink_ref[h]) for x, (b, h) in zip(s, chains)]
    p = [jnp.exp(x - mm) for x, mm in zip(s, m)]
    denom = [jnp.sum(x, axis=-1, keepdims=True) + jnp.exp(sink_ref[h] - mm)
             for x, mm, (b, h) in zip(p, m, chains)]
    o = [_dot(x.astype(BF), vv) / d for x, vv, d in zip(p, vh, denom)]
    for b in range(nb):
        pieces = [o[b * N_KV_HEADS + h][g * WINDOW:(g + 1) * WINDOW]
                  for h in range(N_KV_HEADS) for g in range(GQA_GROUP)]
        o_ref[b] = jnp.concatenate(pieces, axis=-1).astype(o_ref.dtype)


def _swa_prompt(q3, k3, v3, rel_bias, sink_col):
    batch, seq, _ = q3.shape
    cur = lambda i: (0, i, 0)
    prev = lambda i: (0, jnp.maximum(i - 1, 0), 0)
    kv_c = pl.BlockSpec((batch, WINDOW, KV_W), cur)
    kv_p = pl.BlockSpec((batch, WINDOW, KV_W), prev)
    bucket = jnp.asarray(_prompt_bucket_table())
    return pl.pallas_call(
        functools.partial(_swa_prompt_kernel, batch),
        grid=(seq // WINDOW,),
        in_specs=[_const_spec(bucket.shape), pl.BlockSpec(memory_space=pltpu.SMEM),
                  pl.BlockSpec((batch, WINDOW, ATT_W), cur), kv_p, kv_c, kv_p, kv_c,
                  _const_spec(sink_col.shape)],
        out_specs=pl.BlockSpec((batch, WINDOW, ATT_W), cur),
        out_shape=jax.ShapeDtypeStruct((batch, seq, ATT_W), BF),
        scratch_shapes=[pltpu.VMEM((N_KV_HEADS, GQA_GROUP * WINDOW, 2 * WINDOW), F32)],
        compiler_params=_params("arbitrary"),
        name="swa_prompt",
    )(bucket, rel_bias, q3, k3, k3, v3, v3, sink_col)


def _swa_sample_kernel(q_ref, kn_ref, vn_ref, ck_ref, cv_ref, bc_ref, bn_ref, sink_ref, o_ref):
    q = q_ref[...]
    qf = q.astype(F32)
    kn = kn_ref[...].astype(BF).astype(F32)
    vn = vn_ref[...].astype(BF).astype(F32)
    ck, cv = ck_ref[...], cv_ref[...]
    head = lax.broadcasted_iota(jnp.int32, (1, N_Q_HEADS, 1), 1)
    low = head < GQA_GROUP
    s_h, sn_h = [], []
    for h in range(N_KV_HEADS):
        sl = slice(h * HEAD_DIM, (h + 1) * HEAD_DIM)
        s_h.append(jnp.einsum("bgd,bkd->bgk", q, ck[:, :, sl].astype(BF), preferred_element_type=F32))
        sn_h.append(jnp.sum(qf * kn[:, None, sl], axis=-1, keepdims=True))
    s = jnp.where(low, s_h[0], s_h[1]) * ATT_SCALE + bc_ref[...]
    sn = jnp.where(low, sn_h[0], sn_h[1]) * ATT_SCALE + bn_ref[...]
    sink = sink_ref[...]
    m = jnp.maximum(jnp.maximum(jnp.max(s, axis=-1, keepdims=True), sn), sink)
    p = jnp.exp(s - m)
    pn = jnp.exp(sn - m)
    denom = jnp.sum(p, axis=-1, keepdims=True) + pn + jnp.exp(sink - m)
    pb = p.astype(BF)
    pnb = pn.astype(BF).astype(F32)
    o_h = []
    for h in range(N_KV_HEADS):
        sl = slice(h * HEAD_DIM, (h + 1) * HEAD_DIM)
        o = jnp.einsum("bgk,bkd->bgd", pb, cv[:, :, sl].astype(BF), preferred_element_type=F32)
        o_h.append(o + pnb * vn[:, None, sl])
    o_ref[...] = (jnp.where(low, o_h[0], o_h[1]) / denom).astype(o_ref.dtype)


def _swa_sample(q3, kn, vn, ck, cv, bias_c, bias_n, sink3, bb):
    b = q3.shape[0]
    return pl.pallas_call(
        _swa_sample_kernel,
        grid=(b // bb,),
        in_specs=[pl.BlockSpec((bb, N_Q_HEADS, HEAD_DIM), lambda i: (i, 0, 0)),
                  pl.BlockSpec((bb, KV_W), lambda i: (i, 0)),
                  pl.BlockSpec((bb, KV_W), lambda i: (i, 0)),
                  pl.BlockSpec((bb, WINDOW, KV_W), lambda i: (i, 0, 0)),
                  pl.BlockSpec((bb, WINDOW, KV_W), lambda i: (i, 0, 0)),
                  _const_spec(bias_c.shape), _const_spec(bias_n.shape), _const_spec(sink3.shape)],
        out_specs=pl.BlockSpec((bb, N_Q_HEADS, HEAD_DIM), lambda i: (i, 0, 0)),
        out_shape=jax.ShapeDtypeStruct((b, N_Q_HEADS, HEAD_DIM), BF),
        compiler_params=_params("parallel"),
        name="swa_sample",
    )(q3, kn, vn, ck, cv, bias_c, bias_n, sink3)


def _rwkv_scan_kernel(nb, chunk, r_ref, lw_ref, k_ref, v_ref, a_ref, b_ref, g_ref, s0_ref,
                      rk_ref, lg_ref, lb_ref, ones_ref, y_ref, sout_ref, s_scr):
    c = pl.program_id(1)

    @pl.when(c == 0)
    def _():
        s_scr[...] = s0_ref[...]

    ri = lax.broadcasted_iota(jnp.int32, (chunk, chunk), 0)
    ci = lax.broadcasted_iota(jnp.int32, (chunk, chunk), 1)
    incl = ri >= ci
    strict = ri > ci
    tri = incl.astype(BF)
    eye = (ri == ci).astype(F32)
    n_sq = int(math.log2(chunk)) - 1

    chains = [(bi, h) for bi in range(nb) for h in range(RWKV_HEADS)]
    hs = lambda h: slice(h * RWKV_HEAD, (h + 1) * RWKV_HEAD)

    ra, aa, bt, kt, bh, kh, vb, g_all = [], [], [], [], [], [], [], []
    for bi in range(nb):
        r, lw, k, v = r_ref[bi], lw_ref[bi], k_ref[bi], v_ref[bi]
        a, b = a_ref[bi], b_ref[bi]
        hi, lo = _split2(lw)
        lo2 = (lw - hi.astype(F32) - lo.astype(F32)).astype(BF)
        cum = _dot(tri, hi) + _dot(tri, lo) + _dot(tri, lo2)
        tail = cum[chunk - 1:chunk, :]
        g_inv = jnp.exp(-cum)
        g_tail = jnp.exp(tail - cum)
        g_all.append(jnp.exp(tail))
        ra.append((r * jnp.exp(cum)).astype(BF))
        aa.append((a * jnp.exp(cum - lw)).astype(BF))
        bt.append((b * g_inv).astype(BF))
        kt.append((k * g_inv).astype(BF))
        bh.append((b * g_tail).astype(BF))
        kh.append((k * g_tail).astype(BF))
        vb.append(v.astype(BF))

    s_old = [s_scr[bi, h] for bi, h in chains]
    ar = [jnp.concatenate([aa[bi][:, hs(h)], ra[bi][:, hs(h)]], axis=0) for bi, h in chains]
    v_h = [vb[bi][:, hs(h)] for bi, h in chains]
    gb = [_dot_nt(ar[n], bt[bi][:, hs(h)]) for n, (bi, h) in enumerate(chains)]
    gk = [_dot_nt(ar[n], kt[bi][:, hs(h)]) for n, (bi, h) in enumerate(chains)]
    p = [_dot_nt(ar[n], s_old[n].astype(BF)) for n in range(len(chains))]
    l_ab = [jnp.where(strict, x[:chunk], 0.0) for x in gb]
    l_ak = [jnp.where(strict, x[:chunk], 0.0).astype(BF) for x in gk]
    m_rb = [jnp.where(incl, x[chunk:], 0.0).astype(BF) for x in gb]
    m_rk = [jnp.where(incl, x[chunk:], 0.0).astype(BF) for x in gk]
    rhs = [p[n][:chunk] + _dot(l_ak[n], v_h[n]) for n in range(len(chains))]
    t_inv = [eye + x for x in l_ab]
    lp = l_ab
    for _ in range(n_sq):
        lpb = [x.astype(BF) for x in lp]
        lp = [_dot(x, x) for x in lpb]
        t_inv = [t + _dot(x.astype(BF), t.astype(BF)) for x, t in zip(lp, t_inv)]
    ub = [_dot(t.astype(BF), x.astype(BF)).astype(BF) for t, x in zip(t_inv, rhs)]
    y_h = [p[n][chunk:] + _dot(m_rb[n], ub[n]) + _dot(m_rk[n], v_h[n]) for n in range(len(chains))]
    s_new = [s_old[n] * g_all[bi][:, hs(h)] + _dot_tn(ub[n], bh[bi][:, hs(h)]) + _dot_tn(v_h[n], kh[bi][:, hs(h)])
             for n, (bi, h) in enumerate(chains)]
    for n, (bi, h) in enumerate(chains):
        s_scr[bi, h] = s_new[n]

    for bi in range(nb):
        y = jnp.concatenate(y_h[bi * RWKV_HEADS:(bi + 1) * RWKV_HEADS], axis=-1)
        y_ref[bi] = _rwkv_epilogue(y, r_ref[bi], k_ref[bi], v_ref[bi], g_ref[bi],
                                   rk_ref, lg_ref, lb_ref, ones_ref).astype(y_ref.dtype)

    @pl.when(c == pl.num_programs(1) - 1)
    def _():
        sout_ref[...] = s_scr[...]


def _rwkv_epilogue(y, r, k, v, g, rk_ref, lg_ref, lb_ref, ones_ref):
    ones = ones_ref[...]
    inv_n = 1.0 / RWKV_HEAD
    mean = _dot_exact_rhs(y, ones) * inv_n
    d = y - mean
    var = _dot_exact_rhs(d * d, ones) * inv_n
    yn = d * lax.rsqrt(var + GN_EPS) * lg_ref[...] + lb_ref[...]
    bonus = _dot_exact_rhs(r * k * rk_ref[...], ones) * v
    return (yn + bonus) * g


def _rwkv_scan(coef, s0, p, nb, chunk):
    r = coef[0]
    b, t, _ = r.shape
    seq_spec = pl.BlockSpec((nb, chunk, RWKV_W), lambda bi, c: (bi, c, 0))
    st_spec = pl.BlockSpec((nb, RWKV_HEADS, RWKV_HEAD, RWKV_HEAD), lambda bi, c: (bi, 0, 0, 0))
    vec = _const_spec((1, RWKV_W))
    return pl.pallas_call(
        functools.partial(_rwkv_scan_kernel, nb, chunk),
        grid=(b // nb, t // chunk),
        in_specs=[seq_spec] * 7 + [st_spec, vec, vec, vec, _const_spec((RWKV_W, RWKV_W))],
        out_specs=[seq_spec, st_spec],
        out_shape=[jax.ShapeDtypeStruct((b, t, RWKV_W), BF),
                   jax.ShapeDtypeStruct((b, RWKV_HEADS, RWKV_HEAD, RWKV_HEAD), F32)],
        scratch_shapes=[pltpu.VMEM((nb, RWKV_HEADS, RWKV_HEAD, RWKV_HEAD), F32)],
        compiler_params=_params("parallel", "arbitrary"),
        name="rwkv_scan",
    )(*coef, s0, p["r_k"], p["lnx_g"], p["lnx_b"], p["ones_bd"])


def _rwkv_step_kernel(r_ref, lw_ref, k_ref, v_ref, a_ref, b_ref, g_ref, s_ref,
                      rk_ref, lg_ref, lb_ref, ones_ref, y_ref, sout_ref, y_scr):
    n_pair = LANES // RWKV_HEAD
    r, k, v = r_ref[...], k_ref[...], v_ref[...]
    rT, kT, vT = r.T, k.T, v.T
    aT, bT, wT = a_ref[...].T, b_ref[...].T, jnp.exp(lw_ref[...]).T
    for hl in range(LANES // RWKV_HEAD):
        hsl = slice(hl * RWKV_HEAD, (hl + 1) * RWKV_HEAD)
        a_h, b_h, k_h, w_h, r_h = aT[hsl], bT[hsl], kT[hsl], wT[hsl], rT[hsl]
        tiles = [(hl * RWKV_HEAD + n_pair * t) * RWKV_HEAD for t in range(RWKV_HEAD // n_pair)]
        st = [s_ref[:, c0:c0 + LANES].T for c0 in tiles]
        new = []
        for t, x in enumerate(st):
            halves = []
            for il in range(n_pair):
                i = hl * RWKV_HEAD + n_pair * t + il
                slab = x[il * RWKV_HEAD:(il + 1) * RWKV_HEAD]
                sa = jnp.sum(slab * a_h, axis=0, keepdims=True)
                slab = slab * w_h + sa * b_h + vT[i:i + 1] * k_h
                y_scr[i:i + 1, :] = jnp.sum(slab * r_h, axis=0, keepdims=True)
                halves.append(slab)
            new.append(jnp.concatenate(halves, axis=0))
        for c0, x in zip(tiles, new):
            sout_ref[:, c0:c0 + LANES] = x.T
    y_ref[...] = _rwkv_epilogue(y_scr[...].T, r, k, v, g_ref[...],
                                rk_ref, lg_ref, lb_ref, ones_ref).astype(y_ref.dtype)


def _rwkv_step(coef, s0, p):
    b = s0.shape[0]
    per_pair = (LANES // RWKV_HEAD) * RWKV_HEAD * RWKV_HEAD
    n_steps = RWKV_W // LANES
    col = pl.BlockSpec((b, LANES), lambda i: (0, i))
    vec = pl.BlockSpec((1, LANES), lambda i: (0, i))
    st_spec = pl.BlockSpec((b, per_pair), lambda i: (0, i))
    y, s_new = pl.pallas_call(
        _rwkv_step_kernel,
        grid=(n_steps,),
        in_specs=[col] * 7 + [st_spec, vec, vec, vec, _const_spec((LANES, LANES))],
        out_specs=[col, st_spec],
        out_shape=[jax.ShapeDtypeStruct((b, RWKV_W), BF),
                   jax.ShapeDtypeStruct((b, n_steps * per_pair), F32)],
        scratch_shapes=[pltpu.VMEM((LANES, b), F32)],
        compiler_params=_params("parallel"),
        name="rwkv_step",
    )(*coef, s0.reshape(b, n_steps * per_pair), p["r_k"], p["lnx_g"], p["lnx_b"],
      p["ones_bd"][:LANES, :LANES])
    return y, s_new.reshape(s0.shape)


def _merge_kernel(with_router, x_ref, ya_ref, yb_ref, gate_ref, wa_ref, wb_ref, wo_ref, g_ref, *rest):
    if with_router:
        wr_ref, cnt_in_ref, x1_ref, h_ref, route_ref, cnt_out_ref, cnt_scr = rest
    else:
        x1_ref, h_ref = rest
    gate = gate_ref[...]
    pa = _dot(ya_ref[...], wa_ref[...])
    pb = _dot(yb_ref[...], wb_ref[...])
    merged = _sigmoid(gate[:, :D_MODEL]) * pa + _sigmoid(gate[:, D_MODEL:]) * pb
    x1 = x_ref[...] + _dot(merged.astype(BF), wo_ref[...])
    x1_ref[...] = x1
    h = _rms(x1, g_ref[...])
    if not with_router:
        h_ref[...] = h.astype(h_ref.dtype)
        return
    for s in range(D_MODEL // LANES):
        h_ref[:, s, :] = h[:, s * LANES:(s + 1) * LANES]

    @pl.when(pl.program_id(0) == 0)
    def _():
        cnt_scr[...] = cnt_in_ref[...]

    logits = _dot(h.astype(BF), wr_ref[...])
    tm = logits.shape[0]
    lane = lax.broadcasted_iota(jnp.int32, logits.shape, 1).astype(F32)
    logits = jnp.where(lane < N_EXPERTS, logits, -jnp.inf)
    m1 = jnp.max(logits, axis=-1, keepdims=True)
    i1 = jnp.min(jnp.where(logits == m1, lane, float(LANES)), axis=-1, keepdims=True)
    rest_l = jnp.where(lane == i1, -jnp.inf, logits)
    m2 = jnp.max(rest_l, axis=-1, keepdims=True)
    i2 = jnp.min(jnp.where(rest_l == m2, lane, float(LANES)), axis=-1, keepdims=True)
    e2 = jnp.exp(m2 - m1)
    g1 = 1.0 / (1.0 + e2)
    g2 = e2 / (1.0 + e2)
    oh1 = (lane == i1).astype(F32)
    oh2 = (lane == i2).astype(F32)
    both = oh1 + oh2
    ri = lax.broadcasted_iota(jnp.int32, (tm, tm), 0)
    ci = lax.broadcasted_iota(jnp.int32, (tm, tm), 1)
    before = _dot((ri > ci).astype(BF), both.astype(BF)) + cnt_scr[...]
    rank1 = jnp.sum(oh1 * before, axis=-1, keepdims=True)
    rank2 = jnp.sum(oh2 * before, axis=-1, keepdims=True)
    cnt_scr[...] += jnp.sum(both, axis=0, keepdims=True)
    cnt_out_ref[...] = cnt_scr[...]
    route = jnp.zeros_like(logits)
    for n, val in enumerate((i1, i2, g1, g2, rank1, rank2)):
        route = jnp.where(lane == n, val, route)
    route_ref[...] = route


def _merge_out(x, ya, yb, gate, wa, wb, wo, g, tm, router=None, cnt_in=None):
    t = x.shape[0]
    row = lambda w_: pl.BlockSpec((tm, w_), lambda i: (i, 0))
    in_specs = [row(D_MODEL), row(ATT_W), row(RWKV_W), row(2 * D_MODEL),
                _const_spec((ATT_W, D_MODEL)), _const_spec((RWKV_W, D_MODEL)),
                _const_spec((D_MODEL, D_MODEL)), _const_spec((1, D_MODEL))]
    args = [x, ya, yb, gate, wa, wb, wo, g]
    if router is None:
        out_specs = [row(D_MODEL), row(D_MODEL)]
        out_shape = [jax.ShapeDtypeStruct((t, D_MODEL), F32), jax.ShapeDtypeStruct((t, D_MODEL), BF)]
        scratch = []
    else:
        n_sub = D_MODEL // LANES
        in_specs += [_const_spec(router.shape), _const_spec((1, LANES))]
        args += [router, cnt_in]
        out_specs = [row(D_MODEL), pl.BlockSpec((tm, n_sub, LANES), lambda i: (i, 0, 0)), row(LANES),
                     pl.BlockSpec((1, LANES), lambda i: (0, 0))]
        out_shape = [jax.ShapeDtypeStruct((t, D_MODEL), F32), jax.ShapeDtypeStruct((t, n_sub, LANES), F32),
                     jax.ShapeDtypeStruct((t, LANES), F32), jax.ShapeDtypeStruct((1, LANES), F32)]
        scratch = [pltpu.VMEM((1, LANES), F32)]
    return pl.pallas_call(
        functools.partial(_merge_kernel, router is not None),
        grid=(t // tm,),
        in_specs=in_specs, out_specs=out_specs, out_shape=out_shape, scratch_shapes=scratch,
        compiler_params=_params("parallel" if router is None else "arbitrary"),
        name="merge_out",
    )(*args)


def _swiglu(x, wg, wu, wd):
    s = _dot(x, wg)
    u = _dot(x, wu)
    act = (s * _sigmoid(s) * u).astype(BF)
    return _dot(act, wd)


def _dense_ffn_kernel(x1_ref, h_ref, wg_ref, wu_ref, wd_ref, o_ref):
    o_ref[...] = x1_ref[...] + _swiglu(h_ref[...], wg_ref[...], wu_ref[...], wd_ref[...])


def _dense_ffn(x1, h, wg, wu, wd, tm):
    t = x1.shape[0]
    d_ff = wg.shape[1]
    row = lambda: pl.BlockSpec((tm, D_MODEL), lambda i: (i, 0))
    return pl.pallas_call(
        _dense_ffn_kernel,
        grid=(t // tm,),
        in_specs=[row(), row(), _const_spec((D_MODEL, d_ff)), _const_spec((D_MODEL, d_ff)),
                  _const_spec((d_ff, D_MODEL))],
        out_specs=row(),
        out_shape=jax.ShapeDtypeStruct((t, D_MODEL), F32),
        compiler_params=_params("parallel"),
        name="dense_ffn",
    )(x1, h, wg, wu, wd)


def _rows_as_matrix(ref3):
    return jnp.concatenate([ref3[:, s, :] for s in range(ref3.shape[1])], axis=-1)


def _dispatch_kernel(tt, first, dest_ref, fill_ref, h_ref, *rest):
    if first:
        xs_hbm, zbuf, sem, zsem = rest
    else:
        _, xs_hbm, sem = rest
    i = pl.program_id(0)
    base = i * tt

    if first:
        @pl.when(i == 0)
        def _():
            zbuf[...] = jnp.zeros_like(zbuf)
            bm = zbuf.shape[0]
            fills = [pltpu.make_async_copy(zbuf, xs_hbm.at[pl.ds(fill_ref[n] * bm, bm)], zsem)
                     for n in range(fill_ref.shape[0])]
            for f in fills:
                f.start()
            for f in fills:
                f.wait()

    def issue(r, carry):
        for j in range(TOP_K):
            pltpu.make_async_copy(h_ref.at[r], xs_hbm.at[dest_ref[TOP_K * (base + r) + j]], sem).start()
        return carry

    lax.fori_loop(0, tt, issue, 0, unroll=8)
    for j in range(TOP_K):
        pltpu.make_async_copy(h_ref, xs_hbm.at[pl.ds(0, tt)], sem).wait()


def _dispatch(dest, fill_blocks, h3, xs, tt, bm, n_rows):
    t, n_sub, _ = h3.shape
    first = xs is None
    in_specs = [pl.BlockSpec((tt, n_sub, LANES), lambda i, ds, fb: (i, 0, 0))]
    args = [dest, fill_blocks, h3]
    if first:
        scratch = [pltpu.VMEM((bm, n_sub, LANES), F32), pltpu.SemaphoreType.DMA(()), pltpu.SemaphoreType.DMA(())]
        aliases = {}
    else:
        in_specs.append(pl.BlockSpec(memory_space=pl.ANY))
        args.append(xs)
        scratch = [pltpu.SemaphoreType.DMA(())]
        aliases = {3: 0}
    grid_spec = pltpu.PrefetchScalarGridSpec(
        num_scalar_prefetch=2, grid=(t // tt,), in_specs=in_specs,
        out_specs=pl.BlockSpec(memory_space=pl.ANY), scratch_shapes=scratch)
    return pl.pallas_call(
        functools.partial(_dispatch_kernel, tt, first),
        grid_spec=grid_spec,
        out_shape=jax.ShapeDtypeStruct((n_rows, n_sub, LANES), F32),
        input_output_aliases=aliases,
        compiler_params=_params("arbitrary"),
        name="moe_dispatch",
    )(*args)


def _moe_kernel(n_half, blk_e_ref, n_used_ref, xs_ref, wg_ref, wu_ref, wd_ref, y_ref):
    @pl.when(pl.program_id(0) < n_used_ref[0])
    def _():
        x = _rows_as_matrix(xs_ref).astype(BF)
        d_e = wg_ref.shape[2]
        step = d_e // n_half
        acc = None
        for j in range(n_half):
            cs = slice(j * step, (j + 1) * step)
            part = _swiglu(x, wg_ref[0, :, cs], wu_ref[0, :, cs], wd_ref[0, cs, :])
            acc = part if acc is None else acc + part
        y_ref[...] = acc

    @pl.when(pl.program_id(0) >= n_used_ref[0])
    def _():
        y_ref[...] = jnp.zeros_like(y_ref)


def _moe_ffn(xs, blk_e, n_used, wg, wu, wd, bm):
    n_blk = blk_e.shape[0]
    n_rows, n_sub, _ = xs.shape
    d_e = wg.shape[2]
    wspec = lambda shape: pl.BlockSpec(shape, lambda i, be, nu: (be[i], 0, 0), pipeline_mode=pl.Buffered(1))
    rows = pl.BlockSpec((bm, n_sub, LANES), lambda i, be, nu: (i, 0, 0))
    grid_spec = pltpu.PrefetchScalarGridSpec(
        num_scalar_prefetch=2,
        grid=(n_blk,),
        in_specs=[rows, wspec((1, D_MODEL, d_e)), wspec((1, D_MODEL, d_e)), wspec((1, d_e, D_MODEL))],
        out_specs=pl.BlockSpec((bm, D_MODEL), lambda i, be, nu: (i, 0)),
    )
    return pl.pallas_call(
        functools.partial(_moe_kernel, 2),
        grid_spec=grid_spec,
        out_shape=jax.ShapeDtypeStruct((n_rows, D_MODEL), F32),
        compiler_params=_params("arbitrary"),
        name="moe_ffn",
    )(blk_e, n_used, xs, wg, wu, wd)


def _row_layout(counts, bm, n_blk):
    per_e = (counts + bm - 1) // bm
    ends = jnp.cumsum(per_e)
    n_used = ends[-1]
    first_row = (ends - per_e) * bm
    i = jnp.arange(n_blk, dtype=jnp.int32)
    blk_e = jnp.sum((jnp.minimum(i, jnp.maximum(n_used - 1, 0))[:, None] >= ends[None, :]).astype(jnp.int32), axis=1)
    need = jnp.any(i[:, None] == (ends - 1)[None, :], axis=1) | (i >= n_blk - N_EXPERTS)
    fill_blocks = jnp.argsort(jnp.logical_not(need), stable=True)[:2 * N_EXPERTS].astype(jnp.int32)
    return first_row.astype(jnp.int32), blk_e.astype(jnp.int32), n_used.astype(jnp.int32).reshape(1), fill_blocks


def _combine_kernel(tc, dest_ref, x1_ref, route_ref, y_hbm, g_ref, o_ref, ybuf, sem):
    base = pl.program_id(0) * tc

    def issue(r, carry):
        for j in range(TOP_K):
            row = dest_ref[TOP_K * (base + r) + j]
            pltpu.make_async_copy(y_hbm.at[pl.ds(row, 1), :], ybuf.at[j, pl.ds(r, 1), :], sem).start()
        return carry

    lax.fori_loop(0, tc, issue, 0, unroll=4)
    for j in range(TOP_K):
        pltpu.make_async_copy(y_hbm.at[pl.ds(0, tc), :], ybuf.at[j], sem).wait()
    route = route_ref[...]
    x2 = x1_ref[...] + route[:, 2:3] * ybuf[0] + route[:, 3:4] * ybuf[1]
    o_ref[...] = _rms(x2, g_ref[...])


def _moe_combine(dest, x1, route, y, g, tc):
    t = x1.shape[0]
    grid_spec = pltpu.PrefetchScalarGridSpec(
        num_scalar_prefetch=1,
        grid=(t // tc,),
        in_specs=[pl.BlockSpec((tc, D_MODEL), lambda i, ds: (i, 0)),
                  pl.BlockSpec((tc, LANES), lambda i, ds: (i, 0)),
                  pl.BlockSpec(memory_space=pl.ANY),
                  pl.BlockSpec((1, D_MODEL), lambda i, ds: (0, 0))],
        out_specs=pl.BlockSpec((tc, D_MODEL), lambda i, ds: (i, 0)),
        scratch_shapes=[pltpu.VMEM((TOP_K, tc, D_MODEL), F32), pltpu.SemaphoreType.DMA(())],
    )
    return pl.pallas_call(
        functools.partial(_combine_kernel, tc),
        grid_spec=grid_spec,
        out_shape=jax.ShapeDtypeStruct((t, D_MODEL), F32),
        compiler_params=_params("arbitrary"),
        name="moe_combine",
    )(dest, x1, route, y, g)


def _layer_params(l, w_in, mu_shift, w0, w2, a0, a2, g2, k_k, k_a, r_k, lnx_g, lnx_b,
                  w_proj_attn, w_proj_rwkv, w_out, ones_bd):
    row = lambda u: u.reshape(1, -1)
    return dict(
        w_in=w_in[l].astype(BF), mu=row(mu_shift[l]), w0=row(w0[l]), w2=w2[l].astype(BF),
        a0=row(a0[l]), a2=a2[l].astype(BF), g2=g2[l].astype(BF), k_k=row(k_k[l]), k_a=row(k_a[l]),
        r_k=row(r_k[l]), lnx_g=row(lnx_g[l]), lnx_b=row(lnx_b[l]),
        wa=w_proj_attn[l].astype(BF), wb=w_proj_rwkv[l].astype(BF), wo=w_out[l].astype(BF),
        ones_bd=ones_bd)


def kernel(x_prompt, x_sample, cache_k, cache_v, state_wkv, state_shift, norm_mix_g, w_in, attn_sinks, rel_bias, mu_shift, w0, w2, a0, a2, g2, k_k, k_a, r_k, lnx_g, lnx_b, w_proj_attn, w_proj_rwkv, w_out, norm_ffn_g, dense_w_gate, dense_w_up, dense_w_down, router_w, moe_w_gate, moe_w_up, moe_w_down, norm_final_g):
    batch, seq, _ = x_prompt.shape
    dec_batch, dec_seq, _ = x_sample.shape
    assert dec_seq == 1 and seq % WINDOW == 0

    head_id = jnp.arange(RWKV_W, dtype=jnp.int32) // RWKV_HEAD
    ones_bd = (head_id[:, None] == head_id[None, :]).astype(BF)
    bias_by_dist = _bias_lookup(rel_bias, _t5_bucket(np.arange(WINDOW + 1)))
    bias_c = jnp.transpose(bias_by_dist[WINDOW:0:-1], (1, 0))[None]
    bias_n = bias_by_dist[0][None, :, None]

    layers = [_layer_params(l, w_in, mu_shift, w0, w2, a0, a2, g2, k_k, k_a, r_k, lnx_g, lnx_b,
                            w_proj_attn, w_proj_rwkv, w_out, ones_bd) for l in range(DEPTH)]

    def router_pieces(l):
        return jnp.pad(router_w[l // 2], ((0, 0), (0, LANES - N_EXPERTS))).astype(BF)

    n_prompt = batch * seq
    tm_p = 512
    bm = 512
    n_all = n_prompt + dec_batch
    n_blk = -(-(n_all * TOP_K) // bm) + N_EXPERTS

    def mix(x, is_prompt, l, state):
        p = layers[l]
        n_tok = x.shape[0]
        gm = norm_mix_g[l].reshape(1, -1)
        sinks = attn_sinks[l]
        if is_prompt:
            b, t = batch, seq
            prev_z = jnp.zeros((b, 1, RWKV_IN), F32)
            q, k, v, gate, z_last, *coef = _inproj(x, gm, p, prev_z, 256, t)
            sink_col = jnp.repeat(sinks, WINDOW).reshape(N_KV_HEADS, GQA_GROUP * WINDOW, 1)
            ya = _swa_prompt(q.reshape(b, t, ATT_W), k.reshape(b, t, KV_W), v.reshape(b, t, KV_W),
                             rel_bias, sink_col).reshape(n_tok, ATT_W)
            k4 = k.reshape(b, t, N_KV_HEADS, HEAD_DIM)
            v4 = v.reshape(b, t, N_KV_HEADS, HEAD_DIM)
            state["k"].append(k4[:, t - WINDOW:])
            state["v"].append(v4[:, t - WINDOW:])
            s0 = jnp.zeros((b, RWKV_HEADS, RWKV_HEAD, RWKV_HEAD), F32)
            yb3, s_new = _rwkv_scan([u.reshape(b, t, RWKV_W) for u in coef], s0, p, b, 64)
            yb = yb3.reshape(n_tok, RWKV_W)
            state["z"].append(z_last.reshape(b, RWKV_IN))
        else:
            b = dec_batch
            q, k, v, gate, z, *coef = _inproj(x, gm, p, state_shift[l], n_tok, None)
            ck = cache_k[l].reshape(b, WINDOW, KV_W)
            cv = cache_v[l].reshape(b, WINDOW, KV_W)
            ya3 = _swa_sample(q.reshape(b, N_Q_HEADS, HEAD_DIM), k, v, ck, cv, bias_c, bias_n,
                              sinks.reshape(1, N_Q_HEADS, 1), 16)
            ya = ya3.reshape(b, ATT_W)
            state["k"].append(jnp.concatenate([ck[:, 1:], k[:, None]], axis=1)
                              .reshape(b, WINDOW, N_KV_HEADS, HEAD_DIM))
            state["v"].append(jnp.concatenate([cv[:, 1:], v[:, None]], axis=1)
                              .reshape(b, WINDOW, N_KV_HEADS, HEAD_DIM))
            yb, s_new = _rwkv_step(coef, state_wkv[l], p)
            state["z"].append(z)
        state["s"].append(s_new)
        return ya, yb, gate

    groups = [dict(x=x_prompt.reshape(n_prompt, D_MODEL), prompt=True, tm=tm_p, k=[], v=[], s=[], z=[]),
              dict(x=x_sample.reshape(dec_batch, D_MODEL), prompt=False, tm=dec_batch, k=[], v=[], s=[], z=[])]
    for l in range(DEPTH):
        p = layers[l]
        gn = norm_ffn_g[l].reshape(1, -1)
        j = l // 2
        if l % 2 == 0:
            wg, wu, wd = (_to_bf16(w[j], w.shape[1] // 4) for w in (dense_w_gate, dense_w_up, dense_w_down))
            for grp in groups:
                ya, yb, gate = mix(grp["x"], grp["prompt"], l, grp)
                x1, h = _merge_out(grp["x"], ya, yb, gate, p["wa"], p["wb"], p["wo"], gn, grp["tm"])
                grp["x"] = _dense_ffn(x1, h, wg, wu, wd, grp["tm"])
        else:
            assert l == DEPTH - 1
            wg, wu, wd = (w[j].astype(BF) for w in (moe_w_gate, moe_w_up, moe_w_down))
            router = router_pieces(l)
            cnt = jnp.zeros((1, LANES), F32)
            xs = None
            for grp in groups:
                ya, yb, gate = mix(grp["x"], grp["prompt"], l, grp)
                x1, h3, route, cnt = _merge_out(grp["x"], ya, yb, gate, p["wa"], p["wb"], p["wo"], gn,
                                                min(256, grp["tm"]), router=router, cnt_in=cnt)
                grp["x1"], grp["route"], grp["h3"] = x1, route, h3
            counts = cnt[0, :N_EXPERTS].astype(jnp.int32)
            first_row, blk_e, n_used, fill_blocks = _row_layout(counts, bm, n_blk)
            for grp in groups:
                route = grp["route"]
                expert = route[:, 0:TOP_K].astype(jnp.int32)
                rank = route[:, 2 * TOP_K:3 * TOP_K].astype(jnp.int32)
                start = jnp.sum(jnp.where(expert[..., None] == jnp.arange(N_EXPERTS, dtype=jnp.int32),
                                          first_row, 0), axis=-1)
                grp["dest"] = (start + rank).reshape(-1)
                xs = _dispatch(grp["dest"], fill_blocks, grp["h3"], xs, min(2048, grp["h3"].shape[0]), bm,
                               n_blk * bm)
            y = _moe_ffn(xs, blk_e, n_used, wg, wu, wd, bm)
            for grp in groups:
                grp["x"] = _moe_combine(grp["dest"], grp["x1"], grp["route"], y,
                                        norm_final_g.reshape(1, -1), min(1024, grp["x1"].shape[0]))

    gp, gs = groups
    outs = []
    for grp, shape in ((gp, (batch, seq, D_MODEL)), (gs, (dec_batch, dec_seq, D_MODEL))):
        outs.append((grp["x"].reshape(shape), jnp.stack(grp["k"]), jnp.stack(grp["v"]),
                     jnp.stack(grp["s"]), jnp.stack(grp["z"])))
    (y_p, nk_p, nv_p, ns_p, nz_p), (y_s, nk_s, nv_s, ns_s, nz_s) = outs
    return (y_p, y_s, nk_p, nv_p, ns_p, nz_p, nk_s, nv_s, ns_s, nz_s)
```

```python
import functools
import math

import jax
import jax.numpy as jnp
import numpy as np
from jax import lax
from jax.experimental import pallas as pl
from jax.experimental.pallas import tpu as pltpu

BF = jnp.bfloat16
F32 = jnp.float32

D_MODEL = 1024
DEPTH = 2
HEAD_DIM = 64
N_Q_HEADS = 8
N_KV_HEADS = 2
GQA_GROUP = 4
ATT_W = 512
KV_W = 128
WINDOW = 128
ATT_SCALE = HEAD_DIM ** -0.5
NUM_BUCKETS = 32
MAX_EXACT = 16
MAX_DISTANCE = 128
NEG_INF = -1e30
RWKV_HEAD = 64
RWKV_W = 512
RWKV_HEADS = 8
D_DECAY_LORA = 64
D_AAA_LORA = 64
D_GATE_LORA = 128
RWKV_IN = 3 * RWKV_W + D_DECAY_LORA + D_AAA_LORA + D_GATE_LORA
GN_EPS = 64e-5
IN_W = ATT_W + 2 * KV_W + RWKV_IN + 2 * D_MODEL
N_EXPERTS = 8
TOP_K = 2
NORM_EPS = 1e-6

VMEM_LIMIT_BYTES = 56 * 1024 * 1024
LANES = 128


def _params(*sem):
    return pltpu.CompilerParams(dimension_semantics=sem, vmem_limit_bytes=VMEM_LIMIT_BYTES)


def _const_spec(shape):
    zeros = (0,) * len(shape)
    return pl.BlockSpec(shape, lambda *_: zeros, pipeline_mode=pl.Buffered(1))


def _dot(a, b):
    return jnp.dot(a, b, preferred_element_type=F32)


def _dot_nt(a, b):
    return lax.dot_general(a, b, (((1,), (1,)), ((), ())), preferred_element_type=F32)


def _dot_tn(a, b):
    return lax.dot_general(a, b, (((0,), (0,)), ((), ())), preferred_element_type=F32)


def _split2(x):
    hi = x.astype(BF)
    lo = (x - hi.astype(F32)).astype(BF)
    return hi, lo


def _dot_exact_rhs(x, w):
    hi, lo = _split2(x)
    return _dot(hi, w) + _dot(lo, w)


def _rms(x, g):
    ms = jnp.mean(x * x, axis=-1, keepdims=True)
    return x * lax.rsqrt(ms + NORM_EPS) * g


def _sigmoid(x):
    return 1.0 / (1.0 + jnp.exp(-x))


def _cast_kernel(x_ref, o_ref):
    o_ref[...] = x_ref[...].astype(o_ref.dtype)


def _to_bf16(w, row_tile):
    rows, cols = w.shape
    spec = pl.BlockSpec((row_tile, cols), lambda i: (i, 0))
    return pl.pallas_call(
        _cast_kernel, grid=(rows // row_tile,), in_specs=[spec], out_specs=spec,
        out_shape=jax.ShapeDtypeStruct((rows, cols), BF),
        compiler_params=_params("parallel"), name="to_bf16",
    )(w)


def _inproj_kernel(rows_per_seq, x_ref, g_ref, w_ref, pz_ref, mu_ref, w0_ref, w2_ref, a0_ref, a2_ref, g2_ref,
                   kk_ref, ka_ref, ones_ref,
                   q_ref, k_ref, v_ref, gate_ref, zl_ref,
                   r_ref, lw_ref, k2_ref, vr_ref, a_ref, b_ref, gr_ref, *carry):
    n = _rms(x_ref[...], g_ref[...]).astype(BF)
    z0 = ATT_W + 2 * KV_W
    z = _dot(n, w_ref[:, z0:z0 + RWKV_IN])
    q_ref[...] = _dot(n, w_ref[:, 0:ATT_W]).astype(BF)
    k_ref[...] = _dot(n, w_ref[:, ATT_W:ATT_W + KV_W])
    v_ref[...] = _dot(n, w_ref[:, ATT_W + KV_W:ATT_W + 2 * KV_W])
    gate_ref[...] = _dot(n, w_ref[:, z0 + RWKV_IN:IN_W])
    tm = z.shape[0]
    if rows_per_seq is None:
        zs = pz_ref[...]
        zl_ref[...] = z
    else:
        prev_ref, = carry
        tiles_per_seq = rows_per_seq // tm

        @pl.when(pl.program_id(0) % tiles_per_seq == 0)
        def _():
            prev_ref[...] = pz_ref[0]

        row = lax.broadcasted_iota(jnp.int32, (tm, 1), 0)
        zs = jnp.where(row == 0, prev_ref[...], pltpu.roll(z, 1, 0))
        prev_ref[...] = z[tm - 1:tm, :]
        zl_ref[0] = z[tm - 1:tm, :]
    zz = z + (zs - z) * mu_ref[...]
    r = zz[:, 0:RWKV_W]
    k = zz[:, RWKV_W:2 * RWKV_W]
    v = zz[:, 2 * RWKV_W:3 * RWKV_W]
    o = 3 * RWKV_W
    wl = zz[:, o:o + D_DECAY_LORA]
    al = zz[:, o + D_DECAY_LORA:o + D_DECAY_LORA + D_AAA_LORA]
    gl = zz[:, o + D_DECAY_LORA + D_AAA_LORA:RWKV_IN]
    wpre = -(w0_ref[...] + _dot(jnp.tanh(wl).astype(BF), w2_ref[...]))
    softplus = jnp.maximum(wpre, 0.0) + jnp.log1p(jnp.exp(-jnp.abs(wpre)))
    lw = -jnp.exp(-softplus - 0.5)
    a = _sigmoid(a0_ref[...] + _dot(al.astype(BF), a2_ref[...]))
    g = _dot(_sigmoid(gl).astype(BF), g2_ref[...])
    kkv = k * kk_ref[...]
    ss = _dot_exact_rhs(kkv * kkv, ones_ref[...])
    kk = kkv / jnp.maximum(jnp.sqrt(ss), 1e-12)
    k2 = k * (1.0 + (a - 1.0) * ka_ref[...])
    r_ref[...], lw_ref[...], k2_ref[...], vr_ref[...] = r, lw, k2, v
    a_ref[...], b_ref[...], gr_ref[...] = -kk, kk * a, g


def _inproj(x, g, p, prev_z, tm, rows_per_seq):
    t = x.shape[0]
    row = lambda w_: pl.BlockSpec((tm, w_), lambda i: (i, 0))
    vec = lambda w_: _const_spec((1, w_))
    if rows_per_seq is None:
        pz_spec = row(RWKV_IN)
        zl_spec, zl_shape = row(RWKV_IN), (t, RWKV_IN)
        scratch = []
    else:
        tiles_per_seq = rows_per_seq // tm
        pz_spec = pl.BlockSpec((1, 1, RWKV_IN), lambda i: (i // tiles_per_seq, 0, 0))
        zl_spec, zl_shape = pz_spec, (t // rows_per_seq, 1, RWKV_IN)
        scratch = [pltpu.VMEM((1, RWKV_IN), F32)]
    return pl.pallas_call(
        functools.partial(_inproj_kernel, rows_per_seq),
        grid=(t // tm,),
        in_specs=[row(D_MODEL), vec(D_MODEL), _const_spec((D_MODEL, IN_W)), pz_spec,
                  vec(RWKV_IN), vec(RWKV_W), _const_spec((D_DECAY_LORA, RWKV_W)),
                  vec(RWKV_W), _const_spec((D_AAA_LORA, RWKV_W)), _const_spec((D_GATE_LORA, RWKV_W)),
                  vec(RWKV_W), vec(RWKV_W), _const_spec((RWKV_W, RWKV_W))],
        out_specs=[row(ATT_W), row(KV_W), row(KV_W), row(2 * D_MODEL), zl_spec] + [row(RWKV_W)] * 7,
        out_shape=[jax.ShapeDtypeStruct((t, ATT_W), BF),
                   jax.ShapeDtypeStruct((t, KV_W), F32),
                   jax.ShapeDtypeStruct((t, KV_W), F32),
                   jax.ShapeDtypeStruct((t, 2 * D_MODEL), F32),
                   jax.ShapeDtypeStruct(zl_shape, F32)] + [jax.ShapeDtypeStruct((t, RWKV_W), F32)] * 7,
        scratch_shapes=scratch,
        compiler_params=_params("arbitrary"),
        name="inproj",
    )(x, g, p["w_in"], prev_z, p["mu"], p["w0"], p["w2"], p["a0"], p["a2"], p["g2"], p["k_k"], p["k_a"],
      p["ones_bd"])


def _t5_bucket(dist):
    n = np.maximum(dist, 0)
    nf = np.maximum(n, 1).astype(np.float32)
    large = MAX_EXACT + (np.log(nf / MAX_EXACT) / math.log(MAX_DISTANCE / MAX_EXACT)
                         * (NUM_BUCKETS - MAX_EXACT)).astype(np.int32)
    return np.where(n < MAX_EXACT, n, np.minimum(large, NUM_BUCKETS - 1)).astype(np.int32)


def _bias_lookup(rel_bias, bucket):
    hit = bucket[..., None, None] == np.arange(NUM_BUCKETS, dtype=np.int32)[:, None]
    return jnp.sum(jnp.where(hit, rel_bias, 0.0), axis=-2)


def _prompt_bucket_table():
    dist = np.arange(WINDOW)[:, None] - (np.arange(2 * WINDOW) - WINDOW)[None, :]
    return np.where((dist >= 0) & (dist <= WINDOW), _t5_bucket(dist), -1).astype(np.int32)


def _swa_prompt_kernel(nb, bucket_ref, relb_ref, q_ref, kp_ref, kc_ref, vp_ref, vc_ref, sink_ref, o_ref, bias_ref):
    first = pl.program_id(0) == 0

    @pl.when(first)
    def _():
        bucket = bucket_ref[...]
        for head in range(N_Q_HEADS):
            tab = jnp.full(bucket.shape, NEG_INF, F32)
            for n in range(NUM_BUCKETS):
                tab = jnp.where(bucket == n, relb_ref[n, head], tab)
            g = head % GQA_GROUP
            bias_ref[head // GQA_GROUP, g * WINDOW:(g + 1) * WINDOW, :] = tab

    col = lax.broadcasted_iota(jnp.int32, (GQA_GROUP * WINDOW, 2 * WINDOW), 1)
    pad_mask = jnp.logical_and(first, col < WINDOW)
    chains = [(b, h) for b in range(nb) for h in range(N_KV_HEADS)]
    hs = lambda h: slice(h * HEAD_DIM, (h + 1) * HEAD_DIM)
    kh = [jnp.concatenate([kp_ref[b][:, hs(h)], kc_ref[b][:, hs(h)]], axis=0).astype(BF) for b, h in chains]
    vh = [jnp.concatenate([vp_ref[b][:, hs(h)], vc_ref[b][:, hs(h)]], axis=0).astype(BF) for b, h in chains]
    qh = [jnp.concatenate([q_ref[b][:, hs(GQA_GROUP * h + g)] for g in range(GQA_GROUP)], axis=0)
          for b, h in chains]
    s = [_dot_nt(qh[n], kh[n]) * ATT_SCALE + bias_ref[h] for n, (b, h) in enumerate(chains)]
    s = [jnp.where(pad_mask, NEG_INF, x) for x in s]
    m = [jnp.maximum(jnp.max(x, axis=-1, keepdims=True), sink_ref[h]) for x, (b, h) in zip(s, chains)]
    p = [jnp.exp(x - mm) for x, mm in zip(s, m)]
    denom = [jnp.sum(x, axis=-1, keepdims=True) + jnp.exp(sink_ref[h] - mm)
             for x, mm, (b, h) in zip(p, m, chains)]
    o = [_dot(x.astype(BF), vv) / d for x, vv, d in zip(p, vh, denom)]
    for b in range(nb):
        pieces = [o[b * N_KV_HEADS + h][g * WINDOW:(g + 1) * WINDOW]
                  for h in range(N_KV_HEADS) for g in range(GQA_GROUP)]
        o_ref[b] = jnp.concatenate(pieces, axis=-1).astype(o_ref.dtype)


def _swa_prompt(q3, k3, v3, rel_bias, sink_col):
    batch, seq, _ = q3.shape
    cur = lambda i: (0, i, 0)
    prev = lambda i: (0, jnp.maximum(i - 1, 0), 0)
    kv_c = pl.BlockSpec((batch, WINDOW, KV_W), cur)
    kv_p = pl.BlockSpec((batch, WINDOW, KV_W), prev)
    bucket = jnp.asarray(_prompt_bucket_table())
    return pl.pallas_call(
        functools.partial(_swa_prompt_kernel, batch),
        grid=(seq // WINDOW,),
        in_specs=[_const_spec(bucket.shape), pl.BlockSpec(memory_space=pltpu.SMEM),
                  pl.BlockSpec((batch, WINDOW, ATT_W), cur), kv_p, kv_c, kv_p, kv_c,
                  _const_spec(sink_col.shape)],
        out_specs=pl.BlockSpec((batch, WINDOW, ATT_W), cur),
        out_shape=jax.ShapeDtypeStruct((batch, seq, ATT_W), BF),
        scratch_shapes=[pltpu.VMEM((N_KV_HEADS, GQA_GROUP * WINDOW, 2 * WINDOW), F32)],
        compiler_params=_params("arbitrary"),
        name="swa_prompt",
    )(bucket, rel_bias, q3, k3, k3, v3, v3, sink_col)


def _swa_sample_kernel(q_ref, kn_ref, vn_ref, ck_ref, cv_ref, bc_ref, bn_ref, sink_ref, o_ref):
    q = q_ref[...]
    qf = q.astype(F32)
    kn = kn_ref[...].astype(BF).astype(F32)
    vn = vn_ref[...].astype(BF).astype(F32)
    ck, cv = ck_ref[...], cv_ref[...]
    head = lax.broadcasted_iota(jnp.int32, (1, N_Q_HEADS, 1), 1)
    low = head < GQA_GROUP
    s_h, sn_h = [], []
    for h in range(N_KV_HEADS):
        sl = slice(h * HEAD_DIM, (h + 1) * HEAD_DIM)
        s_h.append(jnp.einsum("bgd,bkd->bgk", q, ck[:, :, sl].astype(BF), preferred_element_type=F32))
        sn_h.append(jnp.sum(qf * kn[:, None, sl], axis=-1, keepdims=True))
    s = jnp.where(low, s_h[0], s_h[1]) * ATT_SCALE + bc_ref[...]
    sn = jnp.where(low, sn_h[0], sn_h[1]) * ATT_SCALE + bn_ref[...]
    sink = sink_ref[...]
    m = jnp.maximum(jnp.maximum(jnp.max(s, axis=-1, keepdims=True), sn), sink)
    p = jnp.exp(s - m)
    pn = jnp.exp(sn - m)
    denom = jnp.sum(p, axis=-1, keepdims=True) + pn + jnp.exp(sink - m)
    pb = p.astype(BF)
    pnb = pn.astype(BF).astype(F32)
    o_h = []
    for h in range(N_KV_HEADS):
        sl = slice(h * HEAD_DIM, (h + 1) * HEAD_DIM)
        o = jnp.einsum("bgk,bkd->bgd", pb, cv[:, :, sl].astype(BF), preferred_element_type=F32)
        o_h.append(o + pnb * vn[:, None, sl])
    o_ref[...] = (jnp.where(low, o_h[0], o_h[1]) / denom).astype(o_ref.dtype)


def _swa_sample(q3, kn, vn, ck, cv, bias_c, bias_n, sink3, bb):
    b = q3.shape[0]
    return pl.pallas_call(
        _swa_sample_kernel,
        grid=(b // bb,),
        in_specs=[pl.BlockSpec((bb, N_Q_HEADS, HEAD_DIM), lambda i: (i, 0, 0)),
                  pl.BlockSpec((bb, KV_W), lambda i: (i, 0)),
                  pl.BlockSpec((bb, KV_W), lambda i: (i, 0)),
                  pl.BlockSpec((bb, WINDOW, KV_W), lambda i: (i, 0, 0)),
                  pl.BlockSpec((bb, WINDOW, KV_W), lambda i: (i, 0, 0)),
                  _const_spec(bias_c.shape), _const_spec(bias_n.shape), _const_spec(sink3.shape)],
        out_specs=pl.BlockSpec((bb, N_Q_HEADS, HEAD_DIM), lambda i: (i, 0, 0)),
        out_shape=jax.ShapeDtypeStruct((b, N_Q_HEADS, HEAD_DIM), BF),
        compiler_params=_params("parallel"),
        name="swa_sample",
    )(q3, kn, vn, ck, cv, bias_c, bias_n, sink3)


def _rwkv_scan_kernel(nb, chunk, r_ref, lw_ref, k_ref, v_ref, a_ref, b_ref, g_ref, s0_ref,
                      rk_ref, lg_ref, lb_ref, ones_ref, y_ref, sout_ref, s_scr):
    c = pl.program_id(1)

    @pl.when(c == 0)
    def _():
        s_scr[...] = s0_ref[...]

    ri = lax.broadcasted_iota(jnp.int32, (chunk, chunk), 0)
    ci = lax.broadcasted_iota(jnp.int32, (chunk, chunk), 1)
    incl = ri >= ci
    strict = ri > ci
    tri = incl.astype(BF)
    eye = (ri == ci).astype(F32)
    n_sq = int(math.log2(chunk)) - 1

    chains = [(bi, h) for bi in range(nb) for h in range(RWKV_HEADS)]
    hs = lambda h: slice(h * RWKV_HEAD, (h + 1) * RWKV_HEAD)

    ra, aa, bt, kt, bh, kh, vb, g_all = [], [], [], [], [], [], [], []
    for bi in range(nb):
        r, lw, k, v = r_ref[bi], lw_ref[bi], k_ref[bi], v_ref[bi]
        a, b = a_ref[bi], b_ref[bi]
        hi, lo = _split2(lw)
        lo2 = (lw - hi.astype(F32) - lo.astype(F32)).astype(BF)
        cum = _dot(tri, hi) + _dot(tri, lo) + _dot(tri, lo2)
        tail = cum[chunk - 1:chunk, :]
        g_inv = jnp.exp(-cum)
        g_tail = jnp.exp(tail - cum)
        g_all.append(jnp.exp(tail))
        ra.append((r * jnp.exp(cum)).astype(BF))
        aa.append((a * jnp.exp(cum - lw)).astype(BF))
        bt.append((b * g_inv).astype(BF))
        kt.append((k * g_inv).astype(BF))
        bh.append((b * g_tail).astype(BF))
        kh.append((k * g_tail).astype(BF))
        vb.append(v.astype(BF))

    s_old = [s_scr[bi, h] for bi, h in chains]
    ar = [jnp.concatenate([aa[bi][:, hs(h)], ra[bi][:, hs(h)]], axis=0) for bi, h in chains]
    v_h = [vb[bi][:, hs(h)] for bi, h in chains]
    gb = [_dot_nt(ar[n], bt[bi][:, hs(h)]) for n, (bi, h) in enumerate(chains)]
    gk = [_dot_nt(ar[n], kt[bi][:, hs(h)]) for n, (bi, h) in enumerate(chains)]
    p = [_dot_nt(ar[n], s_old[n].astype(BF)) for n in range(len(chains))]
    l_ab = [jnp.where(strict, x[:chunk], 0.0) for x in gb]
    l_ak = [jnp.where(strict, x[:chunk], 0.0).astype(BF) for x in gk]
    m_rb = [jnp.where(incl, x[chunk:], 0.0).astype(BF) for x in gb]
    m_rk = [jnp.where(incl, x[chunk:], 0.0).astype(BF) for x in gk]
    rhs = [p[n][:chunk] + _dot(l_ak[n], v_h[n]) for n in range(len(chains))]
    t_inv = [eye + x for x in l_ab]
    lp = l_ab
    for _ in range(n_sq):
        lpb = [x.astype(BF) for x in lp]
        lp = [_dot(x, x) for x in lpb]
        t_inv = [t + _dot(x.astype(BF), t.astype(BF)) for x, t in zip(lp, t_inv)]
    ub = [_dot(t.astype(BF), x.astype(BF)).astype(BF) for t, x in zip(t_inv, rhs)]
    y_h = [p[n][chunk:] + _dot(m_rb[n], ub[n]) + _dot(m_rk[n], v_h[n]) for n in range(len(chains))]
    s_new = [s_old[n] * g_all[bi][:, hs(h)] + _dot_tn(ub[n], bh[bi][:, hs(h)]) + _dot_tn(v_h[n], kh[bi][:, hs(h)])
             for n, (bi, h) in enumerate(chains)]
    for n, (bi, h) in enumerate(chains):
        s_scr[bi, h] = s_new[n]

    for bi in range(nb):
        y = jnp.concatenate(y_h[bi * RWKV_HEADS:(bi + 1) * RWKV_HEADS], axis=-1)
        y_ref[bi] = _rwkv_epilogue(y, r_ref[bi], k_ref[bi], v_ref[bi], g_ref[bi],
                                   rk_ref, lg_ref, lb_ref, ones_ref).astype(y_ref.dtype)

    @pl.when(c == pl.num_programs(1) - 1)
    def _():
        sout_ref[...] = s_scr[...]


def _rwkv_epilogue(y, r, k, v, g, rk_ref, lg_ref, lb_ref, ones_ref):
    ones = ones_ref[...]
    inv_n = 1.0 / RWKV_HEAD
    mean = _dot_exact_rhs(y, ones) * inv_n
    d = y - mean
    var = _dot_exact_rhs(d * d, ones) * inv_n
    yn = d * lax.rsqrt(var + GN_EPS) * lg_ref[...] + lb_ref[...]
    bonus = _dot_exact_rhs(r * k * rk_ref[...], ones) * v
    return (yn + bonus) * g


def _rwkv_scan(coef, s0, p, nb, chunk):
    r = coef[0]
    b, t, _ = r.shape
    seq_spec = pl.BlockSpec((nb, chunk, RWKV_W), lambda bi, c: (bi, c, 0))
    st_spec = pl.BlockSpec((nb, RWKV_HEADS, RWKV_HEAD, RWKV_HEAD), lambda bi, c: (bi, 0, 0, 0))
    vec = _const_spec((1, RWKV_W))
    return pl.pallas_call(
        functools.partial(_rwkv_scan_kernel, nb, chunk),
        grid=(b // nb, t // chunk),
        in_specs=[seq_spec] * 7 + [st_spec, vec, vec, vec, _const_spec((RWKV_W, RWKV_W))],
        out_specs=[seq_spec, st_spec],
        out_shape=[jax.ShapeDtypeStruct((b, t, RWKV_W), BF),
                   jax.ShapeDtypeStruct((b, RWKV_HEADS, RWKV_HEAD, RWKV_HEAD), F32)],
        scratch_shapes=[pltpu.VMEM((nb, RWKV_HEADS, RWKV_HEAD, RWKV_HEAD), F32)],
        compiler_params=_params("parallel", "arbitrary"),
        name="rwkv_scan",
    )(*coef, s0, p["r_k"], p["lnx_g"], p["lnx_b"], p["ones_bd"])


def _rwkv_step_kernel(r_ref, lw_ref, k_ref, v_ref, a_ref, b_ref, g_ref, s_ref,
                      rk_ref, lg_ref, lb_ref, ones_ref, y_ref, sout_ref, y_scr):
    n_pair = LANES // RWKV_HEAD
    r, k, v = r_ref[...], k_ref[...], v_ref[...]
    rT, kT, vT = r.T, k.T, v.T
    aT, bT, wT = a_ref[...].T, b_ref[...].T, jnp.exp(lw_ref[...]).T
    for hl in range(LANES // RWKV_HEAD):
        hsl = slice(hl * RWKV_HEAD, (hl + 1) * RWKV_HEAD)
        a_h, b_h, k_h, w_h, r_h = aT[hsl], bT[hsl], kT[hsl], wT[hsl], rT[hsl]
        tiles = [(hl * RWKV_HEAD + n_pair * t) * RWKV_HEAD for t in range(RWKV_HEAD // n_pair)]
        st = [s_ref[:, c0:c0 + LANES].T for c0 in tiles]
        new = []
        for t, x in enumerate(st):
            halves = []
            for il in range(n_pair):
                i = hl * RWKV_HEAD + n_pair * t + il
                slab = x[il * RWKV_HEAD:(il + 1) * RWKV_HEAD]
                sa = jnp.sum(slab * a_h, axis=0, keepdims=True)
                slab = slab * w_h + sa * b_h + vT[i:i + 1] * k_h
                y_scr[i:i + 1, :] = jnp.sum(slab * r_h, axis=0, keepdims=True)
                halves.append(slab)
            new.append(jnp.concatenate(halves, axis=0))
        for c0, x in zip(tiles, new):
            sout_ref[:, c0:c0 + LANES] = x.T
    y_ref[...] = _rwkv_epilogue(y_scr[...].T, r, k, v, g_ref[...],
                                rk_ref, lg_ref, lb_ref, ones_ref).astype(y_ref.dtype)


def _rwkv_step(coef, s0, p):
    b = s0.shape[0]
    per_pair = (LANES // RWKV_HEAD) * RWKV_HEAD * RWKV_HEAD
    n_steps = RWKV_W // LANES
    col = pl.BlockSpec((b, LANES), lambda i: (0, i))
    vec = pl.BlockSpec((1, LANES), lambda i: (0, i))
    st_spec = pl.BlockSpec((b, per_pair), lambda i: (0, i))
    y, s_new = pl.pallas_call(
        _rwkv_step_kernel,
        grid=(n_steps,),
        in_specs=[col] * 7 + [st_spec, vec, vec, vec, _const_spec((LANES, LANES))],
        out_specs=[col, st_spec],
        out_shape=[jax.ShapeDtypeStruct((b, RWKV_W), BF),
                   jax.ShapeDtypeStruct((b, n_steps * per_pair), F32)],
        scratch_shapes=[pltpu.VMEM((LANES, b), F32)],
        compiler_params=_params("parallel"),
        name="rwkv_step",
    )(*coef, s0.reshape(b, n_steps * per_pair), p["r_k"], p["lnx_g"], p["lnx_b"],
      p["ones_bd"][:LANES, :LANES])
    return y, s_new.reshape(s0.shape)


def _merge_kernel(with_router, x_ref, ya_ref, yb_ref, gate_ref, wa_ref, wb_ref, wo_ref, g_ref, *rest):
    if with_router:
        wr_ref, cnt_in_ref, x1_ref, h_ref, route_ref, cnt_out_ref, cnt_scr = rest
    else:
        x1_ref, h_ref = rest
    gate = gate_ref[...]
    pa = _dot(ya_ref[...], wa_ref[...])
    pb = _dot(yb_ref[...], wb_ref[...])
    merged = _sigmoid(gate[:, :D_MODEL]) * pa + _sigmoid(gate[:, D_MODEL:]) * pb
    x1 = x_ref[...] + _dot(merged.astype(BF), wo_ref[...])
    x1_ref[...] = x1
    h = _rms(x1, g_ref[...])
    h_ref[...] = h.astype(h_ref.dtype)
    if not with_router:
        return

    @pl.when(pl.program_id(0) == 0)
    def _():
        cnt_scr[...] = cnt_in_ref[...]

    logits = _dot(h.astype(BF), wr_ref[...])
    tm = logits.shape[0]
    lane = lax.broadcasted_iota(jnp.int32, logits.shape, 1).astype(F32)
    logits = jnp.where(lane < N_EXPERTS, logits, -jnp.inf)
    m1 = jnp.max(logits, axis=-1, keepdims=True)
    i1 = jnp.min(jnp.where(logits == m1, lane, float(LANES)), axis=-1, keepdims=True)
    rest_l = jnp.where(lane == i1, -jnp.inf, logits)
    m2 = jnp.max(rest_l, axis=-1, keepdims=True)
    i2 = jnp.min(jnp.where(rest_l == m2, lane, float(LANES)), axis=-1, keepdims=True)
    e2 = jnp.exp(m2 - m1)
    g1 = 1.0 / (1.0 + e2)
    g2 = e2 / (1.0 + e2)
    oh1 = (lane == i1).astype(F32)
    oh2 = (lane == i2).astype(F32)
    both = oh1 + oh2
    ri = lax.broadcasted_iota(jnp.int32, (tm, tm), 0)
    ci = lax.broadcasted_iota(jnp.int32, (tm, tm), 1)
    before = _dot((ri > ci).astype(BF), both.astype(BF)) + cnt_scr[...]
    rank1 = jnp.sum(oh1 * before, axis=-1, keepdims=True)
    rank2 = jnp.sum(oh2 * before, axis=-1, keepdims=True)
    cnt_scr[...] += jnp.sum(both, axis=0, keepdims=True)
    cnt_out_ref[...] = cnt_scr[...]
    route = jnp.zeros_like(logits)
    for n, val in enumerate((i1, i2, g1, g2, rank1, rank2)):
        route = jnp.where(lane == n, val, route)
    route_ref[...] = route


def _merge_out(x, ya, yb, gate, wa, wb, wo, g, tm, router=None, cnt_in=None):
    t = x.shape[0]
    row = lambda w_: pl.BlockSpec((tm, w_), lambda i: (i, 0))
    in_specs = [row(D_MODEL), row(ATT_W), row(RWKV_W), row(2 * D_MODEL),
                _const_spec((ATT_W, D_MODEL)), _const_spec((RWKV_W, D_MODEL)),
                _const_spec((D_MODEL, D_MODEL)), _const_spec((1, D_MODEL))]
    args = [x, ya, yb, gate, wa, wb, wo, g]
    if router is None:
        out_specs = [row(D_MODEL), row(D_MODEL)]
        out_shape = [jax.ShapeDtypeStruct((t, D_MODEL), F32), jax.ShapeDtypeStruct((t, D_MODEL), BF)]
        scratch = []
    else:
        in_specs += [_const_spec(router.shape), _const_spec((1, LANES))]
        args += [router, cnt_in]
        out_specs = [row(D_MODEL), row(D_MODEL), row(LANES), pl.BlockSpec((1, LANES), lambda i: (0, 0))]
        out_shape = [jax.ShapeDtypeStruct((t, D_MODEL), F32), jax.ShapeDtypeStruct((t, D_MODEL), F32),
                     jax.ShapeDtypeStruct((t, LANES), F32), jax.ShapeDtypeStruct((1, LANES), F32)]
        scratch = [pltpu.VMEM((1, LANES), F32)]
    return pl.pallas_call(
        functools.partial(_merge_kernel, router is not None),
        grid=(t // tm,),
        in_specs=in_specs, out_specs=out_specs, out_shape=out_shape, scratch_shapes=scratch,
        compiler_params=_params("parallel" if router is None else "arbitrary"),
        name="merge_out",
    )(*args)


def _swiglu(x, wg, wu, wd):
    s = _dot(x, wg)
    u = _dot(x, wu)
    act = (s * _sigmoid(s) * u).astype(BF)
    return _dot(act, wd)


def _dense_ffn_kernel(x1_ref, h_ref, wg_ref, wu_ref, wd_ref, o_ref):
    o_ref[...] = x1_ref[...] + _swiglu(h_ref[...], wg_ref[...], wu_ref[...], wd_ref[...])


def _dense_ffn(x1, h, wg, wu, wd, tm):
    t = x1.shape[0]
    d_ff = wg.shape[1]
    row = lambda: pl.BlockSpec((tm, D_MODEL), lambda i: (i, 0))
    return pl.pallas_call(
        _dense_ffn_kernel,
        grid=(t // tm,),
        in_specs=[row(), row(), _const_spec((D_MODEL, d_ff)), _const_spec((D_MODEL, d_ff)),
                  _const_spec((d_ff, D_MODEL))],
        out_specs=row(),
        out_shape=jax.ShapeDtypeStruct((t, D_MODEL), F32),
        compiler_params=_params("parallel"),
        name="dense_ffn",
    )(x1, h, wg, wu, wd)


def _dispatch_kernel(tt, first, dest_ref, fill_ref, h_ref, *rest):
    if first:
        xs_hbm, zbuf, sem, zsem = rest
    else:
        _, xs_hbm, sem = rest
    i = pl.program_id(0)
    base = i * tt

    if first:
        @pl.when(i == 0)
        def _():
            zbuf[...] = jnp.zeros_like(zbuf)
            bm = zbuf.shape[0]
            fills = [pltpu.make_async_copy(zbuf, xs_hbm.at[pl.ds(fill_ref[n] * bm, bm), :], zsem)
                     for n in range(fill_ref.shape[0])]
            for f in fills:
                f.start()
            for f in fills:
                f.wait()

    def issue(r, carry):
        for j in range(TOP_K):
            row = dest_ref[TOP_K * (base + r) + j]
            pltpu.make_async_copy(h_ref.at[pl.ds(r, 1), :], xs_hbm.at[pl.ds(row, 1), :], sem).start()
        return carry

    lax.fori_loop(0, tt, issue, 0, unroll=8)
    for j in range(TOP_K):
        pltpu.make_async_copy(h_ref, xs_hbm.at[pl.ds(0, tt), :], sem).wait()


def _dispatch(dest, fill_blocks, h, xs, tt, bm, n_rows):
    t = h.shape[0]
    first = xs is None
    in_specs = [pl.BlockSpec((tt, D_MODEL), lambda i, ds, fb: (i, 0))]
    args = [dest, fill_blocks, h]
    if first:
        scratch = [pltpu.VMEM((bm, D_MODEL), F32), pltpu.SemaphoreType.DMA(()), pltpu.SemaphoreType.DMA(())]
        aliases = {}
    else:
        in_specs.append(pl.BlockSpec(memory_space=pl.ANY))
        args.append(xs)
        scratch = [pltpu.SemaphoreType.DMA(())]
        aliases = {3: 0}
    grid_spec = pltpu.PrefetchScalarGridSpec(
        num_scalar_prefetch=2, grid=(t // tt,), in_specs=in_specs,
        out_specs=pl.BlockSpec(memory_space=pl.ANY), scratch_shapes=scratch)
    return pl.pallas_call(
        functools.partial(_dispatch_kernel, tt, first),
        grid_spec=grid_spec,
        out_shape=jax.ShapeDtypeStruct((n_rows, D_MODEL), F32),
        input_output_aliases=aliases,
        compiler_params=_params("arbitrary"),
        name="moe_dispatch",
    )(*args)


def _moe_kernel(n_half, blk_e_ref, n_used_ref, xs_ref, wg_ref, wu_ref, wd_ref, y_ref):
    @pl.when(pl.program_id(0) < n_used_ref[0])
    def _():
        x = xs_ref[...].astype(BF)
        d_e = wg_ref.shape[2]
        step = d_e // n_half
        acc = None
        for j in range(n_half):
            cs = slice(j * step, (j + 1) * step)
            part = _swiglu(x, wg_ref[0, :, cs], wu_ref[0, :, cs], wd_ref[0, cs, :])
            acc = part if acc is None else acc + part
        y_ref[...] = acc

    @pl.when(pl.program_id(0) >= n_used_ref[0])
    def _():
        y_ref[...] = jnp.zeros_like(y_ref)


def _moe_ffn(xs, blk_e, n_used, wg, wu, wd, bm):
    n_blk = blk_e.shape[0]
    n_rows = xs.shape[0]
    d_e = wg.shape[2]
    wspec = lambda shape: pl.BlockSpec(shape, lambda i, be, nu: (be[i], 0, 0), pipeline_mode=pl.Buffered(1))
    rows = pl.BlockSpec((bm, D_MODEL), lambda i, be, nu: (i, 0))
    grid_spec = pltpu.PrefetchScalarGridSpec(
        num_scalar_prefetch=2,
        grid=(n_blk,),
        in_specs=[rows, wspec((1, D_MODEL, d_e)), wspec((1, D_MODEL, d_e)), wspec((1, d_e, D_MODEL))],
        out_specs=rows,
    )
    return pl.pallas_call(
        functools.partial(_moe_kernel, 2),
        grid_spec=grid_spec,
        out_shape=jax.ShapeDtypeStruct((n_rows, D_MODEL), F32),
        compiler_params=_params("arbitrary"),
        name="moe_ffn",
    )(blk_e, n_used, xs, wg, wu, wd)


def _row_layout(counts, bm, n_blk):
    per_e = (counts + bm - 1) // bm
    ends = jnp.cumsum(per_e)
    n_used = ends[-1]
    first_row = (ends - per_e) * bm
    i = jnp.arange(n_blk, dtype=jnp.int32)
    blk_e = jnp.sum((jnp.minimum(i, jnp.maximum(n_used - 1, 0))[:, None] >= ends[None, :]).astype(jnp.int32), axis=1)
    need = jnp.any(i[:, None] == (ends - 1)[None, :], axis=1) | (i >= n_blk - N_EXPERTS)
    fill_blocks = jnp.argsort(jnp.logical_not(need), stable=True)[:2 * N_EXPERTS].astype(jnp.int32)
    return first_row.astype(jnp.int32), blk_e.astype(jnp.int32), n_used.astype(jnp.int32).reshape(1), fill_blocks


def _combine_kernel(tc, dest_ref, x1_ref, route_ref, y_hbm, g_ref, o_ref, ybuf, sem):
    base = pl.program_id(0) * tc

    def issue(r, carry):
        for j in range(TOP_K):
            row = dest_ref[TOP_K * (base + r) + j]
            pltpu.make_async_copy(y_hbm.at[pl.ds(row, 1), :], ybuf.at[j, pl.ds(r, 1), :], sem).start()
        return carry

    lax.fori_loop(0, tc, issue, 0, unroll=4)
    for j in range(TOP_K):
        pltpu.make_async_copy(y_hbm.at[pl.ds(0, tc), :], ybuf.at[j], sem).wait()
    route = route_ref[...]
    x2 = x1_ref[...] + route[:, 2:3] * ybuf[0] + route[:, 3:4] * ybuf[1]
    o_ref[...] = _rms(x2, g_ref[...])


def _moe_combine(dest, x1, route, y, g, tc):
    t = x1.shape[0]
    grid_spec = pltpu.PrefetchScalarGridSpec(
        num_scalar_prefetch=1,
        grid=(t // tc,),
        in_specs=[pl.BlockSpec((tc, D_MODEL), lambda i, ds: (i, 0)),
                  pl.BlockSpec((tc, LANES), lambda i, ds: (i, 0)),
                  pl.BlockSpec(memory_space=pl.ANY),
                  pl.BlockSpec((1, D_MODEL), lambda i, ds: (0, 0))],
        out_specs=pl.BlockSpec((tc, D_MODEL), lambda i, ds: (i, 0)),
        scratch_shapes=[pltpu.VMEM((TOP_K, tc, D_MODEL), F32), pltpu.SemaphoreType.DMA(())],
    )
    return pl.pallas_call(
        functools.partial(_combine_kernel, tc),
        grid_spec=grid_spec,
        out_shape=jax.ShapeDtypeStruct((t, D_MODEL), F32),
        compiler_params=_params("arbitrary"),
        name="moe_combine",
    )(dest, x1, route, y, g)


def _layer_params(l, w_in, mu_shift, w0, w2, a0, a2, g2, k_k, k_a, r_k, lnx_g, lnx_b,
                  w_proj_attn, w_proj_rwkv, w_out, ones_bd):
    row = lambda u: u.reshape(1, -1)
    return dict(
        w_in=w_in[l].astype(BF), mu=row(mu_shift[l]), w0=row(w0[l]), w2=w2[l].astype(BF),
        a0=row(a0[l]), a2=a2[l].astype(BF), g2=g2[l].astype(BF), k_k=row(k_k[l]), k_a=row(k_a[l]),
        r_k=row(r_k[l]), lnx_g=row(lnx_g[l]), lnx_b=row(lnx_b[l]),
        wa=w_proj_attn[l].astype(BF), wb=w_proj_rwkv[l].astype(BF), wo=w_out[l].astype(BF),
        ones_bd=ones_bd)


def kernel(x_prompt, x_sample, cache_k, cache_v, state_wkv, state_shift, norm_mix_g, w_in, attn_sinks, rel_bias, mu_shift, w0, w2, a0, a2, g2, k_k, k_a, r_k, lnx_g, lnx_b, w_proj_attn, w_proj_rwkv, w_out, norm_ffn_g, dense_w_gate, dense_w_up, dense_w_down, router_w, moe_w_gate, moe_w_up, moe_w_down, norm_final_g):
    batch, seq, _ = x_prompt.shape
    dec_batch, dec_seq, _ = x_sample.shape
    assert dec_seq == 1 and seq % WINDOW == 0

    head_id = jnp.arange(RWKV_W, dtype=jnp.int32) // RWKV_HEAD
    ones_bd = (head_id[:, None] == head_id[None, :]).astype(BF)
    bias_by_dist = _bias_lookup(rel_bias, _t5_bucket(np.arange(WINDOW + 1)))
    bias_c = jnp.transpose(bias_by_dist[WINDOW:0:-1], (1, 0))[None]
    bias_n = bias_by_dist[0][None, :, None]

    layers = [_layer_params(l, w_in, mu_shift, w0, w2, a0, a2, g2, k_k, k_a, r_k, lnx_g, lnx_b,
                            w_proj_attn, w_proj_rwkv, w_out, ones_bd) for l in range(DEPTH)]

    def router_pieces(l):
        return jnp.pad(router_w[l // 2], ((0, 0), (0, LANES - N_EXPERTS))).astype(BF)

    n_prompt = batch * seq
    tm_p = 512
    bm = 512
    n_all = n_prompt + dec_batch
    n_blk = -(-(n_all * TOP_K) // bm) + N_EXPERTS

    def mix(x, is_prompt, l, state):
        p = layers[l]
        n_tok = x.shape[0]
        gm = norm_mix_g[l].reshape(1, -1)
        sinks = attn_sinks[l]
        if is_prompt:
            b, t = batch, seq
            prev_z = jnp.zeros((b, 1, RWKV_IN), F32)
            q, k, v, gate, z_last, *coef = _inproj(x, gm, p, prev_z, 256, t)
            sink_col = jnp.repeat(sinks, WINDOW).reshape(N_KV_HEADS, GQA_GROUP * WINDOW, 1)
            ya = _swa_prompt(q.reshape(b, t, ATT_W), k.reshape(b, t, KV_W), v.reshape(b, t, KV_W),
                             rel_bias, sink_col).reshape(n_tok, ATT_W)
            k4 = k.reshape(b, t, N_KV_HEADS, HEAD_DIM)
            v4 = v.reshape(b, t, N_KV_HEADS, HEAD_DIM)
            state["k"].append(k4[:, t - WINDOW:])
            state["v"].append(v4[:, t - WINDOW:])
            s0 = jnp.zeros((b, RWKV_HEADS, RWKV_HEAD, RWKV_HEAD), F32)
            yb3, s_new = _rwkv_scan([u.reshape(b, t, RWKV_W) for u in coef], s0, p, b, 64)
            yb = yb3.reshape(n_tok, RWKV_W)
            state["z"].append(z_last.reshape(b, RWKV_IN))
        else:
            b = dec_batch
            q, k, v, gate, z, *coef = _inproj(x, gm, p, state_shift[l], n_tok, None)
            ck = cache_k[l].reshape(b, WINDOW, KV_W)
            cv = cache_v[l].reshape(b, WINDOW, KV_W)
            ya3 = _swa_sample(q.reshape(b, N_Q_HEADS, HEAD_DIM), k, v, ck, cv, bias_c, bias_n,
                              sinks.reshape(1, N_Q_HEADS, 1), 16)
            ya = ya3.reshape(b, ATT_W)
            state["k"].append(jnp.concatenate([ck[:, 1:], k[:, None]], axis=1)
                              .reshape(b, WINDOW, N_KV_HEADS, HEAD_DIM))
            state["v"].append(jnp.concatenate([cv[:, 1:], v[:, None]], axis=1)
                              .reshape(b, WINDOW, N_KV_HEADS, HEAD_DIM))
            yb, s_new = _rwkv_step(coef, state_wkv[l], p)
            state["z"].append(z)
        state["s"].append(s_new)
        return ya, yb, gate

    groups = [dict(x=x_prompt.reshape(n_prompt, D_MODEL), prompt=True, tm=tm_p, k=[], v=[], s=[], z=[]),
              dict(x=x_sample.reshape(dec_batch, D_MODEL), prompt=False, tm=dec_batch, k=[], v=[], s=[], z=[])]
    for l in range(DEPTH):
        p = layers[l]
        gn = norm_ffn_g[l].reshape(1, -1)
        j = l // 2
        if l % 2 == 0:
            wg, wu, wd = (_to_bf16(w[j], w.shape[1] // 4) for w in (dense_w_gate, dense_w_up, dense_w_down))
            for grp in groups:
                ya, yb, gate = mix(grp["x"], grp["prompt"], l, grp)
                x1, h = _merge_out(grp["x"], ya, yb, gate, p["wa"], p["wb"], p["wo"], gn, grp["tm"])
                grp["x"] = _dense_ffn(x1, h, wg, wu, wd, grp["tm"])
        else:
            assert l == DEPTH - 1
            wg, wu, wd = (w[j].astype(BF) for w in (moe_w_gate, moe_w_up, moe_w_down))
            router = router_pieces(l)
            cnt = jnp.zeros((1, LANES), F32)
            xs = None
            for grp in groups:
                ya, yb, gate = mix(grp["x"], grp["prompt"], l, grp)
                x1, h, route, cnt = _merge_out(grp["x"], ya, yb, gate, p["wa"], p["wb"], p["wo"], gn,
                                               min(256, grp["tm"]), router=router, cnt_in=cnt)
                grp["x1"], grp["route"], grp["h"] = x1, route, h
            counts = cnt[0, :N_EXPERTS].astype(jnp.int32)
            first_row, blk_e, n_used, fill_blocks = _row_layout(counts, bm, n_blk)
            for grp in groups:
                route = grp["route"]
                expert = route[:, 0:TOP_K].astype(jnp.int32)
                rank = route[:, 2 * TOP_K:3 * TOP_K].astype(jnp.int32)
                start = jnp.sum(jnp.where(expert[..., None] == jnp.arange(N_EXPERTS, dtype=jnp.int32),
                                          first_row, 0), axis=-1)
                grp["dest"] = (start + rank).reshape(-1)
                xs = _dispatch(grp["dest"], fill_blocks, grp["h"], xs, min(2048, grp["h"].shape[0]), bm,
                               n_blk * bm)
            y = _moe_ffn(xs, blk_e, n_used, wg, wu, wd, bm)
            for grp in groups:
                grp["x"] = _moe_combine(grp["dest"], grp["x1"], grp["route"], y,
                                        norm_final_g.reshape(1, -1), min(1024, grp["x1"].shape[0]))

    gp, gs = groups
    outs = []
    for grp, shape in ((gp, (batch, seq, D_MODEL)), (gs, (dec_batch, dec_seq, D_MODEL))):
        outs.append((grp["x"].reshape(shape), jnp.stack(grp["k"]), jnp.stack(grp["v"]),
                     jnp.stack(grp["s"]), jnp.stack(grp["z"])))
    (y_p, nk_p, nv_p, ns_p, nz_p), (y_s, nk_s, nv_s, ns_s, nz_s) = outs
    return (y_p, y_s, nk_p, nv_p, ns_p, nz_p, nk_s, nv_s, ns_s, nz_s)
```

```python
import functools
import math

import jax
import jax.numpy as jnp
import numpy as np
from jax import lax
from jax.experimental import pallas as pl
from jax.experimental.pallas import tpu as pltpu

BF = jnp.bfloat16
F32 = jnp.float32

D_MODEL = 1024
DEPTH = 2
HEAD_DIM = 64
N_Q_HEADS = 8
N_KV_HEADS = 2
GQA_GROUP = 4
ATT_W = 512
KV_W = 128
WINDOW = 128
ATT_SCALE = HEAD_DIM ** -0.5
NUM_BUCKETS = 32
MAX_EXACT = 16
MAX_DISTANCE = 128
NEG_INF = -1e30
RWKV_HEAD = 64
RWKV_W = 512
RWKV_HEADS = 8
D_DECAY_LORA = 64
D_AAA_LORA = 64
D_GATE_LORA = 128
RWKV_IN = 3 * RWKV_W + D_DECAY_LORA + D_AAA_LORA + D_GATE_LORA
GN_EPS = 64e-5
IN_W = ATT_W + 2 * KV_W + RWKV_IN + 2 * D_MODEL
N_EXPERTS = 8
TOP_K = 2
NORM_EPS = 1e-6

VMEM_LIMIT_BYTES = 56 * 1024 * 1024
LANES = 128


def _params(*sem):
    return pltpu.CompilerParams(dimension_semantics=sem, vmem_limit_bytes=VMEM_LIMIT_BYTES)


def _const_spec(shape):
    zeros = (0,) * len(shape)
    return pl.BlockSpec(shape, lambda *_: zeros, pipeline_mode=pl.Buffered(1))


def _dot(a, b):
    return jnp.dot(a, b, preferred_element_type=F32)


def _dot_nt(a, b):
    return lax.dot_general(a, b, (((1,), (1,)), ((), ())), preferred_element_type=F32)


def _dot_tn(a, b):
    return lax.dot_general(a, b, (((0,), (0,)), ((), ())), preferred_element_type=F32)


def _split2(x):
    hi = x.astype(BF)
    lo = (x - hi.astype(F32)).astype(BF)
    return hi, lo


def _dot_exact_rhs(x, w):
    hi, lo = _split2(x)
    return _dot(hi, w) + _dot(lo, w)


def _rms(x, g):
    ms = jnp.mean(x * x, axis=-1, keepdims=True)
    return x * lax.rsqrt(ms + NORM_EPS) * g


def _sigmoid(x):
    return 1.0 / (1.0 + jnp.exp(-x))


def _cast_kernel(x_ref, o_ref):
    o_ref[...] = x_ref[...].astype(o_ref.dtype)


def _to_bf16(w, row_tile):
    rows, cols = w.shape
    spec = pl.BlockSpec((row_tile, cols), lambda i: (i, 0))
    return pl.pallas_call(
        _cast_kernel, grid=(rows // row_tile,), in_specs=[spec], out_specs=spec,
        out_shape=jax.ShapeDtypeStruct((rows, cols), BF),
        compiler_params=_params("parallel"), name="to_bf16",
    )(w)


def _inproj_kernel(rows_per_seq, x_ref, g_ref, w_ref, pz_ref, mu_ref, w0_ref, w2_ref, a0_ref, a2_ref, g2_ref,
                   kk_ref, ka_ref, ones_ref,
                   q_ref, k_ref, v_ref, gate_ref, zl_ref,
                   r_ref, lw_ref, k2_ref, vr_ref, a_ref, b_ref, gr_ref, *carry):
    n = _rms(x_ref[...], g_ref[...]).astype(BF)
    z0 = ATT_W + 2 * KV_W
    z = _dot(n, w_ref[:, z0:z0 + RWKV_IN])
    q_ref[...] = _dot(n, w_ref[:, 0:ATT_W]).astype(BF)
    k_ref[...] = _dot(n, w_ref[:, ATT_W:ATT_W + KV_W])
    v_ref[...] = _dot(n, w_ref[:, ATT_W + KV_W:ATT_W + 2 * KV_W])
    gate_ref[...] = _dot(n, w_ref[:, z0 + RWKV_IN:IN_W])
    tm = z.shape[0]
    if rows_per_seq is None:
        zs = pz_ref[...]
        zl_ref[...] = z
    else:
        prev_ref, = carry
        tiles_per_seq = rows_per_seq // tm

        @pl.when(pl.program_id(0) % tiles_per_seq == 0)
        def _():
            prev_ref[...] = pz_ref[0]

        row = lax.broadcasted_iota(jnp.int32, (tm, 1), 0)
        zs = jnp.where(row == 0, prev_ref[...], pltpu.roll(z, 1, 0))
        prev_ref[...] = z[tm - 1:tm, :]
        zl_ref[0] = z[tm - 1:tm, :]
    zz = z + (zs - z) * mu_ref[...]
    r = zz[:, 0:RWKV_W]
    k = zz[:, RWKV_W:2 * RWKV_W]
    v = zz[:, 2 * RWKV_W:3 * RWKV_W]
    o = 3 * RWKV_W
    wl = zz[:, o:o + D_DECAY_LORA]
    al = zz[:, o + D_DECAY_LORA:o + D_DECAY_LORA + D_AAA_LORA]
    gl = zz[:, o + D_DECAY_LORA + D_AAA_LORA:RWKV_IN]
    wpre = -(w0_ref[...] + _dot(jnp.tanh(wl).astype(BF), w2_ref[...]))
    softplus = jnp.maximum(wpre, 0.0) + jnp.log1p(jnp.exp(-jnp.abs(wpre)))
    lw = -jnp.exp(-softplus - 0.5)
    a = _sigmoid(a0_ref[...] + _dot(al.astype(BF), a2_ref[...]))
    g = _dot(_sigmoid(gl).astype(BF), g2_ref[...])
    kkv = k * kk_ref[...]
    ss = _dot_exact_rhs(kkv * kkv, ones_ref[...])
    kk = kkv / jnp.maximum(jnp.sqrt(ss), 1e-12)
    k2 = k * (1.0 + (a - 1.0) * ka_ref[...])
    r_ref[...], lw_ref[...], k2_ref[...], vr_ref[...] = r, lw, k2, v
    a_ref[...], b_ref[...], gr_ref[...] = -kk, kk * a, g


def _inproj(x, g, p, prev_z, tm, rows_per_seq):
    t = x.shape[0]
    row = lambda w_: pl.BlockSpec((tm, w_), lambda i: (i, 0))
    vec = lambda w_: _const_spec((1, w_))
    if rows_per_seq is None:
        pz_spec = row(RWKV_IN)
        zl_spec, zl_shape = row(RWKV_IN), (t, RWKV_IN)
        scratch = []
    else:
        tiles_per_seq = rows_per_seq // tm
        pz_spec = pl.BlockSpec((1, 1, RWKV_IN), lambda i: (i // tiles_per_seq, 0, 0))
        zl_spec, zl_shape = pz_spec, (t // rows_per_seq, 1, RWKV_IN)
        scratch = [pltpu.VMEM((1, RWKV_IN), F32)]
    return pl.pallas_call(
        functools.partial(_inproj_kernel, rows_per_seq),
        grid=(t // tm,),
        in_specs=[row(D_MODEL), vec(D_MODEL), _const_spec((D_MODEL, IN_W)), pz_spec,
                  vec(RWKV_IN), vec(RWKV_W), _const_spec((D_DECAY_LORA, RWKV_W)),
                  vec(RWKV_W), _const_spec((D_AAA_LORA, RWKV_W)), _const_spec((D_GATE_LORA, RWKV_W)),
                  vec(RWKV_W), vec(RWKV_W), _const_spec((RWKV_W, RWKV_W))],
        out_specs=[row(ATT_W), row(KV_W), row(KV_W), row(2 * D_MODEL), zl_spec] + [row(RWKV_W)] * 7,
        out_shape=[jax.ShapeDtypeStruct((t, ATT_W), BF),
                   jax.ShapeDtypeStruct((t, KV_W), F32),
                   jax.ShapeDtypeStruct((t, KV_W), F32),
                   jax.ShapeDtypeStruct((t, 2 * D_MODEL), F32),
                   jax.ShapeDtypeStruct(zl_shape, F32)] + [jax.ShapeDtypeStruct((t, RWKV_W), F32)] * 7,
        scratch_shapes=scratch,
        compiler_params=_params("arbitrary"),
        name="inproj",
    )(x, g, p["w_in"], prev_z, p["mu"], p["w0"], p["w2"], p["a0"], p["a2"], p["g2"], p["k_k"], p["k_a"],
      p["ones_bd"])


def _t5_bucket(dist):
    n = np.maximum(dist, 0)
    nf = np.maximum(n, 1).astype(np.float32)
    large = MAX_EXACT + (np.log(nf / MAX_EXACT) / math.log(MAX_DISTANCE / MAX_EXACT)
                         * (NUM_BUCKETS - MAX_EXACT)).astype(np.int32)
    return np.where(n < MAX_EXACT, n, np.minimum(large, NUM_BUCKETS - 1)).astype(np.int32)


def _bias_lookup(rel_bias, bucket):
    hit = bucket[..., None, None] == np.arange(NUM_BUCKETS, dtype=np.int32)[:, None]
    return jnp.sum(jnp.where(hit, rel_bias, 0.0), axis=-2)


def _prompt_bucket_table():
    dist = np.arange(WINDOW)[:, None] - (np.arange(2 * WINDOW) - WINDOW)[None, :]
    return np.where((dist >= 0) & (dist <= WINDOW), _t5_bucket(dist), -1).astype(np.int32)


def _swa_prompt_kernel(nb, bucket_ref, relb_ref, q_ref, kp_ref, kc_ref, vp_ref, vc_ref, sink_ref, o_ref, bias_ref):
    first = pl.program_id(0) == 0

    @pl.when(first)
    def _():
        bucket = bucket_ref[...]
        for head in range(N_Q_HEADS):
            tab = jnp.full(bucket.shape, NEG_INF, F32)
            for n in range(NUM_BUCKETS):
                tab = jnp.where(bucket == n, relb_ref[n, head], tab)
            g = head % GQA_GROUP
            bias_ref[head // GQA_GROUP, g * WINDOW:(g + 1) * WINDOW, :] = tab

    col = lax.broadcasted_iota(jnp.int32, (GQA_GROUP * WINDOW, 2 * WINDOW), 1)
    pad_mask = jnp.logical_and(first, col < WINDOW)
    chains = [(b, h) for b in range(nb) for h in range(N_KV_HEADS)]
    hs = lambda h: slice(h * HEAD_DIM, (h + 1) * HEAD_DIM)
    kh = [jnp.concatenate([kp_ref[b][:, hs(h)], kc_ref[b][:, hs(h)]], axis=0).astype(BF) for b, h in chains]
    vh = [jnp.concatenate([vp_ref[b][:, hs(h)], vc_ref[b][:, hs(h)]], axis=0).astype(BF) for b, h in chains]
    qh = [jnp.concatenate([q_ref[b][:, hs(GQA_GROUP * h + g)] for g in range(GQA_GROUP)], axis=0)
          for b, h in chains]
    s = [_dot_nt(qh[n], kh[n]) * ATT_SCALE + bias_ref[h] for n, (b, h) in enumerate(chains)]
    s = [jnp.where(pad_mask, NEG_INF, x) for x in s]
    m = [jnp.maximum(jnp.max(x, axis=-1, keepdims=True), sink_ref[h]) for x, (b, h) in zip(s, chains)]
    p = [jnp.exp(x - mm) for x, mm in zip(s, m)]
    denom = [jnp.sum(x, axis=-1, keepdims=True) + jnp.exp(sink_ref[h] - mm)
             for x, mm, (b, h) in zip(p, m, chains)]
    o = [_dot(x.astype(BF), vv) / d for x, vv, d in zip(p, vh, denom)]
    for b in range(nb):
        pieces = [o[b * N_KV_HEADS + h][g * WINDOW:(g + 1) * WINDOW]
                  for h in range(N_KV_HEADS) for g in range(GQA_GROUP)]
        o_ref[b] = jnp.concatenate(pieces, axis=-1).astype(o_ref.dtype)


def _swa_prompt(q3, k3, v3, rel_bias, sink_col):
    batch, seq, _ = q3.shape
    cur = lambda i: (0, i, 0)
    prev = lambda i: (0, jnp.maximum(i - 1, 0), 0)
    kv_c = pl.BlockSpec((batch, WINDOW, KV_W), cur)
    kv_p = pl.BlockSpec((batch, WINDOW, KV_W), prev)
    bucket = jnp.asarray(_prompt_bucket_table())
    return pl.pallas_call(
        functools.partial(_swa_prompt_kernel, batch),
        grid=(seq // WINDOW,),
        in_specs=[_const_spec(bucket.shape), pl.BlockSpec(memory_space=pltpu.SMEM),
                  pl.BlockSpec((batch, WINDOW, ATT_W), cur), kv_p, kv_c, kv_p, kv_c,
                  _const_spec(sink_col.shape)],
        out_specs=pl.BlockSpec((batch, WINDOW, ATT_W), cur),
        out_shape=jax.ShapeDtypeStruct((batch, seq, ATT_W), BF),
        scratch_shapes=[pltpu.VMEM((N_KV_HEADS, GQA_GROUP * WINDOW, 2 * WINDOW), F32)],
        compiler_params=_params("arbitrary"),
        name="swa_prompt",
    )(bucket, rel_bias, q3, k3, k3, v3, v3, sink_col)


def _swa_sample_kernel(q_ref, kn_ref, vn_ref, ck_ref, cv_ref, bc_ref, bn_ref, sink_ref, o_ref, nk_ref, nv_ref):
    q = q_ref[...]
    qf = q.astype(F32)
    kn32, vn32 = kn_ref[...], vn_ref[...]
    kn = kn32.astype(BF).astype(F32)
    vn = vn32.astype(BF).astype(F32)
    ck = [ck_ref[0, :, :, h, :] for h in range(N_KV_HEADS)]
    cv = [cv_ref[0, :, :, h, :] for h in range(N_KV_HEADS)]
    nk_ref[:, 0:WINDOW - 1] = ck_ref[0, :, 1:WINDOW]
    nv_ref[:, 0:WINDOW - 1] = cv_ref[0, :, 1:WINDOW]
    for h in range(N_KV_HEADS):
        sl = slice(h * HEAD_DIM, (h + 1) * HEAD_DIM)
        nk_ref[:, WINDOW - 1, h, :] = kn32[:, sl]
        nv_ref[:, WINDOW - 1, h, :] = vn32[:, sl]
    head = lax.broadcasted_iota(jnp.int32, (1, N_Q_HEADS, 1), 1)
    low = head < GQA_GROUP
    s_h, sn_h = [], []
    for h in range(N_KV_HEADS):
        sl = slice(h * HEAD_DIM, (h + 1) * HEAD_DIM)
        s_h.append(jnp.einsum("bgd,bkd->bgk", q, ck[h].astype(BF), preferred_element_type=F32))
        sn_h.append(jnp.sum(qf * kn[:, None, sl], axis=-1, keepdims=True))
    s = jnp.where(low, s_h[0], s_h[1]) * ATT_SCALE + bc_ref[...]
    sn = jnp.where(low, sn_h[0], sn_h[1]) * ATT_SCALE + bn_ref[...]
    sink = sink_ref[...]
    m = jnp.maximum(jnp.maximum(jnp.max(s, axis=-1, keepdims=True), sn), sink)
    p = jnp.exp(s - m)
    pn = jnp.exp(sn - m)
    denom = jnp.sum(p, axis=-1, keepdims=True) + pn + jnp.exp(sink - m)
    pb = p.astype(BF)
    pnb = pn.astype(BF).astype(F32)
    o_h = []
    for h in range(N_KV_HEADS):
        sl = slice(h * HEAD_DIM, (h + 1) * HEAD_DIM)
        o = jnp.einsum("bgk,bkd->bgd", pb, cv[h].astype(BF), preferred_element_type=F32)
        o_h.append(o + pnb * vn[:, None, sl])
    o_ref[...] = (jnp.where(low, o_h[0], o_h[1]) / denom).astype(o_ref.dtype)


def _swa_sample(q3, kn, vn, cache_k, cache_v, layer, bias_c, bias_n, sink3, bb):
    b = q3.shape[0]
    cache_in = pl.BlockSpec((1, bb, WINDOW, N_KV_HEADS, HEAD_DIM), lambda i: (layer, i, 0, 0, 0))
    cache_out = pl.BlockSpec((bb, WINDOW, N_KV_HEADS, HEAD_DIM), lambda i: (i, 0, 0, 0))
    cache_shape = jax.ShapeDtypeStruct((b, WINDOW, N_KV_HEADS, HEAD_DIM), cache_k.dtype)
    return pl.pallas_call(
        _swa_sample_kernel,
        grid=(b // bb,),
        in_specs=[pl.BlockSpec((bb, N_Q_HEADS, HEAD_DIM), lambda i: (i, 0, 0)),
                  pl.BlockSpec((bb, KV_W), lambda i: (i, 0)),
                  pl.BlockSpec((bb, KV_W), lambda i: (i, 0)),
                  cache_in, cache_in,
                  _const_spec(bias_c.shape), _const_spec(bias_n.shape), _const_spec(sink3.shape)],
        out_specs=[pl.BlockSpec((bb, N_Q_HEADS, HEAD_DIM), lambda i: (i, 0, 0)), cache_out, cache_out],
        out_shape=[jax.ShapeDtypeStruct((b, N_Q_HEADS, HEAD_DIM), BF), cache_shape, cache_shape],
        compiler_params=_params("parallel"),
        name="swa_sample",
    )(q3, kn, vn, cache_k, cache_v, bias_c, bias_n, sink3)


def _rwkv_scan_kernel(nb, chunk, r_ref, lw_ref, k_ref, v_ref, a_ref, b_ref, g_ref, s0_ref,
                      rk_ref, lg_ref, lb_ref, ones_ref, y_ref, sout_ref, s_scr):
    c = pl.program_id(1)

    @pl.when(c == 0)
    def _():
        s_scr[...] = s0_ref[...]

    ri = lax.broadcasted_iota(jnp.int32, (chunk, chunk), 0)
    ci = lax.broadcasted_iota(jnp.int32, (chunk, chunk), 1)
    incl = ri >= ci
    strict = ri > ci
    tri = incl.astype(BF)
    eye = (ri == ci).astype(F32)
    n_sq = int(math.log2(chunk)) - 1

    chains = [(bi, h) for bi in range(nb) for h in range(RWKV_HEADS)]
    hs = lambda h: slice(h * RWKV_HEAD, (h + 1) * RWKV_HEAD)

    ra, aa, bt, kt, bh, kh, vb, g_all = [], [], [], [], [], [], [], []
    for bi in range(nb):
        r, lw, k, v = r_ref[bi], lw_ref[bi], k_ref[bi], v_ref[bi]
        a, b = a_ref[bi], b_ref[bi]
        hi, lo = _split2(lw)
        lo2 = (lw - hi.astype(F32) - lo.astype(F32)).astype(BF)
        cum = _dot(tri, hi) + _dot(tri, lo) + _dot(tri, lo2)
        tail = cum[chunk - 1:chunk, :]
        g_inv = jnp.exp(-cum)
        g_tail = jnp.exp(tail - cum)
        g_all.append(jnp.exp(tail))
        ra.append((r * jnp.exp(cum)).astype(BF))
        aa.append((a * jnp.exp(cum - lw)).astype(BF))
        bt.append((b * g_inv).astype(BF))
        kt.append((k * g_inv).astype(BF))
        bh.append((b * g_tail).astype(BF))
        kh.append((k * g_tail).astype(BF))
        vb.append(v.astype(BF))

    s_old = [s_scr[bi, h] for bi, h in chains]
    ar = [jnp.concatenate([aa[bi][:, hs(h)], ra[bi][:, hs(h)]], axis=0) for bi, h in chains]
    v_h = [vb[bi][:, hs(h)] for bi, h in chains]
    gb = [_dot_nt(ar[n], bt[bi][:, hs(h)]) for n, (bi, h) in enumerate(chains)]
    gk = [_dot_nt(ar[n], kt[bi][:, hs(h)]) for n, (bi, h) in enumerate(chains)]
    p = [_dot_nt(ar[n], s_old[n].astype(BF)) for n in range(len(chains))]
    l_ab = [jnp.where(strict, x[:chunk], 0.0) for x in gb]
    l_ak = [jnp.where(strict, x[:chunk], 0.0).astype(BF) for x in gk]
    m_rb = [jnp.where(incl, x[chunk:], 0.0).astype(BF) for x in gb]
    m_rk = [jnp.where(incl, x[chunk:], 0.0).astype(BF) for x in gk]
    rhs = [p[n][:chunk] + _dot(l_ak[n], v_h[n]) for n in range(len(chains))]
    t_inv = [eye + x for x in l_ab]
    lp = l_ab
    for _ in range(n_sq):
        lpb = [x.astype(BF) for x in lp]
        lp = [_dot(x, x) for x in lpb]
        t_inv = [t + _dot(x.astype(BF), t.astype(BF)) for x, t in zip(lp, t_inv)]
    ub = [_dot(t.astype(BF), x.astype(BF)).astype(BF) for t, x in zip(t_inv, rhs)]
    y_h = [p[n][chunk:] + _dot(m_rb[n], ub[n]) + _dot(m_rk[n], v_h[n]) for n in range(len(chains))]
    s_new = [s_old[n] * g_all[bi][:, hs(h)] + _dot_tn(ub[n], bh[bi][:, hs(h)]) + _dot_tn(v_h[n], kh[bi][:, hs(h)])
             for n, (bi, h) in enumerate(chains)]
    for n, (bi, h) in enumerate(chains):
        s_scr[bi, h] = s_new[n]

    for bi in range(nb):
        y = jnp.concatenate(y_h[bi * RWKV_HEADS:(bi + 1) * RWKV_HEADS], axis=-1)
        y_ref[bi] = _rwkv_epilogue(y, r_ref[bi], k_ref[bi], v_ref[bi], g_ref[bi],
                                   rk_ref, lg_ref, lb_ref, ones_ref).astype(y_ref.dtype)

    @pl.when(c == pl.num_programs(1) - 1)
    def _():
        sout_ref[...] = s_scr[...]


def _rwkv_epilogue(y, r, k, v, g, rk_ref, lg_ref, lb_ref, ones_ref):
    ones = ones_ref[...]
    inv_n = 1.0 / RWKV_HEAD
    mean = _dot_exact_rhs(y, ones) * inv_n
    d = y - mean
    var = _dot_exact_rhs(d * d, ones) * inv_n
    yn = d * lax.rsqrt(var + GN_EPS) * lg_ref[...] + lb_ref[...]
    bonus = _dot_exact_rhs(r * k * rk_ref[...], ones) * v
    return (yn + bonus) * g


def _rwkv_scan(coef, s0, p, nb, chunk):
    r = coef[0]
    b, t, _ = r.shape
    seq_spec = pl.BlockSpec((nb, chunk, RWKV_W), lambda bi, c: (bi, c, 0))
    st_spec = pl.BlockSpec((nb, RWKV_HEADS, RWKV_HEAD, RWKV_HEAD), lambda bi, c: (bi, 0, 0, 0))
    vec = _const_spec((1, RWKV_W))
    return pl.pallas_call(
        functools.partial(_rwkv_scan_kernel, nb, chunk),
        grid=(b // nb, t // chunk),
        in_specs=[seq_spec] * 7 + [st_spec, vec, vec, vec, _const_spec((RWKV_W, RWKV_W))],
        out_specs=[seq_spec, st_spec],
        out_shape=[jax.ShapeDtypeStruct((b, t, RWKV_W), BF),
                   jax.ShapeDtypeStruct((b, RWKV_HEADS, RWKV_HEAD, RWKV_HEAD), F32)],
        scratch_shapes=[pltpu.VMEM((nb, RWKV_HEADS, RWKV_HEAD, RWKV_HEAD), F32)],
        compiler_params=_params("parallel", "arbitrary"),
        name="rwkv_scan",
    )(*coef, s0, p["r_k"], p["lnx_g"], p["lnx_b"], p["ones_bd"])


def _rwkv_step_kernel(r_ref, lw_ref, k_ref, v_ref, a_ref, b_ref, g_ref, s_ref,
                      rk_ref, lg_ref, lb_ref, ones_ref, y_ref, sout_ref, y_scr):
    n_pair = LANES // RWKV_HEAD
    r, k, v = r_ref[...], k_ref[...], v_ref[...]
    rT, kT, vT = r.T, k.T, v.T
    aT, bT, wT = a_ref[...].T, b_ref[...].T, jnp.exp(lw_ref[...]).T
    for hl in range(LANES // RWKV_HEAD):
        hsl = slice(hl * RWKV_HEAD, (hl + 1) * RWKV_HEAD)
        a_h, b_h, k_h, w_h, r_h = aT[hsl], bT[hsl], kT[hsl], wT[hsl], rT[hsl]
        tiles = [(hl * RWKV_HEAD + n_pair * t) * RWKV_HEAD for t in range(RWKV_HEAD // n_pair)]
        st = [s_ref[:, c0:c0 + LANES].T for c0 in tiles]
        new = []
        for t, x in enumerate(st):
            halves = []
            for il in range(n_pair):
                i = hl * RWKV_HEAD + n_pair * t + il
                slab = x[il * RWKV_HEAD:(il + 1) * RWKV_HEAD]
                sa = jnp.sum(slab * a_h, axis=0, keepdims=True)
                slab = slab * w_h + sa * b_h + vT[i:i + 1] * k_h
                y_scr[i:i + 1, :] = jnp.sum(slab * r_h, axis=0, keepdims=True)
                halves.append(slab)
            new.append(jnp.concatenate(halves, axis=0))
        for c0, x in zip(tiles, new):
            sout_ref[:, c0:c0 + LANES] = x.T
    y_ref[...] = _rwkv_epilogue(y_scr[...].T, r, k, v, g_ref[...],
                                rk_ref, lg_ref, lb_ref, ones_ref).astype(y_ref.dtype)


def _rwkv_step(coef, s0, p):
    b = s0.shape[0]
    per_pair = (LANES // RWKV_HEAD) * RWKV_HEAD * RWKV_HEAD
    n_steps = RWKV_W // LANES
    col = pl.BlockSpec((b, LANES), lambda i: (0, i))
    vec = pl.BlockSpec((1, LANES), lambda i: (0, i))
    st_spec = pl.BlockSpec((b, per_pair), lambda i: (0, i))
    y, s_new = pl.pallas_call(
        _rwkv_step_kernel,
        grid=(n_steps,),
        in_specs=[col] * 7 + [st_spec, vec, vec, vec, _const_spec((LANES, LANES))],
        out_specs=[col, st_spec],
        out_shape=[jax.ShapeDtypeStruct((b, RWKV_W), BF),
                   jax.ShapeDtypeStruct((b, n_steps * per_pair), F32)],
        scratch_shapes=[pltpu.VMEM((LANES, b), F32)],
        compiler_params=_params("parallel"),
        name="rwkv_step",
    )(*coef, s0.reshape(b, n_steps * per_pair), p["r_k"], p["lnx_g"], p["lnx_b"],
      p["ones_bd"][:LANES, :LANES])
    return y, s_new.reshape(s0.shape)


def _merge_kernel(with_router, x_ref, ya_ref, yb_ref, gate_ref, wa_ref, wb_ref, wo_ref, g_ref, *rest):
    if with_router:
        wr_ref, cnt_in_ref, x1_ref, h_ref, route_ref, cnt_out_ref, cnt_scr = rest
    else:
        x1_ref, h_ref = rest
    gate = gate_ref[...]
    pa = _dot(ya_ref[...], wa_ref[...])
    pb = _dot(yb_ref[...], wb_ref[...])
    merged = _sigmoid(gate[:, :D_MODEL]) * pa + _sigmoid(gate[:, D_MODEL:]) * pb
    x1 = x_ref[...] + _dot(merged.astype(BF), wo_ref[...])
    x1_ref[...] = x1
    h = _rms(x1, g_ref[...])
    h_ref[...] = h.astype(h_ref.dtype)
    if not with_router:
        return

    @pl.when(pl.program_id(0) == 0)
    def _():
        cnt_scr[...] = cnt_in_ref[...]

    logits = _dot(h.astype(BF), wr_ref[...])
    tm = logits.shape[0]
    lane = lax.broadcasted_iota(jnp.int32, logits.shape, 1).astype(F32)
    logits = jnp.where(lane < N_EXPERTS, logits, -jnp.inf)
    m1 = jnp.max(logits, axis=-1, keepdims=True)
    i1 = jnp.min(jnp.where(logits == m1, lane, float(LANES)), axis=-1, keepdims=True)
    rest_l = jnp.where(lane == i1, -jnp.inf, logits)
    m2 = jnp.max(rest_l, axis=-1, keepdims=True)
    i2 = jnp.min(jnp.where(rest_l == m2, lane, float(LANES)), axis=-1, keepdims=True)
    e2 = jnp.exp(m2 - m1)
    g1 = 1.0 / (1.0 + e2)
    g2 = e2 / (1.0 + e2)
    oh1 = (lane == i1).astype(F32)
    oh2 = (lane == i2).astype(F32)
    both = oh1 + oh2
    ri = lax.broadcasted_iota(jnp.int32, (tm, tm), 0)
    ci = lax.broadcasted_iota(jnp.int32, (tm, tm), 1)
    before = _dot((ri > ci).astype(BF), both.astype(BF)) + cnt_scr[...]
    rank1 = jnp.sum(oh1 * before, axis=-1, keepdims=True)
    rank2 = jnp.sum(oh2 * before, axis=-1, keepdims=True)
    cnt_scr[...] += jnp.sum(both, axis=0, keepdims=True)
    cnt_out_ref[...] = cnt_scr[...]
    route = jnp.zeros_like(logits)
    for n, val in enumerate((i1, i2, g1, g2, rank1, rank2)):
        route = jnp.where(lane == n, val, route)
    route_ref[...] = route


def _merge_out(x, ya, yb, gate, wa, wb, wo, g, tm, router=None, cnt_in=None):
    t = x.shape[0]
    row = lambda w_: pl.BlockSpec((tm, w_), lambda i: (i, 0))
    in_specs = [row(D_MODEL), row(ATT_W), row(RWKV_W), row(2 * D_MODEL),
                _const_spec((ATT_W, D_MODEL)), _const_spec((RWKV_W, D_MODEL)),
                _const_spec((D_MODEL, D_MODEL)), _const_spec((1, D_MODEL))]
    args = [x, ya, yb, gate, wa, wb, wo, g]
    if router is None:
        out_specs = [row(D_MODEL), row(D_MODEL)]
        out_shape = [jax.ShapeDtypeStruct((t, D_MODEL), F32), jax.ShapeDtypeStruct((t, D_MODEL), BF)]
        scratch = []
    else:
        in_specs += [_const_spec(router.shape), _const_spec((1, LANES))]
        args += [router, cnt_in]
        out_specs = [row(D_MODEL), row(D_MODEL), row(LANES), pl.BlockSpec((1, LANES), lambda i: (0, 0))]
        out_shape = [jax.ShapeDtypeStruct((t, D_MODEL), F32), jax.ShapeDtypeStruct((t, D_MODEL), F32),
                     jax.ShapeDtypeStruct((t, LANES), F32), jax.ShapeDtypeStruct((1, LANES), F32)]
        scratch = [pltpu.VMEM((1, LANES), F32)]
    return pl.pallas_call(
        functools.partial(_merge_kernel, router is not None),
        grid=(t // tm,),
        in_specs=in_specs, out_specs=out_specs, out_shape=out_shape, scratch_shapes=scratch,
        compiler_params=_params("parallel" if router is None else "arbitrary"),
        name="merge_out",
    )(*args)


def _swiglu(x, wg, wu, wd):
    s = _dot(x, wg)
    u = _dot(x, wu)
    act = (s * _sigmoid(s) * u).astype(BF)
    return _dot(act, wd)


def _dense_ffn_kernel(x1_ref, h_ref, wg_ref, wu_ref, wd_ref, o_ref):
    o_ref[...] = x1_ref[...] + _swiglu(h_ref[...], wg_ref[...], wu_ref[...], wd_ref[...])


def _dense_ffn(x1, h, wg, wu, wd, tm):
    t = x1.shape[0]
    d_ff = wg.shape[1]
    row = lambda: pl.BlockSpec((tm, D_MODEL), lambda i: (i, 0))
    return pl.pallas_call(
        _dense_ffn_kernel,
        grid=(t // tm,),
        in_specs=[row(), row(), _const_spec((D_MODEL, d_ff)), _const_spec((D_MODEL, d_ff)),
                  _const_spec((d_ff, D_MODEL))],
        out_specs=row(),
        out_shape=jax.ShapeDtypeStruct((t, D_MODEL), F32),
        compiler_params=_params("parallel"),
        name="dense_ffn",
    )(x1, h, wg, wu, wd)


def _dispatch_kernel(tt, first, dest_ref, fill_ref, h_ref, *rest):
    if first:
        xs_hbm, zbuf, sem, zsem = rest
    else:
        _, xs_hbm, sem = rest
    i = pl.program_id(0)
    base = i * tt

    if first:
        @pl.when(i == 0)
        def _():
            zbuf[...] = jnp.zeros_like(zbuf)
            bm = zbuf.shape[0]
            fills = [pltpu.make_async_copy(zbuf, xs_hbm.at[pl.ds(fill_ref[n] * bm, bm), :], zsem)
                     for n in range(fill_ref.shape[0])]
            for f in fills:
                f.start()
            for f in fills:
                f.wait()

    def issue(r, carry):
        for j in range(TOP_K):
            row = dest_ref[TOP_K * (base + r) + j]
            pltpu.make_async_copy(h_ref.at[pl.ds(r, 1), :], xs_hbm.at[pl.ds(row, 1), :], sem).start()
        return carry

    lax.fori_loop(0, tt, issue, 0, unroll=8)
    for j in range(TOP_K):
        pltpu.make_async_copy(h_ref, xs_hbm.at[pl.ds(0, tt), :], sem).wait()


def _dispatch(dest, fill_blocks, h, xs, tt, bm, n_rows):
    t = h.shape[0]
    first = xs is None
    in_specs = [pl.BlockSpec((tt, D_MODEL), lambda i, ds, fb: (i, 0))]
    args = [dest, fill_blocks, h]
    if first:
        scratch = [pltpu.VMEM((bm, D_MODEL), F32), pltpu.SemaphoreType.DMA(()), pltpu.SemaphoreType.DMA(())]
        aliases = {}
    else:
        in_specs.append(pl.BlockSpec(memory_space=pl.ANY))
        args.append(xs)
        scratch = [pltpu.SemaphoreType.DMA(())]
        aliases = {3: 0}
    grid_spec = pltpu.PrefetchScalarGridSpec(
        num_scalar_prefetch=2, grid=(t // tt,), in_specs=in_specs,
        out_specs=pl.BlockSpec(memory_space=pl.ANY), scratch_shapes=scratch)
    return pl.pallas_call(
        functools.partial(_dispatch_kernel, tt, first),
        grid_spec=grid_spec,
        out_shape=jax.ShapeDtypeStruct((n_rows, D_MODEL), F32),
        input_output_aliases=aliases,
        compiler_params=_params("arbitrary"),
        name="moe_dispatch",
    )(*args)


def _moe_kernel(n_half, blk_e_ref, n_used_ref, xs_ref, wg_ref, wu_ref, wd_ref, y_ref):
    @pl.when(pl.program_id(0) < n_used_ref[0])
    def _():
        x = xs_ref[...].astype(BF)
        d_e = wg_ref.shape[2]
        step = d_e // n_half
        acc = None
        for j in range(n_half):
            cs = slice(j * step, (j + 1) * step)
            part = _swiglu(x, wg_ref[0, :, cs], wu_ref[0, :, cs], wd_ref[0, cs, :])
            acc = part if acc is None else acc + part
        y_ref[...] = acc

    @pl.when(pl.program_id(0) >= n_used_ref[0])
    def _():
        y_ref[...] = jnp.zeros_like(y_ref)


def _moe_ffn(xs, blk_e, n_used, wg, wu, wd, bm):
    n_blk = blk_e.shape[0]
    n_rows = xs.shape[0]
    d_e = wg.shape[2]
    wspec = lambda shape: pl.BlockSpec(shape, lambda i, be, nu: (be[i], 0, 0), pipeline_mode=pl.Buffered(1))
    rows = pl.BlockSpec((bm, D_MODEL), lambda i, be, nu: (i, 0))
    grid_spec = pltpu.PrefetchScalarGridSpec(
        num_scalar_prefetch=2,
        grid=(n_blk,),
        in_specs=[rows, wspec((1, D_MODEL, d_e)), wspec((1, D_MODEL, d_e)), wspec((1, d_e, D_MODEL))],
        out_specs=rows,
    )
    return pl.pallas_call(
        functools.partial(_moe_kernel, 2),
        grid_spec=grid_spec,
        out_shape=jax.ShapeDtypeStruct((n_rows, D_MODEL), F32),
        compiler_params=_params("arbitrary"),
        name="moe_ffn",
    )(blk_e, n_used, xs, wg, wu, wd)


def _row_layout(counts, bm, n_blk):
    per_e = (counts + bm - 1) // bm
    ends = jnp.cumsum(per_e)
    n_used = ends[-1]
    first_row = (ends - per_e) * bm
    i = jnp.arange(n_blk, dtype=jnp.int32)
    blk_e = jnp.sum((jnp.minimum(i, jnp.maximum(n_used - 1, 0))[:, None] >= ends[None, :]).astype(jnp.int32), axis=1)
    need = jnp.any(i[:, None] == (ends - 1)[None, :], axis=1) | (i >= n_blk - N_EXPERTS)
    fill_blocks = jnp.argsort(jnp.logical_not(need), stable=True)[:2 * N_EXPERTS].astype(jnp.int32)
    return first_row.astype(jnp.int32), blk_e.astype(jnp.int32), n_used.astype(jnp.int32).reshape(1), fill_blocks


def _combine_kernel(tc, dest_ref, x1_ref, route_ref, y_hbm, g_ref, o_ref, ybuf, sem):
    base = pl.program_id(0) * tc

    def issue(r, carry):
        for j in range(TOP_K):
            row = dest_ref[TOP_K * (base + r) + j]
            pltpu.make_async_copy(y_hbm.at[pl.ds(row, 1), :], ybuf.at[j, pl.ds(r, 1), :], sem).start()
        return carry

    lax.fori_loop(0, tc, issue, 0, unroll=4)
    for j in range(TOP_K):
        pltpu.make_async_copy(y_hbm.at[pl.ds(0, tc), :], ybuf.at[j], sem).wait()
    route = route_ref[...]
    x2 = x1_ref[...] + route[:, 2:3] * ybuf[0] + route[:, 3:4] * ybuf[1]
    o_ref[...] = _rms(x2, g_ref[...])


def _moe_combine(dest, x1, route, y, g, tc):
    t = x1.shape[0]
    grid_spec = pltpu.PrefetchScalarGridSpec(
        num_scalar_prefetch=1,
        grid=(t // tc,),
        in_specs=[pl.BlockSpec((tc, D_MODEL), lambda i, ds: (i, 0)),
                  pl.BlockSpec((tc, LANES), lambda i, ds: (i, 0)),
                  pl.BlockSpec(memory_space=pl.ANY),
                  pl.BlockSpec((1, D_MODEL), lambda i, ds: (0, 0))],
        out_specs=pl.BlockSpec((tc, D_MODEL), lambda i, ds: (i, 0)),
        scratch_shapes=[pltpu.VMEM((TOP_K, tc, D_MODEL), F32), pltpu.SemaphoreType.DMA(())],
    )
    return pl.pallas_call(
        functools.partial(_combine_kernel, tc),
        grid_spec=grid_spec,
        out_shape=jax.ShapeDtypeStruct((t, D_MODEL), F32),
        compiler_params=_params("arbitrary"),
        name="moe_combine",
    )(dest, x1, route, y, g)


def _layer_params(l, w_in, mu_shift, w0, w2, a0, a2, g2, k_k, k_a, r_k, lnx_g, lnx_b,
                  w_proj_attn, w_proj_rwkv, w_out, ones_bd):
    row = lambda u: u.reshape(1, -1)
    return dict(
        w_in=w_in[l].astype(BF), mu=row(mu_shift[l]), w0=row(w0[l]), w2=w2[l].astype(BF),
        a0=row(a0[l]), a2=a2[l].astype(BF), g2=g2[l].astype(BF), k_k=row(k_k[l]), k_a=row(k_a[l]),
        r_k=row(r_k[l]), lnx_g=row(lnx_g[l]), lnx_b=row(lnx_b[l]),
        wa=w_proj_attn[l].astype(BF), wb=w_proj_rwkv[l].astype(BF), wo=w_out[l].astype(BF),
        ones_bd=ones_bd)


def kernel(x_prompt, x_sample, cache_k, cache_v, state_wkv, state_shift, norm_mix_g, w_in, attn_sinks, rel_bias, mu_shift, w0, w2, a0, a2, g2, k_k, k_a, r_k, lnx_g, lnx_b, w_proj_attn, w_proj_rwkv, w_out, norm_ffn_g, dense_w_gate, dense_w_up, dense_w_down, router_w, moe_w_gate, moe_w_up, moe_w_down, norm_final_g):
    batch, seq, _ = x_prompt.shape
    dec_batch, dec_seq, _ = x_sample.shape
    assert dec_seq == 1 and seq % WINDOW == 0

    head_id = jnp.arange(RWKV_W, dtype=jnp.int32) // RWKV_HEAD
    ones_bd = (head_id[:, None] == head_id[None, :]).astype(BF)
    bias_by_dist = _bias_lookup(rel_bias, _t5_bucket(np.arange(WINDOW + 1)))
    bias_c = jnp.transpose(bias_by_dist[WINDOW:0:-1], (1, 0))[None]
    bias_n = bias_by_dist[0][None, :, None]

    layers = [_layer_params(l, w_in, mu_shift, w0, w2, a0, a2, g2, k_k, k_a, r_k, lnx_g, lnx_b,
                            w_proj_attn, w_proj_rwkv, w_out, ones_bd) for l in range(DEPTH)]

    def router_pieces(l):
        return jnp.pad(router_w[l // 2], ((0, 0), (0, LANES - N_EXPERTS))).astype(BF)

    n_prompt = batch * seq
    tm_p = 512
    bm = 512
    n_all = n_prompt + dec_batch
    n_blk = -(-(n_all * TOP_K) // bm) + N_EXPERTS

    def mix(x, is_prompt, l, state):
        p = layers[l]
        n_tok = x.shape[0]
        gm = norm_mix_g[l].reshape(1, -1)
        sinks = attn_sinks[l]
        if is_prompt:
            b, t = batch, seq
            prev_z = jnp.zeros((b, 1, RWKV_IN), F32)
            q, k, v, gate, z_last, *coef = _inproj(x, gm, p, prev_z, 256, t)
            sink_col = jnp.repeat(sinks, WINDOW).reshape(N_KV_HEADS, GQA_GROUP * WINDOW, 1)
            ya = _swa_prompt(q.reshape(b, t, ATT_W), k.reshape(b, t, KV_W), v.reshape(b, t, KV_W),
                             rel_bias, sink_col).reshape(n_tok, ATT_W)
            k4 = k.reshape(b, t, N_KV_HEADS, HEAD_DIM)
            v4 = v.reshape(b, t, N_KV_HEADS, HEAD_DIM)
            state["k"].append(k4[:, t - WINDOW:])
            state["v"].append(v4[:, t - WINDOW:])
            s0 = jnp.zeros((b, RWKV_HEADS, RWKV_HEAD, RWKV_HEAD), F32)
            yb3, s_new = _rwkv_scan([u.reshape(b, t, RWKV_W) for u in coef], s0, p, b, 64)
            yb = yb3.reshape(n_tok, RWKV_W)
            state["z"].append(z_last.reshape(b, RWKV_IN))
        else:
            b = dec_batch
            q, k, v, gate, z, *coef = _inproj(x, gm, p, state_shift[l], n_tok, None)
            ya3, nk, nv = _swa_sample(q.reshape(b, N_Q_HEADS, HEAD_DIM), k, v, cache_k, cache_v, l,
                                      bias_c, bias_n, sinks.reshape(1, N_Q_HEADS, 1), 16)
            ya = ya3.reshape(b, ATT_W)
            state["k"].append(nk)
            state["v"].append(nv)
            yb, s_new = _rwkv_step(coef, state_wkv[l], p)
            state["z"].append(z)
        state["s"].append(s_new)
        return ya, yb, gate

    groups = [dict(x=x_prompt.reshape(n_prompt, D_MODEL), prompt=True, tm=tm_p, k=[], v=[], s=[], z=[]),
              dict(x=x_sample.reshape(dec_batch, D_MODEL), prompt=False, tm=dec_batch, k=[], v=[], s=[], z=[])]
    for l in range(DEPTH):
        p = layers[l]
        gn = norm_ffn_g[l].reshape(1, -1)
        j = l // 2
        if l % 2 == 0:
            wg, wu, wd = (_to_bf16(w[j], w.shape[1] // 4) for w in (dense_w_gate, dense_w_up, dense_w_down))
            for grp in groups:
                ya, yb, gate = mix(grp["x"], grp["prompt"], l, grp)
                x1, h = _merge_out(grp["x"], ya, yb, gate, p["wa"], p["wb"], p["wo"], gn, grp["tm"])
                grp["x"] = _dense_ffn(x1, h, wg, wu, wd, grp["tm"])
        else:
            assert l == DEPTH - 1
            wg, wu, wd = (w[j].astype(BF) for w in (moe_w_gate, moe_w_up, moe_w_down))
            router = router_pieces(l)
            cnt = jnp.zeros((1, LANES), F32)
            xs = None
            for grp in groups:
                ya, yb, gate = mix(grp["x"], grp["prompt"], l, grp)
                x1, h, route, cnt = _merge_out(grp["x"], ya, yb, gate, p["wa"], p["wb"], p["wo"], gn,
                                               min(256, grp["tm"]), router=router, cnt_in=cnt)
                grp["x1"], grp["route"], grp["h"] = x1, route, h
            counts = cnt[0, :N_EXPERTS].astype(jnp.int32)
            first_row, blk_e, n_used, fill_blocks = _row_layout(counts, bm, n_blk)
            for grp in groups:
                route = grp["route"]
                expert = route[:, 0:TOP_K].astype(jnp.int32)
                rank = route[:, 2 * TOP_K:3 * TOP_K].astype(jnp.int32)
                start = jnp.sum(jnp.where(expert[..., None] == jnp.arange(N_EXPERTS, dtype=jnp.int32),
                                          first_row, 0), axis=-1)
                grp["dest"] = (start + rank).reshape(-1)
                xs = _dispatch(grp["dest"], fill_blocks, grp["h"], xs, min(2048, grp["h"].shape[0]), bm,
                               n_blk * bm)
            y = _moe_ffn(xs, blk_e, n_used, wg, wu, wd, bm)
            for grp in groups:
                grp["x"] = _moe_combine(grp["dest"], grp["x1"], grp["route"], y,
                                        norm_final_g.reshape(1, -1), min(1024, grp["x1"].shape[0]))

    gp, gs = groups
    outs = []
    for grp, shape in ((gp, (batch, seq, D_MODEL)), (gs, (dec_batch, dec_seq, D_MODEL))):
        outs.append((grp["x"].reshape(shape), jnp.stack(grp["k"]), jnp.stack(grp["v"]),
                     jnp.stack(grp["s"]), jnp.stack(grp["z"])))
    (y_p, nk_p, nv_p, ns_p, nz_p), (y_s, nk_s, nv_s, ns_s, nz_s) = outs
    return (y_p, y_s, nk_p, nv_p, ns_p, nz_p, nk_s, nv_s, ns_s, nz_s)
```

```python
import functools
import math

import jax
import jax.numpy as jnp
import numpy as np
from jax import lax
from jax.experimental import pallas as pl
from jax.experimental.pallas import tpu as pltpu

BF = jnp.bfloat16
F32 = jnp.float32

D_MODEL = 1024
DEPTH = 2
HEAD_DIM = 64
N_Q_HEADS = 8
N_KV_HEADS = 2
GQA_GROUP = 4
ATT_W = 512
KV_W = 128
WINDOW = 128
ATT_SCALE = HEAD_DIM ** -0.5
NUM_BUCKETS = 32
MAX_EXACT = 16
MAX_DISTANCE = 128
NEG_INF = -1e30
RWKV_HEAD = 64
RWKV_W = 512
RWKV_HEADS = 8
D_DECAY_LORA = 64
D_AAA_LORA = 64
D_GATE_LORA = 128
RWKV_IN = 3 * RWKV_W + D_DECAY_LORA + D_AAA_LORA + D_GATE_LORA
GN_EPS = 64e-5
IN_W = ATT_W + 2 * KV_W + RWKV_IN + 2 * D_MODEL
N_EXPERTS = 8
TOP_K = 2
NORM_EPS = 1e-6

VMEM_LIMIT_BYTES = 56 * 1024 * 1024
LANES = 128


def _params(*sem):
    return pltpu.CompilerParams(dimension_semantics=sem, vmem_limit_bytes=VMEM_LIMIT_BYTES)


def _const_spec(shape):
    zeros = (0,) * len(shape)
    return pl.BlockSpec(shape, lambda *_: zeros, pipeline_mode=pl.Buffered(1))


def _dot(a, b):
    return jnp.dot(a, b, preferred_element_type=F32)


def _dot_nt(a, b):
    return lax.dot_general(a, b, (((1,), (1,)), ((), ())), preferred_element_type=F32)


def _dot_tn(a, b):
    return lax.dot_general(a, b, (((0,), (0,)), ((), ())), preferred_element_type=F32)


def _split2(x):
    hi = x.astype(BF)
    lo = (x - hi.astype(F32)).astype(BF)
    return hi, lo


def _dot_exact_rhs(x, w):
    hi, lo = _split2(x)
    return _dot(hi, w) + _dot(lo, w)


def _rms(x, g):
    ms = jnp.mean(x * x, axis=-1, keepdims=True)
    return x * lax.rsqrt(ms + NORM_EPS) * g


def _sigmoid(x):
    return 1.0 / (1.0 + jnp.exp(-x))


def _cast_kernel(x_ref, o_ref):
    o_ref[...] = x_ref[...].astype(o_ref.dtype)


def _to_bf16(w, row_tile):
    rows, cols = w.shape
    spec = pl.BlockSpec((row_tile, cols), lambda i: (i, 0))
    return pl.pallas_call(
        _cast_kernel, grid=(rows // row_tile,), in_specs=[spec], out_specs=spec,
        out_shape=jax.ShapeDtypeStruct((rows, cols), BF),
        compiler_params=_params("parallel"), name="to_bf16",
    )(w)


def _inproj_kernel(rows_per_seq, x_ref, g_ref, w_ref, pz_ref, mu_ref, w0_ref, w2_ref, a0_ref, a2_ref, g2_ref,
                   kk_ref, ka_ref, ones_ref,
                   q_ref, k_ref, v_ref, gate_ref, zl_ref,
                   r_ref, lw_ref, k2_ref, vr_ref, a_ref, b_ref, gr_ref, *carry):
    n = _rms(x_ref[...], g_ref[...]).astype(BF)
    z0 = ATT_W + 2 * KV_W
    z = _dot(n, w_ref[:, z0:z0 + RWKV_IN])
    q_ref[...] = _dot(n, w_ref[:, 0:ATT_W]).astype(BF)
    k_ref[...] = _dot(n, w_ref[:, ATT_W:ATT_W + KV_W])
    v_ref[...] = _dot(n, w_ref[:, ATT_W + KV_W:ATT_W + 2 * KV_W])
    gate_ref[...] = _dot(n, w_ref[:, z0 + RWKV_IN:IN_W]).astype(gate_ref.dtype)
    tm = z.shape[0]
    if rows_per_seq is None:
        zs = pz_ref[...]
        zl_ref[...] = z
    else:
        prev_ref, = carry
        tiles_per_seq = rows_per_seq // tm

        @pl.when(pl.program_id(0) % tiles_per_seq == 0)
        def _():
            prev_ref[...] = pz_ref[0]

        row = lax.broadcasted_iota(jnp.int32, (tm, 1), 0)
        zs = jnp.where(row == 0, prev_ref[...], pltpu.roll(z, 1, 0))
        prev_ref[...] = z[tm - 1:tm, :]
        zl_ref[0] = z[tm - 1:tm, :]
    zz = z + (zs - z) * mu_ref[...]
    r = zz[:, 0:RWKV_W]
    k = zz[:, RWKV_W:2 * RWKV_W]
    v = zz[:, 2 * RWKV_W:3 * RWKV_W]
    o = 3 * RWKV_W
    wl = zz[:, o:o + D_DECAY_LORA]
    al = zz[:, o + D_DECAY_LORA:o + D_DECAY_LORA + D_AAA_LORA]
    gl = zz[:, o + D_DECAY_LORA + D_AAA_LORA:RWKV_IN]
    wpre = -(w0_ref[...] + _dot(jnp.tanh(wl).astype(BF), w2_ref[...]))
    softplus = jnp.maximum(wpre, 0.0) + jnp.log1p(jnp.exp(-jnp.abs(wpre)))
    lw = -jnp.exp(-softplus - 0.5)
    a = _sigmoid(a0_ref[...] + _dot(al.astype(BF), a2_ref[...]))
    g = _dot(_sigmoid(gl).astype(BF), g2_ref[...])
    kkv = k * kk_ref[...]
    ss = _dot_exact_rhs(kkv * kkv, ones_ref[...])
    kk = kkv / jnp.maximum(jnp.sqrt(ss), 1e-12)
    k2 = k * (1.0 + (a - 1.0) * ka_ref[...])
    r_ref[...], lw_ref[...], k2_ref[...], vr_ref[...] = r, lw, k2, v
    a_ref[...], b_ref[...], gr_ref[...] = -kk, kk * a, g


def _inproj(x, g, p, prev_z, tm, rows_per_seq):
    t = x.shape[0]
    row = lambda w_: pl.BlockSpec((tm, w_), lambda i: (i, 0))
    vec = lambda w_: _const_spec((1, w_))
    if rows_per_seq is None:
        pz_spec = row(RWKV_IN)
        zl_spec, zl_shape = row(RWKV_IN), (t, RWKV_IN)
        scratch = []
    else:
        tiles_per_seq = rows_per_seq // tm
        pz_spec = pl.BlockSpec((1, 1, RWKV_IN), lambda i: (i // tiles_per_seq, 0, 0))
        zl_spec, zl_shape = pz_spec, (t // rows_per_seq, 1, RWKV_IN)
        scratch = [pltpu.VMEM((1, RWKV_IN), F32)]
    return pl.pallas_call(
        functools.partial(_inproj_kernel, rows_per_seq),
        grid=(t // tm,),
        in_specs=[row(D_MODEL), vec(D_MODEL), _const_spec((D_MODEL, IN_W)), pz_spec,
                  vec(RWKV_IN), vec(RWKV_W), _const_spec((D_DECAY_LORA, RWKV_W)),
                  vec(RWKV_W), _const_spec((D_AAA_LORA, RWKV_W)), _const_spec((D_GATE_LORA, RWKV_W)),
                  vec(RWKV_W), vec(RWKV_W), _const_spec((RWKV_W, RWKV_W))],
        out_specs=[row(ATT_W), row(KV_W), row(KV_W), row(2 * D_MODEL), zl_spec] + [row(RWKV_W)] * 7,
        out_shape=[jax.ShapeDtypeStruct((t, ATT_W), BF),
                   jax.ShapeDtypeStruct((t, KV_W), F32),
                   jax.ShapeDtypeStruct((t, KV_W), F32),
                   jax.ShapeDtypeStruct((t, 2 * D_MODEL), BF),
                   jax.ShapeDtypeStruct(zl_shape, F32)] + [jax.ShapeDtypeStruct((t, RWKV_W), F32)] * 7,
        scratch_shapes=scratch,
        compiler_params=_params("arbitrary"),
        name="inproj",
    )(x, g, p["w_in"], prev_z, p["mu"], p["w0"], p["w2"], p["a0"], p["a2"], p["g2"], p["k_k"], p["k_a"],
      p["ones_bd"])


def _t5_bucket(dist):
    n = np.maximum(dist, 0)
    nf = np.maximum(n, 1).astype(np.float32)
    large = MAX_EXACT + (np.log(nf / MAX_EXACT) / math.log(MAX_DISTANCE / MAX_EXACT)
                         * (NUM_BUCKETS - MAX_EXACT)).astype(np.int32)
    return np.where(n < MAX_EXACT, n, np.minimum(large, NUM_BUCKETS - 1)).astype(np.int32)


def _bias_lookup(rel_bias, bucket):
    hit = bucket[..., None, None] == np.arange(NUM_BUCKETS, dtype=np.int32)[:, None]
    return jnp.sum(jnp.where(hit, rel_bias, 0.0), axis=-2)


def _prompt_bucket_table():
    dist = np.arange(WINDOW)[:, None] - (np.arange(2 * WINDOW) - WINDOW)[None, :]
    return np.where((dist >= 0) & (dist <= WINDOW), _t5_bucket(dist), -1).astype(np.int32)


def _swa_prompt_kernel(nb, bucket_ref, relb_ref, q_ref, kp_ref, kc_ref, vp_ref, vc_ref, sink_ref, o_ref, bias_ref):
    first = pl.program_id(0) == 0

    @pl.when(first)
    def _():
        bucket = bucket_ref[...]
        for head in range(N_Q_HEADS):
            tab = jnp.full(bucket.shape, NEG_INF, F32)
            for n in range(NUM_BUCKETS):
                tab = jnp.where(bucket == n, relb_ref[n, head], tab)
            g = head % GQA_GROUP
            bias_ref[head // GQA_GROUP, g * WINDOW:(g + 1) * WINDOW, :] = tab

    col = lax.broadcasted_iota(jnp.int32, (GQA_GROUP * WINDOW, 2 * WINDOW), 1)
    pad_mask = jnp.logical_and(first, col < WINDOW)
    chains = [(b, h) for b in range(nb) for h in range(N_KV_HEADS)]
    hs = lambda h: slice(h * HEAD_DIM, (h + 1) * HEAD_DIM)
    kh = [jnp.concatenate([kp_ref[b][:, hs(h)], kc_ref[b][:, hs(h)]], axis=0).astype(BF) for b, h in chains]
    vh = [jnp.concatenate([vp_ref[b][:, hs(h)], vc_ref[b][:, hs(h)]], axis=0).astype(BF) for b, h in chains]
    qh = [jnp.concatenate([q_ref[b][:, hs(GQA_GROUP * h + g)] for g in range(GQA_GROUP)], axis=0)
          for b, h in chains]
    s = [_dot_nt(qh[n], kh[n]) * ATT_SCALE + bias_ref[h] for n, (b, h) in enumerate(chains)]
    s = [jnp.where(pad_mask, NEG_INF, x) for x in s]
    m = [jnp.maximum(jnp.max(x, axis=-1, keepdims=True), sink_ref[h]) for x, (b, h) in zip(s, chains)]
    p = [jnp.exp(x - mm) for x, mm in zip(s, m)]
    denom = [jnp.sum(x, axis=-1, keepdims=True) + jnp.exp(sink_ref[h] - mm)
             for x, mm, (b, h) in zip(p, m, chains)]
    o = [_dot(x.astype(BF), vv) / d for x, vv, d in zip(p, vh, denom)]
    for b in range(nb):
        pieces = [o[b * N_KV_HEADS + h][g * WINDOW:(g + 1) * WINDOW]
                  for h in range(N_KV_HEADS) for g in range(GQA_GROUP)]
        o_ref[b] = jnp.concatenate(pieces, axis=-1).astype(o_ref.dtype)


def _swa_prompt(q3, k3, v3, rel_bias, sink_col):
    batch, seq, _ = q3.shape
    cur = lambda i: (0, i, 0)
    prev = lambda i: (0, jnp.maximum(i - 1, 0), 0)
    kv_c = pl.BlockSpec((batch, WINDOW, KV_W), cur)
    kv_p = pl.BlockSpec((batch, WINDOW, KV_W), prev)
    bucket = jnp.asarray(_prompt_bucket_table())
    return pl.pallas_call(
        functools.partial(_swa_prompt_kernel, batch),
        grid=(seq // WINDOW,),
        in_specs=[_const_spec(bucket.shape), pl.BlockSpec(memory_space=pltpu.SMEM),
                  pl.BlockSpec((batch, WINDOW, ATT_W), cur), kv_p, kv_c, kv_p, kv_c,
                  _const_spec(sink_col.shape)],
        out_specs=pl.BlockSpec((batch, WINDOW, ATT_W), cur),
        out_shape=jax.ShapeDtypeStruct((batch, seq, ATT_W), BF),
        scratch_shapes=[pltpu.VMEM((N_KV_HEADS, GQA_GROUP * WINDOW, 2 * WINDOW), F32)],
        compiler_params=_params("arbitrary"),
        name="swa_prompt",
    )(bucket, rel_bias, q3, k3, k3, v3, v3, sink_col)


def _swa_sample_kernel(q_ref, kn_ref, vn_ref, ck_ref, cv_ref, bc_ref, bn_ref, sink_ref, o_ref):
    q = q_ref[...]
    qf = q.astype(F32)
    kn = kn_ref[...].astype(BF).astype(F32)
    vn = vn_ref[...].astype(BF).astype(F32)
    ck, cv = ck_ref[...], cv_ref[...]
    head = lax.broadcasted_iota(jnp.int32, (1, N_Q_HEADS, 1), 1)
    low = head < GQA_GROUP
    s_h, sn_h = [], []
    for h in range(N_KV_HEADS):
        sl = slice(h * HEAD_DIM, (h + 1) * HEAD_DIM)
        s_h.append(jnp.einsum("bgd,bkd->bgk", q, ck[:, :, sl].astype(BF), preferred_element_type=F32))
        sn_h.append(jnp.sum(qf * kn[:, None, sl], axis=-1, keepdims=True))
    s = jnp.where(low, s_h[0], s_h[1]) * ATT_SCALE + bc_ref[...]
    sn = jnp.where(low, sn_h[0], sn_h[1]) * ATT_SCALE + bn_ref[...]
    sink = sink_ref[...]
    m = jnp.maximum(jnp.maximum(jnp.max(s, axis=-1, keepdims=True), sn), sink)
    p = jnp.exp(s - m)
    pn = jnp.exp(sn - m)
    denom = jnp.sum(p, axis=-1, keepdims=True) + pn + jnp.exp(sink - m)
    pb = p.astype(BF)
    pnb = pn.astype(BF).astype(F32)
    o_h = []
    for h in range(N_KV_HEADS):
        sl = slice(h * HEAD_DIM, (h + 1) * HEAD_DIM)
        o = jnp.einsum("bgk,bkd->bgd", pb, cv[:, :, sl].astype(BF), preferred_element_type=F32)
        o_h.append(o + pnb * vn[:, None, sl])
    o_ref[...] = (jnp.where(low, o_h[0], o_h[1]) / denom).astype(o_ref.dtype)


def _swa_sample(q3, kn, vn, ck, cv, bias_c, bias_n, sink3, bb):
    b = q3.shape[0]
    return pl.pallas_call(
        _swa_sample_kernel,
        grid=(b // bb,),
        in_specs=[pl.BlockSpec((bb, N_Q_HEADS, HEAD_DIM), lambda i: (i, 0, 0)),
                  pl.BlockSpec((bb, KV_W), lambda i: (i, 0)),
                  pl.BlockSpec((bb, KV_W), lambda i: (i, 0)),
                  pl.BlockSpec((bb, WINDOW, KV_W), lambda i: (i, 0, 0)),
                  pl.BlockSpec((bb, WINDOW, KV_W), lambda i: (i, 0, 0)),
                  _const_spec(bias_c.shape), _const_spec(bias_n.shape), _const_spec(sink3.shape)],
        out_specs=pl.BlockSpec((bb, N_Q_HEADS, HEAD_DIM), lambda i: (i, 0, 0)),
        out_shape=jax.ShapeDtypeStruct((b, N_Q_HEADS, HEAD_DIM), BF),
        compiler_params=_params("parallel"),
        name="swa_sample",
    )(q3, kn, vn, ck, cv, bias_c, bias_n, sink3)


def _rwkv_scan_kernel(nb, chunk, r_ref, lw_ref, k_ref, v_ref, a_ref, b_ref, g_ref, s0_ref,
                      rk_ref, lg_ref, lb_ref, ones_ref, y_ref, sout_ref, s_scr):
    c = pl.program_id(1)

    @pl.when(c == 0)
    def _():
        s_scr[...] = s0_ref[...]

    ri = lax.broadcasted_iota(jnp.int32, (chunk, chunk), 0)
    ci = lax.broadcasted_iota(jnp.int32, (chunk, chunk), 1)
    incl = ri >= ci
    strict = ri > ci
    tri = incl.astype(BF)
    eye = (ri == ci).astype(F32)
    n_sq = int(math.log2(chunk)) - 1

    chains = [(bi, h) for bi in range(nb) for h in range(RWKV_HEADS)]
    hs = lambda h: slice(h * RWKV_HEAD, (h + 1) * RWKV_HEAD)

    ra, aa, bt, kt, bh, kh, vb, g_all = [], [], [], [], [], [], [], []
    for bi in range(nb):
        r, lw, k, v = r_ref[bi], lw_ref[bi], k_ref[bi], v_ref[bi]
        a, b = a_ref[bi], b_ref[bi]
        hi, lo = _split2(lw)
        lo2 = (lw - hi.astype(F32) - lo.astype(F32)).astype(BF)
        cum = _dot(tri, hi) + _dot(tri, lo) + _dot(tri, lo2)
        tail = cum[chunk - 1:chunk, :]
        g_inv = jnp.exp(-cum)
        g_tail = jnp.exp(tail - cum)
        g_all.append(jnp.exp(tail))
        ra.append((r * jnp.exp(cum)).astype(BF))
        aa.append((a * jnp.exp(cum - lw)).astype(BF))
        bt.append((b * g_inv).astype(BF))
        kt.append((k * g_inv).astype(BF))
        bh.append((b * g_tail).astype(BF))
        kh.append((k * g_tail).astype(BF))
        vb.append(v.astype(BF))

    s_old = [s_scr[bi, h] for bi, h in chains]
    ar = [jnp.concatenate([aa[bi][:, hs(h)], ra[bi][:, hs(h)]], axis=0) for bi, h in chains]
    v_h = [vb[bi][:, hs(h)] for bi, h in chains]
    gb = [_dot_nt(ar[n], bt[bi][:, hs(h)]) for n, (bi, h) in enumerate(chains)]
    gk = [_dot_nt(ar[n], kt[bi][:, hs(h)]) for n, (bi, h) in enumerate(chains)]
    p = [_dot_nt(ar[n], s_old[n].astype(BF)) for n in range(len(chains))]
    l_ab = [jnp.where(strict, x[:chunk], 0.0) for x in gb]
    l_ak = [jnp.where(strict, x[:chunk], 0.0).astype(BF) for x in gk]
    m_rb = [jnp.where(incl, x[chunk:], 0.0).astype(BF) for x in gb]
    m_rk = [jnp.where(incl, x[chunk:], 0.0).astype(BF) for x in gk]
    rhs = [p[n][:chunk] + _dot(l_ak[n], v_h[n]) for n in range(len(chains))]
    t_inv = [eye + x for x in l_ab]
    lp = l_ab
    for _ in range(n_sq):
        lpb = [x.astype(BF) for x in lp]
        lp = [_dot(x, x) for x in lpb]
        t_inv = [t + _dot(x.astype(BF), t.astype(BF)) for x, t in zip(lp, t_inv)]
    ub = [_dot(t.astype(BF), x.astype(BF)).astype(BF) for t, x in zip(t_inv, rhs)]
    y_h = [p[n][chunk:] + _dot(m_rb[n], ub[n]) + _dot(m_rk[n], v_h[n]) for n in range(len(chains))]
    s_new = [s_old[n] * g_all[bi][:, hs(h)] + _dot_tn(ub[n], bh[bi][:, hs(h)]) + _dot_tn(v_h[n], kh[bi][:, hs(h)])
             for n, (bi, h) in enumerate(chains)]
    for n, (bi, h) in enumerate(chains):
        s_scr[bi, h] = s_new[n]

    for bi in range(nb):
        y = jnp.concatenate(y_h[bi * RWKV_HEADS:(bi + 1) * RWKV_HEADS], axis=-1)
        y_ref[bi] = _rwkv_epilogue(y, r_ref[bi], k_ref[bi], v_ref[bi], g_ref[bi],
                                   rk_ref, lg_ref, lb_ref, ones_ref).astype(y_ref.dtype)

    @pl.when(c == pl.num_programs(1) - 1)
    def _():
        sout_ref[...] = s_scr[...]


def _rwkv_epilogue(y, r, k, v, g, rk_ref, lg_ref, lb_ref, ones_ref):
    ones = ones_ref[...]
    inv_n = 1.0 / RWKV_HEAD
    mean = _dot_exact_rhs(y, ones) * inv_n
    d = y - mean
    var = _dot_exact_rhs(d * d, ones) * inv_n
    yn = d * lax.rsqrt(var + GN_EPS) * lg_ref[...] + lb_ref[...]
    bonus = _dot_exact_rhs(r * k * rk_ref[...], ones) * v
    return (yn + bonus) * g


def _rwkv_scan(coef, s0, p, nb, chunk):
    r = coef[0]
    b, t, _ = r.shape
    seq_spec = pl.BlockSpec((nb, chunk, RWKV_W), lambda bi, c: (bi, c, 0))
    st_spec = pl.BlockSpec((nb, RWKV_HEADS, RWKV_HEAD, RWKV_HEAD), lambda bi, c: (bi, 0, 0, 0))
    vec = _const_spec((1, RWKV_W))
    return pl.pallas_call(
        functools.partial(_rwkv_scan_kernel, nb, chunk),
        grid=(b // nb, t // chunk),
        in_specs=[seq_spec] * 7 + [st_spec, vec, vec, vec, _const_spec((RWKV_W, RWKV_W))],
        out_specs=[seq_spec, st_spec],
        out_shape=[jax.ShapeDtypeStruct((b, t, RWKV_W), BF),
                   jax.ShapeDtypeStruct((b, RWKV_HEADS, RWKV_HEAD, RWKV_HEAD), F32)],
        scratch_shapes=[pltpu.VMEM((nb, RWKV_HEADS, RWKV_HEAD, RWKV_HEAD), F32)],
        compiler_params=_params("parallel", "arbitrary"),
        name="rwkv_scan",
    )(*coef, s0, p["r_k"], p["lnx_g"], p["lnx_b"], p["ones_bd"])


def _rwkv_step_kernel(r_ref, lw_ref, k_ref, v_ref, a_ref, b_ref, g_ref, s_ref,
                      rk_ref, lg_ref, lb_ref, ones_ref, y_ref, sout_ref, y_scr):
    n_pair = LANES // RWKV_HEAD
    r, k, v = r_ref[...], k_ref[...], v_ref[...]
    rT, kT, vT = r.T, k.T, v.T
    aT, bT, wT = a_ref[...].T, b_ref[...].T, jnp.exp(lw_ref[...]).T
    for hl in range(LANES // RWKV_HEAD):
        hsl = slice(hl * RWKV_HEAD, (hl + 1) * RWKV_HEAD)
        a_h, b_h, k_h, w_h, r_h = aT[hsl], bT[hsl], kT[hsl], wT[hsl], rT[hsl]
        tiles = [(hl * RWKV_HEAD + n_pair * t) * RWKV_HEAD for t in range(RWKV_HEAD // n_pair)]
        st = [s_ref[:, c0:c0 + LANES].T for c0 in tiles]
        new = []
        for t, x in enumerate(st):
            halves = []
            for il in range(n_pair):
                i = hl * RWKV_HEAD + n_pair * t + il
                slab = x[il * RWKV_HEAD:(il + 1) * RWKV_HEAD]
                sa = jnp.sum(slab * a_h, axis=0, keepdims=True)
                slab = slab * w_h + sa * b_h + vT[i:i + 1] * k_h
                y_scr[i:i + 1, :] = jnp.sum(slab * r_h, axis=0, keepdims=True)
                halves.append(slab)
            new.append(jnp.concatenate(halves, axis=0))
        for c0, x in zip(tiles, new):
            sout_ref[:, c0:c0 + LANES] = x.T
    y_ref[...] = _rwkv_epilogue(y_scr[...].T, r, k, v, g_ref[...],
                                rk_ref, lg_ref, lb_ref, ones_ref).astype(y_ref.dtype)


def _rwkv_step(coef, s0, p):
    b = s0.shape[0]
    per_pair = (LANES // RWKV_HEAD) * RWKV_HEAD * RWKV_HEAD
    n_steps = RWKV_W // LANES
    col = pl.BlockSpec((b, LANES), lambda i: (0, i))
    vec = pl.BlockSpec((1, LANES), lambda i: (0, i))
    st_spec = pl.BlockSpec((b, per_pair), lambda i: (0, i))
    y, s_new = pl.pallas_call(
        _rwkv_step_kernel,
        grid=(n_steps,),
        in_specs=[col] * 7 + [st_spec, vec, vec, vec, _const_spec((LANES, LANES))],
        out_specs=[col, st_spec],
        out_shape=[jax.ShapeDtypeStruct((b, RWKV_W), BF),
                   jax.ShapeDtypeStruct((b, n_steps * per_pair), F32)],
        scratch_shapes=[pltpu.VMEM((LANES, b), F32)],
        compiler_params=_params("parallel"),
        name="rwkv_step",
    )(*coef, s0.reshape(b, n_steps * per_pair), p["r_k"], p["lnx_g"], p["lnx_b"],
      p["ones_bd"][:LANES, :LANES])
    return y, s_new.reshape(s0.shape)


def _merge_kernel(with_router, x_ref, ya_ref, yb_ref, gate_ref, wa_ref, wb_ref, wo_ref, g_ref, *rest):
    if with_router:
        wr_ref, cnt_in_ref, x1_ref, h_ref, route_ref, cnt_out_ref, cnt_scr = rest
    else:
        x1_ref, h_ref = rest
    gate = gate_ref[...].astype(F32)
    pa = _dot(ya_ref[...], wa_ref[...])
    pb = _dot(yb_ref[...], wb_ref[...])
    merged = _sigmoid(gate[:, :D_MODEL]) * pa + _sigmoid(gate[:, D_MODEL:]) * pb
    x1 = x_ref[...] + _dot(merged.astype(BF), wo_ref[...])
    x1_ref[...] = x1
    h = _rms(x1, g_ref[...])
    h_ref[...] = h.astype(h_ref.dtype)
    if not with_router:
        return

    @pl.when(pl.program_id(0) == 0)
    def _():
        cnt_scr[...] = cnt_in_ref[...]

    logits = _dot(h.astype(BF), wr_ref[...])
    tm = logits.shape[0]
    lane = lax.broadcasted_iota(jnp.int32, logits.shape, 1).astype(F32)
    logits = jnp.where(lane < N_EXPERTS, logits, -jnp.inf)
    m1 = jnp.max(logits, axis=-1, keepdims=True)
    i1 = jnp.min(jnp.where(logits == m1, lane, float(LANES)), axis=-1, keepdims=True)
    rest_l = jnp.where(lane == i1, -jnp.inf, logits)
    m2 = jnp.max(rest_l, axis=-1, keepdims=True)
    i2 = jnp.min(jnp.where(rest_l == m2, lane, float(LANES)), axis=-1, keepdims=True)
    e2 = jnp.exp(m2 - m1)
    g1 = 1.0 / (1.0 + e2)
    g2 = e2 / (1.0 + e2)
    oh1 = (lane == i1).astype(F32)
    oh2 = (lane == i2).astype(F32)
    both = oh1 + oh2
    ri = lax.broadcasted_iota(jnp.int32, (tm, tm), 0)
    ci = lax.broadcasted_iota(jnp.int32, (tm, tm), 1)
    before = _dot((ri > ci).astype(BF), both.astype(BF)) + cnt_scr[...]
    rank1 = jnp.sum(oh1 * before, axis=-1, keepdims=True)
    rank2 = jnp.sum(oh2 * before, axis=-1, keepdims=True)
    cnt_scr[...] += jnp.sum(both, axis=0, keepdims=True)
    cnt_out_ref[...] = cnt_scr[...]
    route = jnp.zeros_like(logits)
    for n, val in enumerate((i1, i2, g1, g2, rank1, rank2)):
        route = jnp.where(lane == n, val, route)
    route_ref[...] = route


def _merge_out(x, ya, yb, gate, wa, wb, wo, g, tm, router=None, cnt_in=None):
    t = x.shape[0]
    row = lambda w_: pl.BlockSpec((tm, w_), lambda i: (i, 0))
    in_specs = [row(D_MODEL), row(ATT_W), row(RWKV_W), row(2 * D_MODEL),
                _const_spec((ATT_W, D_MODEL)), _const_spec((RWKV_W, D_MODEL)),
                _const_spec((D_MODEL, D_MODEL)), _const_spec((1, D_MODEL))]
    args = [x, ya, yb, gate, wa, wb, wo, g]
    if router is None:
        out_specs = [row(D_MODEL), row(D_MODEL)]
        out_shape = [jax.ShapeDtypeStruct((t, D_MODEL), F32), jax.ShapeDtypeStruct((t, D_MODEL), BF)]
        scratch = []
    else:
        in_specs += [_const_spec(router.shape), _const_spec((1, LANES))]
        args += [router, cnt_in]
        out_specs = [row(D_MODEL), row(D_MODEL), row(LANES), pl.BlockSpec((1, LANES), lambda i: (0, 0))]
        out_shape = [jax.ShapeDtypeStruct((t, D_MODEL), F32), jax.ShapeDtypeStruct((t, D_MODEL), F32),
                     jax.ShapeDtypeStruct((t, LANES), F32), jax.ShapeDtypeStruct((1, LANES), F32)]
        scratch = [pltpu.VMEM((1, LANES), F32)]
    return pl.pallas_call(
        functools.partial(_merge_kernel, router is not None),
        grid=(t // tm,),
        in_specs=in_specs, out_specs=out_specs, out_shape=out_shape, scratch_shapes=scratch,
        compiler_params=_params("parallel" if router is None else "arbitrary"),
        name="merge_out",
    )(*args)


def _swiglu(x, wg, wu, wd):
    s = _dot(x, wg)
    u = _dot(x, wu)
    act = (s * _sigmoid(s) * u).astype(BF)
    return _dot(act, wd)


def _dense_ffn_kernel(x1_ref, h_ref, wg_ref, wu_ref, wd_ref, o_ref):
    o_ref[...] = x1_ref[...] + _swiglu(h_ref[...], wg_ref[...], wu_ref[...], wd_ref[...])


def _dense_ffn(x1, h, wg, wu, wd, tm):
    t = x1.shape[0]
    d_ff = wg.shape[1]
    row = lambda: pl.BlockSpec((tm, D_MODEL), lambda i: (i, 0))
    return pl.pallas_call(
        _dense_ffn_kernel,
        grid=(t // tm,),
        in_specs=[row(), row(), _const_spec((D_MODEL, d_ff)), _const_spec((D_MODEL, d_ff)),
                  _const_spec((d_ff, D_MODEL))],
        out_specs=row(),
        out_shape=jax.ShapeDtypeStruct((t, D_MODEL), F32),
        compiler_params=_params("parallel"),
        name="dense_ffn",
    )(x1, h, wg, wu, wd)


def _dispatch_kernel(tt, first, dest_ref, fill_ref, h_ref, *rest):
    if first:
        xs_hbm, zbuf, sem, zsem = rest
    else:
        _, xs_hbm, sem = rest
    i = pl.program_id(0)
    base = i * tt

    if first:
        @pl.when(i == 0)
        def _():
            zbuf[...] = jnp.zeros_like(zbuf)
            bm = zbuf.shape[0]
            fills = [pltpu.make_async_copy(zbuf, xs_hbm.at[pl.ds(fill_ref[n] * bm, bm), :], zsem)
                     for n in range(fill_ref.shape[0])]
            for f in fills:
                f.start()
            for f in fills:
                f.wait()

    def issue(r, carry):
        for j in range(TOP_K):
            row = dest_ref[TOP_K * (base + r) + j]
            pltpu.make_async_copy(h_ref.at[pl.ds(r, 1), :], xs_hbm.at[pl.ds(row, 1), :], sem).start()
        return carry

    lax.fori_loop(0, tt, issue, 0, unroll=8)
    for j in range(TOP_K):
        pltpu.make_async_copy(h_ref, xs_hbm.at[pl.ds(0, tt), :], sem).wait()


def _dispatch(dest, fill_blocks, h, xs, tt, bm, n_rows):
    t = h.shape[0]
    first = xs is None
    in_specs = [pl.BlockSpec((tt, D_MODEL), lambda i, ds, fb: (i, 0))]
    args = [dest, fill_blocks, h]
    if first:
        scratch = [pltpu.VMEM((bm, D_MODEL), F32), pltpu.SemaphoreType.DMA(()), pltpu.SemaphoreType.DMA(())]
        aliases = {}
    else:
        in_specs.append(pl.BlockSpec(memory_space=pl.ANY))
        args.append(xs)
        scratch = [pltpu.SemaphoreType.DMA(())]
        aliases = {3: 0}
    grid_spec = pltpu.PrefetchScalarGridSpec(
        num_scalar_prefetch=2, grid=(t // tt,), in_specs=in_specs,
        out_specs=pl.BlockSpec(memory_space=pl.ANY), scratch_shapes=scratch)
    return pl.pallas_call(
        functools.partial(_dispatch_kernel, tt, first),
        grid_spec=grid_spec,
        out_shape=jax.ShapeDtypeStruct((n_rows, D_MODEL), F32),
        input_output_aliases=aliases,
        compiler_params=_params("arbitrary"),
        name="moe_dispatch",
    )(*args)


def _moe_kernel(n_half, blk_e_ref, n_used_ref, xs_ref, wg_ref, wu_ref, wd_ref, y_ref):
    @pl.when(pl.program_id(0) < n_used_ref[0])
    def _():
        x = xs_ref[...].astype(BF)
        d_e = wg_ref.shape[2]
        step = d_e // n_half
        acc = None
        for j in range(n_half):
            cs = slice(j * step, (j + 1) * step)
            part = _swiglu(x, wg_ref[0, :, cs], wu_ref[0, :, cs], wd_ref[0, cs, :])
            acc = part if acc is None else acc + part
        y_ref[...] = acc

    @pl.when(pl.program_id(0) >= n_used_ref[0])
    def _():
        y_ref[...] = jnp.zeros_like(y_ref)


def _moe_ffn(xs, blk_e, n_used, wg, wu, wd, bm):
    n_blk = blk_e.shape[0]
    n_rows = xs.shape[0]
    d_e = wg.shape[2]
    wspec = lambda shape: pl.BlockSpec(shape, lambda i, be, nu: (be[i], 0, 0), pipeline_mode=pl.Buffered(1))
    rows = pl.BlockSpec((bm, D_MODEL), lambda i, be, nu: (i, 0))
    grid_spec = pltpu.PrefetchScalarGridSpec(
        num_scalar_prefetch=2,
        grid=(n_blk,),
        in_specs=[rows, wspec((1, D_MODEL, d_e)), wspec((1, D_MODEL, d_e)), wspec((1, d_e, D_MODEL))],
        out_specs=rows,
    )
    return pl.pallas_call(
        functools.partial(_moe_kernel, 2),
        grid_spec=grid_spec,
        out_shape=jax.ShapeDtypeStruct((n_rows, D_MODEL), F32),
        compiler_params=_params("arbitrary"),
        name="moe_ffn",
    )(blk_e, n_used, xs, wg, wu, wd)


def _row_layout(counts, bm, n_blk):
    per_e = (counts + bm - 1) // bm
    ends = jnp.cumsum(per_e)
    n_used = ends[-1]
    first_row = (ends - per_e) * bm
    i = jnp.arange(n_blk, dtype=jnp.int32)
    blk_e = jnp.sum((jnp.minimum(i, jnp.maximum(n_used - 1, 0))[:, None] >= ends[None, :]).astype(jnp.int32), axis=1)
    need = jnp.any(i[:, None] == (ends - 1)[None, :], axis=1) | (i >= n_blk - N_EXPERTS)
    fill_blocks = jnp.argsort(jnp.logical_not(need), stable=True)[:2 * N_EXPERTS].astype(jnp.int32)
    return first_row.astype(jnp.int32), blk_e.astype(jnp.int32), n_used.astype(jnp.int32).reshape(1), fill_blocks


def _combine_kernel(tc, dest_ref, x1_ref, route_ref, y_hbm, g_ref, o_ref, ybuf, sems):
    i = pl.program_id(0)
    slot = i % 2

    def gather(step, into):
        base = step * tc

        def issue(r, carry):
            for j in range(TOP_K):
                row = dest_ref[TOP_K * (base + r) + j]
                pltpu.make_async_copy(y_hbm.at[pl.ds(row, 1), :], ybuf.at[into, j, pl.ds(r, 1), :],
                                      sems.at[into]).start()
            return carry

        lax.fori_loop(0, tc, issue, 0, unroll=4)

    @pl.when(i == 0)
    def _():
        gather(0, 0)

    @pl.when(i + 1 < pl.num_programs(0))
    def _():
        gather(i + 1, 1 - slot)

    for j in range(TOP_K):
        pltpu.make_async_copy(y_hbm.at[pl.ds(0, tc), :], ybuf.at[slot, j], sems.at[slot]).wait()
    route = route_ref[...]
    x2 = x1_ref[...] + route[:, 2:3] * ybuf[slot, 0] + route[:, 3:4] * ybuf[slot, 1]
    o_ref[...] = _rms(x2, g_ref[...])


def _moe_combine(dest, x1, route, y, g, tc):
    t = x1.shape[0]
    grid_spec = pltpu.PrefetchScalarGridSpec(
        num_scalar_prefetch=1,
        grid=(t // tc,),
        in_specs=[pl.BlockSpec((tc, D_MODEL), lambda i, ds: (i, 0)),
                  pl.BlockSpec((tc, LANES), lambda i, ds: (i, 0)),
                  pl.BlockSpec(memory_space=pl.ANY),
                  pl.BlockSpec((1, D_MODEL), lambda i, ds: (0, 0))],
        out_specs=pl.BlockSpec((tc, D_MODEL), lambda i, ds: (i, 0)),
        scratch_shapes=[pltpu.VMEM((2, TOP_K, tc, D_MODEL), F32), pltpu.SemaphoreType.DMA((2,))],
    )
    return pl.pallas_call(
        functools.partial(_combine_kernel, tc),
        grid_spec=grid_spec,
        out_shape=jax.ShapeDtypeStruct((t, D_MODEL), F32),
        compiler_params=_params("arbitrary"),
        name="moe_combine",
    )(dest, x1, route, y, g)


def _layer_params(l, w_in, mu_shift, w0, w2, a0, a2, g2, k_k, k_a, r_k, lnx_g, lnx_b,
                  w_proj_attn, w_proj_rwkv, w_out, ones_bd):
    row = lambda u: u.reshape(1, -1)
    return dict(
        w_in=w_in[l].astype(BF), mu=row(mu_shift[l]), w0=row(w0[l]), w2=w2[l].astype(BF),
        a0=row(a0[l]), a2=a2[l].astype(BF), g2=g2[l].astype(BF), k_k=row(k_k[l]), k_a=row(k_a[l]),
        r_k=row(r_k[l]), lnx_g=row(lnx_g[l]), lnx_b=row(lnx_b[l]),
        wa=w_proj_attn[l].astype(BF), wb=w_proj_rwkv[l].astype(BF), wo=w_out[l].astype(BF),
        ones_bd=ones_bd)


def kernel(x_prompt, x_sample, cache_k, cache_v, state_wkv, state_shift, norm_mix_g, w_in, attn_sinks, rel_bias, mu_shift, w0, w2, a0, a2, g2, k_k, k_a, r_k, lnx_g, lnx_b, w_proj_attn, w_proj_rwkv, w_out, norm_ffn_g, dense_w_gate, dense_w_up, dense_w_down, router_w, moe_w_gate, moe_w_up, moe_w_down, norm_final_g):
    batch, seq, _ = x_prompt.shape
    dec_batch, dec_seq, _ = x_sample.shape
    assert dec_seq == 1 and seq % WINDOW == 0

    head_id = jnp.arange(RWKV_W, dtype=jnp.int32) // RWKV_HEAD
    ones_bd = (head_id[:, None] == head_id[None, :]).astype(BF)
    bias_by_dist = _bias_lookup(rel_bias, _t5_bucket(np.arange(WINDOW + 1)))
    bias_c = jnp.transpose(bias_by_dist[WINDOW:0:-1], (1, 0))[None]
    bias_n = bias_by_dist[0][None, :, None]

    layers = [_layer_params(l, w_in, mu_shift, w0, w2, a0, a2, g2, k_k, k_a, r_k, lnx_g, lnx_b,
                            w_proj_attn, w_proj_rwkv, w_out, ones_bd) for l in range(DEPTH)]

    def router_pieces(l):
        return jnp.pad(router_w[l // 2], ((0, 0), (0, LANES - N_EXPERTS))).astype(BF)

    n_prompt = batch * seq
    tm_p = 512
    bm = 512
    n_all = n_prompt + dec_batch
    n_blk = -(-(n_all * TOP_K) // bm) + N_EXPERTS

    def mix(x, is_prompt, l, state):
        p = layers[l]
        n_tok = x.shape[0]
        gm = norm_mix_g[l].reshape(1, -1)
        sinks = attn_sinks[l]
        if is_prompt:
            b, t = batch, seq
            prev_z = jnp.zeros((b, 1, RWKV_IN), F32)
            q, k, v, gate, z_last, *coef = _inproj(x, gm, p, prev_z, 256, t)
            sink_col = jnp.repeat(sinks, WINDOW).reshape(N_KV_HEADS, GQA_GROUP * WINDOW, 1)
            ya = _swa_prompt(q.reshape(b, t, ATT_W), k.reshape(b, t, KV_W), v.reshape(b, t, KV_W),
                             rel_bias, sink_col).reshape(n_tok, ATT_W)
            k4 = k.reshape(b, t, N_KV_HEADS, HEAD_DIM)
            v4 = v.reshape(b, t, N_KV_HEADS, HEAD_DIM)
            state["k"].append(k4[:, t - WINDOW:])
            state["v"].append(v4[:, t - WINDOW:])
            s0 = jnp.zeros((b, RWKV_HEADS, RWKV_HEAD, RWKV_HEAD), F32)
            yb3, s_new = _rwkv_scan([u.reshape(b, t, RWKV_W) for u in coef], s0, p, b, 64)
            yb = yb3.reshape(n_tok, RWKV_W)
            state["z"].append(z_last.reshape(b, RWKV_IN))
        else:
            b = dec_batch
            q, k, v, gate, z, *coef = _inproj(x, gm, p, state_shift[l], n_tok, None)
            ck = cache_k[l].reshape(b, WINDOW, KV_W)
            cv = cache_v[l].reshape(b, WINDOW, KV_W)
            ya3 = _swa_sample(q.reshape(b, N_Q_HEADS, HEAD_DIM), k, v, ck, cv, bias_c, bias_n,
                              sinks.reshape(1, N_Q_HEADS, 1), 16)
            ya = ya3.reshape(b, ATT_W)
            state["k"].append(jnp.concatenate([ck[:, 1:], k[:, None]], axis=1)
                              .reshape(b, WINDOW, N_KV_HEADS, HEAD_DIM))
            state["v"].append(jnp.concatenate([cv[:, 1:], v[:, None]], axis=1)
                              .reshape(b, WINDOW, N_KV_HEADS, HEAD_DIM))
            yb, s_new = _rwkv_step(coef, state_wkv[l], p)
            state["z"].append(z)
        state["s"].append(s_new)
        return ya, yb, gate

    groups = [dict(x=x_prompt.reshape(n_prompt, D_MODEL), prompt=True, tm=tm_p, k=[], v=[], s=[], z=[]),
              dict(x=x_sample.reshape(dec_batch, D_MODEL), prompt=False, tm=dec_batch, k=[], v=[], s=[], z=[])]
    for l in range(DEPTH):
        p = layers[l]
        gn = norm_ffn_g[l].reshape(1, -1)
        j = l // 2
        if l % 2 == 0:
            wg, wu, wd = (_to_bf16(w[j], w.shape[1] // 4) for w in (dense_w_gate, dense_w_up, dense_w_down))
            for grp in groups:
                ya, yb, gate = mix(grp["x"], grp["prompt"], l, grp)
                x1, h = _merge_out(grp["x"], ya, yb, gate, p["wa"], p["wb"], p["wo"], gn, grp["tm"])
                grp["x"] = _dense_ffn(x1, h, wg, wu, wd, grp["tm"])
        else:
            assert l == DEPTH - 1
            wg, wu, wd = (w[j].astype(BF) for w in (moe_w_gate, moe_w_up, moe_w_down))
            router = router_pieces(l)
            cnt = jnp.zeros((1, LANES), F32)
            xs = None
            for grp in groups:
                ya, yb, gate = mix(grp["x"], grp["prompt"], l, grp)
                x1, h, route, cnt = _merge_out(grp["x"], ya, yb, gate, p["wa"], p["wb"], p["wo"], gn,
                                               min(256, grp["tm"]), router=router, cnt_in=cnt)
                grp["x1"], grp["route"], grp["h"] = x1, route, h
            counts = cnt[0, :N_EXPERTS].astype(jnp.int32)
            first_row, blk_e, n_used, fill_blocks = _row_layout(counts, bm, n_blk)
            for grp in groups:
                route = grp["route"]
                expert = route[:, 0:TOP_K].astype(jnp.int32)
                rank = route[:, 2 * TOP_K:3 * TOP_K].astype(jnp.int32)
                start = jnp.sum(jnp.where(expert[..., None] == jnp.arange(N_EXPERTS, dtype=jnp.int32),
                                          first_row, 0), axis=-1)
                grp["dest"] = (start + rank).reshape(-1)
                xs = _dispatch(grp["dest"], fill_blocks, grp["h"], xs, min(2048, grp["h"].shape[0]), bm,
                               n_blk * bm)
            y = _moe_ffn(xs, blk_e, n_used, wg, wu, wd, bm)
            for grp in groups:
                grp["x"] = _moe_combine(grp["dest"], grp["x1"], grp["route"], y,
                                        norm_final_g.reshape(1, -1), min(1024, grp["x1"].shape[0]))

    gp, gs = groups
    outs = []
    for grp, shape in ((gp, (batch, seq, D_MODEL)), (gs, (dec_batch, dec_seq, D_MODEL))):
        outs.append((grp["x"].reshape(shape), jnp.stack(grp["k"]), jnp.stack(grp["v"]),
                     jnp.stack(grp["s"]), jnp.stack(grp["z"])))
    (y_p, nk_p, nv_p, ns_p, nz_p), (y_s, nk_s, nv_s, ns_s, nz_s) = outs
    return (y_p, y_s, nk_p, nv_p, ns_p, nz_p, nk_s, nv_s, ns_s, nz_s)
```

```python
import functools
import math

import jax
import jax.numpy as jnp
import numpy as np
from jax import lax
from jax.experimental import pallas as pl
from jax.experimental.pallas import tpu as pltpu

BF = jnp.bfloat16
F32 = jnp.float32

D_MODEL = 1024
DEPTH = 2
HEAD_DIM = 64
N_Q_HEADS = 8
N_KV_HEADS = 2
GQA_GROUP = 4
ATT_W = 512
KV_W = 128
WINDOW = 128
ATT_SCALE = HEAD_DIM ** -0.5
NUM_BUCKETS = 32
MAX_EXACT = 16
MAX_DISTANCE = 128
NEG_INF = -1e30
RWKV_HEAD = 64
RWKV_W = 512
RWKV_HEADS = 8
D_DECAY_LORA = 64
D_AAA_LORA = 64
D_GATE_LORA = 128
RWKV_IN = 3 * RWKV_W + D_DECAY_LORA + D_AAA_LORA + D_GATE_LORA
GN_EPS = 64e-5
IN_W = ATT_W + 2 * KV_W + RWKV_IN + 2 * D_MODEL
N_EXPERTS = 8
TOP_K = 2
NORM_EPS = 1e-6

VMEM_LIMIT_BYTES = 56 * 1024 * 1024
LANES = 128


def _params(*sem):
    return pltpu.CompilerParams(dimension_semantics=sem, vmem_limit_bytes=VMEM_LIMIT_BYTES)


def _const_spec(shape):
    zeros = (0,) * len(shape)
    return pl.BlockSpec(shape, lambda *_: zeros, pipeline_mode=pl.Buffered(1))


def _dot(a, b):
    return jnp.dot(a, b, preferred_element_type=F32)


def _dot_nt(a, b):
    return lax.dot_general(a, b, (((1,), (1,)), ((), ())), preferred_element_type=F32)


def _dot_tn(a, b):
    return lax.dot_general(a, b, (((0,), (0,)), ((), ())), preferred_element_type=F32)


def _split2(x):
    hi = x.astype(BF)
    lo = (x - hi.astype(F32)).astype(BF)
    return hi, lo


def _dot_exact_rhs(x, w):
    hi, lo = _split2(x)
    return _dot(hi, w) + _dot(lo, w)


def _rms(x, g):
    ms = jnp.mean(x * x, axis=-1, keepdims=True)
    return x * lax.rsqrt(ms + NORM_EPS) * g


def _sigmoid(x):
    return 1.0 / (1.0 + jnp.exp(-x))


def _cast_kernel(x_ref, o_ref):
    o_ref[...] = x_ref[...].astype(o_ref.dtype)


def _to_bf16(w, row_tile):
    rows, cols = w.shape
    spec = pl.BlockSpec((row_tile, cols), lambda i: (i, 0))
    return pl.pallas_call(
        _cast_kernel, grid=(rows // row_tile,), in_specs=[spec], out_specs=spec,
        out_shape=jax.ShapeDtypeStruct((rows, cols), BF),
        compiler_params=_params("parallel"), name="to_bf16",
    )(w)


def _inproj_kernel(rows_per_seq, x_ref, g_ref, w_ref, pz_ref, mu_ref, w0_ref, w2_ref, a0_ref, a2_ref, g2_ref,
                   kk_ref, ka_ref, ones_ref,
                   q_ref, k_ref, v_ref, gate_ref, zl_ref,
                   r_ref, lw_ref, k2_ref, vr_ref, a_ref, b_ref, gr_ref, *carry):
    n = _rms(x_ref[...], g_ref[...]).astype(BF)
    z0 = ATT_W + 2 * KV_W
    z = _dot(n, w_ref[:, z0:z0 + RWKV_IN])
    q_ref[...] = _dot(n, w_ref[:, 0:ATT_W]).astype(BF)
    k_ref[...] = _dot(n, w_ref[:, ATT_W:ATT_W + KV_W])
    v_ref[...] = _dot(n, w_ref[:, ATT_W + KV_W:ATT_W + 2 * KV_W])
    gate_ref[...] = _dot(n, w_ref[:, z0 + RWKV_IN:IN_W]).astype(gate_ref.dtype)
    tm = z.shape[0]
    if rows_per_seq is None:
        zs = pz_ref[...]
        zl_ref[...] = z
    else:
        prev_ref, = carry
        tiles_per_seq = rows_per_seq // tm

        @pl.when(pl.program_id(0) % tiles_per_seq == 0)
        def _():
            prev_ref[...] = pz_ref[0]

        row = lax.broadcasted_iota(jnp.int32, (tm, 1), 0)
        zs = jnp.where(row == 0, prev_ref[...], pltpu.roll(z, 1, 0))
        prev_ref[...] = z[tm - 1:tm, :]
        zl_ref[0] = z[tm - 1:tm, :]
    zz = z + (zs - z) * mu_ref[...]
    r = zz[:, 0:RWKV_W]
    k = zz[:, RWKV_W:2 * RWKV_W]
    v = zz[:, 2 * RWKV_W:3 * RWKV_W]
    o = 3 * RWKV_W
    wl = zz[:, o:o + D_DECAY_LORA]
    al = zz[:, o + D_DECAY_LORA:o + D_DECAY_LORA + D_AAA_LORA]
    gl = zz[:, o + D_DECAY_LORA + D_AAA_LORA:RWKV_IN]
    wpre = -(w0_ref[...] + _dot(jnp.tanh(wl).astype(BF), w2_ref[...]))
    softplus = jnp.maximum(wpre, 0.0) + jnp.log1p(jnp.exp(-jnp.abs(wpre)))
    lw = -jnp.exp(-softplus - 0.5)
    a = _sigmoid(a0_ref[...] + _dot(al.astype(BF), a2_ref[...]))
    g = _dot(_sigmoid(gl).astype(BF), g2_ref[...])
    kkv = k * kk_ref[...]
    ss = _dot_exact_rhs(kkv * kkv, ones_ref[...])
    kk = kkv / jnp.maximum(jnp.sqrt(ss), 1e-12)
    k2 = k * (1.0 + (a - 1.0) * ka_ref[...])
    r_ref[...], lw_ref[...], k2_ref[...], vr_ref[...] = r, lw, k2, v
    a_ref[...], b_ref[...], gr_ref[...] = -kk, kk * a, g


def _inproj(x, g, p, prev_z, tm, rows_per_seq):
    t = x.shape[0]
    row = lambda w_: pl.BlockSpec((tm, w_), lambda i: (i, 0))
    vec = lambda w_: _const_spec((1, w_))
    if rows_per_seq is None:
        pz_spec = row(RWKV_IN)
        zl_spec, zl_shape = row(RWKV_IN), (t, RWKV_IN)
        scratch = []
    else:
        tiles_per_seq = rows_per_seq // tm
        pz_spec = pl.BlockSpec((1, 1, RWKV_IN), lambda i: (i // tiles_per_seq, 0, 0))
        zl_spec, zl_shape = pz_spec, (t // rows_per_seq, 1, RWKV_IN)
        scratch = [pltpu.VMEM((1, RWKV_IN), F32)]
    return pl.pallas_call(
        functools.partial(_inproj_kernel, rows_per_seq),
        grid=(t // tm,),
        in_specs=[row(D_MODEL), vec(D_MODEL), _const_spec((D_MODEL, IN_W)), pz_spec,
                  vec(RWKV_IN), vec(RWKV_W), _const_spec((D_DECAY_LORA, RWKV_W)),
                  vec(RWKV_W), _const_spec((D_AAA_LORA, RWKV_W)), _const_spec((D_GATE_LORA, RWKV_W)),
                  vec(RWKV_W), vec(RWKV_W), _const_spec((RWKV_W, RWKV_W))],
        out_specs=[row(ATT_W), row(KV_W), row(KV_W), row(2 * D_MODEL), zl_spec] + [row(RWKV_W)] * 7,
        out_shape=[jax.ShapeDtypeStruct((t, ATT_W), BF),
                   jax.ShapeDtypeStruct((t, KV_W), F32),
                   jax.ShapeDtypeStruct((t, KV_W), F32),
                   jax.ShapeDtypeStruct((t, 2 * D_MODEL), BF),
                   jax.ShapeDtypeStruct(zl_shape, F32)] + [jax.ShapeDtypeStruct((t, RWKV_W), F32)] * 7,
        scratch_shapes=scratch,
        compiler_params=_params("arbitrary"),
        name="inproj",
    )(x, g, p["w_in"], prev_z, p["mu"], p["w0"], p["w2"], p["a0"], p["a2"], p["g2"], p["k_k"], p["k_a"],
      p["ones_bd"])


def _t5_bucket(dist):
    n = np.maximum(dist, 0)
    nf = np.maximum(n, 1).astype(np.float32)
    large = MAX_EXACT + (np.log(nf / MAX_EXACT) / math.log(MAX_DISTANCE / MAX_EXACT)
                         * (NUM_BUCKETS - MAX_EXACT)).astype(np.int32)
    return np.where(n < MAX_EXACT, n, np.minimum(large, NUM_BUCKETS - 1)).astype(np.int32)


def _bias_lookup(rel_bias, bucket):
    hit = bucket[..., None, None] == np.arange(NUM_BUCKETS, dtype=np.int32)[:, None]
    return jnp.sum(jnp.where(hit, rel_bias, 0.0), axis=-2)


def _prompt_bucket_table():
    dist = np.arange(WINDOW)[:, None] - (np.arange(2 * WINDOW) - WINDOW)[None, :]
    return np.where((dist >= 0) & (dist <= WINDOW), _t5_bucket(dist), -1).astype(np.int32)


def _swa_prompt_kernel(nb, bucket_ref, relb_ref, q_ref, kp_ref, kc_ref, vp_ref, vc_ref, sink_ref, o_ref, bias_ref):
    first = pl.program_id(0) == 0

    @pl.when(first)
    def _():
        bucket = bucket_ref[...]
        for head in range(N_Q_HEADS):
            tab = jnp.full(bucket.shape, NEG_INF, F32)
            for n in range(NUM_BUCKETS):
                tab = jnp.where(bucket == n, relb_ref[n, head], tab)
            g = head % GQA_GROUP
            bias_ref[head // GQA_GROUP, g * WINDOW:(g + 1) * WINDOW, :] = tab

    col = lax.broadcasted_iota(jnp.int32, (GQA_GROUP * WINDOW, 2 * WINDOW), 1)
    pad_mask = jnp.logical_and(first, col < WINDOW)
    chains = [(b, h) for b in range(nb) for h in range(N_KV_HEADS)]
    hs = lambda h: slice(h * HEAD_DIM, (h + 1) * HEAD_DIM)
    kh = [jnp.concatenate([kp_ref[b][:, hs(h)], kc_ref[b][:, hs(h)]], axis=0).astype(BF) for b, h in chains]
    vh = [jnp.concatenate([vp_ref[b][:, hs(h)], vc_ref[b][:, hs(h)]], axis=0).astype(BF) for b, h in chains]
    qh = [jnp.concatenate([q_ref[b][:, hs(GQA_GROUP * h + g)] for g in range(GQA_GROUP)], axis=0)
          for b, h in chains]
    s = [_dot_nt(qh[n], kh[n]) * ATT_SCALE + bias_ref[h] for n, (b, h) in enumerate(chains)]
    s = [jnp.where(pad_mask, NEG_INF, x) for x in s]
    m = [jnp.maximum(jnp.max(x, axis=-1, keepdims=True), sink_ref[h]) for x, (b, h) in zip(s, chains)]
    p = [jnp.exp(x - mm) for x, mm in zip(s, m)]
    denom = [jnp.sum(x, axis=-1, keepdims=True) + jnp.exp(sink_ref[h] - mm)
             for x, mm, (b, h) in zip(p, m, chains)]
    o = [_dot(x.astype(BF), vv) / d for x, vv, d in zip(p, vh, denom)]
    for b in range(nb):
        pieces = [o[b * N_KV_HEADS + h][g * WINDOW:(g + 1) * WINDOW]
                  for h in range(N_KV_HEADS) for g in range(GQA_GROUP)]
        o_ref[b] = jnp.concatenate(pieces, axis=-1).astype(o_ref.dtype)


def _swa_prompt(q3, k3, v3, rel_bias, sink_col):
    batch, seq, _ = q3.shape
    cur = lambda i: (0, i, 0)
    prev = lambda i: (0, jnp.maximum(i - 1, 0), 0)
    kv_c = pl.BlockSpec((batch, WINDOW, KV_W), cur)
    kv_p = pl.BlockSpec((batch, WINDOW, KV_W), prev)
    bucket = jnp.asarray(_prompt_bucket_table())
    return pl.pallas_call(
        functools.partial(_swa_prompt_kernel, batch),
        grid=(seq // WINDOW,),
        in_specs=[_const_spec(bucket.shape), pl.BlockSpec(memory_space=pltpu.SMEM),
                  pl.BlockSpec((batch, WINDOW, ATT_W), cur), kv_p, kv_c, kv_p, kv_c,
                  _const_spec(sink_col.shape)],
        out_specs=pl.BlockSpec((batch, WINDOW, ATT_W), cur),
        out_shape=jax.ShapeDtypeStruct((batch, seq, ATT_W), BF),
        scratch_shapes=[pltpu.VMEM((N_KV_HEADS, GQA_GROUP * WINDOW, 2 * WINDOW), F32)],
        compiler_params=_params("arbitrary"),
        name="swa_prompt",
    )(bucket, rel_bias, q3, k3, k3, v3, v3, sink_col)


def _swa_sample_kernel(q_ref, kn_ref, vn_ref, ck_ref, cv_ref, bc_ref, bn_ref, sink_ref, o_ref):
    q = q_ref[...]
    qf = q.astype(F32)
    kn = kn_ref[...].astype(BF).astype(F32)
    vn = vn_ref[...].astype(BF).astype(F32)
    ck, cv = ck_ref[...], cv_ref[...]
    head = lax.broadcasted_iota(jnp.int32, (1, N_Q_HEADS, 1), 1)
    low = head < GQA_GROUP
    s_h, sn_h = [], []
    for h in range(N_KV_HEADS):
        sl = slice(h * HEAD_DIM, (h + 1) * HEAD_DIM)
        s_h.append(jnp.einsum("bgd,bkd->bgk", q, ck[:, :, sl].astype(BF), preferred_element_type=F32))
        sn_h.append(jnp.sum(qf * kn[:, None, sl], axis=-1, keepdims=True))
    s = jnp.where(low, s_h[0], s_h[1]) * ATT_SCALE + bc_ref[...]
    sn = jnp.where(low, sn_h[0], sn_h[1]) * ATT_SCALE + bn_ref[...]
    sink = sink_ref[...]
    m = jnp.maximum(jnp.maximum(jnp.max(s, axis=-1, keepdims=True), sn), sink)
    p = jnp.exp(s - m)
    pn = jnp.exp(sn - m)
    denom = jnp.sum(p, axis=-1, keepdims=True) + pn + jnp.exp(sink - m)
    pb = p.astype(BF)
    pnb = pn.astype(BF).astype(F32)
    o_h = []
    for h in range(N_KV_HEADS):
        sl = slice(h * HEAD_DIM, (h + 1) * HEAD_DIM)
        o = jnp.einsum("bgk,bkd->bgd", pb, cv[:, :, sl].astype(BF), preferred_element_type=F32)
        o_h.append(o + pnb * vn[:, None, sl])
    o_ref[...] = (jnp.where(low, o_h[0], o_h[1]) / denom).astype(o_ref.dtype)


def _swa_sample(q3, kn, vn, ck, cv, bias_c, bias_n, sink3, bb):
    b = q3.shape[0]
    return pl.pallas_call(
        _swa_sample_kernel,
        grid=(b // bb,),
        in_specs=[pl.BlockSpec((bb, N_Q_HEADS, HEAD_DIM), lambda i: (i, 0, 0)),
                  pl.BlockSpec((bb, KV_W), lambda i: (i, 0)),
                  pl.BlockSpec((bb, KV_W), lambda i: (i, 0)),
                  pl.BlockSpec((bb, WINDOW, KV_W), lambda i: (i, 0, 0)),
                  pl.BlockSpec((bb, WINDOW, KV_W), lambda i: (i, 0, 0)),
                  _const_spec(bias_c.shape), _const_spec(bias_n.shape), _const_spec(sink3.shape)],
        out_specs=pl.BlockSpec((bb, N_Q_HEADS, HEAD_DIM), lambda i: (i, 0, 0)),
        out_shape=jax.ShapeDtypeStruct((b, N_Q_HEADS, HEAD_DIM), BF),
        compiler_params=_params("parallel"),
        name="swa_sample",
    )(q3, kn, vn, ck, cv, bias_c, bias_n, sink3)


def _rwkv_scan_kernel(nb, chunk, r_ref, lw_ref, k_ref, v_ref, a_ref, b_ref, g_ref, s0_ref,
                      rk_ref, lg_ref, lb_ref, ones_ref, y_ref, sout_ref, s_scr):
    c = pl.program_id(1)

    @pl.when(c == 0)
    def _():
        s_scr[...] = s0_ref[...]

    ri = lax.broadcasted_iota(jnp.int32, (chunk, chunk), 0)
    ci = lax.broadcasted_iota(jnp.int32, (chunk, chunk), 1)
    incl = ri >= ci
    strict = ri > ci
    tri = incl.astype(BF)
    eye = (ri == ci).astype(F32)
    n_sq = int(math.log2(chunk)) - 1

    chains = [(bi, h) for bi in range(nb) for h in range(RWKV_HEADS)]
    hs = lambda h: slice(h * RWKV_HEAD, (h + 1) * RWKV_HEAD)

    ra, aa, bt, kt, bh, kh, vb, g_all = [], [], [], [], [], [], [], []
    for bi in range(nb):
        r, lw, k, v = r_ref[bi], lw_ref[bi], k_ref[bi], v_ref[bi]
        a, b = a_ref[bi], b_ref[bi]
        hi, lo = _split2(lw)
        lo2 = (lw - hi.astype(F32) - lo.astype(F32)).astype(BF)
        cum = _dot(tri, hi) + _dot(tri, lo) + _dot(tri, lo2)
        tail = cum[chunk - 1:chunk, :]
        g_inv = jnp.exp(-cum)
        g_tail = jnp.exp(tail - cum)
        g_all.append(jnp.exp(tail))
        ra.append((r * jnp.exp(cum)).astype(BF))
        aa.append((a * jnp.exp(cum - lw)).astype(BF))
        bt.append((b * g_inv).astype(BF))
        kt.append((k * g_inv).astype(BF))
        bh.append((b * g_tail).astype(BF))
        kh.append((k * g_tail).astype(BF))
        vb.append(v.astype(BF))

    s_old = [s_scr[bi, h] for bi, h in chains]
    ar = [jnp.concatenate([aa[bi][:, hs(h)], ra[bi][:, hs(h)]], axis=0) for bi, h in chains]
    v_h = [vb[bi][:, hs(h)] for bi, h in chains]
    gb = [_dot_nt(ar[n], bt[bi][:, hs(h)]) for n, (bi, h) in enumerate(chains)]
    gk = [_dot_nt(ar[n], kt[bi][:, hs(h)]) for n, (bi, h) in enumerate(chains)]
    p = [_dot_nt(ar[n], s_old[n].astype(BF)) for n in range(len(chains))]
    l_ab = [jnp.where(strict, x[:chunk], 0.0) for x in gb]
    l_ak = [jnp.where(strict, x[:chunk], 0.0).astype(BF) for x in gk]
    m_rb = [jnp.where(incl, x[chunk:], 0.0).astype(BF) for x in gb]
    m_rk = [jnp.where(incl, x[chunk:], 0.0).astype(BF) for x in gk]
    rhs = [p[n][:chunk] + _dot(l_ak[n], v_h[n]) for n in range(len(chains))]
    t_inv = [eye + x for x in l_ab]
    lp = l_ab
    for _ in range(n_sq):
        lpb = [x.astype(BF) for x in lp]
        lp = [_dot(x, x) for x in lpb]
        t_inv = [t + _dot(x.astype(BF), t.astype(BF)) for x, t in zip(lp, t_inv)]
    ub = [_dot(t.astype(BF), x.astype(BF)).astype(BF) for t, x in zip(t_inv, rhs)]
    y_h = [p[n][chunk:] + _dot(m_rb[n], ub[n]) + _dot(m_rk[n], v_h[n]) for n in range(len(chains))]
    s_new = [s_old[n] * g_all[bi][:, hs(h)] + _dot_tn(ub[n], bh[bi][:, hs(h)]) + _dot_tn(v_h[n], kh[bi][:, hs(h)])
             for n, (bi, h) in enumerate(chains)]
    for n, (bi, h) in enumerate(chains):
        s_scr[bi, h] = s_new[n]

    for bi in range(nb):
        y = jnp.concatenate(y_h[bi * RWKV_HEADS:(bi + 1) * RWKV_HEADS], axis=-1)
        y_ref[bi] = _rwkv_epilogue(y, r_ref[bi], k_ref[bi], v_ref[bi], g_ref[bi],
                                   rk_ref, lg_ref, lb_ref, ones_ref).astype(y_ref.dtype)

    @pl.when(c == pl.num_programs(1) - 1)
    def _():
        sout_ref[...] = s_scr[...]


def _rwkv_epilogue(y, r, k, v, g, rk_ref, lg_ref, lb_ref, ones_ref):
    ones = ones_ref[...]
    inv_n = 1.0 / RWKV_HEAD
    mean = _dot_exact_rhs(y, ones) * inv_n
    d = y - mean
    var = _dot_exact_rhs(d * d, ones) * inv_n
    yn = d * lax.rsqrt(var + GN_EPS) * lg_ref[...] + lb_ref[...]
    bonus = _dot_exact_rhs(r * k * rk_ref[...], ones) * v
    return (yn + bonus) * g


def _rwkv_scan(coef, s0, p, nb, chunk):
    r = coef[0]
    b, t, _ = r.shape
    seq_spec = pl.BlockSpec((nb, chunk, RWKV_W), lambda bi, c: (bi, c, 0))
    st_spec = pl.BlockSpec((nb, RWKV_HEADS, RWKV_HEAD, RWKV_HEAD), lambda bi, c: (bi, 0, 0, 0))
    vec = _const_spec((1, RWKV_W))
    return pl.pallas_call(
        functools.partial(_rwkv_scan_kernel, nb, chunk),
        grid=(b // nb, t // chunk),
        in_specs=[seq_spec] * 7 + [st_spec, vec, vec, vec, _const_spec((RWKV_W, RWKV_W))],
        out_specs=[seq_spec, st_spec],
        out_shape=[jax.ShapeDtypeStruct((b, t, RWKV_W), BF),
                   jax.ShapeDtypeStruct((b, RWKV_HEADS, RWKV_HEAD, RWKV_HEAD), F32)],
        scratch_shapes=[pltpu.VMEM((nb, RWKV_HEADS, RWKV_HEAD, RWKV_HEAD), F32)],
        compiler_params=_params("parallel", "arbitrary"),
        name="rwkv_scan",
    )(*coef, s0, p["r_k"], p["lnx_g"], p["lnx_b"], p["ones_bd"])


def _rwkv_step_kernel(r_ref, lw_ref, k_ref, v_ref, a_ref, b_ref, g_ref, s_ref,
                      rk_ref, lg_ref, lb_ref, ones_ref, y_ref, sout_ref, y_scr):
    n_pair = LANES // RWKV_HEAD
    r, k, v = r_ref[...], k_ref[...], v_ref[...]
    rT, kT, vT = r.T, k.T, v.T
    aT, bT, wT = a_ref[...].T, b_ref[...].T, jnp.exp(lw_ref[...]).T
    for hl in range(LANES // RWKV_HEAD):
        hsl = slice(hl * RWKV_HEAD, (hl + 1) * RWKV_HEAD)
        a_h, b_h, k_h, w_h, r_h = aT[hsl], bT[hsl], kT[hsl], wT[hsl], rT[hsl]
        tiles = [(hl * RWKV_HEAD + n_pair * t) * RWKV_HEAD for t in range(RWKV_HEAD // n_pair)]
        st = [s_ref[:, c0:c0 + LANES].T for c0 in tiles]
        new = []
        for t, x in enumerate(st):
            halves = []
            for il in range(n_pair):
                i = hl * RWKV_HEAD + n_pair * t + il
                slab = x[il * RWKV_HEAD:(il + 1) * RWKV_HEAD]
                sa = jnp.sum(slab * a_h, axis=0, keepdims=True)
                slab = slab * w_h + sa * b_h + vT[i:i + 1] * k_h
                y_scr[i:i + 1, :] = jnp.sum(slab * r_h, axis=0, keepdims=True)
                halves.append(slab)
            new.append(jnp.concatenate(halves, axis=0))
        for c0, x in zip(tiles, new):
            sout_ref[:, c0:c0 + LANES] = x.T
    y_ref[...] = _rwkv_epilogue(y_scr[...].T, r, k, v, g_ref[...],
                                rk_ref, lg_ref, lb_ref, ones_ref).astype(y_ref.dtype)


def _rwkv_step(coef, s0, p):
    b = s0.shape[0]
    per_pair = (LANES // RWKV_HEAD) * RWKV_HEAD * RWKV_HEAD
    n_steps = RWKV_W // LANES
    col = pl.BlockSpec((b, LANES), lambda i: (0, i))
    vec = pl.BlockSpec((1, LANES), lambda i: (0, i))
    st_spec = pl.BlockSpec((b, per_pair), lambda i: (0, i))
    y, s_new = pl.pallas_call(
        _rwkv_step_kernel,
        grid=(n_steps,),
        in_specs=[col] * 7 + [st_spec, vec, vec, vec, _const_spec((LANES, LANES))],
        out_specs=[col, st_spec],
        out_shape=[jax.ShapeDtypeStruct((b, RWKV_W), BF),
                   jax.ShapeDtypeStruct((b, n_steps * per_pair), F32)],
        scratch_shapes=[pltpu.VMEM((LANES, b), F32)],
        compiler_params=_params("parallel"),
        name="rwkv_step",
    )(*coef, s0.reshape(b, n_steps * per_pair), p["r_k"], p["lnx_g"], p["lnx_b"],
      p["ones_bd"][:LANES, :LANES])
    return y, s_new.reshape(s0.shape)


def _merge_kernel(with_router, x_ref, ya_ref, yb_ref, gate_ref, wa_ref, wb_ref, wo_ref, g_ref, *rest):
    if with_router:
        wr_ref, cnt_in_ref, x1_ref, h_ref, route_ref, cnt_out_ref, cnt_scr = rest
    else:
        x1_ref, h_ref = rest
    gate = gate_ref[...].astype(F32)
    pa = _dot(ya_ref[...], wa_ref[...])
    pb = _dot(yb_ref[...], wb_ref[...])
    merged = _sigmoid(gate[:, :D_MODEL]) * pa + _sigmoid(gate[:, D_MODEL:]) * pb
    x1 = x_ref[...] + _dot(merged.astype(BF), wo_ref[...])
    x1_ref[...] = x1
    h = _rms(x1, g_ref[...])
    h_ref[...] = h.astype(h_ref.dtype)
    if not with_router:
        return

    @pl.when(pl.program_id(0) == 0)
    def _():
        cnt_scr[...] = cnt_in_ref[...]

    logits = _dot(h.astype(BF), wr_ref[...])
    tm = logits.shape[0]
    lane = lax.broadcasted_iota(jnp.int32, logits.shape, 1).astype(F32)
    logits = jnp.where(lane < N_EXPERTS, logits, -jnp.inf)
    m1 = jnp.max(logits, axis=-1, keepdims=True)
    i1 = jnp.min(jnp.where(logits == m1, lane, float(LANES)), axis=-1, keepdims=True)
    rest_l = jnp.where(lane == i1, -jnp.inf, logits)
    m2 = jnp.max(rest_l, axis=-1, keepdims=True)
    i2 = jnp.min(jnp.where(rest_l == m2, lane, float(LANES)), axis=-1, keepdims=True)
    e2 = jnp.exp(m2 - m1)
    g1 = 1.0 / (1.0 + e2)
    g2 = e2 / (1.0 + e2)
    oh1 = (lane == i1).astype(F32)
    oh2 = (lane == i2).astype(F32)
    both = oh1 + oh2
    ri = lax.broadcasted_iota(jnp.int32, (tm, tm), 0)
    ci = lax.broadcasted_iota(jnp.int32, (tm, tm), 1)
    before = _dot((ri > ci).astype(BF), both.astype(BF)) + cnt_scr[...]
    rank1 = jnp.sum(oh1 * before, axis=-1, keepdims=True)
    rank2 = jnp.sum(oh2 * before, axis=-1, keepdims=True)
    cnt_scr[...] += jnp.sum(both, axis=0, keepdims=True)
    cnt_out_ref[...] = cnt_scr[...]
    route = jnp.zeros_like(logits)
    for n, val in enumerate((i1, i2, g1, g2, rank1, rank2)):
        route = jnp.where(lane == n, val, route)
    route_ref[...] = route


def _merge_out(x, ya, yb, gate, wa, wb, wo, g, tm, router=None, cnt_in=None):
    t = x.shape[0]
    row = lambda w_: pl.BlockSpec((tm, w_), lambda i: (i, 0))
    in_specs = [row(D_MODEL), row(ATT_W), row(RWKV_W), row(2 * D_MODEL),
                _const_spec((ATT_W, D_MODEL)), _const_spec((RWKV_W, D_MODEL)),
                _const_spec((D_MODEL, D_MODEL)), _const_spec((1, D_MODEL))]
    args = [x, ya, yb, gate, wa, wb, wo, g]
    if router is None:
        out_specs = [row(D_MODEL), row(D_MODEL)]
        out_shape = [jax.ShapeDtypeStruct((t, D_MODEL), F32), jax.ShapeDtypeStruct((t, D_MODEL), BF)]
        scratch = []
    else:
        in_specs += [_const_spec(router.shape), _const_spec((1, LANES))]
        args += [router, cnt_in]
        out_specs = [row(D_MODEL), row(D_MODEL), row(LANES), pl.BlockSpec((1, LANES), lambda i: (0, 0))]
        out_shape = [jax.ShapeDtypeStruct((t, D_MODEL), F32), jax.ShapeDtypeStruct((t, D_MODEL), F32),
                     jax.ShapeDtypeStruct((t, LANES), F32), jax.ShapeDtypeStruct((1, LANES), F32)]
        scratch = [pltpu.VMEM((1, LANES), F32)]
    return pl.pallas_call(
        functools.partial(_merge_kernel, router is not None),
        grid=(t // tm,),
        in_specs=in_specs, out_specs=out_specs, out_shape=out_shape, scratch_shapes=scratch,
        compiler_params=_params("parallel" if router is None else "arbitrary"),
        name="merge_out",
    )(*args)


def _swiglu(x, wg, wu, wd):
    s = _dot(x, wg)
    u = _dot(x, wu)
    act = (s * _sigmoid(s) * u).astype(BF)
    return _dot(act, wd)


def _dense_ffn_kernel(x1_ref, h_ref, wg_ref, wu_ref, wd_ref, o_ref):
    o_ref[...] = x1_ref[...] + _swiglu(h_ref[...], wg_ref[...], wu_ref[...], wd_ref[...])


def _dense_ffn(x1, h, wg, wu, wd, tm):
    t = x1.shape[0]
    d_ff = wg.shape[1]
    row = lambda: pl.BlockSpec((tm, D_MODEL), lambda i: (i, 0))
    return pl.pallas_call(
        _dense_ffn_kernel,
        grid=(t // tm,),
        in_specs=[row(), row(), _const_spec((D_MODEL, d_ff)), _const_spec((D_MODEL, d_ff)),
                  _const_spec((d_ff, D_MODEL))],
        out_specs=row(),
        out_shape=jax.ShapeDtypeStruct((t, D_MODEL), F32),
        compiler_params=_params("parallel"),
        name="dense_ffn",
    )(x1, h, wg, wu, wd)


def _dispatch_kernel(tt, first, dest_ref, fill_ref, h_ref, *rest):
    if first:
        xs_hbm, zbuf, sem, zsem = rest
    else:
        _, xs_hbm, sem = rest
    i = pl.program_id(0)
    base = i * tt

    if first:
        @pl.when(i == 0)
        def _():
            zbuf[...] = jnp.zeros_like(zbuf)
            bm = zbuf.shape[0]
            fills = [pltpu.make_async_copy(zbuf, xs_hbm.at[pl.ds(fill_ref[n] * bm, bm), :], zsem)
                     for n in range(fill_ref.shape[0])]
            for f in fills:
                f.start()
            for f in fills:
                f.wait()

    def issue(r, carry):
        for j in range(TOP_K):
            row = dest_ref[TOP_K * (base + r) + j]
            pltpu.make_async_copy(h_ref.at[pl.ds(r, 1), :], xs_hbm.at[pl.ds(row, 1), :], sem).start(priority=j)
        return carry

    lax.fori_loop(0, tt, issue, 0, unroll=8)
    for j in range(TOP_K):
        pltpu.make_async_copy(h_ref, xs_hbm.at[pl.ds(0, tt), :], sem).wait()


def _dispatch(dest, fill_blocks, h, xs, tt, bm, n_rows):
    t = h.shape[0]
    first = xs is None
    in_specs = [pl.BlockSpec((tt, D_MODEL), lambda i, ds, fb: (i, 0))]
    args = [dest, fill_blocks, h]
    if first:
        scratch = [pltpu.VMEM((bm, D_MODEL), F32), pltpu.SemaphoreType.DMA(()), pltpu.SemaphoreType.DMA(())]
        aliases = {}
    else:
        in_specs.append(pl.BlockSpec(memory_space=pl.ANY))
        args.append(xs)
        scratch = [pltpu.SemaphoreType.DMA(())]
        aliases = {3: 0}
    grid_spec = pltpu.PrefetchScalarGridSpec(
        num_scalar_prefetch=2, grid=(t // tt,), in_specs=in_specs,
        out_specs=pl.BlockSpec(memory_space=pl.ANY), scratch_shapes=scratch)
    return pl.pallas_call(
        functools.partial(_dispatch_kernel, tt, first),
        grid_spec=grid_spec,
        out_shape=jax.ShapeDtypeStruct((n_rows, D_MODEL), F32),
        input_output_aliases=aliases,
        compiler_params=_params("arbitrary"),
        name="moe_dispatch",
    )(*args)


def _moe_kernel(n_half, blk_e_ref, n_used_ref, xs_ref, wg_ref, wu_ref, wd_ref, y_ref):
    @pl.when(pl.program_id(0) < n_used_ref[0])
    def _():
        x = xs_ref[...].astype(BF)
        d_e = wg_ref.shape[2]
        step = d_e // n_half
        acc = None
        for j in range(n_half):
            cs = slice(j * step, (j + 1) * step)
            part = _swiglu(x, wg_ref[0, :, cs], wu_ref[0, :, cs], wd_ref[0, cs, :])
            acc = part if acc is None else acc + part
        y_ref[...] = acc

    @pl.when(pl.program_id(0) >= n_used_ref[0])
    def _():
        y_ref[...] = jnp.zeros_like(y_ref)


def _moe_ffn(xs, blk_e, n_used, wg, wu, wd, bm):
    n_blk = blk_e.shape[0]
    n_rows = xs.shape[0]
    d_e = wg.shape[2]
    wspec = lambda shape: pl.BlockSpec(shape, lambda i, be, nu: (be[i], 0, 0), pipeline_mode=pl.Buffered(1))
    rows = pl.BlockSpec((bm, D_MODEL), lambda i, be, nu: (i, 0))
    grid_spec = pltpu.PrefetchScalarGridSpec(
        num_scalar_prefetch=2,
        grid=(n_blk,),
        in_specs=[rows, wspec((1, D_MODEL, d_e)), wspec((1, D_MODEL, d_e)), wspec((1, d_e, D_MODEL))],
        out_specs=rows,
    )
    return pl.pallas_call(
        functools.partial(_moe_kernel, 2),
        grid_spec=grid_spec,
        out_shape=jax.ShapeDtypeStruct((n_rows, D_MODEL), F32),
        compiler_params=_params("arbitrary"),
        name="moe_ffn",
    )(blk_e, n_used, xs, wg, wu, wd)


def _row_layout(counts, bm, n_blk):
    per_e = (counts + bm - 1) // bm
    ends = jnp.cumsum(per_e)
    n_used = ends[-1]
    first_row = (ends - per_e) * bm
    i = jnp.arange(n_blk, dtype=jnp.int32)
    blk_e = jnp.sum((jnp.minimum(i, jnp.maximum(n_used - 1, 0))[:, None] >= ends[None, :]).astype(jnp.int32), axis=1)
    need = jnp.any(i[:, None] == (ends - 1)[None, :], axis=1) | (i >= n_blk - N_EXPERTS)
    fill_blocks = jnp.argsort(jnp.logical_not(need), stable=True)[:2 * N_EXPERTS].astype(jnp.int32)
    return first_row.astype(jnp.int32), blk_e.astype(jnp.int32), n_used.astype(jnp.int32).reshape(1), fill_blocks


def _combine_kernel(tc, dest_ref, x1_ref, route_ref, y_hbm, g_ref, o_ref, ybuf, sems):
    i = pl.program_id(0)
    slot = i % 2

    def gather(step, into):
        base = step * tc

        def issue(r, carry):
            for j in range(TOP_K):
                row = dest_ref[TOP_K * (base + r) + j]
                pltpu.make_async_copy(y_hbm.at[pl.ds(row, 1), :], ybuf.at[into, j, pl.ds(r, 1), :],
                                      sems.at[into]).start(priority=j)
            return carry

        lax.fori_loop(0, tc, issue, 0, unroll=4)

    @pl.when(i == 0)
    def _():
        gather(0, 0)

    @pl.when(i + 1 < pl.num_programs(0))
    def _():
        gather(i + 1, 1 - slot)

    for j in range(TOP_K):
        pltpu.make_async_copy(y_hbm.at[pl.ds(0, tc), :], ybuf.at[slot, j], sems.at[slot]).wait()
    route = route_ref[...]
    x2 = x1_ref[...] + route[:, 2:3] * ybuf[slot, 0] + route[:, 3:4] * ybuf[slot, 1]
    o_ref[...] = _rms(x2, g_ref[...])


def _moe_combine(dest, x1, route, y, g, tc):
    t = x1.shape[0]
    grid_spec = pltpu.PrefetchScalarGridSpec(
        num_scalar_prefetch=1,
        grid=(t // tc,),
        in_specs=[pl.BlockSpec((tc, D_MODEL), lambda i, ds: (i, 0)),
                  pl.BlockSpec((tc, LANES), lambda i, ds: (i, 0)),
                  pl.BlockSpec(memory_space=pl.ANY),
                  pl.BlockSpec((1, D_MODEL), lambda i, ds: (0, 0))],
        out_specs=pl.BlockSpec((tc, D_MODEL), lambda i, ds: (i, 0)),
        scratch_shapes=[pltpu.VMEM((2, TOP_K, tc, D_MODEL), F32), pltpu.SemaphoreType.DMA((2,))],
    )
    return pl.pallas_call(
        functools.partial(_combine_kernel, tc),
        grid_spec=grid_spec,
        out_shape=jax.ShapeDtypeStruct((t, D_MODEL), F32),
        compiler_params=_params("arbitrary"),
        name="moe_combine",
    )(dest, x1, route, y, g)


def _layer_params(l, w_in, mu_shift, w0, w2, a0, a2, g2, k_k, k_a, r_k, lnx_g, lnx_b,
                  w_proj_attn, w_proj_rwkv, w_out, ones_bd):
    row = lambda u: u.reshape(1, -1)
    return dict(
        w_in=w_in[l].astype(BF), mu=row(mu_shift[l]), w0=row(w0[l]), w2=w2[l].astype(BF),
        a0=row(a0[l]), a2=a2[l].astype(BF), g2=g2[l].astype(BF), k_k=row(k_k[l]), k_a=row(k_a[l]),
        r_k=row(r_k[l]), lnx_g=row(lnx_g[l]), lnx_b=row(lnx_b[l]),
        wa=w_proj_attn[l].astype(BF), wb=w_proj_rwkv[l].astype(BF), wo=w_out[l].astype(BF),
        ones_bd=ones_bd)


def kernel(x_prompt, x_sample, cache_k, cache_v, state_wkv, state_shift, norm_mix_g, w_in, attn_sinks, rel_bias, mu_shift, w0, w2, a0, a2, g2, k_k, k_a, r_k, lnx_g, lnx_b, w_proj_attn, w_proj_rwkv, w_out, norm_ffn_g, dense_w_gate, dense_w_up, dense_w_down, router_w, moe_w_gate, moe_w_up, moe_w_down, norm_final_g):
    batch, seq, _ = x_prompt.shape
    dec_batch, dec_seq, _ = x_sample.shape
    assert dec_seq == 1 and seq % WINDOW == 0

    head_id = jnp.arange(RWKV_W, dtype=jnp.int32) // RWKV_HEAD
    ones_bd = (head_id[:, None] == head_id[None, :]).astype(BF)
    bias_by_dist = _bias_lookup(rel_bias, _t5_bucket(np.arange(WINDOW + 1)))
    bias_c = jnp.transpose(bias_by_dist[WINDOW:0:-1], (1, 0))[None]
    bias_n = bias_by_dist[0][None, :, None]

    layers = [_layer_params(l, w_in, mu_shift, w0, w2, a0, a2, g2, k_k, k_a, r_k, lnx_g, lnx_b,
                            w_proj_attn, w_proj_rwkv, w_out, ones_bd) for l in range(DEPTH)]

    def router_pieces(l):
        return jnp.pad(router_w[l // 2], ((0, 0), (0, LANES - N_EXPERTS))).astype(BF)

    n_prompt = batch * seq
    tm_p = 512
    bm = 512
    n_all = n_prompt + dec_batch
    n_blk = -(-(n_all * TOP_K) // bm) + N_EXPERTS

    def mix(x, is_prompt, l, state):
        p = layers[l]
        n_tok = x.shape[0]
        gm = norm_mix_g[l].reshape(1, -1)
        sinks = attn_sinks[l]
        if is_prompt:
            b, t = batch, seq
            prev_z = jnp.zeros((b, 1, RWKV_IN), F32)
            q, k, v, gate, z_last, *coef = _inproj(x, gm, p, prev_z, 256, t)
            sink_col = jnp.repeat(sinks, WINDOW).reshape(N_KV_HEADS, GQA_GROUP * WINDOW, 1)
            ya = _swa_prompt(q.reshape(b, t, ATT_W), k.reshape(b, t, KV_W), v.reshape(b, t, KV_W),
                             rel_bias, sink_col).reshape(n_tok, ATT_W)
            k4 = k.reshape(b, t, N_KV_HEADS, HEAD_DIM)
            v4 = v.reshape(b, t, N_KV_HEADS, HEAD_DIM)
            state["k"].append(k4[:, t - WINDOW:])
            state["v"].append(v4[:, t - WINDOW:])
            s0 = jnp.zeros((b, RWKV_HEADS, RWKV_HEAD, RWKV_HEAD), F32)
            yb3, s_new = _rwkv_scan([u.reshape(b, t, RWKV_W) for u in coef], s0, p, b, 64)
            yb = yb3.reshape(n_tok, RWKV_W)
            state["z"].append(z_last.reshape(b, RWKV_IN))
        else:
            b = dec_batch
            q, k, v, gate, z, *coef = _inproj(x, gm, p, state_shift[l], n_tok, None)
            ck = cache_k[l].reshape(b, WINDOW, KV_W)
            cv = cache_v[l].reshape(b, WINDOW, KV_W)
            ya3 = _swa_sample(q.reshape(b, N_Q_HEADS, HEAD_DIM), k, v, ck, cv, bias_c, bias_n,
                              sinks.reshape(1, N_Q_HEADS, 1), 16)
            ya = ya3.reshape(b, ATT_W)
            state["k"].append(jnp.concatenate([ck[:, 1:], k[:, None]], axis=1)
                              .reshape(b, WINDOW, N_KV_HEADS, HEAD_DIM))
            state["v"].append(jnp.concatenate([cv[:, 1:], v[:, None]], axis=1)
                              .reshape(b, WINDOW, N_KV_HEADS, HEAD_DIM))
            yb, s_new = _rwkv_step(coef, state_wkv[l], p)
            state["z"].append(z)
        state["s"].append(s_new)
        return ya, yb, gate

    groups = [dict(x=x_prompt.reshape(n_prompt, D_MODEL), prompt=True, tm=tm_p, k=[], v=[], s=[], z=[]),
              dict(x=x_sample.reshape(dec_batch, D_MODEL), prompt=False, tm=dec_batch, k=[], v=[], s=[], z=[])]
    for l in range(DEPTH):
        p = layers[l]
        gn = norm_ffn_g[l].reshape(1, -1)
        j = l // 2
        if l % 2 == 0:
            wg, wu, wd = (_to_bf16(w[j], w.shape[1] // 4) for w in (dense_w_gate, dense_w_up, dense_w_down))
            for grp in groups:
                ya, yb, gate = mix(grp["x"], grp["prompt"], l, grp)
                x1, h = _merge_out(grp["x"], ya, yb, gate, p["wa"], p["wb"], p["wo"], gn, grp["tm"])
                grp["x"] = _dense_ffn(x1, h, wg, wu, wd, grp["tm"])
        else:
            assert l == DEPTH - 1
            wg, wu, wd = (w[j].astype(BF) for w in (moe_w_gate, moe_w_up, moe_w_down))
            router = router_pieces(l)
            cnt = jnp.zeros((1, LANES), F32)
            xs = None
            for grp in groups:
                ya, yb, gate = mix(grp["x"], grp["prompt"], l, grp)
                x1, h, route, cnt = _merge_out(grp["x"], ya, yb, gate, p["wa"], p["wb"], p["wo"], gn,
                                               min(256, grp["tm"]), router=router, cnt_in=cnt)
                grp["x1"], grp["route"], grp["h"] = x1, route, h
            counts = cnt[0, :N_EXPERTS].astype(jnp.int32)
            first_row, blk_e, n_used, fill_blocks = _row_layout(counts, bm, n_blk)
            for grp in groups:
                route = grp["route"]
                expert = route[:, 0:TOP_K].astype(jnp.int32)
                rank = route[:, 2 * TOP_K:3 * TOP_K].astype(jnp.int32)
                start = jnp.sum(jnp.where(expert[..., None] == jnp.arange(N_EXPERTS, dtype=jnp.int32),
                                          first_row, 0), axis=-1)
                grp["dest"] = (start + rank).reshape(-1)
                xs = _dispatch(grp["dest"], fill_blocks, grp["h"], xs, min(2048, grp["h"].shape[0]), bm,
                               n_blk * bm)
            y = _moe_ffn(xs, blk_e, n_used, wg, wu, wd, bm)
            for grp in groups:
                grp["x"] = _moe_combine(grp["dest"], grp["x1"], grp["route"], y,
                                        norm_final_g.reshape(1, -1), min(1024, grp["x1"].shape[0]))

    gp, gs = groups
    outs = []
    for grp, shape in ((gp, (batch, seq, D_MODEL)), (gs, (dec_batch, dec_seq, D_MODEL))):
        outs.append((grp["x"].reshape(shape), jnp.stack(grp["k"]), jnp.stack(grp["v"]),
                     jnp.stack(grp["s"]), jnp.stack(grp["z"])))
    (y_p, nk_p, nv_p, ns_p, nz_p), (y_s, nk_s, nv_s, ns_s, nz_s) = outs
    return (y_p, y_s, nk_p, nv_p, ns_p, nz_p, nk_s, nv_s, ns_s, nz_s)
```

```python
import functools
import math

import jax
import jax.numpy as jnp
import numpy as np
from jax import lax
from jax.experimental import pallas as pl
from jax.experimental.pallas import tpu as pltpu

BF = jnp.bfloat16
F32 = jnp.float32

D_MODEL = 1024
DEPTH = 2
HEAD_DIM = 64
N_Q_HEADS = 8
N_KV_HEADS = 2
GQA_GROUP = 4
ATT_W = 512
KV_W = 128
WINDOW = 128
ATT_SCALE = HEAD_DIM ** -0.5
NUM_BUCKETS = 32
MAX_EXACT = 16
MAX_DISTANCE = 128
NEG_INF = -1e30
RWKV_HEAD = 64
RWKV_W = 512
RWKV_HEADS = 8
D_DECAY_LORA = 64
D_AAA_LORA = 64
D_GATE_LORA = 128
RWKV_IN = 3 * RWKV_W + D_DECAY_LORA + D_AAA_LORA + D_GATE_LORA
GN_EPS = 64e-5
IN_W = ATT_W + 2 * KV_W + RWKV_IN + 2 * D_MODEL
N_EXPERTS = 8
TOP_K = 2
NORM_EPS = 1e-6

VMEM_LIMIT_BYTES = 56 * 1024 * 1024
LANES = 128

INPROJ_ROWS = 512
DENSE_ROWS = 512
ROUTER_ROWS = 256
MOE_BLOCK_ROWS = 512
DISPATCH_ROWS = 2048
COMBINE_ROWS = 1024
RWKV_CHUNK = 64
SAMPLE_ATTN_SEQS = 16


def _params(*sem):
    return pltpu.CompilerParams(dimension_semantics=sem, vmem_limit_bytes=VMEM_LIMIT_BYTES)


def _const_spec(shape):
    zeros = (0,) * len(shape)
    return pl.BlockSpec(shape, lambda *_: zeros, pipeline_mode=pl.Buffered(1))


def _dot(a, b):
    return jnp.dot(a, b, preferred_element_type=F32)


def _dot_nt(a, b):
    return lax.dot_general(a, b, (((1,), (1,)), ((), ())), preferred_element_type=F32)


def _dot_tn(a, b):
    return lax.dot_general(a, b, (((0,), (0,)), ((), ())), preferred_element_type=F32)


def _split2(x):
    hi = x.astype(BF)
    lo = (x - hi.astype(F32)).astype(BF)
    return hi, lo


def _dot_exact_rhs(x, w):
    hi, lo = _split2(x)
    return _dot(hi, w) + _dot(lo, w)


def _rms(x, g):
    ms = jnp.mean(x * x, axis=-1, keepdims=True)
    return x * lax.rsqrt(ms + NORM_EPS) * g


def _sigmoid(x):
    return 1.0 / (1.0 + jnp.exp(-x))


def _cast_kernel(x_ref, o_ref):
    o_ref[...] = x_ref[...].astype(o_ref.dtype)


def _to_bf16(w, row_tile):
    rows, cols = w.shape
    spec = pl.BlockSpec((row_tile, cols), lambda i: (i, 0))
    return pl.pallas_call(
        _cast_kernel, grid=(rows // row_tile,), in_specs=[spec], out_specs=spec,
        out_shape=jax.ShapeDtypeStruct((rows, cols), BF),
        compiler_params=_params("parallel"), name="to_bf16",
    )(w)


def _inproj_kernel(rows_per_seq, x_ref, g_ref, w_ref, pz_ref, mu_ref, w0_ref, w2_ref, a0_ref, a2_ref, g2_ref,
                   kk_ref, ka_ref, ones_ref,
                   q_ref, k_ref, v_ref, gate_ref, zl_ref,
                   r_ref, lw_ref, k2_ref, vr_ref, a_ref, b_ref, gr_ref, *carry):
    n = _rms(x_ref[...], g_ref[...]).astype(BF)
    z0 = ATT_W + 2 * KV_W
    z = _dot(n, w_ref[:, z0:z0 + RWKV_IN])
    q_ref[...] = _dot(n, w_ref[:, 0:ATT_W]).astype(BF)
    k_ref[...] = _dot(n, w_ref[:, ATT_W:ATT_W + KV_W])
    v_ref[...] = _dot(n, w_ref[:, ATT_W + KV_W:ATT_W + 2 * KV_W])
    gate_ref[...] = _dot(n, w_ref[:, z0 + RWKV_IN:IN_W]).astype(gate_ref.dtype)
    tm = z.shape[0]
    if rows_per_seq is None:
        zs = pz_ref[...]
        zl_ref[...] = z
    else:
        prev_ref, = carry
        tiles_per_seq = rows_per_seq // tm

        @pl.when(pl.program_id(0) % tiles_per_seq == 0)
        def _():
            prev_ref[...] = pz_ref[0]

        row = lax.broadcasted_iota(jnp.int32, (tm, 1), 0)
        zs = jnp.where(row == 0, prev_ref[...], pltpu.roll(z, 1, 0))
        prev_ref[...] = z[tm - 1:tm, :]
        zl_ref[0] = z[tm - 1:tm, :]
    zz = z + (zs - z) * mu_ref[...]
    r = zz[:, 0:RWKV_W]
    k = zz[:, RWKV_W:2 * RWKV_W]
    v = zz[:, 2 * RWKV_W:3 * RWKV_W]
    o = 3 * RWKV_W
    wl = zz[:, o:o + D_DECAY_LORA]
    al = zz[:, o + D_DECAY_LORA:o + D_DECAY_LORA + D_AAA_LORA]
    gl = zz[:, o + D_DECAY_LORA + D_AAA_LORA:RWKV_IN]
    wpre = -(w0_ref[...] + _dot(jnp.tanh(wl).astype(BF), w2_ref[...]))
    softplus = jnp.maximum(wpre, 0.0) + jnp.log1p(jnp.exp(-jnp.abs(wpre)))
    lw = -jnp.exp(-softplus - 0.5)
    a = _sigmoid(a0_ref[...] + _dot(al.astype(BF), a2_ref[...]))
    g = _dot(_sigmoid(gl).astype(BF), g2_ref[...])
    kkv = k * kk_ref[...]
    ss = _dot_exact_rhs(kkv * kkv, ones_ref[...])
    kk = kkv / jnp.maximum(jnp.sqrt(ss), 1e-12)
    k2 = k * (1.0 + (a - 1.0) * ka_ref[...])
    r_ref[...], lw_ref[...], k2_ref[...], vr_ref[...] = r, lw, k2, v
    a_ref[...], b_ref[...], gr_ref[...] = -kk, kk * a, g


def _inproj(x, g, p, prev_z, tm, rows_per_seq):
    t = x.shape[0]
    row = lambda w_: pl.BlockSpec((tm, w_), lambda i: (i, 0))
    vec = lambda w_: _const_spec((1, w_))
    if rows_per_seq is None:
        pz_spec = row(RWKV_IN)
        zl_spec, zl_shape = row(RWKV_IN), (t, RWKV_IN)
        scratch = []
    else:
        tiles_per_seq = rows_per_seq // tm
        pz_spec = pl.BlockSpec((1, 1, RWKV_IN), lambda i: (i // tiles_per_seq, 0, 0))
        zl_spec, zl_shape = pz_spec, (t // rows_per_seq, 1, RWKV_IN)
        scratch = [pltpu.VMEM((1, RWKV_IN), F32)]
    return pl.pallas_call(
        functools.partial(_inproj_kernel, rows_per_seq),
        grid=(t // tm,),
        in_specs=[row(D_MODEL), vec(D_MODEL), _const_spec((D_MODEL, IN_W)), pz_spec,
                  vec(RWKV_IN), vec(RWKV_W), _const_spec((D_DECAY_LORA, RWKV_W)),
                  vec(RWKV_W), _const_spec((D_AAA_LORA, RWKV_W)), _const_spec((D_GATE_LORA, RWKV_W)),
                  vec(RWKV_W), vec(RWKV_W), _const_spec((RWKV_W, RWKV_W))],
        out_specs=[row(ATT_W), row(KV_W), row(KV_W), row(2 * D_MODEL), zl_spec] + [row(RWKV_W)] * 7,
        out_shape=[jax.ShapeDtypeStruct((t, ATT_W), BF),
                   jax.ShapeDtypeStruct((t, KV_W), F32),
                   jax.ShapeDtypeStruct((t, KV_W), F32),
                   jax.ShapeDtypeStruct((t, 2 * D_MODEL), BF),
                   jax.ShapeDtypeStruct(zl_shape, F32)] + [jax.ShapeDtypeStruct((t, RWKV_W), F32)] * 7,
        scratch_shapes=scratch,
        compiler_params=_params("arbitrary"),
        name="inproj",
    )(x, g, p["w_in"], prev_z, p["mu"], p["w0"], p["w2"], p["a0"], p["a2"], p["g2"], p["k_k"], p["k_a"],
      p["ones_bd"])


def _t5_bucket(dist):
    n = np.maximum(dist, 0)
    nf = np.maximum(n, 1).astype(np.float32)
    large = MAX_EXACT + (np.log(nf / MAX_EXACT) / math.log(MAX_DISTANCE / MAX_EXACT)
                         * (NUM_BUCKETS - MAX_EXACT)).astype(np.int32)
    return np.where(n < MAX_EXACT, n, np.minimum(large, NUM_BUCKETS - 1)).astype(np.int32)


def _bias_lookup(rel_bias, bucket):
    hit = bucket[..., None, None] == np.arange(NUM_BUCKETS, dtype=np.int32)[:, None]
    return jnp.sum(jnp.where(hit, rel_bias, 0.0), axis=-2)


def _prompt_bucket_table():
    dist = np.arange(WINDOW)[:, None] - (np.arange(2 * WINDOW) - WINDOW)[None, :]
    return np.where((dist >= 0) & (dist <= WINDOW), _t5_bucket(dist), -1).astype(np.int32)


def _swa_prompt_kernel(nb, bucket_ref, relb_ref, q_ref, kp_ref, kc_ref, vp_ref, vc_ref, sink_ref, o_ref, bias_ref):
    first = pl.program_id(0) == 0

    @pl.when(first)
    def _():
        bucket = bucket_ref[...]
        for head in range(N_Q_HEADS):
            tab = jnp.full(bucket.shape, NEG_INF, F32)
            for n in range(NUM_BUCKETS):
                tab = jnp.where(bucket == n, relb_ref[n, head], tab)
            g = head % GQA_GROUP
            bias_ref[head // GQA_GROUP, g * WINDOW:(g + 1) * WINDOW, :] = tab

    col = lax.broadcasted_iota(jnp.int32, (GQA_GROUP * WINDOW, 2 * WINDOW), 1)
    pad_mask = jnp.logical_and(first, col < WINDOW)
    chains = [(b, h) for b in range(nb) for h in range(N_KV_HEADS)]
    hs = lambda h: slice(h * HEAD_DIM, (h + 1) * HEAD_DIM)
    kh = [jnp.concatenate([kp_ref[b][:, hs(h)], kc_ref[b][:, hs(h)]], axis=0).astype(BF) for b, h in chains]
    vh = [jnp.concatenate([vp_ref[b][:, hs(h)], vc_ref[b][:, hs(h)]], axis=0).astype(BF) for b, h in chains]
    qh = [jnp.concatenate([q_ref[b][:, hs(GQA_GROUP * h + g)] for g in range(GQA_GROUP)], axis=0)
          for b, h in chains]
    s = [_dot_nt(qh[n], kh[n]) * ATT_SCALE + bias_ref[h] for n, (b, h) in enumerate(chains)]
    s = [jnp.where(pad_mask, NEG_INF, x) for x in s]
    m = [jnp.maximum(jnp.max(x, axis=-1, keepdims=True), sink_ref[h]) for x, (b, h) in zip(s, chains)]
    p = [jnp.exp(x - mm) for x, mm in zip(s, m)]
    denom = [jnp.sum(x, axis=-1, keepdims=True) + jnp.exp(sink_ref[h] - mm)
             for x, mm, (b, h) in zip(p, m, chains)]
    o = [_dot(x.astype(BF), vv) / d for x, vv, d in zip(p, vh, denom)]
    for b in range(nb):
        pieces = [o[b * N_KV_HEADS + h][g * WINDOW:(g + 1) * WINDOW]
                  for h in range(N_KV_HEADS) for g in range(GQA_GROUP)]
        o_ref[b] = jnp.concatenate(pieces, axis=-1).astype(o_ref.dtype)


def _swa_prompt(q3, k3, v3, rel_bias, sink_col):
    batch, seq, _ = q3.shape
    cur = lambda i: (0, i, 0)
    prev = lambda i: (0, jnp.maximum(i - 1, 0), 0)
    kv_c = pl.BlockSpec((batch, WINDOW, KV_W), cur)
    kv_p = pl.BlockSpec((batch, WINDOW, KV_W), prev)
    bucket = jnp.asarray(_prompt_bucket_table())
    return pl.pallas_call(
        functools.partial(_swa_prompt_kernel, batch),
        grid=(seq // WINDOW,),
        in_specs=[_const_spec(bucket.shape), pl.BlockSpec(memory_space=pltpu.SMEM),
                  pl.BlockSpec((batch, WINDOW, ATT_W), cur), kv_p, kv_c, kv_p, kv_c,
                  _const_spec(sink_col.shape)],
        out_specs=pl.BlockSpec((batch, WINDOW, ATT_W), cur),
        out_shape=jax.ShapeDtypeStruct((batch, seq, ATT_W), BF),
        scratch_shapes=[pltpu.VMEM((N_KV_HEADS, GQA_GROUP * WINDOW, 2 * WINDOW), F32)],
        compiler_params=_params("arbitrary"),
        name="swa_prompt",
    )(bucket, rel_bias, q3, k3, k3, v3, v3, sink_col)


def _swa_sample_kernel(q_ref, kn_ref, vn_ref, ck_ref, cv_ref, bc_ref, bn_ref, sink_ref, o_ref):
    q = q_ref[...]
    qf = q.astype(F32)
    kn = kn_ref[...].astype(BF).astype(F32)
    vn = vn_ref[...].astype(BF).astype(F32)
    ck, cv = ck_ref[...], cv_ref[...]
    head = lax.broadcasted_iota(jnp.int32, (1, N_Q_HEADS, 1), 1)
    low = head < GQA_GROUP
    s_h, sn_h = [], []
    for h in range(N_KV_HEADS):
        sl = slice(h * HEAD_DIM, (h + 1) * HEAD_DIM)
        s_h.append(jnp.einsum("bgd,bkd->bgk", q, ck[:, :, sl].astype(BF), preferred_element_type=F32))
        sn_h.append(jnp.sum(qf * kn[:, None, sl], axis=-1, keepdims=True))
    s = jnp.where(low, s_h[0], s_h[1]) * ATT_SCALE + bc_ref[...]
    sn = jnp.where(low, sn_h[0], sn_h[1]) * ATT_SCALE + bn_ref[...]
    sink = sink_ref[...]
    m = jnp.maximum(jnp.maximum(jnp.max(s, axis=-1, keepdims=True), sn), sink)
    p = jnp.exp(s - m)
    pn = jnp.exp(sn - m)
    denom = jnp.sum(p, axis=-1, keepdims=True) + pn + jnp.exp(sink - m)
    pb = p.astype(BF)
    pnb = pn.astype(BF).astype(F32)
    o_h = []
    for h in range(N_KV_HEADS):
        sl = slice(h * HEAD_DIM, (h + 1) * HEAD_DIM)
        o = jnp.einsum("bgk,bkd->bgd", pb, cv[:, :, sl].astype(BF), preferred_element_type=F32)
        o_h.append(o + pnb * vn[:, None, sl])
    o_ref[...] = (jnp.where(low, o_h[0], o_h[1]) / denom).astype(o_ref.dtype)


def _swa_sample(q3, kn, vn, ck, cv, bias_c, bias_n, sink3, bb):
    b = q3.shape[0]
    return pl.pallas_call(
        _swa_sample_kernel,
        grid=(b // bb,),
        in_specs=[pl.BlockSpec((bb, N_Q_HEADS, HEAD_DIM), lambda i: (i, 0, 0)),
                  pl.BlockSpec((bb, KV_W), lambda i: (i, 0)),
                  pl.BlockSpec((bb, KV_W), lambda i: (i, 0)),
                  pl.BlockSpec((bb, WINDOW, KV_W), lambda i: (i, 0, 0)),
                  pl.BlockSpec((bb, WINDOW, KV_W), lambda i: (i, 0, 0)),
                  _const_spec(bias_c.shape), _const_spec(bias_n.shape), _const_spec(sink3.shape)],
        out_specs=pl.BlockSpec((bb, N_Q_HEADS, HEAD_DIM), lambda i: (i, 0, 0)),
        out_shape=jax.ShapeDtypeStruct((b, N_Q_HEADS, HEAD_DIM), BF),
        compiler_params=_params("parallel"),
        name="swa_sample",
    )(q3, kn, vn, ck, cv, bias_c, bias_n, sink3)


def _rwkv_scan_kernel(nb, chunk, r_ref, lw_ref, k_ref, v_ref, a_ref, b_ref, g_ref, s0_ref,
                      rk_ref, lg_ref, lb_ref, ones_ref, y_ref, sout_ref, s_scr):
    c = pl.program_id(1)

    @pl.when(c == 0)
    def _():
        s_scr[...] = s0_ref[...]

    ri = lax.broadcasted_iota(jnp.int32, (chunk, chunk), 0)
    ci = lax.broadcasted_iota(jnp.int32, (chunk, chunk), 1)
    incl = ri >= ci
    strict = ri > ci
    tri = incl.astype(BF)
    eye = (ri == ci).astype(F32)
    n_sq = int(math.log2(chunk)) - 1

    chains = [(bi, h) for bi in range(nb) for h in range(RWKV_HEADS)]
    hs = lambda h: slice(h * RWKV_HEAD, (h + 1) * RWKV_HEAD)

    ra, aa, bt, kt, bh, kh, vb, g_all = [], [], [], [], [], [], [], []
    for bi in range(nb):
        r, lw, k, v = r_ref[bi], lw_ref[bi], k_ref[bi], v_ref[bi]
        a, b = a_ref[bi], b_ref[bi]
        hi, lo = _split2(lw)
        lo2 = (lw - hi.astype(F32) - lo.astype(F32)).astype(BF)
        cum = _dot(tri, hi) + _dot(tri, lo) + _dot(tri, lo2)
        tail = cum[chunk - 1:chunk, :]
        g_inv = jnp.exp(-cum)
        g_tail = jnp.exp(tail - cum)
        g_all.append(jnp.exp(tail))
        ra.append((r * jnp.exp(cum)).astype(BF))
        aa.append((a * jnp.exp(cum - lw)).astype(BF))
        bt.append((b * g_inv).astype(BF))
        kt.append((k * g_inv).astype(BF))
        bh.append((b * g_tail).astype(BF))
        kh.append((k * g_tail).astype(BF))
        vb.append(v.astype(BF))

    s_old = [s_scr[bi, h] for bi, h in chains]
    ar = [jnp.concatenate([aa[bi][:, hs(h)], ra[bi][:, hs(h)]], axis=0) for bi, h in chains]
    v_h = [vb[bi][:, hs(h)] for bi, h in chains]
    gb = [_dot_nt(ar[n], bt[bi][:, hs(h)]) for n, (bi, h) in enumerate(chains)]
    gk = [_dot_nt(ar[n], kt[bi][:, hs(h)]) for n, (bi, h) in enumerate(chains)]
    p = [_dot_nt(ar[n], s_old[n].astype(BF)) for n in range(len(chains))]
    l_ab = [jnp.where(strict, x[:chunk], 0.0) for x in gb]
    l_ak = [jnp.where(strict, x[:chunk], 0.0).astype(BF) for x in gk]
    m_rb = [jnp.where(incl, x[chunk:], 0.0).astype(BF) for x in gb]
    m_rk = [jnp.where(incl, x[chunk:], 0.0).astype(BF) for x in gk]
    rhs = [p[n][:chunk] + _dot(l_ak[n], v_h[n]) for n in range(len(chains))]
    t_inv = [eye + x for x in l_ab]
    lp = l_ab
    for _ in range(n_sq):
        lpb = [x.astype(BF) for x in lp]
        lp = [_dot(x, x) for x in lpb]
        t_inv = [t + _dot(x.astype(BF), t.astype(BF)) for x, t in zip(lp, t_inv)]
    ub = [_dot(t.astype(BF), x.astype(BF)).astype(BF) for t, x in zip(t_inv, rhs)]
    y_h = [p[n][chunk:] + _dot(m_rb[n], ub[n]) + _dot(m_rk[n], v_h[n]) for n in range(len(chains))]
    s_new = [s_old[n] * g_all[bi][:, hs(h)] + _dot_tn(ub[n], bh[bi][:, hs(h)]) + _dot_tn(v_h[n], kh[bi][:, hs(h)])
             for n, (bi, h) in enumerate(chains)]
    for n, (bi, h) in enumerate(chains):
        s_scr[bi, h] = s_new[n]

    for bi in range(nb):
        y = jnp.concatenate(y_h[bi * RWKV_HEADS:(bi + 1) * RWKV_HEADS], axis=-1)
        y_ref[bi] = _rwkv_epilogue(y, r_ref[bi], k_ref[bi], v_ref[bi], g_ref[bi],
                                   rk_ref, lg_ref, lb_ref, ones_ref).astype(y_ref.dtype)

    @pl.when(c == pl.num_programs(1) - 1)
    def _():
        sout_ref[...] = s_scr[...]


def _rwkv_epilogue(y, r, k, v, g, rk_ref, lg_ref, lb_ref, ones_ref):
    ones = ones_ref[...]
    inv_n = 1.0 / RWKV_HEAD
    mean = _dot_exact_rhs(y, ones) * inv_n
    d = y - mean
    var = _dot_exact_rhs(d * d, ones) * inv_n
    yn = d * lax.rsqrt(var + GN_EPS) * lg_ref[...] + lb_ref[...]
    bonus = _dot_exact_rhs(r * k * rk_ref[...], ones) * v
    return (yn + bonus) * g


def _rwkv_scan(coef, s0, p, nb, chunk):
    r = coef[0]
    b, t, _ = r.shape
    seq_spec = pl.BlockSpec((nb, chunk, RWKV_W), lambda bi, c: (bi, c, 0))
    st_spec = pl.BlockSpec((nb, RWKV_HEADS, RWKV_HEAD, RWKV_HEAD), lambda bi, c: (bi, 0, 0, 0))
    vec = _const_spec((1, RWKV_W))
    return pl.pallas_call(
        functools.partial(_rwkv_scan_kernel, nb, chunk),
        grid=(b // nb, t // chunk),
        in_specs=[seq_spec] * 7 + [st_spec, vec, vec, vec, _const_spec((RWKV_W, RWKV_W))],
        out_specs=[seq_spec, st_spec],
        out_shape=[jax.ShapeDtypeStruct((b, t, RWKV_W), BF),
                   jax.ShapeDtypeStruct((b, RWKV_HEADS, RWKV_HEAD, RWKV_HEAD), F32)],
        scratch_shapes=[pltpu.VMEM((nb, RWKV_HEADS, RWKV_HEAD, RWKV_HEAD), F32)],
        compiler_params=_params("parallel", "arbitrary"),
        name="rwkv_scan",
    )(*coef, s0, p["r_k"], p["lnx_g"], p["lnx_b"], p["ones_bd"])


def _rwkv_step_kernel(r_ref, lw_ref, k_ref, v_ref, a_ref, b_ref, g_ref, s_ref,
                      rk_ref, lg_ref, lb_ref, ones_ref, y_ref, sout_ref, y_scr):
    n_pair = LANES // RWKV_HEAD
    r, k, v = r_ref[...], k_ref[...], v_ref[...]
    rT, kT, vT = r.T, k.T, v.T
    aT, bT, wT = a_ref[...].T, b_ref[...].T, jnp.exp(lw_ref[...]).T
    for hl in range(LANES // RWKV_HEAD):
        hsl = slice(hl * RWKV_HEAD, (hl + 1) * RWKV_HEAD)
        a_h, b_h, k_h, w_h, r_h = aT[hsl], bT[hsl], kT[hsl], wT[hsl], rT[hsl]
        tiles = [(hl * RWKV_HEAD + n_pair * t) * RWKV_HEAD for t in range(RWKV_HEAD // n_pair)]
        st = [s_ref[:, c0:c0 + LANES].T for c0 in tiles]
        new = []
        for t, x in enumerate(st):
            halves = []
            for il in range(n_pair):
                i = hl * RWKV_HEAD + n_pair * t + il
                slab = x[il * RWKV_HEAD:(il + 1) * RWKV_HEAD]
                sa = jnp.sum(slab * a_h, axis=0, keepdims=True)
                slab = slab * w_h + sa * b_h + vT[i:i + 1] * k_h
                y_scr[i:i + 1, :] = jnp.sum(slab * r_h, axis=0, keepdims=True)
                halves.append(slab)
            new.append(jnp.concatenate(halves, axis=0))
        for c0, x in zip(tiles, new):
            sout_ref[:, c0:c0 + LANES] = x.T
    y_ref[...] = _rwkv_epilogue(y_scr[...].T, r, k, v, g_ref[...],
                                rk_ref, lg_ref, lb_ref, ones_ref).astype(y_ref.dtype)


def _rwkv_step(coef, s0, p):
    b = s0.shape[0]
    per_pair = (LANES // RWKV_HEAD) * RWKV_HEAD * RWKV_HEAD
    n_steps = RWKV_W // LANES
    col = pl.BlockSpec((b, LANES), lambda i: (0, i))
    vec = pl.BlockSpec((1, LANES), lambda i: (0, i))
    st_spec = pl.BlockSpec((b, per_pair), lambda i: (0, i))
    y, s_new = pl.pallas_call(
        _rwkv_step_kernel,
        grid=(n_steps,),
        in_specs=[col] * 7 + [st_spec, vec, vec, vec, _const_spec((LANES, LANES))],
        out_specs=[col, st_spec],
        out_shape=[jax.ShapeDtypeStruct((b, RWKV_W), BF),
                   jax.ShapeDtypeStruct((b, n_steps * per_pair), F32)],
        scratch_shapes=[pltpu.VMEM((LANES, b), F32)],
        compiler_params=_params("parallel"),
        name="rwkv_step",
    )(*coef, s0.reshape(b, n_steps * per_pair), p["r_k"], p["lnx_g"], p["lnx_b"],
      p["ones_bd"][:LANES, :LANES])
    return y, s_new.reshape(s0.shape)


def _merge_kernel(with_router, x_ref, ya_ref, yb_ref, gate_ref, wa_ref, wb_ref, wo_ref, g_ref, *rest):
    if with_router:
        wr_ref, cnt_in_ref, x1_ref, h_ref, route_ref, cnt_out_ref, cnt_scr = rest
    else:
        x1_ref, h_ref = rest
    gate = gate_ref[...].astype(F32)
    pa = _dot(ya_ref[...], wa_ref[...])
    pb = _dot(yb_ref[...], wb_ref[...])
    merged = _sigmoid(gate[:, :D_MODEL]) * pa + _sigmoid(gate[:, D_MODEL:]) * pb
    x1 = x_ref[...] + _dot(merged.astype(BF), wo_ref[...])
    x1_ref[...] = x1
    h = _rms(x1, g_ref[...])
    h_ref[...] = h.astype(h_ref.dtype)
    if not with_router:
        return

    @pl.when(pl.program_id(0) == 0)
    def _():
        cnt_scr[...] = cnt_in_ref[...]

    logits = _dot(h.astype(BF), wr_ref[...])
    tm = logits.shape[0]
    lane = lax.broadcasted_iota(jnp.int32, logits.shape, 1).astype(F32)
    logits = jnp.where(lane < N_EXPERTS, logits, -jnp.inf)
    m1 = jnp.max(logits, axis=-1, keepdims=True)
    i1 = jnp.min(jnp.where(logits == m1, lane, float(LANES)), axis=-1, keepdims=True)
    rest_l = jnp.where(lane == i1, -jnp.inf, logits)
    m2 = jnp.max(rest_l, axis=-1, keepdims=True)
    i2 = jnp.min(jnp.where(rest_l == m2, lane, float(LANES)), axis=-1, keepdims=True)
    e2 = jnp.exp(m2 - m1)
    g1 = 1.0 / (1.0 + e2)
    g2 = e2 / (1.0 + e2)
    oh1 = (lane == i1).astype(F32)
    oh2 = (lane == i2).astype(F32)
    both = oh1 + oh2
    ri = lax.broadcasted_iota(jnp.int32, (tm, tm), 0)
    ci = lax.broadcasted_iota(jnp.int32, (tm, tm), 1)
    before = _dot((ri > ci).astype(BF), both.astype(BF)) + cnt_scr[...]
    rank1 = jnp.sum(oh1 * before, axis=-1, keepdims=True)
    rank2 = jnp.sum(oh2 * before, axis=-1, keepdims=True)
    cnt_scr[...] += jnp.sum(both, axis=0, keepdims=True)
    cnt_out_ref[...] = cnt_scr[...]
    route = jnp.zeros_like(logits)
    for n, val in enumerate((i1, i2, g1, g2, rank1, rank2)):
        route = jnp.where(lane == n, val, route)
    route_ref[...] = route


def _merge_out(x, ya, yb, gate, wa, wb, wo, g, tm, router=None, cnt_in=None):
    t = x.shape[0]
    row = lambda w_: pl.BlockSpec((tm, w_), lambda i: (i, 0))
    in_specs = [row(D_MODEL), row(ATT_W), row(RWKV_W), row(2 * D_MODEL),
                _const_spec((ATT_W, D_MODEL)), _const_spec((RWKV_W, D_MODEL)),
                _const_spec((D_MODEL, D_MODEL)), _const_spec((1, D_MODEL))]
    args = [x, ya, yb, gate, wa, wb, wo, g]
    if router is None:
        out_specs = [row(D_MODEL), row(D_MODEL)]
        out_shape = [jax.ShapeDtypeStruct((t, D_MODEL), F32), jax.ShapeDtypeStruct((t, D_MODEL), BF)]
        scratch = []
    else:
        in_specs += [_const_spec(router.shape), _const_spec((1, LANES))]
        args += [router, cnt_in]
        out_specs = [row(D_MODEL), row(D_MODEL), row(LANES), pl.BlockSpec((1, LANES), lambda i: (0, 0))]
        out_shape = [jax.ShapeDtypeStruct((t, D_MODEL), F32), jax.ShapeDtypeStruct((t, D_MODEL), F32),
                     jax.ShapeDtypeStruct((t, LANES), F32), jax.ShapeDtypeStruct((1, LANES), F32)]
        scratch = [pltpu.VMEM((1, LANES), F32)]
    return pl.pallas_call(
        functools.partial(_merge_kernel, router is not None),
        grid=(t // tm,),
        in_specs=in_specs, out_specs=out_specs, out_shape=out_shape, scratch_shapes=scratch,
        compiler_params=_params("parallel" if router is None else "arbitrary"),
        name="merge_out",
    )(*args)


def _swiglu(x, wg, wu, wd):
    s = _dot(x, wg)
    u = _dot(x, wu)
    act = (s * _sigmoid(s) * u).astype(BF)
    return _dot(act, wd)


def _dense_ffn_kernel(x1_ref, h_ref, wg_ref, wu_ref, wd_ref, o_ref):
    o_ref[...] = x1_ref[...] + _swiglu(h_ref[...], wg_ref[...], wu_ref[...], wd_ref[...])


def _dense_ffn(x1, h, wg, wu, wd, tm):
    t = x1.shape[0]
    d_ff = wg.shape[1]
    row = lambda: pl.BlockSpec((tm, D_MODEL), lambda i: (i, 0))
    return pl.pallas_call(
        _dense_ffn_kernel,
        grid=(t // tm,),
        in_specs=[row(), row(), _const_spec((D_MODEL, d_ff)), _const_spec((D_MODEL, d_ff)),
                  _const_spec((d_ff, D_MODEL))],
        out_specs=row(),
        out_shape=jax.ShapeDtypeStruct((t, D_MODEL), F32),
        compiler_params=_params("parallel"),
        name="dense_ffn",
    )(x1, h, wg, wu, wd)


def _dispatch_kernel(tt, first, dest_ref, fill_ref, h_ref, *rest):
    if first:
        xs_hbm, zbuf, sem, zsem = rest
    else:
        _, xs_hbm, sem = rest
    i = pl.program_id(0)
    base = i * tt

    if first:
        @pl.when(i == 0)
        def _():
            zbuf[...] = jnp.zeros_like(zbuf)
            bm = zbuf.shape[0]
            fills = [pltpu.make_async_copy(zbuf, xs_hbm.at[pl.ds(fill_ref[n] * bm, bm), :], zsem)
                     for n in range(fill_ref.shape[0])]
            for f in fills:
                f.start()
            for f in fills:
                f.wait()

    def issue(r, carry):
        for j in range(TOP_K):
            row = dest_ref[TOP_K * (base + r) + j]
            pltpu.make_async_copy(h_ref.at[pl.ds(r, 1), :], xs_hbm.at[pl.ds(row, 1), :], sem).start()
        return carry

    lax.fori_loop(0, tt, issue, 0, unroll=8)
    for j in range(TOP_K):
        pltpu.make_async_copy(h_ref, xs_hbm.at[pl.ds(0, tt), :], sem).wait()


def _dispatch(dest, fill_blocks, h, xs, tt, bm, n_rows):
    t = h.shape[0]
    first = xs is None
    in_specs = [pl.BlockSpec((tt, D_MODEL), lambda i, ds, fb: (i, 0))]
    args = [dest, fill_blocks, h]
    if first:
        scratch = [pltpu.VMEM((bm, D_MODEL), F32), pltpu.SemaphoreType.DMA(()), pltpu.SemaphoreType.DMA(())]
        aliases = {}
    else:
        in_specs.append(pl.BlockSpec(memory_space=pl.ANY))
        args.append(xs)
        scratch = [pltpu.SemaphoreType.DMA(())]
        aliases = {3: 0}
    grid_spec = pltpu.PrefetchScalarGridSpec(
        num_scalar_prefetch=2, grid=(t // tt,), in_specs=in_specs,
        out_specs=pl.BlockSpec(memory_space=pl.ANY), scratch_shapes=scratch)
    return pl.pallas_call(
        functools.partial(_dispatch_kernel, tt, first),
        grid_spec=grid_spec,
        out_shape=jax.ShapeDtypeStruct((n_rows, D_MODEL), F32),
        input_output_aliases=aliases,
        compiler_params=_params("arbitrary"),
        name="moe_dispatch",
    )(*args)


def _moe_kernel(n_half, blk_e_ref, n_used_ref, xs_ref, wg_ref, wu_ref, wd_ref, y_ref):
    @pl.when(pl.program_id(0) < n_used_ref[0])
    def _():
        x = xs_ref[...].astype(BF)
        d_e = wg_ref.shape[2]
        step = d_e // n_half
        acc = None
        for j in range(n_half):
            cs = slice(j * step, (j + 1) * step)
            part = _swiglu(x, wg_ref[0, :, cs], wu_ref[0, :, cs], wd_ref[0, cs, :])
            acc = part if acc is None else acc + part
        y_ref[...] = acc

    @pl.when(pl.program_id(0) >= n_used_ref[0])
    def _():
        y_ref[...] = jnp.zeros_like(y_ref)


def _moe_ffn(xs, blk_e, n_used, wg, wu, wd, bm):
    n_blk = blk_e.shape[0]
    n_rows = xs.shape[0]
    d_e = wg.shape[2]
    wspec = lambda shape: pl.BlockSpec(shape, lambda i, be, nu: (be[i], 0, 0), pipeline_mode=pl.Buffered(1))
    rows = pl.BlockSpec((bm, D_MODEL), lambda i, be, nu: (i, 0))
    grid_spec = pltpu.PrefetchScalarGridSpec(
        num_scalar_prefetch=2,
        grid=(n_blk,),
        in_specs=[rows, wspec((1, D_MODEL, d_e)), wspec((1, D_MODEL, d_e)), wspec((1, d_e, D_MODEL))],
        out_specs=rows,
    )
    return pl.pallas_call(
        functools.partial(_moe_kernel, 2),
        grid_spec=grid_spec,
        out_shape=jax.ShapeDtypeStruct((n_rows, D_MODEL), F32),
        compiler_params=_params("arbitrary"),
        name="moe_ffn",
    )(blk_e, n_used, xs, wg, wu, wd)


def _row_layout(counts, bm, n_blk):
    per_e = (counts + bm - 1) // bm
    ends = jnp.cumsum(per_e)
    n_used = ends[-1]
    first_row = (ends - per_e) * bm
    i = jnp.arange(n_blk, dtype=jnp.int32)
    blk_e = jnp.sum((jnp.minimum(i, jnp.maximum(n_used - 1, 0))[:, None] >= ends[None, :]).astype(jnp.int32), axis=1)
    need = jnp.any(i[:, None] == (ends - 1)[None, :], axis=1) | (i >= n_blk - N_EXPERTS)
    fill_blocks = jnp.argsort(jnp.logical_not(need), stable=True)[:2 * N_EXPERTS].astype(jnp.int32)
    return first_row.astype(jnp.int32), blk_e.astype(jnp.int32), n_used.astype(jnp.int32).reshape(1), fill_blocks


def _combine_kernel(tc, dest_ref, x1_ref, route_ref, y_hbm, g_ref, o_ref, ybuf, sem):
    base = pl.program_id(0) * tc

    def issue(r, carry):
        for j in range(TOP_K):
            row = dest_ref[TOP_K * (base + r) + j]
            pltpu.make_async_copy(y_hbm.at[pl.ds(row, 1), :], ybuf.at[j, pl.ds(r, 1), :], sem).start()
        return carry

    lax.fori_loop(0, tc, issue, 0, unroll=4)
    for j in range(TOP_K):
        pltpu.make_async_copy(y_hbm.at[pl.ds(0, tc), :], ybuf.at[j], sem).wait()
    route = route_ref[...]
    x2 = x1_ref[...] + route[:, 2:3] * ybuf[0] + route[:, 3:4] * ybuf[1]
    o_ref[...] = _rms(x2, g_ref[...])


def _moe_combine(dest, x1, route, y, g, tc):
    t = x1.shape[0]
    grid_spec = pltpu.PrefetchScalarGridSpec(
        num_scalar_prefetch=1,
        grid=(t // tc,),
        in_specs=[pl.BlockSpec((tc, D_MODEL), lambda i, ds: (i, 0)),
                  pl.BlockSpec((tc, LANES), lambda i, ds: (i, 0)),
                  pl.BlockSpec(memory_space=pl.ANY),
                  pl.BlockSpec((1, D_MODEL), lambda i, ds: (0, 0))],
        out_specs=pl.BlockSpec((tc, D_MODEL), lambda i, ds: (i, 0)),
        scratch_shapes=[pltpu.VMEM((TOP_K, tc, D_MODEL), F32), pltpu.SemaphoreType.DMA(())],
    )
    return pl.pallas_call(
        functools.partial(_combine_kernel, tc),
        grid_spec=grid_spec,
        out_shape=jax.ShapeDtypeStruct((t, D_MODEL), F32),
        compiler_params=_params("arbitrary"),
        name="moe_combine",
    )(dest, x1, route, y, g)


def _layer_params(l, w_in, mu_shift, w0, w2, a0, a2, g2, k_k, k_a, r_k, lnx_g, lnx_b,
                  w_proj_attn, w_proj_rwkv, w_out, ones_bd):
    row = lambda u: u.reshape(1, -1)
    return dict(
        w_in=w_in[l].astype(BF), mu=row(mu_shift[l]), w0=row(w0[l]), w2=w2[l].astype(BF),
        a0=row(a0[l]), a2=a2[l].astype(BF), g2=g2[l].astype(BF), k_k=row(k_k[l]), k_a=row(k_a[l]),
        r_k=row(r_k[l]), lnx_g=row(lnx_g[l]), lnx_b=row(lnx_b[l]),
        wa=w_proj_attn[l].astype(BF), wb=w_proj_rwkv[l].astype(BF), wo=w_out[l].astype(BF),
        ones_bd=ones_bd)


def kernel(x_prompt, x_sample, cache_k, cache_v, state_wkv, state_shift, norm_mix_g, w_in, attn_sinks, rel_bias, mu_shift, w0, w2, a0, a2, g2, k_k, k_a, r_k, lnx_g, lnx_b, w_proj_attn, w_proj_rwkv, w_out, norm_ffn_g, dense_w_gate, dense_w_up, dense_w_down, router_w, moe_w_gate, moe_w_up, moe_w_down, norm_final_g):
    batch, seq, _ = x_prompt.shape
    dec_batch, dec_seq, _ = x_sample.shape
    assert dec_seq == 1 and seq % WINDOW == 0

    head_id = jnp.arange(RWKV_W, dtype=jnp.int32) // RWKV_HEAD
    ones_bd = (head_id[:, None] == head_id[None, :]).astype(BF)
    bias_by_dist = _bias_lookup(rel_bias, _t5_bucket(np.arange(WINDOW + 1)))
    bias_c = jnp.transpose(bias_by_dist[WINDOW:0:-1], (1, 0))[None]
    bias_n = bias_by_dist[0][None, :, None]

    layers = [_layer_params(l, w_in, mu_shift, w0, w2, a0, a2, g2, k_k, k_a, r_k, lnx_g, lnx_b,
                            w_proj_attn, w_proj_rwkv, w_out, ones_bd) for l in range(DEPTH)]

    def router_pieces(l):
        return jnp.pad(router_w[l // 2], ((0, 0), (0, LANES - N_EXPERTS))).astype(BF)

    n_prompt = batch * seq
    bm = MOE_BLOCK_ROWS
    n_all = n_prompt + dec_batch
    n_blk = -(-(n_all * TOP_K) // bm) + N_EXPERTS

    def mix(x, is_prompt, l, state):
        p = layers[l]
        n_tok = x.shape[0]
        gm = norm_mix_g[l].reshape(1, -1)
        sinks = attn_sinks[l]
        if is_prompt:
            b, t = batch, seq
            prev_z = jnp.zeros((b, 1, RWKV_IN), F32)
            q, k, v, gate, z_last, *coef = _inproj(x, gm, p, prev_z, min(INPROJ_ROWS, t), t)
            sink_col = jnp.repeat(sinks, WINDOW).reshape(N_KV_HEADS, GQA_GROUP * WINDOW, 1)
            ya = _swa_prompt(q.reshape(b, t, ATT_W), k.reshape(b, t, KV_W), v.reshape(b, t, KV_W),
                             rel_bias, sink_col).reshape(n_tok, ATT_W)
            k4 = k.reshape(b, t, N_KV_HEADS, HEAD_DIM)
            v4 = v.reshape(b, t, N_KV_HEADS, HEAD_DIM)
            state["k"].append(k4[:, t - WINDOW:])
            state["v"].append(v4[:, t - WINDOW:])
            s0 = jnp.zeros((b, RWKV_HEADS, RWKV_HEAD, RWKV_HEAD), F32)
            yb3, s_new = _rwkv_scan([u.reshape(b, t, RWKV_W) for u in coef], s0, p, b, RWKV_CHUNK)
            yb = yb3.reshape(n_tok, RWKV_W)
            state["z"].append(z_last.reshape(b, RWKV_IN))
        else:
            b = dec_batch
            q, k, v, gate, z, *coef = _inproj(x, gm, p, state_shift[l], n_tok, None)
            ck = cache_k[l].reshape(b, WINDOW, KV_W)
            cv = cache_v[l].reshape(b, WINDOW, KV_W)
            ya3 = _swa_sample(q.reshape(b, N_Q_HEADS, HEAD_DIM), k, v, ck, cv, bias_c, bias_n,
                              sinks.reshape(1, N_Q_HEADS, 1), min(SAMPLE_ATTN_SEQS, b))
            ya = ya3.reshape(b, ATT_W)
            state["k"].append(jnp.concatenate([ck[:, 1:], k[:, None]], axis=1)
                              .reshape(b, WINDOW, N_KV_HEADS, HEAD_DIM))
            state["v"].append(jnp.concatenate([cv[:, 1:], v[:, None]], axis=1)
                              .reshape(b, WINDOW, N_KV_HEADS, HEAD_DIM))
            yb, s_new = _rwkv_step(coef, state_wkv[l], p)
            state["z"].append(z)
        state["s"].append(s_new)
        return ya, yb, gate

    groups = [dict(x=x_prompt.reshape(n_prompt, D_MODEL), prompt=True, tm=min(DENSE_ROWS, n_prompt), k=[], v=[], s=[], z=[]),
              dict(x=x_sample.reshape(dec_batch, D_MODEL), prompt=False, tm=min(DENSE_ROWS, dec_batch), k=[], v=[], s=[], z=[])]
    for l in range(DEPTH):
        p = layers[l]
        gn = norm_ffn_g[l].reshape(1, -1)
        j = l // 2
        if l % 2 == 0:
            wg, wu, wd = (_to_bf16(w[j], w.shape[1] // 4) for w in (dense_w_gate, dense_w_up, dense_w_down))
            for grp in groups:
                ya, yb, gate = mix(grp["x"], grp["prompt"], l, grp)
                x1, h = _merge_out(grp["x"], ya, yb, gate, p["wa"], p["wb"], p["wo"], gn, grp["tm"])
                grp["x"] = _dense_ffn(x1, h, wg, wu, wd, grp["tm"])
        else:
            assert l == DEPTH - 1
            wg, wu, wd = (w[j].astype(BF) for w in (moe_w_gate, moe_w_up, moe_w_down))
            router = router_pieces(l)
            cnt = jnp.zeros((1, LANES), F32)
            xs = None
            for grp in groups:
                ya, yb, gate = mix(grp["x"], grp["prompt"], l, grp)
                x1, h, route, cnt = _merge_out(grp["x"], ya, yb, gate, p["wa"], p["wb"], p["wo"], gn,
                                               min(ROUTER_ROWS, grp["tm"]), router=router, cnt_in=cnt)
                grp["x1"], grp["route"], grp["h"] = x1, route, h
            counts = cnt[0, :N_EXPERTS].astype(jnp.int32)
            first_row, blk_e, n_used, fill_blocks = _row_layout(counts, bm, n_blk)
            for grp in groups:
                route = grp["route"]
                expert = route[:, 0:TOP_K].astype(jnp.int32)
                rank = route[:, 2 * TOP_K:3 * TOP_K].astype(jnp.int32)
                start = jnp.sum(jnp.where(expert[..., None] == jnp.arange(N_EXPERTS, dtype=jnp.int32),
                                          first_row, 0), axis=-1)
                grp["dest"] = (start + rank).reshape(-1)
                xs = _dispatch(grp["dest"], fill_blocks, grp["h"], xs, min(DISPATCH_ROWS, grp["h"].shape[0]), bm,
                               n_blk * bm)
            y = _moe_ffn(xs, blk_e, n_used, wg, wu, wd, bm)
            for grp in groups:
                grp["x"] = _moe_combine(grp["dest"], grp["x1"], grp["route"], y,
                                        norm_final_g.reshape(1, -1), min(COMBINE_ROWS, grp["x1"].shape[0]))

    gp, gs = groups
    outs = []
    for grp, shape in ((gp, (batch, seq, D_MODEL)), (gs, (dec_batch, dec_seq, D_MODEL))):
        outs.append((grp["x"].reshape(shape), jnp.stack(grp["k"]), jnp.stack(grp["v"]),
                     jnp.stack(grp["s"]), jnp.stack(grp["z"])))
    (y_p, nk_p, nv_p, ns_p, nz_p), (y_s, nk_s, nv_s, ns_s, nz_s) = outs
    return (y_p, y_s, nk_p, nv_p, ns_p, nz_p, nk_s, nv_s, ns_s, nz_s)
```

```python
import functools
import math

import jax
import jax.numpy as jnp
import numpy as np
from jax import lax
from jax.experimental import pallas as pl
from jax.experimental.pallas import tpu as pltpu

BF = jnp.bfloat16
F32 = jnp.float32

D_MODEL = 1024
DEPTH = 2
HEAD_DIM = 64
N_Q_HEADS = 8
N_KV_HEADS = 2
GQA_GROUP = 4
ATT_W = 512
KV_W = 128
WINDOW = 128
ATT_SCALE = HEAD_DIM ** -0.5
NUM_BUCKETS = 32
MAX_EXACT = 16
MAX_DISTANCE = 128
NEG_INF = -1e30
RWKV_HEAD = 64
RWKV_W = 512
RWKV_HEADS = 8
D_DECAY_LORA = 64
D_AAA_LORA = 64
D_GATE_LORA = 128
RWKV_IN = 3 * RWKV_W + D_DECAY_LORA + D_AAA_LORA + D_GATE_LORA
GN_EPS = 64e-5
IN_W = ATT_W + 2 * KV_W + RWKV_IN + 2 * D_MODEL
N_EXPERTS = 8
TOP_K = 2
NORM_EPS = 1e-6

VMEM_LIMIT_BYTES = 56 * 1024 * 1024
LANES = 128

INPROJ_ROWS = 512
DENSE_ROWS = 512
ROUTER_ROWS = 512
ROUTER_RANK_ROWS = 256
MOE_BLOCK_ROWS = 512
DISPATCH_ROWS = 2048
COMBINE_ROWS = 1024
RWKV_CHUNK = 64
SAMPLE_ATTN_SEQS = 16


def _params(*sem):
    return pltpu.CompilerParams(dimension_semantics=sem, vmem_limit_bytes=VMEM_LIMIT_BYTES)


def _const_spec(shape):
    zeros = (0,) * len(shape)
    return pl.BlockSpec(shape, lambda *_: zeros, pipeline_mode=pl.Buffered(1))


def _dot(a, b):
    return jnp.dot(a, b, preferred_element_type=F32)


def _dot_nt(a, b):
    return lax.dot_general(a, b, (((1,), (1,)), ((), ())), preferred_element_type=F32)


def _dot_tn(a, b):
    return lax.dot_general(a, b, (((0,), (0,)), ((), ())), preferred_element_type=F32)


def _split2(x):
    hi = x.astype(BF)
    lo = (x - hi.astype(F32)).astype(BF)
    return hi, lo


def _dot_exact_rhs(x, w):
    hi, lo = _split2(x)
    return _dot(hi, w) + _dot(lo, w)


def _rms(x, g):
    ms = jnp.mean(x * x, axis=-1, keepdims=True)
    return x * lax.rsqrt(ms + NORM_EPS) * g


def _sigmoid(x):
    return 1.0 / (1.0 + jnp.exp(-x))


def _cast_kernel(x_ref, o_ref):
    o_ref[...] = x_ref[...].astype(o_ref.dtype)


def _to_bf16(w, row_tile):
    rows, cols = w.shape
    spec = pl.BlockSpec((row_tile, cols), lambda i: (i, 0))
    return pl.pallas_call(
        _cast_kernel, grid=(rows // row_tile,), in_specs=[spec], out_specs=spec,
        out_shape=jax.ShapeDtypeStruct((rows, cols), BF),
        compiler_params=_params("parallel"), name="to_bf16",
    )(w)


def _inproj_kernel(rows_per_seq, x_ref, g_ref, w_ref, pz_ref, mu_ref, w0_ref, w2_ref, a0_ref, a2_ref, g2_ref,
                   kk_ref, ka_ref, ones_ref,
                   q_ref, k_ref, v_ref, gate_ref, zl_ref,
                   r_ref, lw_ref, k2_ref, vr_ref, a_ref, b_ref, gr_ref, *carry):
    n = _rms(x_ref[...], g_ref[...]).astype(BF)
    z0 = ATT_W + 2 * KV_W
    z = _dot(n, w_ref[:, z0:z0 + RWKV_IN])
    q_ref[...] = _dot(n, w_ref[:, 0:ATT_W]).astype(BF)
    k_ref[...] = _dot(n, w_ref[:, ATT_W:ATT_W + KV_W])
    v_ref[...] = _dot(n, w_ref[:, ATT_W + KV_W:ATT_W + 2 * KV_W])
    gate_ref[...] = _dot(n, w_ref[:, z0 + RWKV_IN:IN_W]).astype(gate_ref.dtype)
    tm = z.shape[0]
    if rows_per_seq is None:
        zs = pz_ref[...]
        zl_ref[...] = z
    else:
        prev_ref, = carry
        tiles_per_seq = rows_per_seq // tm

        @pl.when(pl.program_id(0) % tiles_per_seq == 0)
        def _():
            prev_ref[...] = pz_ref[0]

        row = lax.broadcasted_iota(jnp.int32, (tm, 1), 0)
        zs = jnp.where(row == 0, prev_ref[...], pltpu.roll(z, 1, 0))
        prev_ref[...] = z[tm - 1:tm, :]
        zl_ref[0] = z[tm - 1:tm, :]
    zz = z + (zs - z) * mu_ref[...]
    r = zz[:, 0:RWKV_W]
    k = zz[:, RWKV_W:2 * RWKV_W]
    v = zz[:, 2 * RWKV_W:3 * RWKV_W]
    o = 3 * RWKV_W
    wl = zz[:, o:o + D_DECAY_LORA]
    al = zz[:, o + D_DECAY_LORA:o + D_DECAY_LORA + D_AAA_LORA]
    gl = zz[:, o + D_DECAY_LORA + D_AAA_LORA:RWKV_IN]
    wpre = -(w0_ref[...] + _dot(jnp.tanh(wl).astype(BF), w2_ref[...]))
    softplus = jnp.maximum(wpre, 0.0) + jnp.log1p(jnp.exp(-jnp.abs(wpre)))
    lw = -jnp.exp(-softplus - 0.5)
    a = _sigmoid(a0_ref[...] + _dot(al.astype(BF), a2_ref[...]))
    g = _dot(_sigmoid(gl).astype(BF), g2_ref[...])
    kkv = k * kk_ref[...]
    ss = _dot_exact_rhs(kkv * kkv, ones_ref[...])
    kk = kkv / jnp.maximum(jnp.sqrt(ss), 1e-12)
    k2 = k * (1.0 + (a - 1.0) * ka_ref[...])
    r_ref[...], lw_ref[...], k2_ref[...], vr_ref[...] = r, lw, k2, v
    a_ref[...], b_ref[...], gr_ref[...] = -kk, kk * a, g


def _inproj(x, g, p, prev_z, tm, rows_per_seq):
    t = x.shape[0]
    row = lambda w_: pl.BlockSpec((tm, w_), lambda i: (i, 0))
    vec = lambda w_: _const_spec((1, w_))
    if rows_per_seq is None:
        pz_spec = row(RWKV_IN)
        zl_spec, zl_shape = row(RWKV_IN), (t, RWKV_IN)
        scratch = []
    else:
        tiles_per_seq = rows_per_seq // tm
        pz_spec = pl.BlockSpec((1, 1, RWKV_IN), lambda i: (i // tiles_per_seq, 0, 0))
        zl_spec, zl_shape = pz_spec, (t // rows_per_seq, 1, RWKV_IN)
        scratch = [pltpu.VMEM((1, RWKV_IN), F32)]
    return pl.pallas_call(
        functools.partial(_inproj_kernel, rows_per_seq),
        grid=(t // tm,),
        in_specs=[row(D_MODEL), vec(D_MODEL), _const_spec((D_MODEL, IN_W)), pz_spec,
                  vec(RWKV_IN), vec(RWKV_W), _const_spec((D_DECAY_LORA, RWKV_W)),
                  vec(RWKV_W), _const_spec((D_AAA_LORA, RWKV_W)), _const_spec((D_GATE_LORA, RWKV_W)),
                  vec(RWKV_W), vec(RWKV_W), _const_spec((RWKV_W, RWKV_W))],
        out_specs=[row(ATT_W), row(KV_W), row(KV_W), row(2 * D_MODEL), zl_spec] + [row(RWKV_W)] * 7,
        out_shape=[jax.ShapeDtypeStruct((t, ATT_W), BF),
                   jax.ShapeDtypeStruct((t, KV_W), F32),
                   jax.ShapeDtypeStruct((t, KV_W), F32),
                   jax.ShapeDtypeStruct((t, 2 * D_MODEL), BF),
                   jax.ShapeDtypeStruct(zl_shape, F32)] + [jax.ShapeDtypeStruct((t, RWKV_W), F32)] * 7,
        scratch_shapes=scratch,
        compiler_params=_params("arbitrary"),
        name="inproj",
    )(x, g, p["w_in"], prev_z, p["mu"], p["w0"], p["w2"], p["a0"], p["a2"], p["g2"], p["k_k"], p["k_a"],
      p["ones_bd"])


def _t5_bucket(dist):
    n = np.maximum(dist, 0)
    nf = np.maximum(n, 1).astype(np.float32)
    large = MAX_EXACT + (np.log(nf / MAX_EXACT) / math.log(MAX_DISTANCE / MAX_EXACT)
                         * (NUM_BUCKETS - MAX_EXACT)).astype(np.int32)
    return np.where(n < MAX_EXACT, n, np.minimum(large, NUM_BUCKETS - 1)).astype(np.int32)


def _bias_lookup(rel_bias, bucket):
    hit = bucket[..., None, None] == np.arange(NUM_BUCKETS, dtype=np.int32)[:, None]
    return jnp.sum(jnp.where(hit, rel_bias, 0.0), axis=-2)


def _prompt_bucket_table():
    dist = np.arange(WINDOW)[:, None] - (np.arange(2 * WINDOW) - WINDOW)[None, :]
    return np.where((dist >= 0) & (dist <= WINDOW), _t5_bucket(dist), -1).astype(np.int32)


def _swa_prompt_kernel(nb, bucket_ref, relb_ref, q_ref, kp_ref, kc_ref, vp_ref, vc_ref, sink_ref, o_ref, bias_ref):
    first = pl.program_id(0) == 0

    @pl.when(first)
    def _():
        bucket = bucket_ref[...]
        for head in range(N_Q_HEADS):
            tab = jnp.full(bucket.shape, NEG_INF, F32)
            for n in range(NUM_BUCKETS):
                tab = jnp.where(bucket == n, relb_ref[n, head], tab)
            g = head % GQA_GROUP
            bias_ref[head // GQA_GROUP, g * WINDOW:(g + 1) * WINDOW, :] = tab

    col = lax.broadcasted_iota(jnp.int32, (GQA_GROUP * WINDOW, 2 * WINDOW), 1)
    pad_mask = jnp.logical_and(first, col < WINDOW)
    chains = [(b, h) for b in range(nb) for h in range(N_KV_HEADS)]
    hs = lambda h: slice(h * HEAD_DIM, (h + 1) * HEAD_DIM)
    kh = [jnp.concatenate([kp_ref[b][:, hs(h)], kc_ref[b][:, hs(h)]], axis=0).astype(BF) for b, h in chains]
    vh = [jnp.concatenate([vp_ref[b][:, hs(h)], vc_ref[b][:, hs(h)]], axis=0).astype(BF) for b, h in chains]
    qh = [jnp.concatenate([q_ref[b][:, hs(GQA_GROUP * h + g)] for g in range(GQA_GROUP)], axis=0)
          for b, h in chains]
    s = [_dot_nt(qh[n], kh[n]) * ATT_SCALE + bias_ref[h] for n, (b, h) in enumerate(chains)]
    s = [jnp.where(pad_mask, NEG_INF, x) for x in s]
    m = [jnp.maximum(jnp.max(x, axis=-1, keepdims=True), sink_ref[h]) for x, (b, h) in zip(s, chains)]
    p = [jnp.exp(x - mm) for x, mm in zip(s, m)]
    denom = [jnp.sum(x, axis=-1, keepdims=True) + jnp.exp(sink_ref[h] - mm)
             for x, mm, (b, h) in zip(p, m, chains)]
    o = [_dot(x.astype(BF), vv) / d for x, vv, d in zip(p, vh, denom)]
    for b in range(nb):
        pieces = [o[b * N_KV_HEADS + h][g * WINDOW:(g + 1) * WINDOW]
                  for h in range(N_KV_HEADS) for g in range(GQA_GROUP)]
        o_ref[b] = jnp.concatenate(pieces, axis=-1).astype(o_ref.dtype)


def _swa_prompt(q3, k3, v3, rel_bias, sink_col):
    batch, seq, _ = q3.shape
    cur = lambda i: (0, i, 0)
    prev = lambda i: (0, jnp.maximum(i - 1, 0), 0)
    kv_c = pl.BlockSpec((batch, WINDOW, KV_W), cur)
    kv_p = pl.BlockSpec((batch, WINDOW, KV_W), prev)
    bucket = jnp.asarray(_prompt_bucket_table())
    return pl.pallas_call(
        functools.partial(_swa_prompt_kernel, batch),
        grid=(seq // WINDOW,),
        in_specs=[_const_spec(bucket.shape), pl.BlockSpec(memory_space=pltpu.SMEM),
                  pl.BlockSpec((batch, WINDOW, ATT_W), cur), kv_p, kv_c, kv_p, kv_c,
                  _const_spec(sink_col.shape)],
        out_specs=pl.BlockSpec((batch, WINDOW, ATT_W), cur),
        out_shape=jax.ShapeDtypeStruct((batch, seq, ATT_W), BF),
        scratch_shapes=[pltpu.VMEM((N_KV_HEADS, GQA_GROUP * WINDOW, 2 * WINDOW), F32)],
        compiler_params=_params("arbitrary"),
        name="swa_prompt",
    )(bucket, rel_bias, q3, k3, k3, v3, v3, sink_col)


def _swa_sample_kernel(q_ref, kn_ref, vn_ref, ck_ref, cv_ref, bc_ref, bn_ref, sink_ref, o_ref):
    q = q_ref[...]
    qf = q.astype(F32)
    kn = kn_ref[...].astype(BF).astype(F32)
    vn = vn_ref[...].astype(BF).astype(F32)
    ck, cv = ck_ref[...], cv_ref[...]
    head = lax.broadcasted_iota(jnp.int32, (1, N_Q_HEADS, 1), 1)
    low = head < GQA_GROUP
    s_h, sn_h = [], []
    for h in range(N_KV_HEADS):
        sl = slice(h * HEAD_DIM, (h + 1) * HEAD_DIM)
        s_h.append(jnp.einsum("bgd,bkd->bgk", q, ck[:, :, sl].astype(BF), preferred_element_type=F32))
        sn_h.append(jnp.sum(qf * kn[:, None, sl], axis=-1, keepdims=True))
    s = jnp.where(low, s_h[0], s_h[1]) * ATT_SCALE + bc_ref[...]
    sn = jnp.where(low, sn_h[0], sn_h[1]) * ATT_SCALE + bn_ref[...]
    sink = sink_ref[...]
    m = jnp.maximum(jnp.maximum(jnp.max(s, axis=-1, keepdims=True), sn), sink)
    p = jnp.exp(s - m)
    pn = jnp.exp(sn - m)
    denom = jnp.sum(p, axis=-1, keepdims=True) + pn + jnp.exp(sink - m)
    pb = p.astype(BF)
    pnb = pn.astype(BF).astype(F32)
    o_h = []
    for h in range(N_KV_HEADS):
        sl = slice(h * HEAD_DIM, (h + 1) * HEAD_DIM)
        o = jnp.einsum("bgk,bkd->bgd", pb, cv[:, :, sl].astype(BF), preferred_element_type=F32)
        o_h.append(o + pnb * vn[:, None, sl])
    o_ref[...] = (jnp.where(low, o_h[0], o_h[1]) / denom).astype(o_ref.dtype)


def _swa_sample(q3, kn, vn, ck, cv, bias_c, bias_n, sink3, bb):
    b = q3.shape[0]
    return pl.pallas_call(
        _swa_sample_kernel,
        grid=(b // bb,),
        in_specs=[pl.BlockSpec((bb, N_Q_HEADS, HEAD_DIM), lambda i: (i, 0, 0)),
                  pl.BlockSpec((bb, KV_W), lambda i: (i, 0)),
                  pl.BlockSpec((bb, KV_W), lambda i: (i, 0)),
                  pl.BlockSpec((bb, WINDOW, KV_W), lambda i: (i, 0, 0)),
                  pl.BlockSpec((bb, WINDOW, KV_W), lambda i: (i, 0, 0)),
                  _const_spec(bias_c.shape), _const_spec(bias_n.shape), _const_spec(sink3.shape)],
        out_specs=pl.BlockSpec((bb, N_Q_HEADS, HEAD_DIM), lambda i: (i, 0, 0)),
        out_shape=jax.ShapeDtypeStruct((b, N_Q_HEADS, HEAD_DIM), BF),
        compiler_params=_params("parallel"),
        name="swa_sample",
    )(q3, kn, vn, ck, cv, bias_c, bias_n, sink3)


def _rwkv_scan_kernel(nb, chunk, r_ref, lw_ref, k_ref, v_ref, a_ref, b_ref, g_ref, s0_ref,
                      rk_ref, lg_ref, lb_ref, ones_ref, y_ref, sout_ref, s_scr):
    c = pl.program_id(1)

    @pl.when(c == 0)
    def _():
        s_scr[...] = s0_ref[...]

    ri = lax.broadcasted_iota(jnp.int32, (chunk, chunk), 0)
    ci = lax.broadcasted_iota(jnp.int32, (chunk, chunk), 1)
    incl = ri >= ci
    strict = ri > ci
    tri = incl.astype(BF)
    eye = (ri == ci).astype(F32)
    n_sq = int(math.log2(chunk)) - 1

    chains = [(bi, h) for bi in range(nb) for h in range(RWKV_HEADS)]
    hs = lambda h: slice(h * RWKV_HEAD, (h + 1) * RWKV_HEAD)

    ra, aa, bt, kt, bh, kh, vb, g_all = [], [], [], [], [], [], [], []
    for bi in range(nb):
        r, lw, k, v = r_ref[bi], lw_ref[bi], k_ref[bi], v_ref[bi]
        a, b = a_ref[bi], b_ref[bi]
        hi, lo = _split2(lw)
        lo2 = (lw - hi.astype(F32) - lo.astype(F32)).astype(BF)
        cum = _dot(tri, hi) + _dot(tri, lo) + _dot(tri, lo2)
        tail = cum[chunk - 1:chunk, :]
        g_inv = jnp.exp(-cum)
        g_tail = jnp.exp(tail - cum)
        g_all.append(jnp.exp(tail))
        ra.append((r * jnp.exp(cum)).astype(BF))
        aa.append((a * jnp.exp(cum - lw)).astype(BF))
        bt.append((b * g_inv).astype(BF))
        kt.append((k * g_inv).astype(BF))
        bh.append((b * g_tail).astype(BF))
        kh.append((k * g_tail).astype(BF))
        vb.append(v.astype(BF))

    s_old = [s_scr[bi, h] for bi, h in chains]
    ar = [jnp.concatenate([aa[bi][:, hs(h)], ra[bi][:, hs(h)]], axis=0) for bi, h in chains]
    v_h = [vb[bi][:, hs(h)] for bi, h in chains]
    gb = [_dot_nt(ar[n], bt[bi][:, hs(h)]) for n, (bi, h) in enumerate(chains)]
    gk = [_dot_nt(ar[n], kt[bi][:, hs(h)]) for n, (bi, h) in enumerate(chains)]
    p = [_dot_nt(ar[n], s_old[n].astype(BF)) for n in range(len(chains))]
    l_ab = [jnp.where(strict, x[:chunk], 0.0) for x in gb]
    l_ak = [jnp.where(strict, x[:chunk], 0.0).astype(BF) for x in gk]
    m_rb = [jnp.where(incl, x[chunk:], 0.0).astype(BF) for x in gb]
    m_rk = [jnp.where(incl, x[chunk:], 0.0).astype(BF) for x in gk]
    rhs = [p[n][:chunk] + _dot(l_ak[n], v_h[n]) for n in range(len(chains))]
    t_inv = [eye + x for x in l_ab]
    lp = l_ab
    for _ in range(n_sq):
        lpb = [x.astype(BF) for x in lp]
        lp = [_dot(x, x) for x in lpb]
        t_inv = [t + _dot(x.astype(BF), t.astype(BF)) for x, t in zip(lp, t_inv)]
    ub = [_dot(t.astype(BF), x.astype(BF)).astype(BF) for t, x in zip(t_inv, rhs)]
    y_h = [p[n][chunk:] + _dot(m_rb[n], ub[n]) + _dot(m_rk[n], v_h[n]) for n in range(len(chains))]
    s_new = [s_old[n] * g_all[bi][:, hs(h)] + _dot_tn(ub[n], bh[bi][:, hs(h)]) + _dot_tn(v_h[n], kh[bi][:, hs(h)])
             for n, (bi, h) in enumerate(chains)]
    for n, (bi, h) in enumerate(chains):
        s_scr[bi, h] = s_new[n]

    for bi in range(nb):
        y = jnp.concatenate(y_h[bi * RWKV_HEADS:(bi + 1) * RWKV_HEADS], axis=-1)
        y_ref[bi] = _rwkv_epilogue(y, r_ref[bi], k_ref[bi], v_ref[bi], g_ref[bi],
                                   rk_ref, lg_ref, lb_ref, ones_ref).astype(y_ref.dtype)

    @pl.when(c == pl.num_programs(1) - 1)
    def _():
        sout_ref[...] = s_scr[...]


def _rwkv_epilogue(y, r, k, v, g, rk_ref, lg_ref, lb_ref, ones_ref):
    ones = ones_ref[...]
    inv_n = 1.0 / RWKV_HEAD
    mean = _dot_exact_rhs(y, ones) * inv_n
    d = y - mean
    var = _dot_exact_rhs(d * d, ones) * inv_n
    yn = d * lax.rsqrt(var + GN_EPS) * lg_ref[...] + lb_ref[...]
    bonus = _dot_exact_rhs(r * k * rk_ref[...], ones) * v
    return (yn + bonus) * g


def _rwkv_scan(coef, s0, p, nb, chunk):
    r = coef[0]
    b, t, _ = r.shape
    seq_spec = pl.BlockSpec((nb, chunk, RWKV_W), lambda bi, c: (bi, c, 0))
    st_spec = pl.BlockSpec((nb, RWKV_HEADS, RWKV_HEAD, RWKV_HEAD), lambda bi, c: (bi, 0, 0, 0))
    vec = _const_spec((1, RWKV_W))
    return pl.pallas_call(
        functools.partial(_rwkv_scan_kernel, nb, chunk),
        grid=(b // nb, t // chunk),
        in_specs=[seq_spec] * 7 + [st_spec, vec, vec, vec, _const_spec((RWKV_W, RWKV_W))],
        out_specs=[seq_spec, st_spec],
        out_shape=[jax.ShapeDtypeStruct((b, t, RWKV_W), BF),
                   jax.ShapeDtypeStruct((b, RWKV_HEADS, RWKV_HEAD, RWKV_HEAD), F32)],
        scratch_shapes=[pltpu.VMEM((nb, RWKV_HEADS, RWKV_HEAD, RWKV_HEAD), F32)],
        compiler_params=_params("parallel", "arbitrary"),
        name="rwkv_scan",
    )(*coef, s0, p["r_k"], p["lnx_g"], p["lnx_b"], p["ones_bd"])


def _rwkv_step_kernel(r_ref, lw_ref, k_ref, v_ref, a_ref, b_ref, g_ref, s_ref,
                      rk_ref, lg_ref, lb_ref, ones_ref, y_ref, sout_ref, y_scr):
    n_pair = LANES // RWKV_HEAD
    r, k, v = r_ref[...], k_ref[...], v_ref[...]
    rT, kT, vT = r.T, k.T, v.T
    aT, bT, wT = a_ref[...].T, b_ref[...].T, jnp.exp(lw_ref[...]).T
    for hl in range(LANES // RWKV_HEAD):
        hsl = slice(hl * RWKV_HEAD, (hl + 1) * RWKV_HEAD)
        a_h, b_h, k_h, w_h, r_h = aT[hsl], bT[hsl], kT[hsl], wT[hsl], rT[hsl]
        tiles = [(hl * RWKV_HEAD + n_pair * t) * RWKV_HEAD for t in range(RWKV_HEAD // n_pair)]
        st = [s_ref[:, c0:c0 + LANES].T for c0 in tiles]
        new = []
        for t, x in enumerate(st):
            halves = []
            for il in range(n_pair):
                i = hl * RWKV_HEAD + n_pair * t + il
                slab = x[il * RWKV_HEAD:(il + 1) * RWKV_HEAD]
                sa = jnp.sum(slab * a_h, axis=0, keepdims=True)
                slab = slab * w_h + sa * b_h + vT[i:i + 1] * k_h
                y_scr[i:i + 1, :] = jnp.sum(slab * r_h, axis=0, keepdims=True)
                halves.append(slab)
            new.append(jnp.concatenate(halves, axis=0))
        for c0, x in zip(tiles, new):
            sout_ref[:, c0:c0 + LANES] = x.T
    y_ref[...] = _rwkv_epilogue(y_scr[...].T, r, k, v, g_ref[...],
                                rk_ref, lg_ref, lb_ref, ones_ref).astype(y_ref.dtype)


def _rwkv_step(coef, s0, p):
    b = s0.shape[0]
    per_pair = (LANES // RWKV_HEAD) * RWKV_HEAD * RWKV_HEAD
    n_steps = RWKV_W // LANES
    col = pl.BlockSpec((b, LANES), lambda i: (0, i))
    vec = pl.BlockSpec((1, LANES), lambda i: (0, i))
    st_spec = pl.BlockSpec((b, per_pair), lambda i: (0, i))
    y, s_new = pl.pallas_call(
        _rwkv_step_kernel,
        grid=(n_steps,),
        in_specs=[col] * 7 + [st_spec, vec, vec, vec, _const_spec((LANES, LANES))],
        out_specs=[col, st_spec],
        out_shape=[jax.ShapeDtypeStruct((b, RWKV_W), BF),
                   jax.ShapeDtypeStruct((b, n_steps * per_pair), F32)],
        scratch_shapes=[pltpu.VMEM((LANES, b), F32)],
        compiler_params=_params("parallel"),
        name="rwkv_step",
    )(*coef, s0.reshape(b, n_steps * per_pair), p["r_k"], p["lnx_g"], p["lnx_b"],
      p["ones_bd"][:LANES, :LANES])
    return y, s_new.reshape(s0.shape)


def _merge_kernel(with_router, x_ref, ya_ref, yb_ref, gate_ref, wa_ref, wb_ref, wo_ref, g_ref, *rest):
    if with_router:
        wr_ref, cnt_in_ref, x1_ref, h_ref, route_ref, cnt_out_ref, cnt_scr = rest
    else:
        x1_ref, h_ref = rest
    gate = gate_ref[...].astype(F32)
    pa = _dot(ya_ref[...], wa_ref[...])
    pb = _dot(yb_ref[...], wb_ref[...])
    merged = _sigmoid(gate[:, :D_MODEL]) * pa + _sigmoid(gate[:, D_MODEL:]) * pb
    x1 = x_ref[...] + _dot(merged.astype(BF), wo_ref[...])
    x1_ref[...] = x1
    h = _rms(x1, g_ref[...])
    h_ref[...] = h.astype(h_ref.dtype)
    if not with_router:
        return

    @pl.when(pl.program_id(0) == 0)
    def _():
        cnt_scr[...] = cnt_in_ref[...]

    logits = _dot(h.astype(BF), wr_ref[...])
    tm = logits.shape[0]
    lane = lax.broadcasted_iota(jnp.int32, logits.shape, 1).astype(F32)
    logits = jnp.where(lane < N_EXPERTS, logits, -jnp.inf)
    m1 = jnp.max(logits, axis=-1, keepdims=True)
    i1 = jnp.min(jnp.where(logits == m1, lane, float(LANES)), axis=-1, keepdims=True)
    rest_l = jnp.where(lane == i1, -jnp.inf, logits)
    m2 = jnp.max(rest_l, axis=-1, keepdims=True)
    i2 = jnp.min(jnp.where(rest_l == m2, lane, float(LANES)), axis=-1, keepdims=True)
    e2 = jnp.exp(m2 - m1)
    g1 = 1.0 / (1.0 + e2)
    g2 = e2 / (1.0 + e2)
    oh1 = (lane == i1).astype(F32)
    oh2 = (lane == i2).astype(F32)
    both = oh1 + oh2
    sub = min(tm, ROUTER_RANK_ROWS)
    ri = lax.broadcasted_iota(jnp.int32, (sub, sub), 0)
    ci = lax.broadcasted_iota(jnp.int32, (sub, sub), 1)
    earlier = (ri > ci).astype(BF)
    running = cnt_scr[...]
    before = []
    for c in range(tm // sub):
        part = both[c * sub:(c + 1) * sub]
        before.append(_dot(earlier, part.astype(BF)) + running)
        running = running + jnp.sum(part, axis=0, keepdims=True)
    before = jnp.concatenate(before, axis=0)
    rank1 = jnp.sum(oh1 * before, axis=-1, keepdims=True)
    rank2 = jnp.sum(oh2 * before, axis=-1, keepdims=True)
    cnt_scr[...] = running
    cnt_out_ref[...] = running
    route = jnp.zeros_like(logits)
    for n, val in enumerate((i1, i2, g1, g2, rank1, rank2)):
        route = jnp.where(lane == n, val, route)
    route_ref[...] = route


def _merge_out(x, ya, yb, gate, wa, wb, wo, g, tm, router=None, cnt_in=None):
    t = x.shape[0]
    row = lambda w_: pl.BlockSpec((tm, w_), lambda i: (i, 0))
    in_specs = [row(D_MODEL), row(ATT_W), row(RWKV_W), row(2 * D_MODEL),
                _const_spec((ATT_W, D_MODEL)), _const_spec((RWKV_W, D_MODEL)),
                _const_spec((D_MODEL, D_MODEL)), _const_spec((1, D_MODEL))]
    args = [x, ya, yb, gate, wa, wb, wo, g]
    if router is None:
        out_specs = [row(D_MODEL), row(D_MODEL)]
        out_shape = [jax.ShapeDtypeStruct((t, D_MODEL), F32), jax.ShapeDtypeStruct((t, D_MODEL), BF)]
        scratch = []
    else:
        in_specs += [_const_spec(router.shape), _const_spec((1, LANES))]
        args += [router, cnt_in]
        out_specs = [row(D_MODEL), row(D_MODEL), row(LANES), pl.BlockSpec((1, LANES), lambda i: (0, 0))]
        out_shape = [jax.ShapeDtypeStruct((t, D_MODEL), F32), jax.ShapeDtypeStruct((t, D_MODEL), F32),
                     jax.ShapeDtypeStruct((t, LANES), F32), jax.ShapeDtypeStruct((1, LANES), F32)]
        scratch = [pltpu.VMEM((1, LANES), F32)]
    return pl.pallas_call(
        functools.partial(_merge_kernel, router is not None),
        grid=(t // tm,),
        in_specs=in_specs, out_specs=out_specs, out_shape=out_shape, scratch_shapes=scratch,
        compiler_params=_params("parallel" if router is None else "arbitrary"),
        name="merge_out",
    )(*args)


def _swiglu(x, wg, wu, wd):
    s = _dot(x, wg)
    u = _dot(x, wu)
    act = (s * _sigmoid(s) * u).astype(BF)
    return _dot(act, wd)


def _dense_ffn_kernel(x1_ref, h_ref, wg_ref, wu_ref, wd_ref, o_ref):
    o_ref[...] = x1_ref[...] + _swiglu(h_ref[...], wg_ref[...], wu_ref[...], wd_ref[...])


def _dense_ffn(x1, h, wg, wu, wd, tm):
    t = x1.shape[0]
    d_ff = wg.shape[1]
    row = lambda: pl.BlockSpec((tm, D_MODEL), lambda i: (i, 0))
    return pl.pallas_call(
        _dense_ffn_kernel,
        grid=(t // tm,),
        in_specs=[row(), row(), _const_spec((D_MODEL, d_ff)), _const_spec((D_MODEL, d_ff)),
                  _const_spec((d_ff, D_MODEL))],
        out_specs=row(),
        out_shape=jax.ShapeDtypeStruct((t, D_MODEL), F32),
        compiler_params=_params("parallel"),
        name="dense_ffn",
    )(x1, h, wg, wu, wd)


def _dispatch_kernel(tt, first, dest_ref, fill_ref, h_ref, *rest):
    if first:
        xs_hbm, zbuf, sem, zsem = rest
    else:
        _, xs_hbm, sem = rest
    i = pl.program_id(0)
    base = i * tt

    if first:
        @pl.when(i == 0)
        def _():
            zbuf[...] = jnp.zeros_like(zbuf)
            bm = zbuf.shape[0]
            fills = [pltpu.make_async_copy(zbuf, xs_hbm.at[pl.ds(fill_ref[n] * bm, bm), :], zsem)
                     for n in range(fill_ref.shape[0])]
            for f in fills:
                f.start()
            for f in fills:
                f.wait()

    def issue(r, carry):
        for j in range(TOP_K):
            row = dest_ref[TOP_K * (base + r) + j]
            pltpu.make_async_copy(h_ref.at[pl.ds(r, 1), :], xs_hbm.at[pl.ds(row, 1), :], sem).start()
        return carry

    lax.fori_loop(0, tt, issue, 0, unroll=8)
    for j in range(TOP_K):
        pltpu.make_async_copy(h_ref, xs_hbm.at[pl.ds(0, tt), :], sem).wait()


def _dispatch(dest, fill_blocks, h, xs, tt, bm, n_rows):
    t = h.shape[0]
    first = xs is None
    in_specs = [pl.BlockSpec((tt, D_MODEL), lambda i, ds, fb: (i, 0))]
    args = [dest, fill_blocks, h]
    if first:
        scratch = [pltpu.VMEM((bm, D_MODEL), F32), pltpu.SemaphoreType.DMA(()), pltpu.SemaphoreType.DMA(())]
        aliases = {}
    else:
        in_specs.append(pl.BlockSpec(memory_space=pl.ANY))
        args.append(xs)
        scratch = [pltpu.SemaphoreType.DMA(())]
        aliases = {3: 0}
    grid_spec = pltpu.PrefetchScalarGridSpec(
        num_scalar_prefetch=2, grid=(t // tt,), in_specs=in_specs,
        out_specs=pl.BlockSpec(memory_space=pl.ANY), scratch_shapes=scratch)
    return pl.pallas_call(
        functools.partial(_dispatch_kernel, tt, first),
        grid_spec=grid_spec,
        out_shape=jax.ShapeDtypeStruct((n_rows, D_MODEL), F32),
        input_output_aliases=aliases,
        compiler_params=_params("arbitrary"),
        name="moe_dispatch",
    )(*args)


def _moe_kernel(n_half, blk_e_ref, n_used_ref, xs_ref, wg_ref, wu_ref, wd_ref, y_ref):
    @pl.when(pl.program_id(0) < n_used_ref[0])
    def _():
        x = xs_ref[...].astype(BF)
        d_e = wg_ref.shape[2]
        step = d_e // n_half
        acc = None
        for j in range(n_half):
            cs = slice(j * step, (j + 1) * step)
            part = _swiglu(x, wg_ref[0, :, cs], wu_ref[0, :, cs], wd_ref[0, cs, :])
            acc = part if acc is None else acc + part
        y_ref[...] = acc

    @pl.when(pl.program_id(0) >= n_used_ref[0])
    def _():
        y_ref[...] = jnp.zeros_like(y_ref)


def _moe_ffn(xs, blk_e, n_used, wg, wu, wd, bm):
    n_blk = blk_e.shape[0]
    n_rows = xs.shape[0]
    d_e = wg.shape[2]
    wspec = lambda shape: pl.BlockSpec(shape, lambda i, be, nu: (be[i], 0, 0), pipeline_mode=pl.Buffered(1))
    rows = pl.BlockSpec((bm, D_MODEL), lambda i, be, nu: (i, 0))
    grid_spec = pltpu.PrefetchScalarGridSpec(
        num_scalar_prefetch=2,
        grid=(n_blk,),
        in_specs=[rows, wspec((1, D_MODEL, d_e)), wspec((1, D_MODEL, d_e)), wspec((1, d_e, D_MODEL))],
        out_specs=rows,
    )
    return pl.pallas_call(
        functools.partial(_moe_kernel, 2),
        grid_spec=grid_spec,
        out_shape=jax.ShapeDtypeStruct((n_rows, D_MODEL), F32),
        compiler_params=_params("arbitrary"),
        name="moe_ffn",
    )(blk_e, n_used, xs, wg, wu, wd)


def _row_layout(counts, bm, n_blk):
    per_e = (counts + bm - 1) // bm
    ends = jnp.cumsum(per_e)
    n_used = ends[-1]
    first_row = (ends - per_e) * bm
    i = jnp.arange(n_blk, dtype=jnp.int32)
    blk_e = jnp.sum((jnp.minimum(i, jnp.maximum(n_used - 1, 0))[:, None] >= ends[None, :]).astype(jnp.int32), axis=1)
    need = jnp.any(i[:, None] == (ends - 1)[None, :], axis=1) | (i >= n_blk - N_EXPERTS)
    fill_blocks = jnp.argsort(jnp.logical_not(need), stable=True)[:2 * N_EXPERTS].astype(jnp.int32)
    return first_row.astype(jnp.int32), blk_e.astype(jnp.int32), n_used.astype(jnp.int32).reshape(1), fill_blocks


def _combine_kernel(tc, dest_ref, x1_ref, route_ref, y_hbm, g_ref, o_ref, ybuf, sem):
    base = pl.program_id(0) * tc

    def issue(r, carry):
        for j in range(TOP_K):
            row = dest_ref[TOP_K * (base + r) + j]
            pltpu.make_async_copy(y_hbm.at[pl.ds(row, 1), :], ybuf.at[j, pl.ds(r, 1), :], sem).start()
        return carry

    lax.fori_loop(0, tc, issue, 0, unroll=4)
    for j in range(TOP_K):
        pltpu.make_async_copy(y_hbm.at[pl.ds(0, tc), :], ybuf.at[j], sem).wait()
    route = route_ref[...]
    x2 = x1_ref[...] + route[:, 2:3] * ybuf[0] + route[:, 3:4] * ybuf[1]
    o_ref[...] = _rms(x2, g_ref[...])


def _moe_combine(dest, x1, route, y, g, tc):
    t = x1.shape[0]
    grid_spec = pltpu.PrefetchScalarGridSpec(
        num_scalar_prefetch=1,
        grid=(t // tc,),
        in_specs=[pl.BlockSpec((tc, D_MODEL), lambda i, ds: (i, 0)),
                  pl.BlockSpec((tc, LANES), lambda i, ds: (i, 0)),
                  pl.BlockSpec(memory_space=pl.ANY),
                  pl.BlockSpec((1, D_MODEL), lambda i, ds: (0, 0))],
        out_specs=pl.BlockSpec((tc, D_MODEL), lambda i, ds: (i, 0)),
        scratch_shapes=[pltpu.VMEM((TOP_K, tc, D_MODEL), F32), pltpu.SemaphoreType.DMA(())],
    )
    return pl.pallas_call(
        functools.partial(_combine_kernel, tc),
        grid_spec=grid_spec,
        out_shape=jax.ShapeDtypeStruct((t, D_MODEL), F32),
        compiler_params=_params("arbitrary"),
        name="moe_combine",
    )(dest, x1, route, y, g)


def _layer_params(l, w_in, mu_shift, w0, w2, a0, a2, g2, k_k, k_a, r_k, lnx_g, lnx_b,
                  w_proj_attn, w_proj_rwkv, w_out, ones_bd):
    row = lambda u: u.reshape(1, -1)
    return dict(
        w_in=w_in[l].astype(BF), mu=row(mu_shift[l]), w0=row(w0[l]), w2=w2[l].astype(BF),
        a0=row(a0[l]), a2=a2[l].astype(BF), g2=g2[l].astype(BF), k_k=row(k_k[l]), k_a=row(k_a[l]),
        r_k=row(r_k[l]), lnx_g=row(lnx_g[l]), lnx_b=row(lnx_b[l]),
        wa=w_proj_attn[l].astype(BF), wb=w_proj_rwkv[l].astype(BF), wo=w_out[l].astype(BF),
        ones_bd=ones_bd)


def kernel(x_prompt, x_sample, cache_k, cache_v, state_wkv, state_shift, norm_mix_g, w_in, attn_sinks, rel_bias, mu_shift, w0, w2, a0, a2, g2, k_k, k_a, r_k, lnx_g, lnx_b, w_proj_attn, w_proj_rwkv, w_out, norm_ffn_g, dense_w_gate, dense_w_up, dense_w_down, router_w, moe_w_gate, moe_w_up, moe_w_down, norm_final_g):
    batch, seq, _ = x_prompt.shape
    dec_batch, dec_seq, _ = x_sample.shape
    assert dec_seq == 1 and seq % WINDOW == 0

    head_id = jnp.arange(RWKV_W, dtype=jnp.int32) // RWKV_HEAD
    ones_bd = (head_id[:, None] == head_id[None, :]).astype(BF)
    bias_by_dist = _bias_lookup(rel_bias, _t5_bucket(np.arange(WINDOW + 1)))
    bias_c = jnp.transpose(bias_by_dist[WINDOW:0:-1], (1, 0))[None]
    bias_n = bias_by_dist[0][None, :, None]

    layers = [_layer_params(l, w_in, mu_shift, w0, w2, a0, a2, g2, k_k, k_a, r_k, lnx_g, lnx_b,
                            w_proj_attn, w_proj_rwkv, w_out, ones_bd) for l in range(DEPTH)]

    def router_pieces(l):
        return jnp.pad(router_w[l // 2], ((0, 0), (0, LANES - N_EXPERTS))).astype(BF)

    n_prompt = batch * seq
    bm = MOE_BLOCK_ROWS
    n_all = n_prompt + dec_batch
    n_blk = -(-(n_all * TOP_K) // bm) + N_EXPERTS

    def mix(x, is_prompt, l, state):
        p = layers[l]
        n_tok = x.shape[0]
        gm = norm_mix_g[l].reshape(1, -1)
        sinks = attn_sinks[l]
        if is_prompt:
            b, t = batch, seq
            prev_z = jnp.zeros((b, 1, RWKV_IN), F32)
            q, k, v, gate, z_last, *coef = _inproj(x, gm, p, prev_z, min(INPROJ_ROWS, t), t)
            sink_col = jnp.repeat(sinks, WINDOW).reshape(N_KV_HEADS, GQA_GROUP * WINDOW, 1)
            ya = _swa_prompt(q.reshape(b, t, ATT_W), k.reshape(b, t, KV_W), v.reshape(b, t, KV_W),
                             rel_bias, sink_col).reshape(n_tok, ATT_W)
            k4 = k.reshape(b, t, N_KV_HEADS, HEAD_DIM)
            v4 = v.reshape(b, t, N_KV_HEADS, HEAD_DIM)
            state["k"].append(k4[:, t - WINDOW:])
            state["v"].append(v4[:, t - WINDOW:])
            s0 = jnp.zeros((b, RWKV_HEADS, RWKV_HEAD, RWKV_HEAD), F32)
            yb3, s_new = _rwkv_scan([u.reshape(b, t, RWKV_W) for u in coef], s0, p, b, RWKV_CHUNK)
            yb = yb3.reshape(n_tok, RWKV_W)
            state["z"].append(z_last.reshape(b, RWKV_IN))
        else:
            b = dec_batch
            q, k, v, gate, z, *coef = _inproj(x, gm, p, state_shift[l], n_tok, None)
            ck = cache_k[l].reshape(b, WINDOW, KV_W)
            cv = cache_v[l].reshape(b, WINDOW, KV_W)
            ya3 = _swa_sample(q.reshape(b, N_Q_HEADS, HEAD_DIM), k, v, ck, cv, bias_c, bias_n,
                              sinks.reshape(1, N_Q_HEADS, 1), min(SAMPLE_ATTN_SEQS, b))
            ya = ya3.reshape(b, ATT_W)
            state["k"].append(jnp.concatenate([ck[:, 1:], k[:, None]], axis=1)
                              .reshape(b, WINDOW, N_KV_HEADS, HEAD_DIM))
            state["v"].append(jnp.concatenate([cv[:, 1:], v[:, None]], axis=1)
                              .reshape(b, WINDOW, N_KV_HEADS, HEAD_DIM))
            yb, s_new = _rwkv_step(coef, state_wkv[l], p)
            state["z"].append(z)
        state["s"].append(s_new)
        return ya, yb, gate

    groups = [dict(x=x_prompt.reshape(n_prompt, D_MODEL), prompt=True, tm=min(DENSE_ROWS, n_prompt), k=[], v=[], s=[], z=[]),
              dict(x=x_sample.reshape(dec_batch, D_MODEL), prompt=False, tm=min(DENSE_ROWS, dec_batch), k=[], v=[], s=[], z=[])]
    for l in range(DEPTH):
        p = layers[l]
        gn = norm_ffn_g[l].reshape(1, -1)
        j = l // 2
        if l % 2 == 0:
            wg, wu, wd = (_to_bf16(w[j], w.shape[1] // 4) for w in (dense_w_gate, dense_w_up, dense_w_down))
            for grp in groups:
                ya, yb, gate = mix(grp["x"], grp["prompt"], l, grp)
                x1, h = _merge_out(grp["x"], ya, yb, gate, p["wa"], p["wb"], p["wo"], gn, grp["tm"])
                grp["x"] = _dense_ffn(x1, h, wg, wu, wd, grp["tm"])
        else:
            assert l == DEPTH - 1
            wg, wu, wd = (w[j].astype(BF) for w in (moe_w_gate, moe_w_up, moe_w_down))
            router = router_pieces(l)
            cnt = jnp.zeros((1, LANES), F32)
            xs = None
            for grp in groups:
                ya, yb, gate = mix(grp["x"], grp["prompt"], l, grp)
                x1, h, route, cnt = _merge_out(grp["x"], ya, yb, gate, p["wa"], p["wb"], p["wo"], gn,
                                               min(ROUTER_ROWS, grp["tm"]), router=router, cnt_in=cnt)
                grp["x1"], grp["route"], grp["h"] = x1, route, h
            counts = cnt[0, :N_EXPERTS].astype(jnp.int32)
            first_row, blk_e, n_used, fill_blocks = _row_layout(counts, bm, n_blk)
            for grp in groups:
                route = grp["route"]
                expert = route[:, 0:TOP_K].astype(jnp.int32)
                rank = route[:, 2 * TOP_K:3 * TOP_K].astype(jnp.int32)
                start = jnp.sum(jnp.where(expert[..., None] == jnp.arange(N_EXPERTS, dtype=jnp.int32),
                                          first_row, 0), axis=-1)
                grp["dest"] = (start + rank).reshape(-1)
                xs = _dispatch(grp["dest"], fill_blocks, grp["h"], xs, min(DISPATCH_ROWS, grp["h"].shape[0]), bm,
                               n_blk * bm)
            y = _moe_ffn(xs, blk_e, n_used, wg, wu, wd, bm)
            for grp in groups:
                grp["x"] = _moe_combine(grp["dest"], grp["x1"], grp["route"], y,
                                        norm_final_g.reshape(1, -1), min(COMBINE_ROWS, grp["x1"].shape[0]))

    gp, gs = groups
    outs = []
    for grp, shape in ((gp, (batch, seq, D_MODEL)), (gs, (dec_batch, dec_seq, D_MODEL))):
        outs.append((grp["x"].reshape(shape), jnp.stack(grp["k"]), jnp.stack(grp["v"]),
                     jnp.stack(grp["s"]), jnp.stack(grp["z"])))
    (y_p, nk_p, nv_p, ns_p, nz_p), (y_s, nk_s, nv_s, ns_s, nz_s) = outs
    return (y_p, y_s, nk_p, nv_p, ns_p, nz_p, nk_s, nv_s, ns_s, nz_s)
```

```python
import functools
import math

import jax
import jax.numpy as jnp
import numpy as np
from jax import lax
from jax.experimental import pallas as pl
from jax.experimental.pallas import tpu as pltpu

BF = jnp.bfloat16
F32 = jnp.float32

D_MODEL = 1024
DEPTH = 2
HEAD_DIM = 64
N_Q_HEADS = 8
N_KV_HEADS = 2
GQA_GROUP = 4
ATT_W = 512
KV_W = 128
WINDOW = 128
ATT_SCALE = HEAD_DIM ** -0.5
NUM_BUCKETS = 32
MAX_EXACT = 16
MAX_DISTANCE = 128
NEG_INF = -1e30
RWKV_HEAD = 64
RWKV_W = 512
RWKV_HEADS = 8
D_DECAY_LORA = 64
D_AAA_LORA = 64
D_GATE_LORA = 128
RWKV_IN = 3 * RWKV_W + D_DECAY_LORA + D_AAA_LORA + D_GATE_LORA
GN_EPS = 64e-5
IN_W = ATT_W + 2 * KV_W + RWKV_IN + 2 * D_MODEL
N_EXPERTS = 8
TOP_K = 2
NORM_EPS = 1e-6

VMEM_LIMIT_BYTES = 56 * 1024 * 1024
LANES = 128

INPROJ_ROWS = 512
DENSE_ROWS = 512
ROUTER_ROWS = 512
ROUTER_RANK_ROWS = 256
MOE_BLOCK_ROWS = 512
DISPATCH_ROWS = 4096
COMBINE_ROWS = 2048
RWKV_CHUNK = 64
SAMPLE_ATTN_SEQS = 16


def _params(*sem):
    return pltpu.CompilerParams(dimension_semantics=sem, vmem_limit_bytes=VMEM_LIMIT_BYTES)


def _const_spec(shape):
    zeros = (0,) * len(shape)
    return pl.BlockSpec(shape, lambda *_: zeros, pipeline_mode=pl.Buffered(1))


def _dot(a, b):
    return jnp.dot(a, b, preferred_element_type=F32)


def _dot_nt(a, b):
    return lax.dot_general(a, b, (((1,), (1,)), ((), ())), preferred_element_type=F32)


def _dot_tn(a, b):
    return lax.dot_general(a, b, (((0,), (0,)), ((), ())), preferred_element_type=F32)


def _split2(x):
    hi = x.astype(BF)
    lo = (x - hi.astype(F32)).astype(BF)
    return hi, lo


def _dot_exact_rhs(x, w):
    hi, lo = _split2(x)
    return _dot(hi, w) + _dot(lo, w)


def _rms(x, g):
    ms = jnp.mean(x * x, axis=-1, keepdims=True)
    return x * lax.rsqrt(ms + NORM_EPS) * g


def _sigmoid(x):
    return 1.0 / (1.0 + jnp.exp(-x))


def _cast_kernel(x_ref, o_ref):
    o_ref[...] = x_ref[...].astype(o_ref.dtype)


def _to_bf16(w, row_tile):
    rows, cols = w.shape
    spec = pl.BlockSpec((row_tile, cols), lambda i: (i, 0))
    return pl.pallas_call(
        _cast_kernel, grid=(rows // row_tile,), in_specs=[spec], out_specs=spec,
        out_shape=jax.ShapeDtypeStruct((rows, cols), BF),
        compiler_params=_params("parallel"), name="to_bf16",
    )(w)


def _inproj_kernel(rows_per_seq, x_ref, g_ref, w_ref, pz_ref, mu_ref, w0_ref, w2_ref, a0_ref, a2_ref, g2_ref,
                   kk_ref, ka_ref, ones_ref,
                   q_ref, k_ref, v_ref, gate_ref, zl_ref,
                   r_ref, lw_ref, k2_ref, vr_ref, a_ref, b_ref, gr_ref, *carry):
    n = _rms(x_ref[...], g_ref[...]).astype(BF)
    z0 = ATT_W + 2 * KV_W
    z = _dot(n, w_ref[:, z0:z0 + RWKV_IN])
    q_ref[...] = _dot(n, w_ref[:, 0:ATT_W]).astype(BF)
    k_ref[...] = _dot(n, w_ref[:, ATT_W:ATT_W + KV_W])
    v_ref[...] = _dot(n, w_ref[:, ATT_W + KV_W:ATT_W + 2 * KV_W])
    gate_ref[...] = _dot(n, w_ref[:, z0 + RWKV_IN:IN_W]).astype(gate_ref.dtype)
    tm = z.shape[0]
    if rows_per_seq is None:
        zs = pz_ref[...]
        zl_ref[...] = z
    else:
        prev_ref, = carry
        tiles_per_seq = rows_per_seq // tm

        @pl.when(pl.program_id(0) % tiles_per_seq == 0)
        def _():
            prev_ref[...] = pz_ref[0]

        row = lax.broadcasted_iota(jnp.int32, (tm, 1), 0)
        zs = jnp.where(row == 0, prev_ref[...], pltpu.roll(z, 1, 0))
        prev_ref[...] = z[tm - 1:tm, :]
        zl_ref[0] = z[tm - 1:tm, :]
    zz = z + (zs - z) * mu_ref[...]
    r = zz[:, 0:RWKV_W]
    k = zz[:, RWKV_W:2 * RWKV_W]
    v = zz[:, 2 * RWKV_W:3 * RWKV_W]
    o = 3 * RWKV_W
    wl = zz[:, o:o + D_DECAY_LORA]
    al = zz[:, o + D_DECAY_LORA:o + D_DECAY_LORA + D_AAA_LORA]
    gl = zz[:, o + D_DECAY_LORA + D_AAA_LORA:RWKV_IN]
    wpre = -(w0_ref[...] + _dot(jnp.tanh(wl).astype(BF), w2_ref[...]))
    softplus = jnp.maximum(wpre, 0.0) + jnp.log1p(jnp.exp(-jnp.abs(wpre)))
    lw = -jnp.exp(-softplus - 0.5)
    a = _sigmoid(a0_ref[...] + _dot(al.astype(BF), a2_ref[...]))
    g = _dot(_sigmoid(gl).astype(BF), g2_ref[...])
    kkv = k * kk_ref[...]
    ss = _dot_exact_rhs(kkv * kkv, ones_ref[...])
    kk = kkv / jnp.maximum(jnp.sqrt(ss), 1e-12)
    k2 = k * (1.0 + (a - 1.0) * ka_ref[...])
    r_ref[...], lw_ref[...], k2_ref[...], vr_ref[...] = r, lw, k2, v
    a_ref[...], b_ref[...], gr_ref[...] = -kk, kk * a, g


def _inproj(x, g, p, prev_z, tm, rows_per_seq):
    t = x.shape[0]
    row = lambda w_: pl.BlockSpec((tm, w_), lambda i: (i, 0))
    vec = lambda w_: _const_spec((1, w_))
    if rows_per_seq is None:
        pz_spec = row(RWKV_IN)
        zl_spec, zl_shape = row(RWKV_IN), (t, RWKV_IN)
        scratch = []
    else:
        tiles_per_seq = rows_per_seq // tm
        pz_spec = pl.BlockSpec((1, 1, RWKV_IN), lambda i: (i // tiles_per_seq, 0, 0))
        zl_spec, zl_shape = pz_spec, (t // rows_per_seq, 1, RWKV_IN)
        scratch = [pltpu.VMEM((1, RWKV_IN), F32)]
    return pl.pallas_call(
        functools.partial(_inproj_kernel, rows_per_seq),
        grid=(t // tm,),
        in_specs=[row(D_MODEL), vec(D_MODEL), _const_spec((D_MODEL, IN_W)), pz_spec,
                  vec(RWKV_IN), vec(RWKV_W), _const_spec((D_DECAY_LORA, RWKV_W)),
                  vec(RWKV_W), _const_spec((D_AAA_LORA, RWKV_W)), _const_spec((D_GATE_LORA, RWKV_W)),
                  vec(RWKV_W), vec(RWKV_W), _const_spec((RWKV_W, RWKV_W))],
        out_specs=[row(ATT_W), row(KV_W), row(KV_W), row(2 * D_MODEL), zl_spec] + [row(RWKV_W)] * 7,
        out_shape=[jax.ShapeDtypeStruct((t, ATT_W), BF),
                   jax.ShapeDtypeStruct((t, KV_W), F32),
                   jax.ShapeDtypeStruct((t, KV_W), F32),
                   jax.ShapeDtypeStruct((t, 2 * D_MODEL), BF),
                   jax.ShapeDtypeStruct(zl_shape, F32)] + [jax.ShapeDtypeStruct((t, RWKV_W), F32)] * 7,
        scratch_shapes=scratch,
        compiler_params=_params("arbitrary"),
        name="inproj",
    )(x, g, p["w_in"], prev_z, p["mu"], p["w0"], p["w2"], p["a0"], p["a2"], p["g2"], p["k_k"], p["k_a"],
      p["ones_bd"])


def _t5_bucket(dist):
    n = np.maximum(dist, 0)
    nf = np.maximum(n, 1).astype(np.float32)
    large = MAX_EXACT + (np.log(nf / MAX_EXACT) / math.log(MAX_DISTANCE / MAX_EXACT)
                         * (NUM_BUCKETS - MAX_EXACT)).astype(np.int32)
    return np.where(n < MAX_EXACT, n, np.minimum(large, NUM_BUCKETS - 1)).astype(np.int32)


def _bias_lookup(rel_bias, bucket):
    hit = bucket[..., None, None] == np.arange(NUM_BUCKETS, dtype=np.int32)[:, None]
    return jnp.sum(jnp.where(hit, rel_bias, 0.0), axis=-2)


def _prompt_bucket_table():
    dist = np.arange(WINDOW)[:, None] - (np.arange(2 * WINDOW) - WINDOW)[None, :]
    return np.where((dist >= 0) & (dist <= WINDOW), _t5_bucket(dist), -1).astype(np.int32)


def _swa_prompt_kernel(nb, bucket_ref, relb_ref, q_ref, kp_ref, kc_ref, vp_ref, vc_ref, sink_ref, o_ref, bias_ref):
    first = pl.program_id(0) == 0

    @pl.when(first)
    def _():
        bucket = bucket_ref[...]
        for head in range(N_Q_HEADS):
            tab = jnp.full(bucket.shape, NEG_INF, F32)
            for n in range(NUM_BUCKETS):
                tab = jnp.where(bucket == n, relb_ref[n, head], tab)
            g = head % GQA_GROUP
            bias_ref[head // GQA_GROUP, g * WINDOW:(g + 1) * WINDOW, :] = tab

    col = lax.broadcasted_iota(jnp.int32, (GQA_GROUP * WINDOW, 2 * WINDOW), 1)
    pad_mask = jnp.logical_and(first, col < WINDOW)
    chains = [(b, h) for b in range(nb) for h in range(N_KV_HEADS)]
    hs = lambda h: slice(h * HEAD_DIM, (h + 1) * HEAD_DIM)
    kh = [jnp.concatenate([kp_ref[b][:, hs(h)], kc_ref[b][:, hs(h)]], axis=0).astype(BF) for b, h in chains]
    vh = [jnp.concatenate([vp_ref[b][:, hs(h)], vc_ref[b][:, hs(h)]], axis=0).astype(BF) for b, h in chains]
    qh = [jnp.concatenate([q_ref[b][:, hs(GQA_GROUP * h + g)] for g in range(GQA_GROUP)], axis=0)
          for b, h in chains]
    s = [_dot_nt(qh[n], kh[n]) * ATT_SCALE + bias_ref[h] for n, (b, h) in enumerate(chains)]
    s = [jnp.where(pad_mask, NEG_INF, x) for x in s]
    m = [jnp.maximum(jnp.max(x, axis=-1, keepdims=True), sink_ref[h]) for x, (b, h) in zip(s, chains)]
    p = [jnp.exp(x - mm) for x, mm in zip(s, m)]
    denom = [jnp.sum(x, axis=-1, keepdims=True) + jnp.exp(sink_ref[h] - mm)
             for x, mm, (b, h) in zip(p, m, chains)]
    o = [_dot(x.astype(BF), vv) / d for x, vv, d in zip(p, vh, denom)]
    for b in range(nb):
        pieces = [o[b * N_KV_HEADS + h][g * WINDOW:(g + 1) * WINDOW]
                  for h in range(N_KV_HEADS) for g in range(GQA_GROUP)]
        o_ref[b] = jnp.concatenate(pieces, axis=-1).astype(o_ref.dtype)


def _swa_prompt(q3, k3, v3, rel_bias, sink_col):
    batch, seq, _ = q3.shape
    cur = lambda i: (0, i, 0)
    prev = lambda i: (0, jnp.maximum(i - 1, 0), 0)
    kv_c = pl.BlockSpec((batch, WINDOW, KV_W), cur)
    kv_p = pl.BlockSpec((batch, WINDOW, KV_W), prev)
    bucket = jnp.asarray(_prompt_bucket_table())
    return pl.pallas_call(
        functools.partial(_swa_prompt_kernel, batch),
        grid=(seq // WINDOW,),
        in_specs=[_const_spec(bucket.shape), pl.BlockSpec(memory_space=pltpu.SMEM),
                  pl.BlockSpec((batch, WINDOW, ATT_W), cur), kv_p, kv_c, kv_p, kv_c,
                  _const_spec(sink_col.shape)],
        out_specs=pl.BlockSpec((batch, WINDOW, ATT_W), cur),
        out_shape=jax.ShapeDtypeStruct((batch, seq, ATT_W), BF),
        scratch_shapes=[pltpu.VMEM((N_KV_HEADS, GQA_GROUP * WINDOW, 2 * WINDOW), F32)],
        compiler_params=_params("arbitrary"),
        name="swa_prompt",
    )(bucket, rel_bias, q3, k3, k3, v3, v3, sink_col)


def _swa_sample_kernel(q_ref, kn_ref, vn_ref, ck_ref, cv_ref, bc_ref, bn_ref, sink_ref, o_ref):
    q = q_ref[...]
    qf = q.astype(F32)
    kn = kn_ref[...].astype(BF).astype(F32)
    vn = vn_ref[...].astype(BF).astype(F32)
    ck, cv = ck_ref[...], cv_ref[...]
    head = lax.broadcasted_iota(jnp.int32, (1, N_Q_HEADS, 1), 1)
    low = head < GQA_GROUP
    s_h, sn_h = [], []
    for h in range(N_KV_HEADS):
        sl = slice(h * HEAD_DIM, (h + 1) * HEAD_DIM)
        s_h.append(jnp.einsum("bgd,bkd->bgk", q, ck[:, :, sl].astype(BF), preferred_element_type=F32))
        sn_h.append(jnp.sum(qf * kn[:, None, sl], axis=-1, keepdims=True))
    s = jnp.where(low, s_h[0], s_h[1]) * ATT_SCALE + bc_ref[...]
    sn = jnp.where(low, sn_h[0], sn_h[1]) * ATT_SCALE + bn_ref[...]
    sink = sink_ref[...]
    m = jnp.maximum(jnp.maximum(jnp.max(s, axis=-1, keepdims=True), sn), sink)
    p = jnp.exp(s - m)
    pn = jnp.exp(sn - m)
    denom = jnp.sum(p, axis=-1, keepdims=True) + pn + jnp.exp(sink - m)
    pb = p.astype(BF)
    pnb = pn.astype(BF).astype(F32)
    o_h = []
    for h in range(N_KV_HEADS):
        sl = slice(h * HEAD_DIM, (h + 1) * HEAD_DIM)
        o = jnp.einsum("bgk,bkd->bgd", pb, cv[:, :, sl].astype(BF), preferred_element_type=F32)
        o_h.append(o + pnb * vn[:, None, sl])
    o_ref[...] = (jnp.where(low, o_h[0], o_h[1]) / denom).astype(o_ref.dtype)


def _swa_sample(q3, kn, vn, ck, cv, bias_c, bias_n, sink3, bb):
    b = q3.shape[0]
    return pl.pallas_call(
        _swa_sample_kernel,
        grid=(b // bb,),
        in_specs=[pl.BlockSpec((bb, N_Q_HEADS, HEAD_DIM), lambda i: (i, 0, 0)),
                  pl.BlockSpec((bb, KV_W), lambda i: (i, 0)),
                  pl.BlockSpec((bb, KV_W), lambda i: (i, 0)),
                  pl.BlockSpec((bb, WINDOW, KV_W), lambda i: (i, 0, 0)),
                  pl.BlockSpec((bb, WINDOW, KV_W), lambda i: (i, 0, 0)),
                  _const_spec(bias_c.shape), _const_spec(bias_n.shape), _const_spec(sink3.shape)],
        out_specs=pl.BlockSpec((bb, N_Q_HEADS, HEAD_DIM), lambda i: (i, 0, 0)),
        out_shape=jax.ShapeDtypeStruct((b, N_Q_HEADS, HEAD_DIM), BF),
        compiler_params=_params("parallel"),
        name="swa_sample",
    )(q3, kn, vn, ck, cv, bias_c, bias_n, sink3)


def _rwkv_scan_kernel(nb, chunk, r_ref, lw_ref, k_ref, v_ref, a_ref, b_ref, g_ref, s0_ref,
                      rk_ref, lg_ref, lb_ref, ones_ref, y_ref, sout_ref, s_scr):
    c = pl.program_id(1)

    @pl.when(c == 0)
    def _():
        s_scr[...] = s0_ref[...]

    ri = lax.broadcasted_iota(jnp.int32, (chunk, chunk), 0)
    ci = lax.broadcasted_iota(jnp.int32, (chunk, chunk), 1)
    incl = ri >= ci
    strict = ri > ci
    tri = incl.astype(BF)
    eye = (ri == ci).astype(F32)
    n_sq = int(math.log2(chunk)) - 1

    chains = [(bi, h) for bi in range(nb) for h in range(RWKV_HEADS)]
    hs = lambda h: slice(h * RWKV_HEAD, (h + 1) * RWKV_HEAD)

    ra, aa, bt, kt, bh, kh, vb, g_all = [], [], [], [], [], [], [], []
    for bi in range(nb):
        r, lw, k, v = r_ref[bi], lw_ref[bi], k_ref[bi], v_ref[bi]
        a, b = a_ref[bi], b_ref[bi]
        hi, lo = _split2(lw)
        lo2 = (lw - hi.astype(F32) - lo.astype(F32)).astype(BF)
        cum = _dot(tri, hi) + _dot(tri, lo) + _dot(tri, lo2)
        tail = cum[chunk - 1:chunk, :]
        g_inv = jnp.exp(-cum)
        g_tail = jnp.exp(tail - cum)
        g_all.append(jnp.exp(tail))
        ra.append((r * jnp.exp(cum)).astype(BF))
        aa.append((a * jnp.exp(cum - lw)).astype(BF))
        bt.append((b * g_inv).astype(BF))
        kt.append((k * g_inv).astype(BF))
        bh.append((b * g_tail).astype(BF))
        kh.append((k * g_tail).astype(BF))
        vb.append(v.astype(BF))

    s_old = [s_scr[bi, h] for bi, h in chains]
    ar = [jnp.concatenate([aa[bi][:, hs(h)], ra[bi][:, hs(h)]], axis=0) for bi, h in chains]
    v_h = [vb[bi][:, hs(h)] for bi, h in chains]
    gb = [_dot_nt(ar[n], bt[bi][:, hs(h)]) for n, (bi, h) in enumerate(chains)]
    gk = [_dot_nt(ar[n], kt[bi][:, hs(h)]) for n, (bi, h) in enumerate(chains)]
    p = [_dot_nt(ar[n], s_old[n].astype(BF)) for n in range(len(chains))]
    l_ab = [jnp.where(strict, x[:chunk], 0.0) for x in gb]
    l_ak = [jnp.where(strict, x[:chunk], 0.0).astype(BF) for x in gk]
    m_rb = [jnp.where(incl, x[chunk:], 0.0).astype(BF) for x in gb]
    m_rk = [jnp.where(incl, x[chunk:], 0.0).astype(BF) for x in gk]
    rhs = [p[n][:chunk] + _dot(l_ak[n], v_h[n]) for n in range(len(chains))]
    t_inv = [eye + x for x in l_ab]
    lp = l_ab
    for _ in range(n_sq):
        lpb = [x.astype(BF) for x in lp]
        lp = [_dot(x, x) for x in lpb]
        t_inv = [t + _dot(x.astype(BF), t.astype(BF)) for x, t in zip(lp, t_inv)]
    ub = [_dot(t.astype(BF), x.astype(BF)).astype(BF) for t, x in zip(t_inv, rhs)]
    y_h = [p[n][chunk:] + _dot(m_rb[n], ub[n]) + _dot(m_rk[n], v_h[n]) for n in range(len(chains))]
    s_new = [s_old[n] * g_all[bi][:, hs(h)] + _dot_tn(ub[n], bh[bi][:, hs(h)]) + _dot_tn(v_h[n], kh[bi][:, hs(h)])
             for n, (bi, h) in enumerate(chains)]
    for n, (bi, h) in enumerate(chains):
        s_scr[bi, h] = s_new[n]

    for bi in range(nb):
        y = jnp.concatenate(y_h[bi * RWKV_HEADS:(bi + 1) * RWKV_HEADS], axis=-1)
        y_ref[bi] = _rwkv_epilogue(y, r_ref[bi], k_ref[bi], v_ref[bi], g_ref[bi],
                                   rk_ref, lg_ref, lb_ref, ones_ref).astype(y_ref.dtype)

    @pl.when(c == pl.num_programs(1) - 1)
    def _():
        sout_ref[...] = s_scr[...]


def _rwkv_epilogue(y, r, k, v, g, rk_ref, lg_ref, lb_ref, ones_ref):
    ones = ones_ref[...]
    inv_n = 1.0 / RWKV_HEAD
    mean = _dot_exact_rhs(y, ones) * inv_n
    d = y - mean
    var = _dot_exact_rhs(d * d, ones) * inv_n
    yn = d * lax.rsqrt(var + GN_EPS) * lg_ref[...] + lb_ref[...]
    bonus = _dot_exact_rhs(r * k * rk_ref[...], ones) * v
    return (yn + bonus) * g


def _rwkv_scan(coef, s0, p, nb, chunk):
    r = coef[0]
    b, t, _ = r.shape
    seq_spec = pl.BlockSpec((nb, chunk, RWKV_W), lambda bi, c: (bi, c, 0))
    st_spec = pl.BlockSpec((nb, RWKV_HEADS, RWKV_HEAD, RWKV_HEAD), lambda bi, c: (bi, 0, 0, 0))
    vec = _const_spec((1, RWKV_W))
    return pl.pallas_call(
        functools.partial(_rwkv_scan_kernel, nb, chunk),
        grid=(b // nb, t // chunk),
        in_specs=[seq_spec] * 7 + [st_spec, vec, vec, vec, _const_spec((RWKV_W, RWKV_W))],
        out_specs=[seq_spec, st_spec],
        out_shape=[jax.ShapeDtypeStruct((b, t, RWKV_W), BF),
                   jax.ShapeDtypeStruct((b, RWKV_HEADS, RWKV_HEAD, RWKV_HEAD), F32)],
        scratch_shapes=[pltpu.VMEM((nb, RWKV_HEADS, RWKV_HEAD, RWKV_HEAD), F32)],
        compiler_params=_params("parallel", "arbitrary"),
        name="rwkv_scan",
    )(*coef, s0, p["r_k"], p["lnx_g"], p["lnx_b"], p["ones_bd"])


def _rwkv_step_kernel(r_ref, lw_ref, k_ref, v_ref, a_ref, b_ref, g_ref, s_ref,
                      rk_ref, lg_ref, lb_ref, ones_ref, y_ref, sout_ref, y_scr):
    n_pair = LANES // RWKV_HEAD
    r, k, v = r_ref[...], k_ref[...], v_ref[...]
    rT, kT, vT = r.T, k.T, v.T
    aT, bT, wT = a_ref[...].T, b_ref[...].T, jnp.exp(lw_ref[...]).T
    for hl in range(LANES // RWKV_HEAD):
        hsl = slice(hl * RWKV_HEAD, (hl + 1) * RWKV_HEAD)
        a_h, b_h, k_h, w_h, r_h = aT[hsl], bT[hsl], kT[hsl], wT[hsl], rT[hsl]
        tiles = [(hl * RWKV_HEAD + n_pair * t) * RWKV_HEAD for t in range(RWKV_HEAD // n_pair)]
        st = [s_ref[:, c0:c0 + LANES].T for c0 in tiles]
        new = []
        for t, x in enumerate(st):
            halves = []
            for il in range(n_pair):
                i = hl * RWKV_HEAD + n_pair * t + il
                slab = x[il * RWKV_HEAD:(il + 1) * RWKV_HEAD]
                sa = jnp.sum(slab * a_h, axis=0, keepdims=True)
                slab = slab * w_h + sa * b_h + vT[i:i + 1] * k_h
                y_scr[i:i + 1, :] = jnp.sum(slab * r_h, axis=0, keepdims=True)
                halves.append(slab)
            new.append(jnp.concatenate(halves, axis=0))
        for c0, x in zip(tiles, new):
            sout_ref[:, c0:c0 + LANES] = x.T
    y_ref[...] = _rwkv_epilogue(y_scr[...].T, r, k, v, g_ref[...],
                                rk_ref, lg_ref, lb_ref, ones_ref).astype(y_ref.dtype)


def _rwkv_step(coef, s0, p):
    b = s0.shape[0]
    per_pair = (LANES // RWKV_HEAD) * RWKV_HEAD * RWKV_HEAD
    n_steps = RWKV_W // LANES
    col = pl.BlockSpec((b, LANES), lambda i: (0, i))
    vec = pl.BlockSpec((1, LANES), lambda i: (0, i))
    st_spec = pl.BlockSpec((b, per_pair), lambda i: (0, i))
    y, s_new = pl.pallas_call(
        _rwkv_step_kernel,
        grid=(n_steps,),
        in_specs=[col] * 7 + [st_spec, vec, vec, vec, _const_spec((LANES, LANES))],
        out_specs=[col, st_spec],
        out_shape=[jax.ShapeDtypeStruct((b, RWKV_W), BF),
                   jax.ShapeDtypeStruct((b, n_steps * per_pair), F32)],
        scratch_shapes=[pltpu.VMEM((LANES, b), F32)],
        compiler_params=_params("parallel"),
        name="rwkv_step",
    )(*coef, s0.reshape(b, n_steps * per_pair), p["r_k"], p["lnx_g"], p["lnx_b"],
      p["ones_bd"][:LANES, :LANES])
    return y, s_new.reshape(s0.shape)


def _merge_kernel(with_router, x_ref, ya_ref, yb_ref, gate_ref, wa_ref, wb_ref, wo_ref, g_ref, *rest):
    if with_router:
        wr_ref, cnt_in_ref, x1_ref, h_ref, route_ref, cnt_out_ref, cnt_scr = rest
    else:
        x1_ref, h_ref = rest
    gate = gate_ref[...].astype(F32)
    pa = _dot(ya_ref[...], wa_ref[...])
    pb = _dot(yb_ref[...], wb_ref[...])
    merged = _sigmoid(gate[:, :D_MODEL]) * pa + _sigmoid(gate[:, D_MODEL:]) * pb
    x1 = x_ref[...] + _dot(merged.astype(BF), wo_ref[...])
    x1_ref[...] = x1
    h = _rms(x1, g_ref[...])
    h_ref[...] = h.astype(h_ref.dtype)
    if not with_router:
        return

    @pl.when(pl.program_id(0) == 0)
    def _():
        cnt_scr[...] = cnt_in_ref[...]

    logits = _dot(h.astype(BF), wr_ref[...])
    tm = logits.shape[0]
    lane = lax.broadcasted_iota(jnp.int32, logits.shape, 1).astype(F32)
    logits = jnp.where(lane < N_EXPERTS, logits, -jnp.inf)
    m1 = jnp.max(logits, axis=-1, keepdims=True)
    i1 = jnp.min(jnp.where(logits == m1, lane, float(LANES)), axis=-1, keepdims=True)
    rest_l = jnp.where(lane == i1, -jnp.inf, logits)
    m2 = jnp.max(rest_l, axis=-1, keepdims=True)
    i2 = jnp.min(jnp.where(rest_l == m2, lane, float(LANES)), axis=-1, keepdims=True)
    e2 = jnp.exp(m2 - m1)
    g1 = 1.0 / (1.0 + e2)
    g2 = e2 / (1.0 + e2)
    oh1 = (lane == i1).astype(F32)
    oh2 = (lane == i2).astype(F32)
    both = oh1 + oh2
    sub = min(tm, ROUTER_RANK_ROWS)
    ri = lax.broadcasted_iota(jnp.int32, (sub, sub), 0)
    ci = lax.broadcasted_iota(jnp.int32, (sub, sub), 1)
    earlier = (ri > ci).astype(BF)
    running = cnt_scr[...]
    before = []
    for c in range(tm // sub):
        part = both[c * sub:(c + 1) * sub]
        before.append(_dot(earlier, part.astype(BF)) + running)
        running = running + jnp.sum(part, axis=0, keepdims=True)
    before = jnp.concatenate(before, axis=0)
    rank1 = jnp.sum(oh1 * before, axis=-1, keepdims=True)
    rank2 = jnp.sum(oh2 * before, axis=-1, keepdims=True)
    cnt_scr[...] = running
    cnt_out_ref[...] = running
    route = jnp.zeros_like(logits)
    for n, val in enumerate((i1, i2, g1, g2, rank1, rank2)):
        route = jnp.where(lane == n, val, route)
    route_ref[...] = route


def _merge_out(x, ya, yb, gate, wa, wb, wo, g, tm, router=None, cnt_in=None):
    t = x.shape[0]
    row = lambda w_: pl.BlockSpec((tm, w_), lambda i: (i, 0))
    in_specs = [row(D_MODEL), row(ATT_W), row(RWKV_W), row(2 * D_MODEL),
                _const_spec((ATT_W, D_MODEL)), _const_spec((RWKV_W, D_MODEL)),
                _const_spec((D_MODEL, D_MODEL)), _const_spec((1, D_MODEL))]
    args = [x, ya, yb, gate, wa, wb, wo, g]
    if router is None:
        out_specs = [row(D_MODEL), row(D_MODEL)]
        out_shape = [jax.ShapeDtypeStruct((t, D_MODEL), F32), jax.ShapeDtypeStruct((t, D_MODEL), BF)]
        scratch = []
    else:
        in_specs += [_const_spec(router.shape), _const_spec((1, LANES))]
        args += [router, cnt_in]
        out_specs = [row(D_MODEL), row(D_MODEL), row(LANES), pl.BlockSpec((1, LANES), lambda i: (0, 0))]
        out_shape = [jax.ShapeDtypeStruct((t, D_MODEL), F32), jax.ShapeDtypeStruct((t, D_MODEL), F32),
                     jax.ShapeDtypeStruct((t, LANES), F32), jax.ShapeDtypeStruct((1, LANES), F32)]
        scratch = [pltpu.VMEM((1, LANES), F32)]
    return pl.pallas_call(
        functools.partial(_merge_kernel, router is not None),
        grid=(t // tm,),
        in_specs=in_specs, out_specs=out_specs, out_shape=out_shape, scratch_shapes=scratch,
        compiler_params=_params("parallel" if router is None else "arbitrary"),
        name="merge_out",
    )(*args)


def _swiglu(x, wg, wu, wd):
    s = _dot(x, wg)
    u = _dot(x, wu)
    act = (s * _sigmoid(s) * u).astype(BF)
    return _dot(act, wd)


def _dense_ffn_kernel(x1_ref, h_ref, wg_ref, wu_ref, wd_ref, o_ref):
    o_ref[...] = x1_ref[...] + _swiglu(h_ref[...], wg_ref[...], wu_ref[...], wd_ref[...])


def _dense_ffn(x1, h, wg, wu, wd, tm):
    t = x1.shape[0]
    d_ff = wg.shape[1]
    row = lambda: pl.BlockSpec((tm, D_MODEL), lambda i: (i, 0))
    return pl.pallas_call(
        _dense_ffn_kernel,
        grid=(t // tm,),
        in_specs=[row(), row(), _const_spec((D_MODEL, d_ff)), _const_spec((D_MODEL, d_ff)),
                  _const_spec((d_ff, D_MODEL))],
        out_specs=row(),
        out_shape=jax.ShapeDtypeStruct((t, D_MODEL), F32),
        compiler_params=_params("parallel"),
        name="dense_ffn",
    )(x1, h, wg, wu, wd)


def _dispatch_kernel(tt, first, dest_ref, fill_ref, h_ref, *rest):
    if first:
        xs_hbm, zbuf, sem, zsem = rest
    else:
        _, xs_hbm, sem = rest
    i = pl.program_id(0)
    base = i * tt

    if first:
        @pl.when(i == 0)
        def _():
            zbuf[...] = jnp.zeros_like(zbuf)
            bm = zbuf.shape[0]
            fills = [pltpu.make_async_copy(zbuf, xs_hbm.at[pl.ds(fill_ref[n] * bm, bm), :], zsem)
                     for n in range(fill_ref.shape[0])]
            for f in fills:
                f.start()
            for f in fills:
                f.wait()

    def issue(r, carry):
        for j in range(TOP_K):
            row = dest_ref[TOP_K * (base + r) + j]
            pltpu.make_async_copy(h_ref.at[pl.ds(r, 1), :], xs_hbm.at[pl.ds(row, 1), :], sem).start()
        return carry

    lax.fori_loop(0, tt, issue, 0, unroll=8)
    for j in range(TOP_K):
        pltpu.make_async_copy(h_ref, xs_hbm.at[pl.ds(0, tt), :], sem).wait()


def _dispatch(dest, fill_blocks, h, xs, tt, bm, n_rows):
    t = h.shape[0]
    first = xs is None
    in_specs = [pl.BlockSpec((tt, D_MODEL), lambda i, ds, fb: (i, 0))]
    args = [dest, fill_blocks, h]
    if first:
        scratch = [pltpu.VMEM((bm, D_MODEL), F32), pltpu.SemaphoreType.DMA(()), pltpu.SemaphoreType.DMA(())]
        aliases = {}
    else:
        in_specs.append(pl.BlockSpec(memory_space=pl.ANY))
        args.append(xs)
        scratch = [pltpu.SemaphoreType.DMA(())]
        aliases = {3: 0}
    grid_spec = pltpu.PrefetchScalarGridSpec(
        num_scalar_prefetch=2, grid=(t // tt,), in_specs=in_specs,
        out_specs=pl.BlockSpec(memory_space=pl.ANY), scratch_shapes=scratch)
    return pl.pallas_call(
        functools.partial(_dispatch_kernel, tt, first),
        grid_spec=grid_spec,
        out_shape=jax.ShapeDtypeStruct((n_rows, D_MODEL), F32),
        input_output_aliases=aliases,
        compiler_params=_params("arbitrary"),
        name="moe_dispatch",
    )(*args)


def _moe_kernel(n_half, blk_e_ref, n_used_ref, xs_ref, wg_ref, wu_ref, wd_ref, y_ref):
    @pl.when(pl.program_id(0) < n_used_ref[0])
    def _():
        x = xs_ref[...].astype(BF)
        d_e = wg_ref.shape[2]
        step = d_e // n_half
        acc = None
        for j in range(n_half):
            cs = slice(j * step, (j + 1) * step)
            part = _swiglu(x, wg_ref[0, :, cs], wu_ref[0, :, cs], wd_ref[0, cs, :])
            acc = part if acc is None else acc + part
        y_ref[...] = acc

    @pl.when(pl.program_id(0) >= n_used_ref[0])
    def _():
        y_ref[...] = jnp.zeros_like(y_ref)


def _moe_ffn(xs, blk_e, n_used, wg, wu, wd, bm):
    n_blk = blk_e.shape[0]
    n_rows = xs.shape[0]
    d_e = wg.shape[2]
    wspec = lambda shape: pl.BlockSpec(shape, lambda i, be, nu: (be[i], 0, 0), pipeline_mode=pl.Buffered(1))
    rows = pl.BlockSpec((bm, D_MODEL), lambda i, be, nu: (i, 0))
    grid_spec = pltpu.PrefetchScalarGridSpec(
        num_scalar_prefetch=2,
        grid=(n_blk,),
        in_specs=[rows, wspec((1, D_MODEL, d_e)), wspec((1, D_MODEL, d_e)), wspec((1, d_e, D_MODEL))],
        out_specs=rows,
    )
    return pl.pallas_call(
        functools.partial(_moe_kernel, 2),
        grid_spec=grid_spec,
        out_shape=jax.ShapeDtypeStruct((n_rows, D_MODEL), F32),
        compiler_params=_params("arbitrary"),
        name="moe_ffn",
    )(blk_e, n_used, xs, wg, wu, wd)


def _row_layout(counts, bm, n_blk):
    per_e = (counts + bm - 1) // bm
    ends = jnp.cumsum(per_e)
    n_used = ends[-1]
    first_row = (ends - per_e) * bm
    i = jnp.arange(n_blk, dtype=jnp.int32)
    blk_e = jnp.sum((jnp.minimum(i, jnp.maximum(n_used - 1, 0))[:, None] >= ends[None, :]).astype(jnp.int32), axis=1)
    need = jnp.any(i[:, None] == (ends - 1)[None, :], axis=1) | (i >= n_blk - N_EXPERTS)
    fill_blocks = jnp.argsort(jnp.logical_not(need), stable=True)[:2 * N_EXPERTS].astype(jnp.int32)
    return first_row.astype(jnp.int32), blk_e.astype(jnp.int32), n_used.astype(jnp.int32).reshape(1), fill_blocks


def _combine_kernel(tc, dest_ref, x1_ref, route_ref, y_hbm, g_ref, o_ref, ybuf, sem):
    base = pl.program_id(0) * tc

    def issue(r, carry):
        for j in range(TOP_K):
            row = dest_ref[TOP_K * (base + r) + j]
            pltpu.make_async_copy(y_hbm.at[pl.ds(row, 1), :], ybuf.at[j, pl.ds(r, 1), :], sem).start()
        return carry

    lax.fori_loop(0, tc, issue, 0, unroll=4)
    for j in range(TOP_K):
        pltpu.make_async_copy(y_hbm.at[pl.ds(0, tc), :], ybuf.at[j], sem).wait()
    route = route_ref[...]
    x2 = x1_ref[...] + route[:, 2:3] * ybuf[0] + route[:, 3:4] * ybuf[1]
    o_ref[...] = _rms(x2, g_ref[...])


def _moe_combine(dest, x1, route, y, g, tc):
    t = x1.shape[0]
    grid_spec = pltpu.PrefetchScalarGridSpec(
        num_scalar_prefetch=1,
        grid=(t // tc,),
        in_specs=[pl.BlockSpec((tc, D_MODEL), lambda i, ds: (i, 0)),
                  pl.BlockSpec((tc, LANES), lambda i, ds: (i, 0)),
                  pl.BlockSpec(memory_space=pl.ANY),
                  pl.BlockSpec((1, D_MODEL), lambda i, ds: (0, 0))],
        out_specs=pl.BlockSpec((tc, D_MODEL), lambda i, ds: (i, 0)),
        scratch_shapes=[pltpu.VMEM((TOP_K, tc, D_MODEL), F32), pltpu.SemaphoreType.DMA(())],
    )
    return pl.pallas_call(
        functools.partial(_combine_kernel, tc),
        grid_spec=grid_spec,
        out_shape=jax.ShapeDtypeStruct((t, D_MODEL), F32),
        compiler_params=_params("arbitrary"),
        name="moe_combine",
    )(dest, x1, route, y, g)


def _layer_params(l, w_in, mu_shift, w0, w2, a0, a2, g2, k_k, k_a, r_k, lnx_g, lnx_b,
                  w_proj_attn, w_proj_rwkv, w_out, ones_bd):
    row = lambda u: u.reshape(1, -1)
    return dict(
        w_in=w_in[l].astype(BF), mu=row(mu_shift[l]), w0=row(w0[l]), w2=w2[l].astype(BF),
        a0=row(a0[l]), a2=a2[l].astype(BF), g2=g2[l].astype(BF), k_k=row(k_k[l]), k_a=row(k_a[l]),
        r_k=row(r_k[l]), lnx_g=row(lnx_g[l]), lnx_b=row(lnx_b[l]),
        wa=w_proj_attn[l].astype(BF), wb=w_proj_rwkv[l].astype(BF), wo=w_out[l].astype(BF),
        ones_bd=ones_bd)


def kernel(x_prompt, x_sample, cache_k, cache_v, state_wkv, state_shift, norm_mix_g, w_in, attn_sinks, rel_bias, mu_shift, w0, w2, a0, a2, g2, k_k, k_a, r_k, lnx_g, lnx_b, w_proj_attn, w_proj_rwkv, w_out, norm_ffn_g, dense_w_gate, dense_w_up, dense_w_down, router_w, moe_w_gate, moe_w_up, moe_w_down, norm_final_g):
    batch, seq, _ = x_prompt.shape
    dec_batch, dec_seq, _ = x_sample.shape
    assert dec_seq == 1 and seq % WINDOW == 0

    head_id = jnp.arange(RWKV_W, dtype=jnp.int32) // RWKV_HEAD
    ones_bd = (head_id[:, None] == head_id[None, :]).astype(BF)
    bias_by_dist = _bias_lookup(rel_bias, _t5_bucket(np.arange(WINDOW + 1)))
    bias_c = jnp.transpose(bias_by_dist[WINDOW:0:-1], (1, 0))[None]
    bias_n = bias_by_dist[0][None, :, None]

    layers = [_layer_params(l, w_in, mu_shift, w0, w2, a0, a2, g2, k_k, k_a, r_k, lnx_g, lnx_b,
                            w_proj_attn, w_proj_rwkv, w_out, ones_bd) for l in range(DEPTH)]

    def router_pieces(l):
        return jnp.pad(router_w[l // 2], ((0, 0), (0, LANES - N_EXPERTS))).astype(BF)

    n_prompt = batch * seq
    bm = MOE_BLOCK_ROWS
    n_all = n_prompt + dec_batch
    n_blk = -(-(n_all * TOP_K) // bm) + N_EXPERTS

    def mix(x, is_prompt, l, state):
        p = layers[l]
        n_tok = x.shape[0]
        gm = norm_mix_g[l].reshape(1, -1)
        sinks = attn_sinks[l]
        if is_prompt:
            b, t = batch, seq
            prev_z = jnp.zeros((b, 1, RWKV_IN), F32)
            q, k, v, gate, z_last, *coef = _inproj(x, gm, p, prev_z, min(INPROJ_ROWS, t), t)
            sink_col = jnp.repeat(sinks, WINDOW).reshape(N_KV_HEADS, GQA_GROUP * WINDOW, 1)
            ya = _swa_prompt(q.reshape(b, t, ATT_W), k.reshape(b, t, KV_W), v.reshape(b, t, KV_W),
                             rel_bias, sink_col).reshape(n_tok, ATT_W)
            k4 = k.reshape(b, t, N_KV_HEADS, HEAD_DIM)
            v4 = v.reshape(b, t, N_KV_HEADS, HEAD_DIM)
            state["k"].append(k4[:, t - WINDOW:])
            state["v"].append(v4[:, t - WINDOW:])
            s0 = jnp.zeros((b, RWKV_HEADS, RWKV_HEAD, RWKV_HEAD), F32)
            yb3, s_new = _rwkv_scan([u.reshape(b, t, RWKV_W) for u in coef], s0, p, b, RWKV_CHUNK)
            yb = yb3.reshape(n_tok, RWKV_W)
            state["z"].append(z_last.reshape(b, RWKV_IN))
        else:
            b = dec_batch
            q, k, v, gate, z, *coef = _inproj(x, gm, p, state_shift[l], n_tok, None)
            ck = cache_k[l].reshape(b, WINDOW, KV_W)
            cv = cache_v[l].reshape(b, WINDOW, KV_W)
            ya3 = _swa_sample(q.reshape(b, N_Q_HEADS, HEAD_DIM), k, v, ck, cv, bias_c, bias_n,
                              sinks.reshape(1, N_Q_HEADS, 1), min(SAMPLE_ATTN_SEQS, b))
            ya = ya3.reshape(b, ATT_W)
            state["k"].append(jnp.concatenate([ck[:, 1:], k[:, None]], axis=1)
                              .reshape(b, WINDOW, N_KV_HEADS, HEAD_DIM))
            state["v"].append(jnp.concatenate([cv[:, 1:], v[:, None]], axis=1)
                              .reshape(b, WINDOW, N_KV_HEADS, HEAD_DIM))
            yb, s_new = _rwkv_step(coef, state_wkv[l], p)
            state["z"].append(z)
        state["s"].append(s_new)
        return ya, yb, gate

    groups = [dict(x=x_prompt.reshape(n_prompt, D_MODEL), prompt=True, tm=min(DENSE_ROWS, n_prompt), k=[], v=[], s=[], z=[]),
              dict(x=x_sample.reshape(dec_batch, D_MODEL), prompt=False, tm=min(DENSE_ROWS, dec_batch), k=[], v=[], s=[], z=[])]
    for l in range(DEPTH):
        p = layers[l]
        gn = norm_ffn_g[l].reshape(1, -1)
        j = l // 2
        if l % 2 == 0:
            wg, wu, wd = (_to_bf16(w[j], w.shape[1] // 4) for w in (dense_w_gate, dense_w_up, dense_w_down))
            for grp in groups:
                ya, yb, gate = mix(grp["x"], grp["prompt"], l, grp)
                x1, h = _merge_out(grp["x"], ya, yb, gate, p["wa"], p["wb"], p["wo"], gn, grp["tm"])
                grp["x"] = _dense_ffn(x1, h, wg, wu, wd, grp["tm"])
        else:
            assert l == DEPTH - 1
            wg, wu, wd = (w[j].astype(BF) for w in (moe_w_gate, moe_w_up, moe_w_down))
            router = router_pieces(l)
            cnt = jnp.zeros((1, LANES), F32)
            xs = None
            for grp in groups:
                ya, yb, gate = mix(grp["x"], grp["prompt"], l, grp)
                x1, h, route, cnt = _merge_out(grp["x"], ya, yb, gate, p["wa"], p["wb"], p["wo"], gn,
                                               min(ROUTER_ROWS, grp["tm"]), router=router, cnt_in=cnt)
                grp["x1"], grp["route"], grp["h"] = x1, route, h
            counts = cnt[0, :N_EXPERTS].astype(jnp.int32)
            first_row, blk_e, n_used, fill_blocks = _row_layout(counts, bm, n_blk)
            for grp in groups:
                route = grp["route"]
                expert = route[:, 0:TOP_K].astype(jnp.int32)
                rank = route[:, 2 * TOP_K:3 * TOP_K].astype(jnp.int32)
                start = jnp.sum(jnp.where(expert[..., None] == jnp.arange(N_EXPERTS, dtype=jnp.int32),
                                          first_row, 0), axis=-1)
                grp["dest"] = (start + rank).reshape(-1)
                xs = _dispatch(grp["dest"], fill_blocks, grp["h"], xs, min(DISPATCH_ROWS, grp["h"].shape[0]), bm,
                               n_blk * bm)
            y = _moe_ffn(xs, blk_e, n_used, wg, wu, wd, bm)
            for grp in groups:
                grp["x"] = _moe_combine(grp["dest"], grp["x1"], grp["route"], y,
                                        norm_final_g.reshape(1, -1), min(COMBINE_ROWS, grp["x1"].shape[0]))

    gp, gs = groups
    outs = []
    for grp, shape in ((gp, (batch, seq, D_MODEL)), (gs, (dec_batch, dec_seq, D_MODEL))):
        outs.append((grp["x"].reshape(shape), jnp.stack(grp["k"]), jnp.stack(grp["v"]),
                     jnp.stack(grp["s"]), jnp.stack(grp["z"])))
    (y_p, nk_p, nv_p, ns_p, nz_p), (y_s, nk_s, nv_s, ns_s, nz_s) = outs
    return (y_p, y_s, nk_p, nv_p, ns_p, nz_p, nk_s, nv_s, ns_s, nz_s)
```

```python
import functools
import math

import jax
import jax.numpy as jnp
import numpy as np
from jax import lax
from jax.experimental import pallas as pl
from jax.experimental.pallas import tpu as pltpu

BF = jnp.bfloat16
F32 = jnp.float32

D_MODEL = 1024
DEPTH = 2
HEAD_DIM = 64
N_Q_HEADS = 8
N_KV_HEADS = 2
GQA_GROUP = 4
ATT_W = 512
KV_W = 128
WINDOW = 128
ATT_SCALE = HEAD_DIM ** -0.5
NUM_BUCKETS = 32
MAX_EXACT = 16
MAX_DISTANCE = 128
NEG_INF = -1e30
RWKV_HEAD = 64
RWKV_W = 512
RWKV_HEADS = 8
D_DECAY_LORA = 64
D_AAA_LORA = 64
D_GATE_LORA = 128
RWKV_IN = 3 * RWKV_W + D_DECAY_LORA + D_AAA_LORA + D_GATE_LORA
GN_EPS = 64e-5
IN_W = ATT_W + 2 * KV_W + RWKV_IN + 2 * D_MODEL
N_EXPERTS = 8
TOP_K = 2
NORM_EPS = 1e-6

VMEM_LIMIT_BYTES = 56 * 1024 * 1024
LANES = 128

INPROJ_ROWS = 512
DENSE_ROWS = 512
ROUTER_ROWS = 512
ROUTER_RANK_ROWS = 256
MOE_BLOCK_ROWS = 512
DISPATCH_ROWS = 2048
COMBINE_ROWS = 1024
RWKV_CHUNK = 64
SAMPLE_ATTN_SEQS = 16


def _params(*sem):
    return pltpu.CompilerParams(dimension_semantics=sem, vmem_limit_bytes=VMEM_LIMIT_BYTES)


def _const_spec(shape):
    zeros = (0,) * len(shape)
    return pl.BlockSpec(shape, lambda *_: zeros, pipeline_mode=pl.Buffered(1))


def _dot(a, b):
    return jnp.dot(a, b, preferred_element_type=F32)


def _dot_nt(a, b):
    return lax.dot_general(a, b, (((1,), (1,)), ((), ())), preferred_element_type=F32)


def _dot_tn(a, b):
    return lax.dot_general(a, b, (((0,), (0,)), ((), ())), preferred_element_type=F32)


def _split2(x):
    hi = x.astype(BF)
    lo = (x - hi.astype(F32)).astype(BF)
    return hi, lo


def _dot_exact_rhs(x, w):
    hi, lo = _split2(x)
    return _dot(hi, w) + _dot(lo, w)


def _rms(x, g):
    ms = jnp.mean(x * x, axis=-1, keepdims=True)
    return x * lax.rsqrt(ms + NORM_EPS) * g


def _sigmoid(x):
    return 1.0 / (1.0 + jnp.exp(-x))


def _cast_kernel(x_ref, o_ref):
    o_ref[...] = x_ref[...].astype(o_ref.dtype)


def _to_bf16(w, row_tile):
    rows, cols = w.shape
    spec = pl.BlockSpec((row_tile, cols), lambda i: (i, 0))
    return pl.pallas_call(
        _cast_kernel, grid=(rows // row_tile,), in_specs=[spec], out_specs=spec,
        out_shape=jax.ShapeDtypeStruct((rows, cols), BF),
        compiler_params=_params("parallel"), name="to_bf16",
    )(w)


def _inproj_kernel(rows_per_seq, x_ref, g_ref, w_ref, pz_ref, mu_ref, w0_ref, w2_ref, a0_ref, a2_ref, g2_ref,
                   kk_ref, ka_ref, ones_ref,
                   q_ref, k_ref, v_ref, gate_ref, zl_ref,
                   r_ref, lw_ref, k2_ref, vr_ref, a_ref, b_ref, gr_ref, *carry):
    n = _rms(x_ref[...], g_ref[...]).astype(BF)
    z0 = ATT_W + 2 * KV_W
    z = _dot(n, w_ref[:, z0:z0 + RWKV_IN])
    q_ref[...] = _dot(n, w_ref[:, 0:ATT_W]).astype(BF)
    k_ref[...] = _dot(n, w_ref[:, ATT_W:ATT_W + KV_W])
    v_ref[...] = _dot(n, w_ref[:, ATT_W + KV_W:ATT_W + 2 * KV_W])
    gate_ref[...] = _dot(n, w_ref[:, z0 + RWKV_IN:IN_W]).astype(gate_ref.dtype)
    tm = z.shape[0]
    if rows_per_seq is None:
        zs = pz_ref[...]
        zl_ref[...] = z
    else:
        prev_ref, = carry
        tiles_per_seq = rows_per_seq // tm

        @pl.when(pl.program_id(0) % tiles_per_seq == 0)
        def _():
            prev_ref[...] = pz_ref[0]

        row = lax.broadcasted_iota(jnp.int32, (tm, 1), 0)
        zs = jnp.where(row == 0, prev_ref[...], pltpu.roll(z, 1, 0))
        prev_ref[...] = z[tm - 1:tm, :]
        zl_ref[0] = z[tm - 1:tm, :]
    zz = z + (zs - z) * mu_ref[...]
    r = zz[:, 0:RWKV_W]
    k = zz[:, RWKV_W:2 * RWKV_W]
    v = zz[:, 2 * RWKV_W:3 * RWKV_W]
    o = 3 * RWKV_W
    wl = zz[:, o:o + D_DECAY_LORA]
    al = zz[:, o + D_DECAY_LORA:o + D_DECAY_LORA + D_AAA_LORA]
    gl = zz[:, o + D_DECAY_LORA + D_AAA_LORA:RWKV_IN]
    lw = -math.exp(-0.5) * _sigmoid(w0_ref[...] + _dot(jnp.tanh(wl).astype(BF), w2_ref[...]))
    a = _sigmoid(a0_ref[...] + _dot(al.astype(BF), a2_ref[...]))
    g = _dot(_sigmoid(gl).astype(BF), g2_ref[...])
    kkv = k * kk_ref[...]
    ss = _dot_exact_rhs(kkv * kkv, ones_ref[...])
    kk = kkv * lax.rsqrt(jnp.maximum(ss, 1e-24))
    k2 = k * (1.0 + (a - 1.0) * ka_ref[...])
    r_ref[...], lw_ref[...], k2_ref[...], vr_ref[...] = r, lw, k2, v
    a_ref[...], b_ref[...], gr_ref[...] = -kk, kk * a, g


def _inproj(x, g, p, prev_z, tm, rows_per_seq):
    t = x.shape[0]
    row = lambda w_: pl.BlockSpec((tm, w_), lambda i: (i, 0))
    vec = lambda w_: _const_spec((1, w_))
    if rows_per_seq is None:
        pz_spec = row(RWKV_IN)
        zl_spec, zl_shape = row(RWKV_IN), (t, RWKV_IN)
        scratch = []
    else:
        tiles_per_seq = rows_per_seq // tm
        pz_spec = pl.BlockSpec((1, 1, RWKV_IN), lambda i: (i // tiles_per_seq, 0, 0))
        zl_spec, zl_shape = pz_spec, (t // rows_per_seq, 1, RWKV_IN)
        scratch = [pltpu.VMEM((1, RWKV_IN), F32)]
    return pl.pallas_call(
        functools.partial(_inproj_kernel, rows_per_seq),
        grid=(t // tm,),
        in_specs=[row(D_MODEL), vec(D_MODEL), _const_spec((D_MODEL, IN_W)), pz_spec,
                  vec(RWKV_IN), vec(RWKV_W), _const_spec((D_DECAY_LORA, RWKV_W)),
                  vec(RWKV_W), _const_spec((D_AAA_LORA, RWKV_W)), _const_spec((D_GATE_LORA, RWKV_W)),
                  vec(RWKV_W), vec(RWKV_W), _const_spec((RWKV_W, RWKV_W))],
        out_specs=[row(ATT_W), row(KV_W), row(KV_W), row(2 * D_MODEL), zl_spec] + [row(RWKV_W)] * 7,
        out_shape=[jax.ShapeDtypeStruct((t, ATT_W), BF),
                   jax.ShapeDtypeStruct((t, KV_W), F32),
                   jax.ShapeDtypeStruct((t, KV_W), F32),
                   jax.ShapeDtypeStruct((t, 2 * D_MODEL), BF),
                   jax.ShapeDtypeStruct(zl_shape, F32)] + [jax.ShapeDtypeStruct((t, RWKV_W), F32)] * 7,
        scratch_shapes=scratch,
        compiler_params=_params("arbitrary"),
        name="inproj",
    )(x, g, p["w_in"], prev_z, p["mu"], p["w0"], p["w2"], p["a0"], p["a2"], p["g2"], p["k_k"], p["k_a"],
      p["ones_bd"])


def _t5_bucket(dist):
    n = np.maximum(dist, 0)
    nf = np.maximum(n, 1).astype(np.float32)
    large = MAX_EXACT + (np.log(nf / MAX_EXACT) / math.log(MAX_DISTANCE / MAX_EXACT)
                         * (NUM_BUCKETS - MAX_EXACT)).astype(np.int32)
    return np.where(n < MAX_EXACT, n, np.minimum(large, NUM_BUCKETS - 1)).astype(np.int32)


def _bias_lookup(rel_bias, bucket):
    hit = bucket[..., None, None] == np.arange(NUM_BUCKETS, dtype=np.int32)[:, None]
    return jnp.sum(jnp.where(hit, rel_bias, 0.0), axis=-2)


def _prompt_bucket_table():
    dist = np.arange(WINDOW)[:, None] - (np.arange(2 * WINDOW) - WINDOW)[None, :]
    return np.where((dist >= 0) & (dist <= WINDOW), _t5_bucket(dist), -1).astype(np.int32)


def _swa_prompt_kernel(nb, bucket_ref, relb_ref, q_ref, kp_ref, kc_ref, vp_ref, vc_ref, sink_ref, o_ref, bias_ref):
    first = pl.program_id(0) == 0

    @pl.when(first)
    def _():
        bucket = bucket_ref[...]
        for head in range(N_Q_HEADS):
            tab = jnp.full(bucket.shape, NEG_INF, F32)
            for n in range(NUM_BUCKETS):
                tab = jnp.where(bucket == n, relb_ref[n, head], tab)
            g = head % GQA_GROUP
            bias_ref[head // GQA_GROUP, g * WINDOW:(g + 1) * WINDOW, :] = tab

    col = lax.broadcasted_iota(jnp.int32, (GQA_GROUP * WINDOW, 2 * WINDOW), 1)
    pad_mask = jnp.logical_and(first, col < WINDOW)
    chains = [(b, h) for b in range(nb) for h in range(N_KV_HEADS)]
    hs = lambda h: slice(h * HEAD_DIM, (h + 1) * HEAD_DIM)
    kh = [jnp.concatenate([kp_ref[b][:, hs(h)], kc_ref[b][:, hs(h)]], axis=0).astype(BF) for b, h in chains]
    vh = [jnp.concatenate([vp_ref[b][:, hs(h)], vc_ref[b][:, hs(h)]], axis=0).astype(BF) for b, h in chains]
    qh = [jnp.concatenate([q_ref[b][:, hs(GQA_GROUP * h + g)] for g in range(GQA_GROUP)], axis=0)
          for b, h in chains]
    s = [_dot_nt(qh[n], kh[n]) * ATT_SCALE + bias_ref[h] for n, (b, h) in enumerate(chains)]
    s = [jnp.where(pad_mask, NEG_INF, x) for x in s]
    m = [jnp.maximum(jnp.max(x, axis=-1, keepdims=True), sink_ref[h]) for x, (b, h) in zip(s, chains)]
    p = [jnp.exp(x - mm) for x, mm in zip(s, m)]
    denom = [jnp.sum(x, axis=-1, keepdims=True) + jnp.exp(sink_ref[h] - mm)
             for x, mm, (b, h) in zip(p, m, chains)]
    o = [_dot(x.astype(BF), vv) / d for x, vv, d in zip(p, vh, denom)]
    for b in range(nb):
        pieces = [o[b * N_KV_HEADS + h][g * WINDOW:(g + 1) * WINDOW]
                  for h in range(N_KV_HEADS) for g in range(GQA_GROUP)]
        o_ref[b] = jnp.concatenate(pieces, axis=-1).astype(o_ref.dtype)


def _swa_prompt(q3, k3, v3, rel_bias, sink_col):
    batch, seq, _ = q3.shape
    cur = lambda i: (0, i, 0)
    prev = lambda i: (0, jnp.maximum(i - 1, 0), 0)
    kv_c = pl.BlockSpec((batch, WINDOW, KV_W), cur)
    kv_p = pl.BlockSpec((batch, WINDOW, KV_W), prev)
    bucket = jnp.asarray(_prompt_bucket_table())
    return pl.pallas_call(
        functools.partial(_swa_prompt_kernel, batch),
        grid=(seq // WINDOW,),
        in_specs=[_const_spec(bucket.shape), pl.BlockSpec(memory_space=pltpu.SMEM),
                  pl.BlockSpec((batch, WINDOW, ATT_W), cur), kv_p, kv_c, kv_p, kv_c,
                  _const_spec(sink_col.shape)],
        out_specs=pl.BlockSpec((batch, WINDOW, ATT_W), cur),
        out_shape=jax.ShapeDtypeStruct((batch, seq, ATT_W), BF),
        scratch_shapes=[pltpu.VMEM((N_KV_HEADS, GQA_GROUP * WINDOW, 2 * WINDOW), F32)],
        compiler_params=_params("arbitrary"),
        name="swa_prompt",
    )(bucket, rel_bias, q3, k3, k3, v3, v3, sink_col)


def _swa_sample_kernel(q_ref, kn_ref, vn_ref, ck_ref, cv_ref, bc_ref, bn_ref, sink_ref, o_ref):
    q = q_ref[...]
    qf = q.astype(F32)
    kn = kn_ref[...].astype(BF).astype(F32)
    vn = vn_ref[...].astype(BF).astype(F32)
    ck, cv = ck_ref[...], cv_ref[...]
    head = lax.broadcasted_iota(jnp.int32, (1, N_Q_HEADS, 1), 1)
    low = head < GQA_GROUP
    s_h, sn_h = [], []
    for h in range(N_KV_HEADS):
        sl = slice(h * HEAD_DIM, (h + 1) * HEAD_DIM)
        s_h.append(jnp.einsum("bgd,bkd->bgk", q, ck[:, :, sl].astype(BF), preferred_element_type=F32))
        sn_h.append(jnp.sum(qf * kn[:, None, sl], axis=-1, keepdims=True))
    s = jnp.where(low, s_h[0], s_h[1]) * ATT_SCALE + bc_ref[...]
    sn = jnp.where(low, sn_h[0], sn_h[1]) * ATT_SCALE + bn_ref[...]
    sink = sink_ref[...]
    m = jnp.maximum(jnp.maximum(jnp.max(s, axis=-1, keepdims=True), sn), sink)
    p = jnp.exp(s - m)
    pn = jnp.exp(sn - m)
    denom = jnp.sum(p, axis=-1, keepdims=True) + pn + jnp.exp(sink - m)
    pb = p.astype(BF)
    pnb = pn.astype(BF).astype(F32)
    o_h = []
    for h in range(N_KV_HEADS):
        sl = slice(h * HEAD_DIM, (h + 1) * HEAD_DIM)
        o = jnp.einsum("bgk,bkd->bgd", pb, cv[:, :, sl].astype(BF), preferred_element_type=F32)
        o_h.append(o + pnb * vn[:, None, sl])
    o_ref[...] = (jnp.where(low, o_h[0], o_h[1]) / denom).astype(o_ref.dtype)


def _swa_sample(q3, kn, vn, ck, cv, bias_c, bias_n, sink3, bb):
    b = q3.shape[0]
    return pl.pallas_call(
        _swa_sample_kernel,
        grid=(b // bb,),
        in_specs=[pl.BlockSpec((bb, N_Q_HEADS, HEAD_DIM), lambda i: (i, 0, 0)),
                  pl.BlockSpec((bb, KV_W), lambda i: (i, 0)),
                  pl.BlockSpec((bb, KV_W), lambda i: (i, 0)),
                  pl.BlockSpec((bb, WINDOW, KV_W), lambda i: (i, 0, 0)),
                  pl.BlockSpec((bb, WINDOW, KV_W), lambda i: (i, 0, 0)),
                  _const_spec(bias_c.shape), _const_spec(bias_n.shape), _const_spec(sink3.shape)],
        out_specs=pl.BlockSpec((bb, N_Q_HEADS, HEAD_DIM), lambda i: (i, 0, 0)),
        out_shape=jax.ShapeDtypeStruct((b, N_Q_HEADS, HEAD_DIM), BF),
        compiler_params=_params("parallel"),
        name="swa_sample",
    )(q3, kn, vn, ck, cv, bias_c, bias_n, sink3)


def _rwkv_scan_kernel(nb, chunk, r_ref, lw_ref, k_ref, v_ref, a_ref, b_ref, g_ref, s0_ref,
                      rk_ref, lg_ref, lb_ref, ones_ref, y_ref, sout_ref, s_scr):
    c = pl.program_id(1)

    @pl.when(c == 0)
    def _():
        s_scr[...] = s0_ref[...]

    ri = lax.broadcasted_iota(jnp.int32, (chunk, chunk), 0)
    ci = lax.broadcasted_iota(jnp.int32, (chunk, chunk), 1)
    incl = ri >= ci
    strict = ri > ci
    tri = incl.astype(BF)
    eye = (ri == ci).astype(F32)
    n_sq = int(math.log2(chunk)) - 1

    chains = [(bi, h) for bi in range(nb) for h in range(RWKV_HEADS)]
    hs = lambda h: slice(h * RWKV_HEAD, (h + 1) * RWKV_HEAD)

    ra, aa, bt, kt, bh, kh, vb, g_all = [], [], [], [], [], [], [], []
    for bi in range(nb):
        r, lw, k, v = r_ref[bi], lw_ref[bi], k_ref[bi], v_ref[bi]
        a, b = a_ref[bi], b_ref[bi]
        hi, lo = _split2(lw)
        lo2 = (lw - hi.astype(F32) - lo.astype(F32)).astype(BF)
        cum = _dot(tri, hi) + _dot(tri, lo) + _dot(tri, lo2)
        tail = cum[chunk - 1:chunk, :]
        g_inv = jnp.exp(-cum)
        g_tail = jnp.exp(tail - cum)
        g_all.append(jnp.exp(tail))
        ra.append((r * jnp.exp(cum)).astype(BF))
        aa.append((a * jnp.exp(cum - lw)).astype(BF))
        bt.append((b * g_inv).astype(BF))
        kt.append((k * g_inv).astype(BF))
        bh.append((b * g_tail).astype(BF))
        kh.append((k * g_tail).astype(BF))
        vb.append(v.astype(BF))

    s_old = [s_scr[bi, h] for bi, h in chains]
    ar = [jnp.concatenate([aa[bi][:, hs(h)], ra[bi][:, hs(h)]], axis=0) for bi, h in chains]
    v_h = [vb[bi][:, hs(h)] for bi, h in chains]
    gb = [_dot_nt(ar[n], bt[bi][:, hs(h)]) for n, (bi, h) in enumerate(chains)]
    gk = [_dot_nt(ar[n], kt[bi][:, hs(h)]) for n, (bi, h) in enumerate(chains)]
    p = [_dot_nt(ar[n], s_old[n].astype(BF)) for n in range(len(chains))]
    l_ab = [jnp.where(strict, x[:chunk], 0.0) for x in gb]
    l_ak = [jnp.where(strict, x[:chunk], 0.0).astype(BF) for x in gk]
    m_rb = [jnp.where(incl, x[chunk:], 0.0).astype(BF) for x in gb]
    m_rk = [jnp.where(incl, x[chunk:], 0.0).astype(BF) for x in gk]
    rhs = [p[n][:chunk] + _dot(l_ak[n], v_h[n]) for n in range(len(chains))]
    t_inv = [eye + x for x in l_ab]
    lp = l_ab
    for _ in range(n_sq):
        lpb = [x.astype(BF) for x in lp]
        lp = [_dot(x, x) for x in lpb]
        t_inv = [t + _dot(x.astype(BF), t.astype(BF)) for x, t in zip(lp, t_inv)]
    ub = [_dot(t.astype(BF), x.astype(BF)).astype(BF) for t, x in zip(t_inv, rhs)]
    y_h = [p[n][chunk:] + _dot(m_rb[n], ub[n]) + _dot(m_rk[n], v_h[n]) for n in range(len(chains))]
    s_new = [s_old[n] * g_all[bi][:, hs(h)] + _dot_tn(ub[n], bh[bi][:, hs(h)]) + _dot_tn(v_h[n], kh[bi][:, hs(h)])
             for n, (bi, h) in enumerate(chains)]
    for n, (bi, h) in enumerate(chains):
        s_scr[bi, h] = s_new[n]

    for bi in range(nb):
        y = jnp.concatenate(y_h[bi * RWKV_HEADS:(bi + 1) * RWKV_HEADS], axis=-1)
        y_ref[bi] = _rwkv_epilogue(y, r_ref[bi], k_ref[bi], v_ref[bi], g_ref[bi],
                                   rk_ref, lg_ref, lb_ref, ones_ref).astype(y_ref.dtype)

    @pl.when(c == pl.num_programs(1) - 1)
    def _():
        sout_ref[...] = s_scr[...]


def _rwkv_epilogue(y, r, k, v, g, rk_ref, lg_ref, lb_ref, ones_ref):
    ones = ones_ref[...]
    inv_n = 1.0 / RWKV_HEAD
    mean = _dot_exact_rhs(y, ones) * inv_n
    d = y - mean
    var = _dot_exact_rhs(d * d, ones) * inv_n
    yn = d * lax.rsqrt(var + GN_EPS) * lg_ref[...] + lb_ref[...]
    bonus = _dot_exact_rhs(r * k * rk_ref[...], ones) * v
    return (yn + bonus) * g


def _rwkv_scan(coef, s0, p, nb, chunk):
    r = coef[0]
    b, t, _ = r.shape
    seq_spec = pl.BlockSpec((nb, chunk, RWKV_W), lambda bi, c: (bi, c, 0))
    st_spec = pl.BlockSpec((nb, RWKV_HEADS, RWKV_HEAD, RWKV_HEAD), lambda bi, c: (bi, 0, 0, 0))
    vec = _const_spec((1, RWKV_W))
    return pl.pallas_call(
        functools.partial(_rwkv_scan_kernel, nb, chunk),
        grid=(b // nb, t // chunk),
        in_specs=[seq_spec] * 7 + [st_spec, vec, vec, vec, _const_spec((RWKV_W, RWKV_W))],
        out_specs=[seq_spec, st_spec],
        out_shape=[jax.ShapeDtypeStruct((b, t, RWKV_W), BF),
                   jax.ShapeDtypeStruct((b, RWKV_HEADS, RWKV_HEAD, RWKV_HEAD), F32)],
        scratch_shapes=[pltpu.VMEM((nb, RWKV_HEADS, RWKV_HEAD, RWKV_HEAD), F32)],
        compiler_params=_params("parallel", "arbitrary"),
        name="rwkv_scan",
    )(*coef, s0, p["r_k"], p["lnx_g"], p["lnx_b"], p["ones_bd"])


def _rwkv_step_kernel(r_ref, lw_ref, k_ref, v_ref, a_ref, b_ref, g_ref, s_ref,
                      rk_ref, lg_ref, lb_ref, ones_ref, y_ref, sout_ref, y_scr):
    n_pair = LANES // RWKV_HEAD
    r, k, v = r_ref[...], k_ref[...], v_ref[...]
    rT, kT, vT = r.T, k.T, v.T
    aT, bT, wT = a_ref[...].T, b_ref[...].T, jnp.exp(lw_ref[...]).T
    for hl in range(LANES // RWKV_HEAD):
        hsl = slice(hl * RWKV_HEAD, (hl + 1) * RWKV_HEAD)
        a_h, b_h, k_h, w_h, r_h = aT[hsl], bT[hsl], kT[hsl], wT[hsl], rT[hsl]
        tiles = [(hl * RWKV_HEAD + n_pair * t) * RWKV_HEAD for t in range(RWKV_HEAD // n_pair)]
        st = [s_ref[:, c0:c0 + LANES].T for c0 in tiles]
        new = []
        for t, x in enumerate(st):
            halves = []
            for il in range(n_pair):
                i = hl * RWKV_HEAD + n_pair * t + il
                slab = x[il * RWKV_HEAD:(il + 1) * RWKV_HEAD]
                sa = jnp.sum(slab * a_h, axis=0, keepdims=True)
                slab = slab * w_h + sa * b_h + vT[i:i + 1] * k_h
                y_scr[i:i + 1, :] = jnp.sum(slab * r_h, axis=0, keepdims=True)
                halves.append(slab)
            new.append(jnp.concatenate(halves, axis=0))
        for c0, x in zip(tiles, new):
            sout_ref[:, c0:c0 + LANES] = x.T
    y_ref[...] = _rwkv_epilogue(y_scr[...].T, r, k, v, g_ref[...],
                                rk_ref, lg_ref, lb_ref, ones_ref).astype(y_ref.dtype)


def _rwkv_step(coef, s0, p):
    b = s0.shape[0]
    per_pair = (LANES // RWKV_HEAD) * RWKV_HEAD * RWKV_HEAD
    n_steps = RWKV_W // LANES
    col = pl.BlockSpec((b, LANES), lambda i: (0, i))
    vec = pl.BlockSpec((1, LANES), lambda i: (0, i))
    st_spec = pl.BlockSpec((b, per_pair), lambda i: (0, i))
    y, s_new = pl.pallas_call(
        _rwkv_step_kernel,
        grid=(n_steps,),
        in_specs=[col] * 7 + [st_spec, vec, vec, vec, _const_spec((LANES, LANES))],
        out_specs=[col, st_spec],
        out_shape=[jax.ShapeDtypeStruct((b, RWKV_W), BF),
                   jax.ShapeDtypeStruct((b, n_steps * per_pair), F32)],
        scratch_shapes=[pltpu.VMEM((LANES, b), F32)],
        compiler_params=_params("parallel"),
        name="rwkv_step",
    )(*coef, s0.reshape(b, n_steps * per_pair), p["r_k"], p["lnx_g"], p["lnx_b"],
      p["ones_bd"][:LANES, :LANES])
    return y, s_new.reshape(s0.shape)


def _merge_kernel(with_router, x_ref, ya_ref, yb_ref, gate_ref, wa_ref, wb_ref, wo_ref, g_ref, *rest):
    if with_router:
        wr_ref, cnt_in_ref, x1_ref, h_ref, route_ref, cnt_out_ref, cnt_scr = rest
    else:
        x1_ref, h_ref = rest
    gate = gate_ref[...].astype(F32)
    pa = _dot(ya_ref[...], wa_ref[...])
    pb = _dot(yb_ref[...], wb_ref[...])
    merged = _sigmoid(gate[:, :D_MODEL]) * pa + _sigmoid(gate[:, D_MODEL:]) * pb
    x1 = x_ref[...] + _dot(merged.astype(BF), wo_ref[...])
    x1_ref[...] = x1
    h = _rms(x1, g_ref[...])
    h_ref[...] = h.astype(h_ref.dtype)
    if not with_router:
        return

    @pl.when(pl.program_id(0) == 0)
    def _():
        cnt_scr[...] = cnt_in_ref[...]

    logits = _dot(h.astype(BF), wr_ref[...])
    tm = logits.shape[0]
    lane = lax.broadcasted_iota(jnp.int32, logits.shape, 1).astype(F32)
    logits = jnp.where(lane < N_EXPERTS, logits, -jnp.inf)
    m1 = jnp.max(logits, axis=-1, keepdims=True)
    i1 = jnp.min(jnp.where(logits == m1, lane, float(LANES)), axis=-1, keepdims=True)
    rest_l = jnp.where(lane == i1, -jnp.inf, logits)
    m2 = jnp.max(rest_l, axis=-1, keepdims=True)
    i2 = jnp.min(jnp.where(rest_l == m2, lane, float(LANES)), axis=-1, keepdims=True)
    e2 = jnp.exp(m2 - m1)
    g1 = 1.0 / (1.0 + e2)
    g2 = e2 / (1.0 + e2)
    oh1 = (lane == i1).astype(F32)
    oh2 = (lane == i2).astype(F32)
    both = oh1 + oh2
    sub = min(tm, ROUTER_RANK_ROWS)
    ri = lax.broadcasted_iota(jnp.int32, (sub, sub), 0)
    ci = lax.broadcasted_iota(jnp.int32, (sub, sub), 1)
    earlier = (ri > ci).astype(BF)
    running = cnt_scr[...]
    before = []
    for c in range(tm // sub):
        part = both[c * sub:(c + 1) * sub]
        before.append(_dot(earlier, part.astype(BF)) + running)
        running = running + jnp.sum(part, axis=0, keepdims=True)
    before = jnp.concatenate(before, axis=0)
    rank1 = jnp.sum(oh1 * before, axis=-1, keepdims=True)
    rank2 = jnp.sum(oh2 * before, axis=-1, keepdims=True)
    cnt_scr[...] = running
    cnt_out_ref[...] = running
    route = jnp.zeros_like(logits)
    for n, val in enumerate((i1, i2, g1, g2, rank1, rank2)):
        route = jnp.where(lane == n, val, route)
    route_ref[...] = route


def _merge_out(x, ya, yb, gate, wa, wb, wo, g, tm, router=None, cnt_in=None):
    t = x.shape[0]
    row = lambda w_: pl.BlockSpec((tm, w_), lambda i: (i, 0))
    in_specs = [row(D_MODEL), row(ATT_W), row(RWKV_W), row(2 * D_MODEL),
                _const_spec((ATT_W, D_MODEL)), _const_spec((RWKV_W, D_MODEL)),
                _const_spec((D_MODEL, D_MODEL)), _const_spec((1, D_MODEL))]
    args = [x, ya, yb, gate, wa, wb, wo, g]
    if router is None:
        out_specs = [row(D_MODEL), row(D_MODEL)]
        out_shape = [jax.ShapeDtypeStruct((t, D_MODEL), F32), jax.ShapeDtypeStruct((t, D_MODEL), BF)]
        scratch = []
    else:
        in_specs += [_const_spec(router.shape), _const_spec((1, LANES))]
        args += [router, cnt_in]
        out_specs = [row(D_MODEL), row(D_MODEL), row(LANES), pl.BlockSpec((1, LANES), lambda i: (0, 0))]
        out_shape = [jax.ShapeDtypeStruct((t, D_MODEL), F32), jax.ShapeDtypeStruct((t, D_MODEL), F32),
                     jax.ShapeDtypeStruct((t, LANES), F32), jax.ShapeDtypeStruct((1, LANES), F32)]
        scratch = [pltpu.VMEM((1, LANES), F32)]
    return pl.pallas_call(
        functools.partial(_merge_kernel, router is not None),
        grid=(t // tm,),
        in_specs=in_specs, out_specs=out_specs, out_shape=out_shape, scratch_shapes=scratch,
        compiler_params=_params("parallel" if router is None else "arbitrary"),
        name="merge_out",
    )(*args)


def _swiglu(x, wg, wu, wd):
    s = _dot(x, wg)
    u = _dot(x, wu)
    act = (s * _sigmoid(s) * u).astype(BF)
    return _dot(act, wd)


def _dense_ffn_kernel(x1_ref, h_ref, wg_ref, wu_ref, wd_ref, o_ref):
    o_ref[...] = x1_ref[...] + _swiglu(h_ref[...], wg_ref[...], wu_ref[...], wd_ref[...])


def _dense_ffn(x1, h, wg, wu, wd, tm):
    t = x1.shape[0]
    d_ff = wg.shape[1]
    row = lambda: pl.BlockSpec((tm, D_MODEL), lambda i: (i, 0))
    return pl.pallas_call(
        _dense_ffn_kernel,
        grid=(t // tm,),
        in_specs=[row(), row(), _const_spec((D_MODEL, d_ff)), _const_spec((D_MODEL, d_ff)),
                  _const_spec((d_ff, D_MODEL))],
        out_specs=row(),
        out_shape=jax.ShapeDtypeStruct((t, D_MODEL), F32),
        compiler_params=_params("parallel"),
        name="dense_ffn",
    )(x1, h, wg, wu, wd)


def _dispatch_kernel(tt, first, dest_ref, fill_ref, h_ref, *rest):
    if first:
        xs_hbm, zbuf, sem, zsem = rest
    else:
        _, xs_hbm, sem = rest
    i = pl.program_id(0)
    base = i * tt

    if first:
        @pl.when(i == 0)
        def _():
            zbuf[...] = jnp.zeros_like(zbuf)
            bm = zbuf.shape[0]
            fills = [pltpu.make_async_copy(zbuf, xs_hbm.at[pl.ds(fill_ref[n] * bm, bm), :], zsem)
                     for n in range(fill_ref.shape[0])]
            for f in fills:
                f.start()
            for f in fills:
                f.wait()

    def issue(r, carry):
        for j in range(TOP_K):
            row = dest_ref[TOP_K * (base + r) + j]
            pltpu.make_async_copy(h_ref.at[pl.ds(r, 1), :], xs_hbm.at[pl.ds(row, 1), :], sem).start()
        return carry

    lax.fori_loop(0, tt, issue, 0, unroll=8)
    for j in range(TOP_K):
        pltpu.make_async_copy(h_ref, xs_hbm.at[pl.ds(0, tt), :], sem).wait()


def _dispatch(dest, fill_blocks, h, xs, tt, bm, n_rows):
    t = h.shape[0]
    first = xs is None
    in_specs = [pl.BlockSpec((tt, D_MODEL), lambda i, ds, fb: (i, 0))]
    args = [dest, fill_blocks, h]
    if first:
        scratch = [pltpu.VMEM((bm, D_MODEL), F32), pltpu.SemaphoreType.DMA(()), pltpu.SemaphoreType.DMA(())]
        aliases = {}
    else:
        in_specs.append(pl.BlockSpec(memory_space=pl.ANY))
        args.append(xs)
        scratch = [pltpu.SemaphoreType.DMA(())]
        aliases = {3: 0}
    grid_spec = pltpu.PrefetchScalarGridSpec(
        num_scalar_prefetch=2, grid=(t // tt,), in_specs=in_specs,
        out_specs=pl.BlockSpec(memory_space=pl.ANY), scratch_shapes=scratch)
    return pl.pallas_call(
        functools.partial(_dispatch_kernel, tt, first),
        grid_spec=grid_spec,
        out_shape=jax.ShapeDtypeStruct((n_rows, D_MODEL), F32),
        input_output_aliases=aliases,
        compiler_params=_params("arbitrary"),
        name="moe_dispatch",
    )(*args)


def _moe_kernel(n_half, blk_e_ref, n_used_ref, xs_ref, wg_ref, wu_ref, wd_ref, y_ref):
    @pl.when(pl.program_id(0) < n_used_ref[0])
    def _():
        x = xs_ref[...].astype(BF)
        d_e = wg_ref.shape[2]
        step = d_e // n_half
        acc = None
        for j in range(n_half):
            cs = slice(j * step, (j + 1) * step)
            part = _swiglu(x, wg_ref[0, :, cs], wu_ref[0, :, cs], wd_ref[0, cs, :])
            acc = part if acc is None else acc + part
        y_ref[...] = acc

    @pl.when(pl.program_id(0) >= n_used_ref[0])
    def _():
        y_ref[...] = jnp.zeros_like(y_ref)


def _moe_ffn(xs, blk_e, n_used, wg, wu, wd, bm):
    n_blk = blk_e.shape[0]
    n_rows = xs.shape[0]
    d_e = wg.shape[2]
    wspec = lambda shape: pl.BlockSpec(shape, lambda i, be, nu: (be[i], 0, 0), pipeline_mode=pl.Buffered(1))
    rows = pl.BlockSpec((bm, D_MODEL), lambda i, be, nu: (i, 0))
    grid_spec = pltpu.PrefetchScalarGridSpec(
        num_scalar_prefetch=2,
        grid=(n_blk,),
        in_specs=[rows, wspec((1, D_MODEL, d_e)), wspec((1, D_MODEL, d_e)), wspec((1, d_e, D_MODEL))],
        out_specs=rows,
    )
    return pl.pallas_call(
        functools.partial(_moe_kernel, 2),
        grid_spec=grid_spec,
        out_shape=jax.ShapeDtypeStruct((n_rows, D_MODEL), F32),
        compiler_params=_params("arbitrary"),
        name="moe_ffn",
    )(blk_e, n_used, xs, wg, wu, wd)


def _row_layout(counts, bm, n_blk):
    per_e = (counts + bm - 1) // bm
    ends = jnp.cumsum(per_e)
    n_used = ends[-1]
    first_row = (ends - per_e) * bm
    i = jnp.arange(n_blk, dtype=jnp.int32)
    blk_e = jnp.sum((jnp.minimum(i, jnp.maximum(n_used - 1, 0))[:, None] >= ends[None, :]).astype(jnp.int32), axis=1)
    need = jnp.any(i[:, None] == (ends - 1)[None, :], axis=1) | (i >= n_blk - N_EXPERTS)
    fill_blocks = jnp.argsort(jnp.logical_not(need), stable=True)[:2 * N_EXPERTS].astype(jnp.int32)
    return first_row.astype(jnp.int32), blk_e.astype(jnp.int32), n_used.astype(jnp.int32).reshape(1), fill_blocks


def _combine_kernel(tc, dest_ref, x1_ref, route_ref, y_hbm, g_ref, o_ref, ybuf, sem):
    base = pl.program_id(0) * tc

    def issue(r, carry):
        for j in range(TOP_K):
            row = dest_ref[TOP_K * (base + r) + j]
            pltpu.make_async_copy(y_hbm.at[pl.ds(row, 1), :], ybuf.at[j, pl.ds(r, 1), :], sem).start()
        return carry

    lax.fori_loop(0, tc, issue, 0, unroll=4)
    for j in range(TOP_K):
        pltpu.make_async_copy(y_hbm.at[pl.ds(0, tc), :], ybuf.at[j], sem).wait()
    route = route_ref[...]
    x2 = x1_ref[...] + route[:, 2:3] * ybuf[0] + route[:, 3:4] * ybuf[1]
    o_ref[...] = _rms(x2, g_ref[...])


def _moe_combine(dest, x1, route, y, g, tc):
    t = x1.shape[0]
    grid_spec = pltpu.PrefetchScalarGridSpec(
        num_scalar_prefetch=1,
        grid=(t // tc,),
        in_specs=[pl.BlockSpec((tc, D_MODEL), lambda i, ds: (i, 0)),
                  pl.BlockSpec((tc, LANES), lambda i, ds: (i, 0)),
                  pl.BlockSpec(memory_space=pl.ANY),
                  pl.BlockSpec((1, D_MODEL), lambda i, ds: (0, 0))],
        out_specs=pl.BlockSpec((tc, D_MODEL), lambda i, ds: (i, 0)),
        scratch_shapes=[pltpu.VMEM((TOP_K, tc, D_MODEL), F32), pltpu.SemaphoreType.DMA(())],
    )
    return pl.pallas_call(
        functools.partial(_combine_kernel, tc),
        grid_spec=grid_spec,
        out_shape=jax.ShapeDtypeStruct((t, D_MODEL), F32),
        compiler_params=_params("arbitrary"),
        name="moe_combine",
    )(dest, x1, route, y, g)


def _layer_params(l, w_in, mu_shift, w0, w2, a0, a2, g2, k_k, k_a, r_k, lnx_g, lnx_b,
                  w_proj_attn, w_proj_rwkv, w_out, ones_bd):
    row = lambda u: u.reshape(1, -1)
    return dict(
        w_in=w_in[l].astype(BF), mu=row(mu_shift[l]), w0=row(w0[l]), w2=w2[l].astype(BF),
        a0=row(a0[l]), a2=a2[l].astype(BF), g2=g2[l].astype(BF), k_k=row(k_k[l]), k_a=row(k_a[l]),
        r_k=row(r_k[l]), lnx_g=row(lnx_g[l]), lnx_b=row(lnx_b[l]),
        wa=w_proj_attn[l].astype(BF), wb=w_proj_rwkv[l].astype(BF), wo=w_out[l].astype(BF),
        ones_bd=ones_bd)


def kernel(x_prompt, x_sample, cache_k, cache_v, state_wkv, state_shift, norm_mix_g, w_in, attn_sinks, rel_bias, mu_shift, w0, w2, a0, a2, g2, k_k, k_a, r_k, lnx_g, lnx_b, w_proj_attn, w_proj_rwkv, w_out, norm_ffn_g, dense_w_gate, dense_w_up, dense_w_down, router_w, moe_w_gate, moe_w_up, moe_w_down, norm_final_g):
    batch, seq, _ = x_prompt.shape
    dec_batch, dec_seq, _ = x_sample.shape
    assert dec_seq == 1 and seq % WINDOW == 0

    head_id = jnp.arange(RWKV_W, dtype=jnp.int32) // RWKV_HEAD
    ones_bd = (head_id[:, None] == head_id[None, :]).astype(BF)
    bias_by_dist = _bias_lookup(rel_bias, _t5_bucket(np.arange(WINDOW + 1)))
    bias_c = jnp.transpose(bias_by_dist[WINDOW:0:-1], (1, 0))[None]
    bias_n = bias_by_dist[0][None, :, None]

    layers = [_layer_params(l, w_in, mu_shift, w0, w2, a0, a2, g2, k_k, k_a, r_k, lnx_g, lnx_b,
                            w_proj_attn, w_proj_rwkv, w_out, ones_bd) for l in range(DEPTH)]

    def router_pieces(l):
        return jnp.pad(router_w[l // 2], ((0, 0), (0, LANES - N_EXPERTS))).astype(BF)

    n_prompt = batch * seq
    bm = MOE_BLOCK_ROWS
    n_all = n_prompt + dec_batch
    n_blk = -(-(n_all * TOP_K) // bm) + N_EXPERTS

    def mix(x, is_prompt, l, state):
        p = layers[l]
        n_tok = x.shape[0]
        gm = norm_mix_g[l].reshape(1, -1)
        sinks = attn_sinks[l]
        if is_prompt:
            b, t = batch, seq
            prev_z = jnp.zeros((b, 1, RWKV_IN), F32)
            q, k, v, gate, z_last, *coef = _inproj(x, gm, p, prev_z, min(INPROJ_ROWS, t), t)
            sink_col = jnp.repeat(sinks, WINDOW).reshape(N_KV_HEADS, GQA_GROUP * WINDOW, 1)
            ya = _swa_prompt(q.reshape(b, t, ATT_W), k.reshape(b, t, KV_W), v.reshape(b, t, KV_W),
                             rel_bias, sink_col).reshape(n_tok, ATT_W)
            k4 = k.reshape(b, t, N_KV_HEADS, HEAD_DIM)
            v4 = v.reshape(b, t, N_KV_HEADS, HEAD_DIM)
            state["k"].append(k4[:, t - WINDOW:])
            state["v"].append(v4[:, t - WINDOW:])
            s0 = jnp.zeros((b, RWKV_HEADS, RWKV_HEAD, RWKV_HEAD), F32)
            yb3, s_new = _rwkv_scan([u.reshape(b, t, RWKV_W) for u in coef], s0, p, b, RWKV_CHUNK)
            yb = yb3.reshape(n_tok, RWKV_W)
            state["z"].append(z_last.reshape(b, RWKV_IN))
        else:
            b = dec_batch
            q, k, v, gate, z, *coef = _inproj(x, gm, p, state_shift[l], n_tok, None)
            ck = cache_k[l].reshape(b, WINDOW, KV_W)
            cv = cache_v[l].reshape(b, WINDOW, KV_W)
            ya3 = _swa_sample(q.reshape(b, N_Q_HEADS, HEAD_DIM), k, v, ck, cv, bias_c, bias_n,
                              sinks.reshape(1, N_Q_HEADS, 1), min(SAMPLE_ATTN_SEQS, b))
            ya = ya3.reshape(b, ATT_W)
            state["k"].append(jnp.concatenate([ck[:, 1:], k[:, None]], axis=1)
                              .reshape(b, WINDOW, N_KV_HEADS, HEAD_DIM))
            state["v"].append(jnp.concatenate([cv[:, 1:], v[:, None]], axis=1)
                              .reshape(b, WINDOW, N_KV_HEADS, HEAD_DIM))
            yb, s_new = _rwkv_step(coef, state_wkv[l], p)
            state["z"].append(z)
        state["s"].append(s_new)
        return ya, yb, gate

    groups = [dict(x=x_prompt.reshape(n_prompt, D_MODEL), prompt=True, tm=min(DENSE_ROWS, n_prompt), k=[], v=[], s=[], z=[]),
              dict(x=x_sample.reshape(dec_batch, D_MODEL), prompt=False, tm=min(DENSE_ROWS, dec_batch), k=[], v=[], s=[], z=[])]
    for l in range(DEPTH):
        p = layers[l]
        gn = norm_ffn_g[l].reshape(1, -1)
        j = l // 2
        if l % 2 == 0:
            wg, wu, wd = (_to_bf16(w[j], w.shape[1] // 4) for w in (dense_w_gate, dense_w_up, dense_w_down))
            for grp in groups:
                ya, yb, gate = mix(grp["x"], grp["prompt"], l, grp)
                x1, h = _merge_out(grp["x"], ya, yb, gate, p["wa"], p["wb"], p["wo"], gn, grp["tm"])
                grp["x"] = _dense_ffn(x1, h, wg, wu, wd, grp["tm"])
        else:
            assert l == DEPTH - 1
            wg, wu, wd = (w[j].astype(BF) for w in (moe_w_gate, moe_w_up, moe_w_down))
            router = router_pieces(l)
            cnt = jnp.zeros((1, LANES), F32)
            xs = None
            for grp in groups:
                ya, yb, gate = mix(grp["x"], grp["prompt"], l, grp)
                x1, h, route, cnt = _merge_out(grp["x"], ya, yb, gate, p["wa"], p["wb"], p["wo"], gn,
                                               min(ROUTER_ROWS, grp["tm"]), router=router, cnt_in=cnt)
                grp["x1"], grp["route"], grp["h"] = x1, route, h
            counts = cnt[0, :N_EXPERTS].astype(jnp.int32)
            first_row, blk_e, n_used, fill_blocks = _row_layout(counts, bm, n_blk)
            for grp in groups:
                route = grp["route"]
                expert = route[:, 0:TOP_K].astype(jnp.int32)
                rank = route[:, 2 * TOP_K:3 * TOP_K].astype(jnp.int32)
                start = jnp.sum(jnp.where(expert[..., None] == jnp.arange(N_EXPERTS, dtype=jnp.int32),
                                          first_row, 0), axis=-1)
                grp["dest"] = (start + rank).reshape(-1)
                xs = _dispatch(grp["dest"], fill_blocks, grp["h"], xs, min(DISPATCH_ROWS, grp["h"].shape[0]), bm,
                               n_blk * bm)
            y = _moe_ffn(xs, blk_e, n_used, wg, wu, wd, bm)
            for grp in groups:
                grp["x"] = _moe_combine(grp["dest"], grp["x1"], grp["route"], y,
                                        norm_final_g.reshape(1, -1), min(COMBINE_ROWS, grp["x1"].shape[0]))

    gp, gs = groups
    outs = []
    for grp, shape in ((gp, (batch, seq, D_MODEL)), (gs, (dec_batch, dec_seq, D_MODEL))):
        outs.append((grp["x"].reshape(shape), jnp.stack(grp["k"]), jnp.stack(grp["v"]),
                     jnp.stack(grp["s"]), jnp.stack(grp["z"])))
    (y_p, nk_p, nv_p, ns_p, nz_p), (y_s, nk_s, nv_s, ns_s, nz_s) = outs
    return (y_p, y_s, nk_p, nv_p, ns_p, nz_p, nk_s, nv_s, ns_s, nz_s)
```

```python
import functools
import math

import jax
import jax.numpy as jnp
import numpy as np
from jax import lax
from jax.experimental import pallas as pl
from jax.experimental.pallas import tpu as pltpu

BF = jnp.bfloat16
F32 = jnp.float32

D_MODEL = 1024
DEPTH = 2
HEAD_DIM = 64
N_Q_HEADS = 8
N_KV_HEADS = 2
GQA_GROUP = 4
ATT_W = 512
KV_W = 128
WINDOW = 128
ATT_SCALE = HEAD_DIM ** -0.5
NUM_BUCKETS = 32
MAX_EXACT = 16
MAX_DISTANCE = 128
NEG_INF = -1e30
RWKV_HEAD = 64
RWKV_W = 512
RWKV_HEADS = 8
D_DECAY_LORA = 64
D_AAA_LORA = 64
D_GATE_LORA = 128
RWKV_IN = 3 * RWKV_W + D_DECAY_LORA + D_AAA_LORA + D_GATE_LORA
GN_EPS = 64e-5
IN_W = ATT_W + 2 * KV_W + RWKV_IN + 2 * D_MODEL
N_EXPERTS = 8
TOP_K = 2
NORM_EPS = 1e-6

VMEM_LIMIT_BYTES = 56 * 1024 * 1024
LANES = 128

INPROJ_ROWS = 512
DENSE_ROWS = 512
ROUTER_ROWS = 512
ROUTER_RANK_ROWS = 256
MOE_BLOCK_ROWS = 512
DISPATCH_ROWS = 2048
COMBINE_ROWS = 1024
RWKV_CHUNK = 64
SAMPLE_ATTN_SEQS = 16


def _params(*sem):
    return pltpu.CompilerParams(dimension_semantics=sem, vmem_limit_bytes=VMEM_LIMIT_BYTES)


def _const_spec(shape):
    zeros = (0,) * len(shape)
    return pl.BlockSpec(shape, lambda *_: zeros, pipeline_mode=pl.Buffered(1))


def _dot(a, b):
    return jnp.dot(a, b, preferred_element_type=F32)


def _dot_nt(a, b):
    return lax.dot_general(a, b, (((1,), (1,)), ((), ())), preferred_element_type=F32)


def _dot_tn(a, b):
    return lax.dot_general(a, b, (((0,), (0,)), ((), ())), preferred_element_type=F32)


def _split2(x):
    hi = x.astype(BF)
    lo = (x - hi.astype(F32)).astype(BF)
    return hi, lo


def _dot_exact_rhs(x, w):
    hi, lo = _split2(x)
    return _dot(hi, w) + _dot(lo, w)


def _rms(x, g):
    ms = jnp.mean(x * x, axis=-1, keepdims=True)
    return x * lax.rsqrt(ms + NORM_EPS) * g


def _sigmoid(x):
    return 1.0 / (1.0 + jnp.exp(-x))


def _cast_kernel(x_ref, o_ref):
    o_ref[...] = x_ref[...].astype(o_ref.dtype)


def _to_bf16(w, row_tile):
    rows, cols = w.shape
    spec = pl.BlockSpec((row_tile, cols), lambda i: (i, 0))
    return pl.pallas_call(
        _cast_kernel, grid=(rows // row_tile,), in_specs=[spec], out_specs=spec,
        out_shape=jax.ShapeDtypeStruct((rows, cols), BF),
        compiler_params=_params("parallel"), name="to_bf16",
    )(w)


def _inproj_kernel(rows_per_seq, x_ref, g_ref, w_ref, pz_ref, mu_ref, w0_ref, w2_ref, a0_ref, a2_ref, g2_ref,
                   kk_ref, ka_ref, ones_ref,
                   q_ref, k_ref, v_ref, gate_ref, zl_ref,
                   r_ref, lw_ref, k2_ref, vr_ref, a_ref, b_ref, gr_ref, *carry):
    n = _rms(x_ref[...], g_ref[...]).astype(BF)
    z0 = ATT_W + 2 * KV_W
    z = _dot(n, w_ref[:, z0:z0 + RWKV_IN])
    q_ref[...] = _dot(n, w_ref[:, 0:ATT_W]).astype(BF)
    k_ref[...] = _dot(n, w_ref[:, ATT_W:ATT_W + KV_W])
    v_ref[...] = _dot(n, w_ref[:, ATT_W + KV_W:ATT_W + 2 * KV_W])
    gate_ref[...] = _dot(n, w_ref[:, z0 + RWKV_IN:IN_W]).astype(gate_ref.dtype)
    tm = z.shape[0]
    if rows_per_seq is None:
        zs = pz_ref[...]
        zl_ref[...] = z
    else:
        prev_ref, = carry
        tiles_per_seq = rows_per_seq // tm

        @pl.when(pl.program_id(0) % tiles_per_seq == 0)
        def _():
            prev_ref[...] = pz_ref[0]

        row = lax.broadcasted_iota(jnp.int32, (tm, 1), 0)
        zs = jnp.where(row == 0, prev_ref[...], pltpu.roll(z, 1, 0))
        prev_ref[...] = z[tm - 1:tm, :]
        zl_ref[0] = z[tm - 1:tm, :]
    zz = z + (zs - z) * mu_ref[...]
    r = zz[:, 0:RWKV_W]
    k = zz[:, RWKV_W:2 * RWKV_W]
    v = zz[:, 2 * RWKV_W:3 * RWKV_W]
    o = 3 * RWKV_W
    wl = zz[:, o:o + D_DECAY_LORA]
    al = zz[:, o + D_DECAY_LORA:o + D_DECAY_LORA + D_AAA_LORA]
    gl = zz[:, o + D_DECAY_LORA + D_AAA_LORA:RWKV_IN]
    lw = -math.exp(-0.5) * _sigmoid(w0_ref[...] + _dot(jnp.tanh(wl).astype(BF), w2_ref[...]))
    a = _sigmoid(a0_ref[...] + _dot(al.astype(BF), a2_ref[...]))
    g = _dot(_sigmoid(gl).astype(BF), g2_ref[...])
    kkv = k * kk_ref[...]
    ss = _dot_exact_rhs(kkv * kkv, ones_ref[...])
    kk = kkv * lax.rsqrt(jnp.maximum(ss, 1e-24))
    k2 = k * (1.0 + (a - 1.0) * ka_ref[...])
    r_ref[...], lw_ref[...], k2_ref[...], vr_ref[...] = r, lw, k2, v
    a_ref[...], b_ref[...], gr_ref[...] = -kk, kk * a, g


def _inproj(x, g, p, prev_z, tm, rows_per_seq):
    t = x.shape[0]
    row = lambda w_: pl.BlockSpec((tm, w_), lambda i: (i, 0))
    vec = lambda w_: _const_spec((1, w_))
    if rows_per_seq is None:
        pz_spec = row(RWKV_IN)
        zl_spec, zl_shape = row(RWKV_IN), (t, RWKV_IN)
        scratch = []
    else:
        tiles_per_seq = rows_per_seq // tm
        pz_spec = pl.BlockSpec((1, 1, RWKV_IN), lambda i: (i // tiles_per_seq, 0, 0))
        zl_spec, zl_shape = pz_spec, (t // rows_per_seq, 1, RWKV_IN)
        scratch = [pltpu.VMEM((1, RWKV_IN), F32)]
    return pl.pallas_call(
        functools.partial(_inproj_kernel, rows_per_seq),
        grid=(t // tm,),
        in_specs=[row(D_MODEL), vec(D_MODEL), _const_spec((D_MODEL, IN_W)), pz_spec,
                  vec(RWKV_IN), vec(RWKV_W), _const_spec((D_DECAY_LORA, RWKV_W)),
                  vec(RWKV_W), _const_spec((D_AAA_LORA, RWKV_W)), _const_spec((D_GATE_LORA, RWKV_W)),
                  vec(RWKV_W), vec(RWKV_W), _const_spec((RWKV_W, RWKV_W))],
        out_specs=[row(ATT_W), row(KV_W), row(KV_W), row(2 * D_MODEL), zl_spec] + [row(RWKV_W)] * 7,
        out_shape=[jax.ShapeDtypeStruct((t, ATT_W), BF),
                   jax.ShapeDtypeStruct((t, KV_W), F32),
                   jax.ShapeDtypeStruct((t, KV_W), F32),
                   jax.ShapeDtypeStruct((t, 2 * D_MODEL), BF),
                   jax.ShapeDtypeStruct(zl_shape, F32)] + [jax.ShapeDtypeStruct((t, RWKV_W), F32)] * 7,
        scratch_shapes=scratch,
        compiler_params=_params("arbitrary"),
        name="inproj",
    )(x, g, p["w_in"], prev_z, p["mu"], p["w0"], p["w2"], p["a0"], p["a2"], p["g2"], p["k_k"], p["k_a"],
      p["ones_bd"])


def _t5_bucket(dist):
    n = np.maximum(dist, 0)
    nf = np.maximum(n, 1).astype(np.float32)
    large = MAX_EXACT + (np.log(nf / MAX_EXACT) / math.log(MAX_DISTANCE / MAX_EXACT)
                         * (NUM_BUCKETS - MAX_EXACT)).astype(np.int32)
    return np.where(n < MAX_EXACT, n, np.minimum(large, NUM_BUCKETS - 1)).astype(np.int32)


def _bias_lookup(rel_bias, bucket):
    hit = bucket[..., None, None] == np.arange(NUM_BUCKETS, dtype=np.int32)[:, None]
    return jnp.sum(jnp.where(hit, rel_bias, 0.0), axis=-2)


def _prompt_bucket_table():
    dist = np.arange(WINDOW)[:, None] - (np.arange(2 * WINDOW) - WINDOW)[None, :]
    return np.where((dist >= 0) & (dist <= WINDOW), _t5_bucket(dist), -1).astype(np.int32)


def _swa_prompt_kernel(nb, bucket_ref, relb_ref, q_ref, kp_ref, kc_ref, vp_ref, vc_ref, sink_ref, o_ref, bias_ref):
    first = pl.program_id(0) == 0

    @pl.when(first)
    def _():
        bucket = bucket_ref[...]
        for head in range(N_Q_HEADS):
            tab = jnp.full(bucket.shape, NEG_INF, F32)
            for n in range(NUM_BUCKETS):
                tab = jnp.where(bucket == n, relb_ref[n, head], tab)
            g = head % GQA_GROUP
            bias_ref[head // GQA_GROUP, g * WINDOW:(g + 1) * WINDOW, :] = tab

    col = lax.broadcasted_iota(jnp.int32, (GQA_GROUP * WINDOW, 2 * WINDOW), 1)
    pad_mask = jnp.logical_and(first, col < WINDOW)
    chains = [(b, h) for b in range(nb) for h in range(N_KV_HEADS)]
    hs = lambda h: slice(h * HEAD_DIM, (h + 1) * HEAD_DIM)
    kh = [jnp.concatenate([kp_ref[b][:, hs(h)], kc_ref[b][:, hs(h)]], axis=0).astype(BF) for b, h in chains]
    vh = [jnp.concatenate([vp_ref[b][:, hs(h)], vc_ref[b][:, hs(h)]], axis=0).astype(BF) for b, h in chains]
    qh = [jnp.concatenate([q_ref[b][:, hs(GQA_GROUP * h + g)] for g in range(GQA_GROUP)], axis=0)
          for b, h in chains]
    s = [_dot_nt(qh[n], kh[n]) * ATT_SCALE + bias_ref[h] for n, (b, h) in enumerate(chains)]
    s = [jnp.where(pad_mask, NEG_INF, x) for x in s]
    m = [jnp.maximum(jnp.max(x, axis=-1, keepdims=True), sink_ref[h]) for x, (b, h) in zip(s, chains)]
    p = [jnp.exp(x - mm) for x, mm in zip(s, m)]
    denom = [jnp.sum(x, axis=-1, keepdims=True) + jnp.exp(sink_ref[h] - mm)
             for x, mm, (b, h) in zip(p, m, chains)]
    o = [_dot(x.astype(BF), vv) / d for x, vv, d in zip(p, vh, denom)]
    for b in range(nb):
        pieces = [o[b * N_KV_HEADS + h][g * WINDOW:(g + 1) * WINDOW]
                  for h in range(N_KV_HEADS) for g in range(GQA_GROUP)]
        o_ref[b] = jnp.concatenate(pieces, axis=-1).astype(o_ref.dtype)


def _swa_prompt(q3, k3, v3, rel_bias, sink_col):
    batch, seq, _ = q3.shape
    cur = lambda i: (0, i, 0)
    prev = lambda i: (0, jnp.maximum(i - 1, 0), 0)
    kv_c = pl.BlockSpec((batch, WINDOW, KV_W), cur)
    kv_p = pl.BlockSpec((batch, WINDOW, KV_W), prev)
    bucket = jnp.asarray(_prompt_bucket_table())
    return pl.pallas_call(
        functools.partial(_swa_prompt_kernel, batch),
        grid=(seq // WINDOW,),
        in_specs=[_const_spec(bucket.shape), pl.BlockSpec(memory_space=pltpu.SMEM),
                  pl.BlockSpec((batch, WINDOW, ATT_W), cur), kv_p, kv_c, kv_p, kv_c,
                  _const_spec(sink_col.shape)],
        out_specs=pl.BlockSpec((batch, WINDOW, ATT_W), cur),
        out_shape=jax.ShapeDtypeStruct((batch, seq, ATT_W), BF),
        scratch_shapes=[pltpu.VMEM((N_KV_HEADS, GQA_GROUP * WINDOW, 2 * WINDOW), F32)],
        compiler_params=_params("arbitrary"),
        name="swa_prompt",
    )(bucket, rel_bias, q3, k3, k3, v3, v3, sink_col)


def _swa_sample_kernel(q_ref, kn_ref, vn_ref, ck_ref, cv_ref, bc_ref, bn_ref, sink_ref, o_ref):
    q = q_ref[...]
    qf = q.astype(F32)
    kn = kn_ref[...].astype(BF).astype(F32)
    vn = vn_ref[...].astype(BF).astype(F32)
    ck, cv = ck_ref[...], cv_ref[...]
    head = lax.broadcasted_iota(jnp.int32, (1, N_Q_HEADS, 1), 1)
    low = head < GQA_GROUP
    s_h, sn_h = [], []
    for h in range(N_KV_HEADS):
        sl = slice(h * HEAD_DIM, (h + 1) * HEAD_DIM)
        s_h.append(jnp.einsum("bgd,bkd->bgk", q, ck[:, :, sl].astype(BF), preferred_element_type=F32))
        sn_h.append(jnp.sum(qf * kn[:, None, sl], axis=-1, keepdims=True))
    s = jnp.where(low, s_h[0], s_h[1]) * ATT_SCALE + bc_ref[...]
    sn = jnp.where(low, sn_h[0], sn_h[1]) * ATT_SCALE + bn_ref[...]
    sink = sink_ref[...]
    m = jnp.maximum(jnp.maximum(jnp.max(s, axis=-1, keepdims=True), sn), sink)
    p = jnp.exp(s - m)
    pn = jnp.exp(sn - m)
    denom = jnp.sum(p, axis=-1, keepdims=True) + pn + jnp.exp(sink - m)
    pb = p.astype(BF)
    pnb = pn.astype(BF).astype(F32)
    o_h = []
    for h in range(N_KV_HEADS):
        sl = slice(h * HEAD_DIM, (h + 1) * HEAD_DIM)
        o = jnp.einsum("bgk,bkd->bgd", pb, cv[:, :, sl].astype(BF), preferred_element_type=F32)
        o_h.append(o + pnb * vn[:, None, sl])
    o_ref[...] = (jnp.where(low, o_h[0], o_h[1]) / denom).astype(o_ref.dtype)


def _swa_sample(q3, kn, vn, ck, cv, bias_c, bias_n, sink3, bb):
    b = q3.shape[0]
    return pl.pallas_call(
        _swa_sample_kernel,
        grid=(b // bb,),
        in_specs=[pl.BlockSpec((bb, N_Q_HEADS, HEAD_DIM), lambda i: (i, 0, 0)),
                  pl.BlockSpec((bb, KV_W), lambda i: (i, 0)),
                  pl.BlockSpec((bb, KV_W), lambda i: (i, 0)),
                  pl.BlockSpec((bb, WINDOW, KV_W), lambda i: (i, 0, 0)),
                  pl.BlockSpec((bb, WINDOW, KV_W), lambda i: (i, 0, 0)),
                  _const_spec(bias_c.shape), _const_spec(bias_n.shape), _const_spec(sink3.shape)],
        out_specs=pl.BlockSpec((bb, N_Q_HEADS, HEAD_DIM), lambda i: (i, 0, 0)),
        out_shape=jax.ShapeDtypeStruct((b, N_Q_HEADS, HEAD_DIM), BF),
        compiler_params=_params("parallel"),
        name="swa_sample",
    )(q3, kn, vn, ck, cv, bias_c, bias_n, sink3)


def _rwkv_scan_kernel(nb, chunk, r_ref, lw_ref, k_ref, v_ref, a_ref, b_ref, g_ref, s0_ref,
                      rk_ref, lg_ref, lb_ref, ones_ref, y_ref, sout_ref, s_scr):
    c = pl.program_id(1)

    @pl.when(c == 0)
    def _():
        s_scr[...] = s0_ref[...]

    ri = lax.broadcasted_iota(jnp.int32, (chunk, chunk), 0)
    ci = lax.broadcasted_iota(jnp.int32, (chunk, chunk), 1)
    incl = ri >= ci
    strict = ri > ci
    tri = incl.astype(BF)
    eye = (ri == ci).astype(F32)
    n_sq = int(math.log2(chunk)) - 1

    chains = [(bi, h) for bi in range(nb) for h in range(RWKV_HEADS)]
    hs = lambda h: slice(h * RWKV_HEAD, (h + 1) * RWKV_HEAD)

    ra, aa, bt, kt, bh, kh, vb, g_all = [], [], [], [], [], [], [], []
    for bi in range(nb):
        r, lw, k, v = r_ref[bi], lw_ref[bi], k_ref[bi], v_ref[bi]
        a, b = a_ref[bi], b_ref[bi]
        hi, lo = _split2(lw)
        lo2 = (lw - hi.astype(F32) - lo.astype(F32)).astype(BF)
        cum = _dot(tri, hi) + _dot(tri, lo) + _dot(tri, lo2)
        tail = cum[chunk - 1:chunk, :]
        g_inv = jnp.exp(-cum)
        g_tail = jnp.exp(tail - cum)
        g_all.append(jnp.exp(tail))
        ra.append((r * jnp.exp(cum)).astype(BF))
        aa.append((a * jnp.exp(cum - lw)).astype(BF))
        bt.append((b * g_inv).astype(BF))
        kt.append((k * g_inv).astype(BF))
        bh.append((b * g_tail).astype(BF))
        kh.append((k * g_tail).astype(BF))
        vb.append(v.astype(BF))

    s_old = [s_scr[bi, h] for bi, h in chains]
    ar = [jnp.concatenate([aa[bi][:, hs(h)], ra[bi][:, hs(h)]], axis=0) for bi, h in chains]
    v_h = [vb[bi][:, hs(h)] for bi, h in chains]
    gb = [_dot_nt(ar[n], bt[bi][:, hs(h)]) for n, (bi, h) in enumerate(chains)]
    gk = [_dot_nt(ar[n], kt[bi][:, hs(h)]) for n, (bi, h) in enumerate(chains)]
    p = [_dot_nt(ar[n], s_old[n].astype(BF)) for n in range(len(chains))]
    l_ab = [jnp.where(strict, x[:chunk], 0.0) for x in gb]
    l_ak = [jnp.where(strict, x[:chunk], 0.0).astype(BF) for x in gk]
    m_rb = [jnp.where(incl, x[chunk:], 0.0).astype(BF) for x in gb]
    m_rk = [jnp.where(incl, x[chunk:], 0.0).astype(BF) for x in gk]
    rhs = [p[n][:chunk] + _dot(l_ak[n], v_h[n]) for n in range(len(chains))]
    t_inv = [eye + x for x in l_ab]
    lp = l_ab
    for _ in range(n_sq):
        lpb = [x.astype(BF) for x in lp]
        lp = [_dot(x, x) for x in lpb]
        t_inv = [t + _dot(x.astype(BF), t.astype(BF)) for x, t in zip(lp, t_inv)]
    ub = [_dot(t.astype(BF), x.astype(BF)).astype(BF) for t, x in zip(t_inv, rhs)]
    y_h = [p[n][chunk:] + _dot(m_rb[n], ub[n]) + _dot(m_rk[n], v_h[n]) for n in range(len(chains))]
    s_new = [s_old[n] * g_all[bi][:, hs(h)] + _dot_tn(ub[n], bh[bi][:, hs(h)]) + _dot_tn(v_h[n], kh[bi][:, hs(h)])
             for n, (bi, h) in enumerate(chains)]
    for n, (bi, h) in enumerate(chains):
        s_scr[bi, h] = s_new[n]

    for bi in range(nb):
        y = jnp.concatenate(y_h[bi * RWKV_HEADS:(bi + 1) * RWKV_HEADS], axis=-1)
        y_ref[bi] = _rwkv_epilogue(y, r_ref[bi], k_ref[bi], v_ref[bi], g_ref[bi],
                                   rk_ref, lg_ref, lb_ref, ones_ref).astype(y_ref.dtype)

    @pl.when(c == pl.num_programs(1) - 1)
    def _():
        sout_ref[...] = s_scr[...]


def _rwkv_epilogue(y, r, k, v, g, rk_ref, lg_ref, lb_ref, ones_ref):
    ones = ones_ref[...]
    inv_n = 1.0 / RWKV_HEAD
    n = y.shape[0]
    pieces = _split2(y) + _split2(r * k * rk_ref[...])
    sums = _dot(jnp.concatenate(pieces, axis=0), ones)
    mean = (sums[0:n] + sums[n:2 * n]) * inv_n
    bonus = (sums[2 * n:3 * n] + sums[3 * n:4 * n]) * v
    d = y - mean
    sq = _dot(jnp.concatenate(_split2(d * d), axis=0), ones)
    var = (sq[0:n] + sq[n:2 * n]) * inv_n
    yn = d * lax.rsqrt(var + GN_EPS) * lg_ref[...] + lb_ref[...]
    return (yn + bonus) * g


def _rwkv_scan(coef, s0, p, nb, chunk):
    r = coef[0]
    b, t, _ = r.shape
    seq_spec = pl.BlockSpec((nb, chunk, RWKV_W), lambda bi, c: (bi, c, 0))
    st_spec = pl.BlockSpec((nb, RWKV_HEADS, RWKV_HEAD, RWKV_HEAD), lambda bi, c: (bi, 0, 0, 0))
    vec = _const_spec((1, RWKV_W))
    return pl.pallas_call(
        functools.partial(_rwkv_scan_kernel, nb, chunk),
        grid=(b // nb, t // chunk),
        in_specs=[seq_spec] * 7 + [st_spec, vec, vec, vec, _const_spec((RWKV_W, RWKV_W))],
        out_specs=[seq_spec, st_spec],
        out_shape=[jax.ShapeDtypeStruct((b, t, RWKV_W), BF),
                   jax.ShapeDtypeStruct((b, RWKV_HEADS, RWKV_HEAD, RWKV_HEAD), F32)],
        scratch_shapes=[pltpu.VMEM((nb, RWKV_HEADS, RWKV_HEAD, RWKV_HEAD), F32)],
        compiler_params=_params("parallel", "arbitrary"),
        name="rwkv_scan",
    )(*coef, s0, p["r_k"], p["lnx_g"], p["lnx_b"], p["ones_bd"])


def _rwkv_step_kernel(r_ref, lw_ref, k_ref, v_ref, a_ref, b_ref, g_ref, s_ref,
                      rk_ref, lg_ref, lb_ref, ones_ref, y_ref, sout_ref, y_scr):
    n_pair = LANES // RWKV_HEAD
    r, k, v = r_ref[...], k_ref[...], v_ref[...]
    rT, kT, vT = r.T, k.T, v.T
    aT, bT, wT = a_ref[...].T, b_ref[...].T, jnp.exp(lw_ref[...]).T
    for hl in range(LANES // RWKV_HEAD):
        hsl = slice(hl * RWKV_HEAD, (hl + 1) * RWKV_HEAD)
        a_h, b_h, k_h, w_h, r_h = aT[hsl], bT[hsl], kT[hsl], wT[hsl], rT[hsl]
        tiles = [(hl * RWKV_HEAD + n_pair * t) * RWKV_HEAD for t in range(RWKV_HEAD // n_pair)]
        st = [s_ref[:, c0:c0 + LANES].T for c0 in tiles]
        new = []
        for t, x in enumerate(st):
            halves = []
            for il in range(n_pair):
                i = hl * RWKV_HEAD + n_pair * t + il
                slab = x[il * RWKV_HEAD:(il + 1) * RWKV_HEAD]
                sa = jnp.sum(slab * a_h, axis=0, keepdims=True)
                slab = slab * w_h + sa * b_h + vT[i:i + 1] * k_h
                y_scr[i:i + 1, :] = jnp.sum(slab * r_h, axis=0, keepdims=True)
                halves.append(slab)
            new.append(jnp.concatenate(halves, axis=0))
        for c0, x in zip(tiles, new):
            sout_ref[:, c0:c0 + LANES] = x.T
    y_ref[...] = _rwkv_epilogue(y_scr[...].T, r, k, v, g_ref[...],
                                rk_ref, lg_ref, lb_ref, ones_ref).astype(y_ref.dtype)


def _rwkv_step(coef, s0, p):
    b = s0.shape[0]
    per_pair = (LANES // RWKV_HEAD) * RWKV_HEAD * RWKV_HEAD
    n_steps = RWKV_W // LANES
    col = pl.BlockSpec((b, LANES), lambda i: (0, i))
    vec = pl.BlockSpec((1, LANES), lambda i: (0, i))
    st_spec = pl.BlockSpec((b, per_pair), lambda i: (0, i))
    y, s_new = pl.pallas_call(
        _rwkv_step_kernel,
        grid=(n_steps,),
        in_specs=[col] * 7 + [st_spec, vec, vec, vec, _const_spec((LANES, LANES))],
        out_specs=[col, st_spec],
        out_shape=[jax.ShapeDtypeStruct((b, RWKV_W), BF),
                   jax.ShapeDtypeStruct((b, n_steps * per_pair), F32)],
        scratch_shapes=[pltpu.VMEM((LANES, b), F32)],
        compiler_params=_params("parallel"),
        name="rwkv_step",
    )(*coef, s0.reshape(b, n_steps * per_pair), p["r_k"], p["lnx_g"], p["lnx_b"],
      p["ones_bd"][:LANES, :LANES])
    return y, s_new.reshape(s0.shape)


def _merge_kernel(with_router, x_ref, ya_ref, yb_ref, gate_ref, wa_ref, wb_ref, wo_ref, g_ref, *rest):
    if with_router:
        wr_ref, cnt_in_ref, x1_ref, h_ref, route_ref, cnt_out_ref, cnt_scr = rest
    else:
        x1_ref, h_ref = rest
    gate = gate_ref[...].astype(F32)
    pa = _dot(ya_ref[...], wa_ref[...])
    pb = _dot(yb_ref[...], wb_ref[...])
    merged = _sigmoid(gate[:, :D_MODEL]) * pa + _sigmoid(gate[:, D_MODEL:]) * pb
    x1 = x_ref[...] + _dot(merged.astype(BF), wo_ref[...])
    x1_ref[...] = x1
    h = _rms(x1, g_ref[...])
    h_ref[...] = h.astype(h_ref.dtype)
    if not with_router:
        return

    @pl.when(pl.program_id(0) == 0)
    def _():
        cnt_scr[...] = cnt_in_ref[...]

    logits = _dot(h.astype(BF), wr_ref[...])
    tm = logits.shape[0]
    lane = lax.broadcasted_iota(jnp.int32, logits.shape, 1).astype(F32)
    logits = jnp.where(lane < N_EXPERTS, logits, -jnp.inf)
    m1 = jnp.max(logits, axis=-1, keepdims=True)
    i1 = jnp.min(jnp.where(logits == m1, lane, float(LANES)), axis=-1, keepdims=True)
    rest_l = jnp.where(lane == i1, -jnp.inf, logits)
    m2 = jnp.max(rest_l, axis=-1, keepdims=True)
    i2 = jnp.min(jnp.where(rest_l == m2, lane, float(LANES)), axis=-1, keepdims=True)
    e2 = jnp.exp(m2 - m1)
    g1 = 1.0 / (1.0 + e2)
    g2 = e2 / (1.0 + e2)
    oh1 = (lane == i1).astype(F32)
    oh2 = (lane == i2).astype(F32)
    both = oh1 + oh2
    sub = min(tm, ROUTER_RANK_ROWS)
    ri = lax.broadcasted_iota(jnp.int32, (sub, sub), 0)
    ci = lax.broadcasted_iota(jnp.int32, (sub, sub), 1)
    earlier = (ri > ci).astype(BF)
    running = cnt_scr[...]
    before = []
    for c in range(tm // sub):
        part = both[c * sub:(c + 1) * sub]
        before.append(_dot(earlier, part.astype(BF)) + running)
        running = running + jnp.sum(part, axis=0, keepdims=True)
    before = jnp.concatenate(before, axis=0)
    rank1 = jnp.sum(oh1 * before, axis=-1, keepdims=True)
    rank2 = jnp.sum(oh2 * before, axis=-1, keepdims=True)
    cnt_scr[...] = running
    cnt_out_ref[...] = running
    route = jnp.zeros_like(logits)
    for n, val in enumerate((i1, i2, g1, g2, rank1, rank2)):
        route = jnp.where(lane == n, val, route)
    route_ref[...] = route


def _merge_out(x, ya, yb, gate, wa, wb, wo, g, tm, router=None, cnt_in=None):
    t = x.shape[0]
    row = lambda w_: pl.BlockSpec((tm, w_), lambda i: (i, 0))
    in_specs = [row(D_MODEL), row(ATT_W), row(RWKV_W), row(2 * D_MODEL),
                _const_spec((ATT_W, D_MODEL)), _const_spec((RWKV_W, D_MODEL)),
                _const_spec((D_MODEL, D_MODEL)), _const_spec((1, D_MODEL))]
    args = [x, ya, yb, gate, wa, wb, wo, g]
    if router is None:
        out_specs = [row(D_MODEL), row(D_MODEL)]
        out_shape = [jax.ShapeDtypeStruct((t, D_MODEL), F32), jax.ShapeDtypeStruct((t, D_MODEL), BF)]
        scratch = []
    else:
        in_specs += [_const_spec(router.shape), _const_spec((1, LANES))]
        args += [router, cnt_in]
        out_specs = [row(D_MODEL), row(D_MODEL), row(LANES), pl.BlockSpec((1, LANES), lambda i: (0, 0))]
        out_shape = [jax.ShapeDtypeStruct((t, D_MODEL), F32), jax.ShapeDtypeStruct((t, D_MODEL), F32),
                     jax.ShapeDtypeStruct((t, LANES), F32), jax.ShapeDtypeStruct((1, LANES), F32)]
        scratch = [pltpu.VMEM((1, LANES), F32)]
    return pl.pallas_call(
        functools.partial(_merge_kernel, router is not None),
        grid=(t // tm,),
        in_specs=in_specs, out_specs=out_specs, out_shape=out_shape, scratch_shapes=scratch,
        compiler_params=_params("parallel" if router is None else "arbitrary"),
        name="merge_out",
    )(*args)


def _swiglu(x, wg, wu, wd):
    s = _dot(x, wg)
    u = _dot(x, wu)
    act = (s * _sigmoid(s) * u).astype(BF)
    return _dot(act, wd)


def _dense_ffn_kernel(x1_ref, h_ref, wg_ref, wu_ref, wd_ref, o_ref):
    o_ref[...] = x1_ref[...] + _swiglu(h_ref[...], wg_ref[...], wu_ref[...], wd_ref[...])


def _dense_ffn(x1, h, wg, wu, wd, tm):
    t = x1.shape[0]
    d_ff = wg.shape[1]
    row = lambda: pl.BlockSpec((tm, D_MODEL), lambda i: (i, 0))
    return pl.pallas_call(
        _dense_ffn_kernel,
        grid=(t // tm,),
        in_specs=[row(), row(), _const_spec((D_MODEL, d_ff)), _const_spec((D_MODEL, d_ff)),
                  _const_spec((d_ff, D_MODEL))],
        out_specs=row(),
        out_shape=jax.ShapeDtypeStruct((t, D_MODEL), F32),
        compiler_params=_params("parallel"),
        name="dense_ffn",
    )(x1, h, wg, wu, wd)


def _dispatch_kernel(tt, first, dest_ref, fill_ref, h_ref, *rest):
    if first:
        xs_hbm, zbuf, sem, zsem = rest
    else:
        _, xs_hbm, sem = rest
    i = pl.program_id(0)
    base = i * tt

    if first:
        @pl.when(i == 0)
        def _():
            zbuf[...] = jnp.zeros_like(zbuf)
            bm = zbuf.shape[0]
            fills = [pltpu.make_async_copy(zbuf, xs_hbm.at[pl.ds(fill_ref[n] * bm, bm), :], zsem)
                     for n in range(fill_ref.shape[0])]
            for f in fills:
                f.start()
            for f in fills:
                f.wait()

    def issue(r, carry):
        for j in range(TOP_K):
            row = dest_ref[TOP_K * (base + r) + j]
            pltpu.make_async_copy(h_ref.at[pl.ds(r, 1), :], xs_hbm.at[pl.ds(row, 1), :], sem).start()
        return carry

    lax.fori_loop(0, tt, issue, 0, unroll=8)
    for j in range(TOP_K):
        pltpu.make_async_copy(h_ref, xs_hbm.at[pl.ds(0, tt), :], sem).wait()


def _dispatch(dest, fill_blocks, h, xs, tt, bm, n_rows):
    t = h.shape[0]
    first = xs is None
    in_specs = [pl.BlockSpec((tt, D_MODEL), lambda i, ds, fb: (i, 0))]
    args = [dest, fill_blocks, h]
    if first:
        scratch = [pltpu.VMEM((bm, D_MODEL), F32), pltpu.SemaphoreType.DMA(()), pltpu.SemaphoreType.DMA(())]
        aliases = {}
    else:
        in_specs.append(pl.BlockSpec(memory_space=pl.ANY))
        args.append(xs)
        scratch = [pltpu.SemaphoreType.DMA(())]
        aliases = {3: 0}
    grid_spec = pltpu.PrefetchScalarGridSpec(
        num_scalar_prefetch=2, grid=(t // tt,), in_specs=in_specs,
        out_specs=pl.BlockSpec(memory_space=pl.ANY), scratch_shapes=scratch)
    return pl.pallas_call(
        functools.partial(_dispatch_kernel, tt, first),
        grid_spec=grid_spec,
        out_shape=jax.ShapeDtypeStruct((n_rows, D_MODEL), F32),
        input_output_aliases=aliases,
        compiler_params=_params("arbitrary"),
        name="moe_dispatch",
    )(*args)


def _moe_kernel(n_half, blk_e_ref, n_used_ref, xs_ref, wg_ref, wu_ref, wd_ref, y_ref):
    @pl.when(pl.program_id(0) < n_used_ref[0])
    def _():
        x = xs_ref[...].astype(BF)
        d_e = wg_ref.shape[2]
        step = d_e // n_half
        acc = None
        for j in range(n_half):
            cs = slice(j * step, (j + 1) * step)
            part = _swiglu(x, wg_ref[0, :, cs], wu_ref[0, :, cs], wd_ref[0, cs, :])
            acc = part if acc is None else acc + part
        y_ref[...] = acc

    @pl.when(pl.program_id(0) >= n_used_ref[0])
    def _():
        y_ref[...] = jnp.zeros_like(y_ref)


def _moe_ffn(xs, blk_e, n_used, wg, wu, wd, bm):
    n_blk = blk_e.shape[0]
    n_rows = xs.shape[0]
    d_e = wg.shape[2]
    wspec = lambda shape: pl.BlockSpec(shape, lambda i, be, nu: (be[i], 0, 0), pipeline_mode=pl.Buffered(1))
    rows = pl.BlockSpec((bm, D_MODEL), lambda i, be, nu: (i, 0))
    grid_spec = pltpu.PrefetchScalarGridSpec(
        num_scalar_prefetch=2,
        grid=(n_blk,),
        in_specs=[rows, wspec((1, D_MODEL, d_e)), wspec((1, D_MODEL, d_e)), wspec((1, d_e, D_MODEL))],
        out_specs=rows,
    )
    return pl.pallas_call(
        functools.partial(_moe_kernel, 2),
        grid_spec=grid_spec,
        out_shape=jax.ShapeDtypeStruct((n_rows, D_MODEL), F32),
        compiler_params=_params("arbitrary"),
        name="moe_ffn",
    )(blk_e, n_used, xs, wg, wu, wd)


def _row_layout(counts, bm, n_blk):
    per_e = (counts + bm - 1) // bm
    ends = jnp.cumsum(per_e)
    n_used = ends[-1]
    first_row = (ends - per_e) * bm
    i = jnp.arange(n_blk, dtype=jnp.int32)
    blk_e = jnp.sum((jnp.minimum(i, jnp.maximum(n_used - 1, 0))[:, None] >= ends[None, :]).astype(jnp.int32), axis=1)
    need = jnp.any(i[:, None] == (ends - 1)[None, :], axis=1) | (i >= n_blk - N_EXPERTS)
    fill_blocks = jnp.argsort(jnp.logical_not(need), stable=True)[:2 * N_EXPERTS].astype(jnp.int32)
    return first_row.astype(jnp.int32), blk_e.astype(jnp.int32), n_used.astype(jnp.int32).reshape(1), fill_blocks


def _combine_kernel(tc, dest_ref, x1_ref, route_ref, y_hbm, g_ref, o_ref, ybuf, sem):
    base = pl.program_id(0) * tc

    def issue(r, carry):
        for j in range(TOP_K):
            row = dest_ref[TOP_K * (base + r) + j]
            pltpu.make_async_copy(y_hbm.at[pl.ds(row, 1), :], ybuf.at[j, pl.ds(r, 1), :], sem).start()
        return carry

    lax.fori_loop(0, tc, issue, 0, unroll=4)
    for j in range(TOP_K):
        pltpu.make_async_copy(y_hbm.at[pl.ds(0, tc), :], ybuf.at[j], sem).wait()
    route = route_ref[...]
    x2 = x1_ref[...] + route[:, 2:3] * ybuf[0] + route[:, 3:4] * ybuf[1]
    o_ref[...] = _rms(x2, g_ref[...])


def _moe_combine(dest, x1, route, y, g, tc):
    t = x1.shape[0]
    grid_spec = pltpu.PrefetchScalarGridSpec(
        num_scalar_prefetch=1,
        grid=(t // tc,),
        in_specs=[pl.BlockSpec((tc, D_MODEL), lambda i, ds: (i, 0)),
                  pl.BlockSpec((tc, LANES), lambda i, ds: (i, 0)),
                  pl.BlockSpec(memory_space=pl.ANY),
                  pl.BlockSpec((1, D_MODEL), lambda i, ds: (0, 0))],
        out_specs=pl.BlockSpec((tc, D_MODEL), lambda i, ds: (i, 0)),
        scratch_shapes=[pltpu.VMEM((TOP_K, tc, D_MODEL), F32), pltpu.SemaphoreType.DMA(())],
    )
    return pl.pallas_call(
        functools.partial(_combine_kernel, tc),
        grid_spec=grid_spec,
        out_shape=jax.ShapeDtypeStruct((t, D_MODEL), F32),
        compiler_params=_params("arbitrary"),
        name="moe_combine",
    )(dest, x1, route, y, g)


def _layer_params(l, w_in, mu_shift, w0, w2, a0, a2, g2, k_k, k_a, r_k, lnx_g, lnx_b,
                  w_proj_attn, w_proj_rwkv, w_out, ones_bd):
    row = lambda u: u.reshape(1, -1)
    return dict(
        w_in=w_in[l].astype(BF), mu=row(mu_shift[l]), w0=row(w0[l]), w2=w2[l].astype(BF),
        a0=row(a0[l]), a2=a2[l].astype(BF), g2=g2[l].astype(BF), k_k=row(k_k[l]), k_a=row(k_a[l]),
        r_k=row(r_k[l]), lnx_g=row(lnx_g[l]), lnx_b=row(lnx_b[l]),
        wa=w_proj_attn[l].astype(BF), wb=w_proj_rwkv[l].astype(BF), wo=w_out[l].astype(BF),
        ones_bd=ones_bd)


def kernel(x_prompt, x_sample, cache_k, cache_v, state_wkv, state_shift, norm_mix_g, w_in, attn_sinks, rel_bias, mu_shift, w0, w2, a0, a2, g2, k_k, k_a, r_k, lnx_g, lnx_b, w_proj_attn, w_proj_rwkv, w_out, norm_ffn_g, dense_w_gate, dense_w_up, dense_w_down, router_w, moe_w_gate, moe_w_up, moe_w_down, norm_final_g):
    batch, seq, _ = x_prompt.shape
    dec_batch, dec_seq, _ = x_sample.shape
    assert dec_seq == 1 and seq % WINDOW == 0

    head_id = jnp.arange(RWKV_W, dtype=jnp.int32) // RWKV_HEAD
    ones_bd = (head_id[:, None] == head_id[None, :]).astype(BF)
    bias_by_dist = _bias_lookup(rel_bias, _t5_bucket(np.arange(WINDOW + 1)))
    bias_c = jnp.transpose(bias_by_dist[WINDOW:0:-1], (1, 0))[None]
    bias_n = bias_by_dist[0][None, :, None]

    layers = [_layer_params(l, w_in, mu_shift, w0, w2, a0, a2, g2, k_k, k_a, r_k, lnx_g, lnx_b,
                            w_proj_attn, w_proj_rwkv, w_out, ones_bd) for l in range(DEPTH)]

    def router_pieces(l):
        return jnp.pad(router_w[l // 2], ((0, 0), (0, LANES - N_EXPERTS))).astype(BF)

    n_prompt = batch * seq
    bm = MOE_BLOCK_ROWS
    n_all = n_prompt + dec_batch
    n_blk = -(-(n_all * TOP_K) // bm) + N_EXPERTS

    def mix(x, is_prompt, l, state):
        p = layers[l]
        n_tok = x.shape[0]
        gm = norm_mix_g[l].reshape(1, -1)
        sinks = attn_sinks[l]
        if is_prompt:
            b, t = batch, seq
            prev_z = jnp.zeros((b, 1, RWKV_IN), F32)
            q, k, v, gate, z_last, *coef = _inproj(x, gm, p, prev_z, min(INPROJ_ROWS, t), t)
            sink_col = jnp.repeat(sinks, WINDOW).reshape(N_KV_HEADS, GQA_GROUP * WINDOW, 1)
            ya = _swa_prompt(q.reshape(b, t, ATT_W), k.reshape(b, t, KV_W), v.reshape(b, t, KV_W),
                             rel_bias, sink_col).reshape(n_tok, ATT_W)
            k4 = k.reshape(b, t, N_KV_HEADS, HEAD_DIM)
            v4 = v.reshape(b, t, N_KV_HEADS, HEAD_DIM)
            state["k"].append(k4[:, t - WINDOW:])
            state["v"].append(v4[:, t - WINDOW:])
            s0 = jnp.zeros((b, RWKV_HEADS, RWKV_HEAD, RWKV_HEAD), F32)
            yb3, s_new = _rwkv_scan([u.reshape(b, t, RWKV_W) for u in coef], s0, p, b, RWKV_CHUNK)
            yb = yb3.reshape(n_tok, RWKV_W)
            state["z"].append(z_last.reshape(b, RWKV_IN))
        else:
            b = dec_batch
            q, k, v, gate, z, *coef = _inproj(x, gm, p, state_shift[l], n_tok, None)
            ck = cache_k[l].reshape(b, WINDOW, KV_W)
            cv = cache_v[l].reshape(b, WINDOW, KV_W)
            ya3 = _swa_sample(q.reshape(b, N_Q_HEADS, HEAD_DIM), k, v, ck, cv, bias_c, bias_n,
                              sinks.reshape(1, N_Q_HEADS, 1), min(SAMPLE_ATTN_SEQS, b))
            ya = ya3.reshape(b, ATT_W)
            state["k"].append(jnp.concatenate([ck[:, 1:], k[:, None]], axis=1)
                              .reshape(b, WINDOW, N_KV_HEADS, HEAD_DIM))
            state["v"].append(jnp.concatenate([cv[:, 1:], v[:, None]], axis=1)
                              .reshape(b, WINDOW, N_KV_HEADS, HEAD_DIM))
            yb, s_new = _rwkv_step(coef, state_wkv[l], p)
            state["z"].append(z)
        state["s"].append(s_new)
        return ya, yb, gate

    groups = [dict(x=x_prompt.reshape(n_prompt, D_MODEL), prompt=True, tm=min(DENSE_ROWS, n_prompt), k=[], v=[], s=[], z=[]),
              dict(x=x_sample.reshape(dec_batch, D_MODEL), prompt=False, tm=min(DENSE_ROWS, dec_batch), k=[], v=[], s=[], z=[])]
    for l in range(DEPTH):
        p = layers[l]
        gn = norm_ffn_g[l].reshape(1, -1)
        j = l // 2
        if l % 2 == 0:
            wg, wu, wd = (_to_bf16(w[j], w.shape[1] // 4) for w in (dense_w_gate, dense_w_up, dense_w_down))
            for grp in groups:
                ya, yb, gate = mix(grp["x"], grp["prompt"], l, grp)
                x1, h = _merge_out(grp["x"], ya, yb, gate, p["wa"], p["wb"], p["wo"], gn, grp["tm"])
                grp["x"] = _dense_ffn(x1, h, wg, wu, wd, grp["tm"])
        else:
            assert l == DEPTH - 1
            wg, wu, wd = (w[j].astype(BF) for w in (moe_w_gate, moe_w_up, moe_w_down))
            router = router_pieces(l)
            cnt = jnp.zeros((1, LANES), F32)
            xs = None
            for grp in groups:
                ya, yb, gate = mix(grp["x"], grp["prompt"], l, grp)
                x1, h, route, cnt = _merge_out(grp["x"], ya, yb, gate, p["wa"], p["wb"], p["wo"], gn,
                                               min(ROUTER_ROWS, grp["tm"]), router=router, cnt_in=cnt)
                grp["x1"], grp["route"], grp["h"] = x1, route, h
            counts = cnt[0, :N_EXPERTS].astype(jnp.int32)
            first_row, blk_e, n_used, fill_blocks = _row_layout(counts, bm, n_blk)
            for grp in groups:
                route = grp["route"]
                expert = route[:, 0:TOP_K].astype(jnp.int32)
                rank = route[:, 2 * TOP_K:3 * TOP_K].astype(jnp.int32)
                start = jnp.sum(jnp.where(expert[..., None] == jnp.arange(N_EXPERTS, dtype=jnp.int32),
                                          first_row, 0), axis=-1)
                grp["dest"] = (start + rank).reshape(-1)
                xs = _dispatch(grp["dest"], fill_blocks, grp["h"], xs, min(DISPATCH_ROWS, grp["h"].shape[0]), bm,
                               n_blk * bm)
            y = _moe_ffn(xs, blk_e, n_used, wg, wu, wd, bm)
            for grp in groups:
                grp["x"] = _moe_combine(grp["dest"], grp["x1"], grp["route"], y,
                                        norm_final_g.reshape(1, -1), min(COMBINE_ROWS, grp["x1"].shape[0]))

    gp, gs = groups
    outs = []
    for grp, shape in ((gp, (batch, seq, D_MODEL)), (gs, (dec_batch, dec_seq, D_MODEL))):
        outs.append((grp["x"].reshape(shape), jnp.stack(grp["k"]), jnp.stack(grp["v"]),
                     jnp.stack(grp["s"]), jnp.stack(grp["z"])))
    (y_p, nk_p, nv_p, ns_p, nz_p), (y_s, nk_s, nv_s, ns_s, nz_s) = outs
    return (y_p, y_s, nk_p, nv_p, ns_p, nz_p, nk_s, nv_s, ns_s, nz_s)
```

```python
import functools
import math

import jax
import jax.numpy as jnp
import numpy as np
from jax import lax
from jax.experimental import pallas as pl
from jax.experimental.pallas import tpu as pltpu

BF = jnp.bfloat16
F32 = jnp.float32

D_MODEL = 1024
DEPTH = 2
HEAD_DIM = 64
N_Q_HEADS = 8
N_KV_HEADS = 2
GQA_GROUP = 4
ATT_W = 512
KV_W = 128
WINDOW = 128
ATT_SCALE = HEAD_DIM ** -0.5
NUM_BUCKETS = 32
MAX_EXACT = 16
MAX_DISTANCE = 128
NEG_INF = -1e30
RWKV_HEAD = 64
RWKV_W = 512
RWKV_HEADS = 8
D_DECAY_LORA = 64
D_AAA_LORA = 64
D_GATE_LORA = 128
RWKV_IN = 3 * RWKV_W + D_DECAY_LORA + D_AAA_LORA + D_GATE_LORA
GN_EPS = 64e-5
IN_W = ATT_W + 2 * KV_W + RWKV_IN + 2 * D_MODEL
N_EXPERTS = 8
TOP_K = 2
NORM_EPS = 1e-6

VMEM_LIMIT_BYTES = 56 * 1024 * 1024
LANES = 128

INPROJ_ROWS = 512
DENSE_ROWS = 512
MERGE_DENSE_ROWS = 256
ROUTER_ROWS = 512
ROUTER_RANK_ROWS = 256
MOE_BLOCK_ROWS = 512
DISPATCH_ROWS = 2048
COMBINE_ROWS = 1024
RWKV_CHUNK = 64
SAMPLE_ATTN_SEQS = 16


def _params(*sem):
    return pltpu.CompilerParams(dimension_semantics=sem, vmem_limit_bytes=VMEM_LIMIT_BYTES)


def _const_spec(shape):
    zeros = (0,) * len(shape)
    return pl.BlockSpec(shape, lambda *_: zeros, pipeline_mode=pl.Buffered(1))


def _dot(a, b):
    return jnp.dot(a, b, preferred_element_type=F32)


def _dot_nt(a, b):
    return lax.dot_general(a, b, (((1,), (1,)), ((), ())), preferred_element_type=F32)


def _dot_tn(a, b):
    return lax.dot_general(a, b, (((0,), (0,)), ((), ())), preferred_element_type=F32)


def _split2(x):
    hi = x.astype(BF)
    lo = (x - hi.astype(F32)).astype(BF)
    return hi, lo


def _dot_exact_rhs(x, w):
    hi, lo = _split2(x)
    return _dot(hi, w) + _dot(lo, w)


def _rms(x, g):
    ms = jnp.mean(x * x, axis=-1, keepdims=True)
    return x * lax.rsqrt(ms + NORM_EPS) * g


def _sigmoid(x):
    return 1.0 / (1.0 + jnp.exp(-x))


def _cast_kernel(x_ref, o_ref):
    o_ref[...] = x_ref[...].astype(o_ref.dtype)


def _to_bf16(w, row_tile):
    rows, cols = w.shape
    spec = pl.BlockSpec((row_tile, cols), lambda i: (i, 0))
    return pl.pallas_call(
        _cast_kernel, grid=(rows // row_tile,), in_specs=[spec], out_specs=spec,
        out_shape=jax.ShapeDtypeStruct((rows, cols), BF),
        compiler_params=_params("parallel"), name="to_bf16",
    )(w)


def _inproj_kernel(rows_per_seq, x_ref, g_ref, w_ref, pz_ref, mu_ref, w0_ref, w2_ref, a0_ref, a2_ref, g2_ref,
                   kk_ref, ka_ref, ones_ref,
                   q_ref, k_ref, v_ref, gate_ref, zl_ref,
                   r_ref, lw_ref, k2_ref, vr_ref, a_ref, b_ref, gr_ref, *carry):
    n = _rms(x_ref[...], g_ref[...]).astype(BF)
    z0 = ATT_W + 2 * KV_W
    z = _dot(n, w_ref[:, z0:z0 + RWKV_IN])
    q_ref[...] = _dot(n, w_ref[:, 0:ATT_W]).astype(BF)
    k_ref[...] = _dot(n, w_ref[:, ATT_W:ATT_W + KV_W])
    v_ref[...] = _dot(n, w_ref[:, ATT_W + KV_W:ATT_W + 2 * KV_W])
    gate_ref[...] = _dot(n, w_ref[:, z0 + RWKV_IN:IN_W]).astype(gate_ref.dtype)
    tm = z.shape[0]
    if rows_per_seq is None:
        zs = pz_ref[...]
        zl_ref[...] = z
    else:
        prev_ref, = carry
        tiles_per_seq = rows_per_seq // tm

        @pl.when(pl.program_id(0) % tiles_per_seq == 0)
        def _():
            prev_ref[...] = pz_ref[0]

        row = lax.broadcasted_iota(jnp.int32, (tm, 1), 0)
        zs = jnp.where(row == 0, prev_ref[...], pltpu.roll(z, 1, 0))
        prev_ref[...] = z[tm - 1:tm, :]
        zl_ref[0] = z[tm - 1:tm, :]
    zz = z + (zs - z) * mu_ref[...]
    r = zz[:, 0:RWKV_W]
    k = zz[:, RWKV_W:2 * RWKV_W]
    v = zz[:, 2 * RWKV_W:3 * RWKV_W]
    o = 3 * RWKV_W
    wl = zz[:, o:o + D_DECAY_LORA]
    al = zz[:, o + D_DECAY_LORA:o + D_DECAY_LORA + D_AAA_LORA]
    gl = zz[:, o + D_DECAY_LORA + D_AAA_LORA:RWKV_IN]
    lw = -math.exp(-0.5) * _sigmoid(w0_ref[...] + _dot(jnp.tanh(wl).astype(BF), w2_ref[...]))
    a = _sigmoid(a0_ref[...] + _dot(al.astype(BF), a2_ref[...]))
    g = _dot(_sigmoid(gl).astype(BF), g2_ref[...])
    kkv = k * kk_ref[...]
    ss = _dot_exact_rhs(kkv * kkv, ones_ref[...])
    kk = kkv * lax.rsqrt(jnp.maximum(ss, 1e-24))
    k2 = k * (1.0 + (a - 1.0) * ka_ref[...])
    r_ref[...], lw_ref[...], k2_ref[...], vr_ref[...] = r, lw, k2, v
    a_ref[...], b_ref[...], gr_ref[...] = -kk, kk * a, g


def _inproj(x, g, p, prev_z, tm, rows_per_seq):
    t = x.shape[0]
    row = lambda w_: pl.BlockSpec((tm, w_), lambda i: (i, 0))
    vec = lambda w_: _const_spec((1, w_))
    if rows_per_seq is None:
        pz_spec = row(RWKV_IN)
        zl_spec, zl_shape = row(RWKV_IN), (t, RWKV_IN)
        scratch = []
    else:
        tiles_per_seq = rows_per_seq // tm
        pz_spec = pl.BlockSpec((1, 1, RWKV_IN), lambda i: (i // tiles_per_seq, 0, 0))
        zl_spec, zl_shape = pz_spec, (t // rows_per_seq, 1, RWKV_IN)
        scratch = [pltpu.VMEM((1, RWKV_IN), F32)]
    return pl.pallas_call(
        functools.partial(_inproj_kernel, rows_per_seq),
        grid=(t // tm,),
        in_specs=[row(D_MODEL), vec(D_MODEL), _const_spec((D_MODEL, IN_W)), pz_spec,
                  vec(RWKV_IN), vec(RWKV_W), _const_spec((D_DECAY_LORA, RWKV_W)),
                  vec(RWKV_W), _const_spec((D_AAA_LORA, RWKV_W)), _const_spec((D_GATE_LORA, RWKV_W)),
                  vec(RWKV_W), vec(RWKV_W), _const_spec((RWKV_W, RWKV_W))],
        out_specs=[row(ATT_W), row(KV_W), row(KV_W), row(2 * D_MODEL), zl_spec] + [row(RWKV_W)] * 7,
        out_shape=[jax.ShapeDtypeStruct((t, ATT_W), BF),
                   jax.ShapeDtypeStruct((t, KV_W), F32),
                   jax.ShapeDtypeStruct((t, KV_W), F32),
                   jax.ShapeDtypeStruct((t, 2 * D_MODEL), BF),
                   jax.ShapeDtypeStruct(zl_shape, F32)] + [jax.ShapeDtypeStruct((t, RWKV_W), F32)] * 7,
        scratch_shapes=scratch,
        compiler_params=_params("arbitrary"),
        name="inproj",
    )(x, g, p["w_in"], prev_z, p["mu"], p["w0"], p["w2"], p["a0"], p["a2"], p["g2"], p["k_k"], p["k_a"],
      p["ones_bd"])


def _t5_bucket(dist):
    n = np.maximum(dist, 0)
    nf = np.maximum(n, 1).astype(np.float32)
    large = MAX_EXACT + (np.log(nf / MAX_EXACT) / math.log(MAX_DISTANCE / MAX_EXACT)
                         * (NUM_BUCKETS - MAX_EXACT)).astype(np.int32)
    return np.where(n < MAX_EXACT, n, np.minimum(large, NUM_BUCKETS - 1)).astype(np.int32)


def _bias_lookup(rel_bias, bucket):
    hit = bucket[..., None, None] == np.arange(NUM_BUCKETS, dtype=np.int32)[:, None]
    return jnp.sum(jnp.where(hit, rel_bias, 0.0), axis=-2)


def _prompt_bucket_table():
    dist = np.arange(WINDOW)[:, None] - (np.arange(2 * WINDOW) - WINDOW)[None, :]
    return np.where((dist >= 0) & (dist <= WINDOW), _t5_bucket(dist), -1).astype(np.int32)


def _swa_prompt_kernel(nb, bucket_ref, relb_ref, q_ref, kp_ref, kc_ref, vp_ref, vc_ref, sink_ref, o_ref, bias_ref):
    first = pl.program_id(0) == 0

    @pl.when(first)
    def _():
        bucket = bucket_ref[...]
        for head in range(N_Q_HEADS):
            tab = jnp.full(bucket.shape, NEG_INF, F32)
            for n in range(NUM_BUCKETS):
                tab = jnp.where(bucket == n, relb_ref[n, head], tab)
            g = head % GQA_GROUP
            bias_ref[head // GQA_GROUP, g * WINDOW:(g + 1) * WINDOW, :] = tab

    col = lax.broadcasted_iota(jnp.int32, (GQA_GROUP * WINDOW, 2 * WINDOW), 1)
    pad_mask = jnp.logical_and(first, col < WINDOW)
    chains = [(b, h) for b in range(nb) for h in range(N_KV_HEADS)]
    hs = lambda h: slice(h * HEAD_DIM, (h + 1) * HEAD_DIM)
    kh = [jnp.concatenate([kp_ref[b][:, hs(h)], kc_ref[b][:, hs(h)]], axis=0).astype(BF) for b, h in chains]
    vh = [jnp.concatenate([vp_ref[b][:, hs(h)], vc_ref[b][:, hs(h)]], axis=0).astype(BF) for b, h in chains]
    qh = [jnp.concatenate([q_ref[b][:, hs(GQA_GROUP * h + g)] for g in range(GQA_GROUP)], axis=0)
          for b, h in chains]
    s = [_dot_nt(qh[n], kh[n]) * ATT_SCALE + bias_ref[h] for n, (b, h) in enumerate(chains)]
    s = [jnp.where(pad_mask, NEG_INF, x) for x in s]
    m = [jnp.maximum(jnp.max(x, axis=-1, keepdims=True), sink_ref[h]) for x, (b, h) in zip(s, chains)]
    p = [jnp.exp(x - mm) for x, mm in zip(s, m)]
    denom = [jnp.sum(x, axis=-1, keepdims=True) + jnp.exp(sink_ref[h] - mm)
             for x, mm, (b, h) in zip(p, m, chains)]
    o = [_dot(x.astype(BF), vv) / d for x, vv, d in zip(p, vh, denom)]
    for b in range(nb):
        pieces = [o[b * N_KV_HEADS + h][g * WINDOW:(g + 1) * WINDOW]
                  for h in range(N_KV_HEADS) for g in range(GQA_GROUP)]
        o_ref[b] = jnp.concatenate(pieces, axis=-1).astype(o_ref.dtype)


def _swa_prompt(q3, k3, v3, rel_bias, sink_col):
    batch, seq, _ = q3.shape
    cur = lambda i: (0, i, 0)
    prev = lambda i: (0, jnp.maximum(i - 1, 0), 0)
    kv_c = pl.BlockSpec((batch, WINDOW, KV_W), cur)
    kv_p = pl.BlockSpec((batch, WINDOW, KV_W), prev)
    bucket = jnp.asarray(_prompt_bucket_table())
    return pl.pallas_call(
        functools.partial(_swa_prompt_kernel, batch),
        grid=(seq // WINDOW,),
        in_specs=[_const_spec(bucket.shape), pl.BlockSpec(memory_space=pltpu.SMEM),
                  pl.BlockSpec((batch, WINDOW, ATT_W), cur), kv_p, kv_c, kv_p, kv_c,
                  _const_spec(sink_col.shape)],
        out_specs=pl.BlockSpec((batch, WINDOW, ATT_W), cur),
        out_shape=jax.ShapeDtypeStruct((batch, seq, ATT_W), BF),
        scratch_shapes=[pltpu.VMEM((N_KV_HEADS, GQA_GROUP * WINDOW, 2 * WINDOW), F32)],
        compiler_params=_params("arbitrary"),
        name="swa_prompt",
    )(bucket, rel_bias, q3, k3, k3, v3, v3, sink_col)


def _swa_sample_kernel(q_ref, kn_ref, vn_ref, ck_ref, cv_ref, bc_ref, bn_ref, sink_ref, o_ref):
    q = q_ref[...]
    qf = q.astype(F32)
    kn = kn_ref[...].astype(BF).astype(F32)
    vn = vn_ref[...].astype(BF).astype(F32)
    ck, cv = ck_ref[...], cv_ref[...]
    head = lax.broadcasted_iota(jnp.int32, (1, N_Q_HEADS, 1), 1)
    low = head < GQA_GROUP
    s_h, sn_h = [], []
    for h in range(N_KV_HEADS):
        sl = slice(h * HEAD_DIM, (h + 1) * HEAD_DIM)
        s_h.append(jnp.einsum("bgd,bkd->bgk", q, ck[:, :, sl].astype(BF), preferred_element_type=F32))
        sn_h.append(jnp.sum(qf * kn[:, None, sl], axis=-1, keepdims=True))
    s = jnp.where(low, s_h[0], s_h[1]) * ATT_SCALE + bc_ref[...]
    sn = jnp.where(low, sn_h[0], sn_h[1]) * ATT_SCALE + bn_ref[...]
    sink = sink_ref[...]
    m = jnp.maximum(jnp.maximum(jnp.max(s, axis=-1, keepdims=True), sn), sink)
    p = jnp.exp(s - m)
    pn = jnp.exp(sn - m)
    denom = jnp.sum(p, axis=-1, keepdims=True) + pn + jnp.exp(sink - m)
    pb = p.astype(BF)
    pnb = pn.astype(BF).astype(F32)
    o_h = []
    for h in range(N_KV_HEADS):
        sl = slice(h * HEAD_DIM, (h + 1) * HEAD_DIM)
        o = jnp.einsum("bgk,bkd->bgd", pb, cv[:, :, sl].astype(BF), preferred_element_type=F32)
        o_h.append(o + pnb * vn[:, None, sl])
    o_ref[...] = (jnp.where(low, o_h[0], o_h[1]) / denom).astype(o_ref.dtype)


def _swa_sample(q3, kn, vn, ck, cv, bias_c, bias_n, sink3, bb):
    b = q3.shape[0]
    return pl.pallas_call(
        _swa_sample_kernel,
        grid=(b // bb,),
        in_specs=[pl.BlockSpec((bb, N_Q_HEADS, HEAD_DIM), lambda i: (i, 0, 0)),
                  pl.BlockSpec((bb, KV_W), lambda i: (i, 0)),
                  pl.BlockSpec((bb, KV_W), lambda i: (i, 0)),
                  pl.BlockSpec((bb, WINDOW, KV_W), lambda i: (i, 0, 0)),
                  pl.BlockSpec((bb, WINDOW, KV_W), lambda i: (i, 0, 0)),
                  _const_spec(bias_c.shape), _const_spec(bias_n.shape), _const_spec(sink3.shape)],
        out_specs=pl.BlockSpec((bb, N_Q_HEADS, HEAD_DIM), lambda i: (i, 0, 0)),
        out_shape=jax.ShapeDtypeStruct((b, N_Q_HEADS, HEAD_DIM), BF),
        compiler_params=_params("parallel"),
        name="swa_sample",
    )(q3, kn, vn, ck, cv, bias_c, bias_n, sink3)


def _rwkv_scan_kernel(nb, chunk, r_ref, lw_ref, k_ref, v_ref, a_ref, b_ref, g_ref, s0_ref,
                      rk_ref, lg_ref, lb_ref, ones_ref, y_ref, sout_ref, s_scr):
    c = pl.program_id(1)

    @pl.when(c == 0)
    def _():
        s_scr[...] = s0_ref[...]

    ri = lax.broadcasted_iota(jnp.int32, (chunk, chunk), 0)
    ci = lax.broadcasted_iota(jnp.int32, (chunk, chunk), 1)
    incl = ri >= ci
    strict = ri > ci
    tri = incl.astype(BF)
    eye = (ri == ci).astype(F32)
    n_sq = int(math.log2(chunk)) - 1

    chains = [(bi, h) for bi in range(nb) for h in range(RWKV_HEADS)]
    hs = lambda h: slice(h * RWKV_HEAD, (h + 1) * RWKV_HEAD)

    ra, aa, bt, kt, bh, kh, vb, g_all = [], [], [], [], [], [], [], []
    for bi in range(nb):
        r, lw, k, v = r_ref[bi], lw_ref[bi], k_ref[bi], v_ref[bi]
        a, b = a_ref[bi], b_ref[bi]
        hi, lo = _split2(lw)
        lo2 = (lw - hi.astype(F32) - lo.astype(F32)).astype(BF)
        cum = _dot(tri, hi) + _dot(tri, lo) + _dot(tri, lo2)
        tail = cum[chunk - 1:chunk, :]
        g_inv = jnp.exp(-cum)
        g_tail = jnp.exp(tail - cum)
        g_all.append(jnp.exp(tail))
        ra.append((r * jnp.exp(cum)).astype(BF))
        aa.append((a * jnp.exp(cum - lw)).astype(BF))
        bt.append((b * g_inv).astype(BF))
        kt.append((k * g_inv).astype(BF))
        bh.append((b * g_tail).astype(BF))
        kh.append((k * g_tail).astype(BF))
        vb.append(v.astype(BF))

    s_old = [s_scr[bi, h] for bi, h in chains]
    ar = [jnp.concatenate([aa[bi][:, hs(h)], ra[bi][:, hs(h)]], axis=0) for bi, h in chains]
    v_h = [vb[bi][:, hs(h)] for bi, h in chains]
    gb = [_dot_nt(ar[n], bt[bi][:, hs(h)]) for n, (bi, h) in enumerate(chains)]
    gk = [_dot_nt(ar[n], kt[bi][:, hs(h)]) for n, (bi, h) in enumerate(chains)]
    p = [_dot_nt(ar[n], s_old[n].astype(BF)) for n in range(len(chains))]
    l_ab = [jnp.where(strict, x[:chunk], 0.0) for x in gb]
    l_ak = [jnp.where(strict, x[:chunk], 0.0).astype(BF) for x in gk]
    m_rb = [jnp.where(incl, x[chunk:], 0.0).astype(BF) for x in gb]
    m_rk = [jnp.where(incl, x[chunk:], 0.0).astype(BF) for x in gk]
    rhs = [p[n][:chunk] + _dot(l_ak[n], v_h[n]) for n in range(len(chains))]
    t_inv = [eye + x for x in l_ab]
    lp = l_ab
    for _ in range(n_sq):
        lpb = [x.astype(BF) for x in lp]
        lp = [_dot(x, x) for x in lpb]
        t_inv = [t + _dot(x.astype(BF), t.astype(BF)) for x, t in zip(lp, t_inv)]
    ub = [_dot(t.astype(BF), x.astype(BF)).astype(BF) for t, x in zip(t_inv, rhs)]
    y_h = [p[n][chunk:] + _dot(m_rb[n], ub[n]) + _dot(m_rk[n], v_h[n]) for n in range(len(chains))]
    s_new = [s_old[n] * g_all[bi][:, hs(h)] + _dot_tn(ub[n], bh[bi][:, hs(h)]) + _dot_tn(v_h[n], kh[bi][:, hs(h)])
             for n, (bi, h) in enumerate(chains)]
    for n, (bi, h) in enumerate(chains):
        s_scr[bi, h] = s_new[n]

    for bi in range(nb):
        y = jnp.concatenate(y_h[bi * RWKV_HEADS:(bi + 1) * RWKV_HEADS], axis=-1)
        y_ref[bi] = _rwkv_epilogue(y, r_ref[bi], k_ref[bi], v_ref[bi], g_ref[bi],
                                   rk_ref, lg_ref, lb_ref, ones_ref).astype(y_ref.dtype)

    @pl.when(c == pl.num_programs(1) - 1)
    def _():
        sout_ref[...] = s_scr[...]


def _rwkv_epilogue(y, r, k, v, g, rk_ref, lg_ref, lb_ref, ones_ref):
    ones = ones_ref[...]
    inv_n = 1.0 / RWKV_HEAD
    n = y.shape[0]
    pieces = _split2(y) + _split2(r * k * rk_ref[...])
    sums = _dot(jnp.concatenate(pieces, axis=0), ones)
    mean = (sums[0:n] + sums[n:2 * n]) * inv_n
    bonus = (sums[2 * n:3 * n] + sums[3 * n:4 * n]) * v
    d = y - mean
    sq = _dot(jnp.concatenate(_split2(d * d), axis=0), ones)
    var = (sq[0:n] + sq[n:2 * n]) * inv_n
    yn = d * lax.rsqrt(var + GN_EPS) * lg_ref[...] + lb_ref[...]
    return (yn + bonus) * g


def _rwkv_scan(coef, s0, p, nb, chunk):
    r = coef[0]
    b, t, _ = r.shape
    seq_spec = pl.BlockSpec((nb, chunk, RWKV_W), lambda bi, c: (bi, c, 0))
    st_spec = pl.BlockSpec((nb, RWKV_HEADS, RWKV_HEAD, RWKV_HEAD), lambda bi, c: (bi, 0, 0, 0))
    vec = _const_spec((1, RWKV_W))
    return pl.pallas_call(
        functools.partial(_rwkv_scan_kernel, nb, chunk),
        grid=(b // nb, t // chunk),
        in_specs=[seq_spec] * 7 + [st_spec, vec, vec, vec, _const_spec((RWKV_W, RWKV_W))],
        out_specs=[seq_spec, st_spec],
        out_shape=[jax.ShapeDtypeStruct((b, t, RWKV_W), BF),
                   jax.ShapeDtypeStruct((b, RWKV_HEADS, RWKV_HEAD, RWKV_HEAD), F32)],
        scratch_shapes=[pltpu.VMEM((nb, RWKV_HEADS, RWKV_HEAD, RWKV_HEAD), F32)],
        compiler_params=_params("parallel", "arbitrary"),
        name="rwkv_scan",
    )(*coef, s0, p["r_k"], p["lnx_g"], p["lnx_b"], p["ones_bd"])


def _rwkv_step_kernel(r_ref, lw_ref, k_ref, v_ref, a_ref, b_ref, g_ref, s_ref,
                      rk_ref, lg_ref, lb_ref, ones_ref, y_ref, sout_ref, y_scr):
    n_pair = LANES // RWKV_HEAD
    r, k, v = r_ref[...], k_ref[...], v_ref[...]
    rT, kT, vT = r.T, k.T, v.T
    aT, bT, wT = a_ref[...].T, b_ref[...].T, jnp.exp(lw_ref[...]).T
    for hl in range(LANES // RWKV_HEAD):
        hsl = slice(hl * RWKV_HEAD, (hl + 1) * RWKV_HEAD)
        a_h, b_h, k_h, w_h, r_h = aT[hsl], bT[hsl], kT[hsl], wT[hsl], rT[hsl]
        tiles = [(hl * RWKV_HEAD + n_pair * t) * RWKV_HEAD for t in range(RWKV_HEAD // n_pair)]
        st = [s_ref[:, c0:c0 + LANES].T for c0 in tiles]
        new = []
        for t, x in enumerate(st):
            halves = []
            for il in range(n_pair):
                i = hl * RWKV_HEAD + n_pair * t + il
                slab = x[il * RWKV_HEAD:(il + 1) * RWKV_HEAD]
                sa = jnp.sum(slab * a_h, axis=0, keepdims=True)
                slab = slab * w_h + sa * b_h + vT[i:i + 1] * k_h
                y_scr[i:i + 1, :] = jnp.sum(slab * r_h, axis=0, keepdims=True)
                halves.append(slab)
            new.append(jnp.concatenate(halves, axis=0))
        for c0, x in zip(tiles, new):
            sout_ref[:, c0:c0 + LANES] = x.T
    y_ref[...] = _rwkv_epilogue(y_scr[...].T, r, k, v, g_ref[...],
                                rk_ref, lg_ref, lb_ref, ones_ref).astype(y_ref.dtype)


def _rwkv_step(coef, s0, p):
    b = s0.shape[0]
    per_pair = (LANES // RWKV_HEAD) * RWKV_HEAD * RWKV_HEAD
    n_steps = RWKV_W // LANES
    col = pl.BlockSpec((b, LANES), lambda i: (0, i))
    vec = pl.BlockSpec((1, LANES), lambda i: (0, i))
    st_spec = pl.BlockSpec((b, per_pair), lambda i: (0, i))
    y, s_new = pl.pallas_call(
        _rwkv_step_kernel,
        grid=(n_steps,),
        in_specs=[col] * 7 + [st_spec, vec, vec, vec, _const_spec((LANES, LANES))],
        out_specs=[col, st_spec],
        out_shape=[jax.ShapeDtypeStruct((b, RWKV_W), BF),
                   jax.ShapeDtypeStruct((b, n_steps * per_pair), F32)],
        scratch_shapes=[pltpu.VMEM((LANES, b), F32)],
        compiler_params=_params("parallel"),
        name="rwkv_step",
    )(*coef, s0.reshape(b, n_steps * per_pair), p["r_k"], p["lnx_g"], p["lnx_b"],
      p["ones_bd"][:LANES, :LANES])
    return y, s_new.reshape(s0.shape)


def _merge_kernel(with_router, x_ref, ya_ref, yb_ref, gate_ref, wa_ref, wb_ref, wo_ref, g_ref, *rest):
    if with_router:
        wr_ref, cnt_in_ref, x1_ref, h_ref, route_ref, cnt_out_ref, cnt_scr = rest
    else:
        x1_ref, h_ref = rest
    gate = gate_ref[...].astype(F32)
    pa = _dot(ya_ref[...], wa_ref[...])
    pb = _dot(yb_ref[...], wb_ref[...])
    merged = _sigmoid(gate[:, :D_MODEL]) * pa + _sigmoid(gate[:, D_MODEL:]) * pb
    x1 = x_ref[...] + _dot(merged.astype(BF), wo_ref[...])
    x1_ref[...] = x1
    h = _rms(x1, g_ref[...])
    h_ref[...] = h.astype(h_ref.dtype)
    if not with_router:
        return

    @pl.when(pl.program_id(0) == 0)
    def _():
        cnt_scr[...] = cnt_in_ref[...]

    logits = _dot(h.astype(BF), wr_ref[...])
    tm = logits.shape[0]
    lane = lax.broadcasted_iota(jnp.int32, logits.shape, 1).astype(F32)
    logits = jnp.where(lane < N_EXPERTS, logits, -jnp.inf)
    m1 = jnp.max(logits, axis=-1, keepdims=True)
    i1 = jnp.min(jnp.where(logits == m1, lane, float(LANES)), axis=-1, keepdims=True)
    rest_l = jnp.where(lane == i1, -jnp.inf, logits)
    m2 = jnp.max(rest_l, axis=-1, keepdims=True)
    i2 = jnp.min(jnp.where(rest_l == m2, lane, float(LANES)), axis=-1, keepdims=True)
    e2 = jnp.exp(m2 - m1)
    g1 = 1.0 / (1.0 + e2)
    g2 = e2 / (1.0 + e2)
    oh1 = (lane == i1).astype(F32)
    oh2 = (lane == i2).astype(F32)
    both = oh1 + oh2
    sub = min(tm, ROUTER_RANK_ROWS)
    ri = lax.broadcasted_iota(jnp.int32, (sub, sub), 0)
    ci = lax.broadcasted_iota(jnp.int32, (sub, sub), 1)
    earlier = (ri > ci).astype(BF)
    running = cnt_scr[...]
    before = []
    for c in range(tm // sub):
        part = both[c * sub:(c + 1) * sub]
        before.append(_dot(earlier, part.astype(BF)) + running)
        running = running + jnp.sum(part, axis=0, keepdims=True)
    before = jnp.concatenate(before, axis=0)
    rank1 = jnp.sum(oh1 * before, axis=-1, keepdims=True)
    rank2 = jnp.sum(oh2 * before, axis=-1, keepdims=True)
    cnt_scr[...] = running
    cnt_out_ref[...] = running
    route = jnp.zeros_like(logits)
    for n, val in enumerate((i1, i2, g1, g2, rank1, rank2)):
        route = jnp.where(lane == n, val, route)
    route_ref[...] = route


def _merge_out(x, ya, yb, gate, wa, wb, wo, g, tm, router=None, cnt_in=None):
    t = x.shape[0]
    row = lambda w_: pl.BlockSpec((tm, w_), lambda i: (i, 0))
    in_specs = [row(D_MODEL), row(ATT_W), row(RWKV_W), row(2 * D_MODEL),
                _const_spec((ATT_W, D_MODEL)), _const_spec((RWKV_W, D_MODEL)),
                _const_spec((D_MODEL, D_MODEL)), _const_spec((1, D_MODEL))]
    args = [x, ya, yb, gate, wa, wb, wo, g]
    if router is None:
        out_specs = [row(D_MODEL), row(D_MODEL)]
        out_shape = [jax.ShapeDtypeStruct((t, D_MODEL), F32), jax.ShapeDtypeStruct((t, D_MODEL), BF)]
        scratch = []
    else:
        in_specs += [_const_spec(router.shape), _const_spec((1, LANES))]
        args += [router, cnt_in]
        out_specs = [row(D_MODEL), row(D_MODEL), row(LANES), pl.BlockSpec((1, LANES), lambda i: (0, 0))]
        out_shape = [jax.ShapeDtypeStruct((t, D_MODEL), F32), jax.ShapeDtypeStruct((t, D_MODEL), F32),
                     jax.ShapeDtypeStruct((t, LANES), F32), jax.ShapeDtypeStruct((1, LANES), F32)]
        scratch = [pltpu.VMEM((1, LANES), F32)]
    return pl.pallas_call(
        functools.partial(_merge_kernel, router is not None),
        grid=(t // tm,),
        in_specs=in_specs, out_specs=out_specs, out_shape=out_shape, scratch_shapes=scratch,
        compiler_params=_params("parallel" if router is None else "arbitrary"),
        name="merge_out",
    )(*args)


def _swiglu(x, wg, wu, wd):
    s = _dot(x, wg)
    u = _dot(x, wu)
    act = (s * _sigmoid(s) * u).astype(BF)
    return _dot(act, wd)


def _merge_dense_kernel(x_ref, ya_ref, yb_ref, gate_ref, wa_ref, wb_ref, wo_ref, g_ref,
                        wg_ref, wu_ref, wd_ref, o_ref):
    gate = gate_ref[...].astype(F32)
    pa = _dot(ya_ref[...], wa_ref[...])
    pb = _dot(yb_ref[...], wb_ref[...])
    merged = _sigmoid(gate[:, :D_MODEL]) * pa + _sigmoid(gate[:, D_MODEL:]) * pb
    x1 = x_ref[...] + _dot(merged.astype(BF), wo_ref[...])
    h = _rms(x1, g_ref[...]).astype(BF)
    o_ref[...] = x1 + _swiglu(h, wg_ref[...], wu_ref[...], wd_ref[...])


def _merge_dense(x, ya, yb, gate, wa, wb, wo, g, wg, wu, wd, tm):
    t = x.shape[0]
    d_ff = wg.shape[1]
    row = lambda w_: pl.BlockSpec((tm, w_), lambda i: (i, 0))
    return pl.pallas_call(
        _merge_dense_kernel,
        grid=(t // tm,),
        in_specs=[row(D_MODEL), row(ATT_W), row(RWKV_W), row(2 * D_MODEL),
                  _const_spec((ATT_W, D_MODEL)), _const_spec((RWKV_W, D_MODEL)),
                  _const_spec((D_MODEL, D_MODEL)), _const_spec((1, D_MODEL)),
                  _const_spec((D_MODEL, d_ff)), _const_spec((D_MODEL, d_ff)), _const_spec((d_ff, D_MODEL))],
        out_specs=row(D_MODEL),
        out_shape=jax.ShapeDtypeStruct((t, D_MODEL), F32),
        compiler_params=_params("parallel"),
        name="merge_dense",
    )(x, ya, yb, gate, wa, wb, wo, g, wg, wu, wd)


def _dispatch_kernel(tt, first, dest_ref, fill_ref, h_ref, *rest):
    if first:
        xs_hbm, zbuf, sem, zsem = rest
    else:
        _, xs_hbm, sem = rest
    i = pl.program_id(0)
    base = i * tt

    if first:
        @pl.when(i == 0)
        def _():
            zbuf[...] = jnp.zeros_like(zbuf)
            bm = zbuf.shape[0]
            fills = [pltpu.make_async_copy(zbuf, xs_hbm.at[pl.ds(fill_ref[n] * bm, bm), :], zsem)
                     for n in range(fill_ref.shape[0])]
            for f in fills:
                f.start()
            for f in fills:
                f.wait()

    def issue(r, carry):
        for j in range(TOP_K):
            row = dest_ref[TOP_K * (base + r) + j]
            pltpu.make_async_copy(h_ref.at[pl.ds(r, 1), :], xs_hbm.at[pl.ds(row, 1), :], sem).start()
        return carry

    lax.fori_loop(0, tt, issue, 0, unroll=8)
    for j in range(TOP_K):
        pltpu.make_async_copy(h_ref, xs_hbm.at[pl.ds(0, tt), :], sem).wait()


def _dispatch(dest, fill_blocks, h, xs, tt, bm, n_rows):
    t = h.shape[0]
    first = xs is None
    in_specs = [pl.BlockSpec((tt, D_MODEL), lambda i, ds, fb: (i, 0))]
    args = [dest, fill_blocks, h]
    if first:
        scratch = [pltpu.VMEM((bm, D_MODEL), F32), pltpu.SemaphoreType.DMA(()), pltpu.SemaphoreType.DMA(())]
        aliases = {}
    else:
        in_specs.append(pl.BlockSpec(memory_space=pl.ANY))
        args.append(xs)
        scratch = [pltpu.SemaphoreType.DMA(())]
        aliases = {3: 0}
    grid_spec = pltpu.PrefetchScalarGridSpec(
        num_scalar_prefetch=2, grid=(t // tt,), in_specs=in_specs,
        out_specs=pl.BlockSpec(memory_space=pl.ANY), scratch_shapes=scratch)
    return pl.pallas_call(
        functools.partial(_dispatch_kernel, tt, first),
        grid_spec=grid_spec,
        out_shape=jax.ShapeDtypeStruct((n_rows, D_MODEL), F32),
        input_output_aliases=aliases,
        compiler_params=_params("arbitrary"),
        name="moe_dispatch",
    )(*args)


def _moe_kernel(n_half, blk_e_ref, n_used_ref, xs_ref, wg_ref, wu_ref, wd_ref, y_ref):
    @pl.when(pl.program_id(0) < n_used_ref[0])
    def _():
        x = xs_ref[...].astype(BF)
        d_e = wg_ref.shape[2]
        step = d_e // n_half
        acc = None
        for j in range(n_half):
            cs = slice(j * step, (j + 1) * step)
            part = _swiglu(x, wg_ref[0, :, cs], wu_ref[0, :, cs], wd_ref[0, cs, :])
            acc = part if acc is None else acc + part
        y_ref[...] = acc

    @pl.when(pl.program_id(0) >= n_used_ref[0])
    def _():
        y_ref[...] = jnp.zeros_like(y_ref)


def _moe_ffn(xs, blk_e, n_used, wg, wu, wd, bm):
    n_blk = blk_e.shape[0]
    n_rows = xs.shape[0]
    d_e = wg.shape[2]
    wspec = lambda shape: pl.BlockSpec(shape, lambda i, be, nu: (be[i], 0, 0), pipeline_mode=pl.Buffered(1))
    rows = pl.BlockSpec((bm, D_MODEL), lambda i, be, nu: (i, 0))
    grid_spec = pltpu.PrefetchScalarGridSpec(
        num_scalar_prefetch=2,
        grid=(n_blk,),
        in_specs=[rows, wspec((1, D_MODEL, d_e)), wspec((1, D_MODEL, d_e)), wspec((1, d_e, D_MODEL))],
        out_specs=rows,
    )
    return pl.pallas_call(
        functools.partial(_moe_kernel, 2),
        grid_spec=grid_spec,
        out_shape=jax.ShapeDtypeStruct((n_rows, D_MODEL), F32),
        compiler_params=_params("arbitrary"),
        name="moe_ffn",
    )(blk_e, n_used, xs, wg, wu, wd)


def _row_layout(counts, bm, n_blk):
    per_e = (counts + bm - 1) // bm
    ends = jnp.cumsum(per_e)
    n_used = ends[-1]
    first_row = (ends - per_e) * bm
    i = jnp.arange(n_blk, dtype=jnp.int32)
    blk_e = jnp.sum((jnp.minimum(i, jnp.maximum(n_used - 1, 0))[:, None] >= ends[None, :]).astype(jnp.int32), axis=1)
    need = jnp.any(i[:, None] == (ends - 1)[None, :], axis=1) | (i >= n_blk - N_EXPERTS)
    fill_blocks = jnp.argsort(jnp.logical_not(need), stable=True)[:2 * N_EXPERTS].astype(jnp.int32)
    return first_row.astype(jnp.int32), blk_e.astype(jnp.int32), n_used.astype(jnp.int32).reshape(1), fill_blocks


def _combine_kernel(tc, dest_ref, x1_ref, route_ref, y_hbm, g_ref, o_ref, ybuf, sem):
    base = pl.program_id(0) * tc

    def issue(r, carry):
        for j in range(TOP_K):
            row = dest_ref[TOP_K * (base + r) + j]
            pltpu.make_async_copy(y_hbm.at[pl.ds(row, 1), :], ybuf.at[j, pl.ds(r, 1), :], sem).start()
        return carry

    lax.fori_loop(0, tc, issue, 0, unroll=4)
    for j in range(TOP_K):
        pltpu.make_async_copy(y_hbm.at[pl.ds(0, tc), :], ybuf.at[j], sem).wait()
    route = route_ref[...]
    x2 = x1_ref[...] + route[:, 2:3] * ybuf[0] + route[:, 3:4] * ybuf[1]
    o_ref[...] = _rms(x2, g_ref[...])


def _moe_combine(dest, x1, route, y, g, tc):
    t = x1.shape[0]
    grid_spec = pltpu.PrefetchScalarGridSpec(
        num_scalar_prefetch=1,
        grid=(t // tc,),
        in_specs=[pl.BlockSpec((tc, D_MODEL), lambda i, ds: (i, 0)),
                  pl.BlockSpec((tc, LANES), lambda i, ds: (i, 0)),
                  pl.BlockSpec(memory_space=pl.ANY),
                  pl.BlockSpec((1, D_MODEL), lambda i, ds: (0, 0))],
        out_specs=pl.BlockSpec((tc, D_MODEL), lambda i, ds: (i, 0)),
        scratch_shapes=[pltpu.VMEM((TOP_K, tc, D_MODEL), F32), pltpu.SemaphoreType.DMA(())],
    )
    return pl.pallas_call(
        functools.partial(_combine_kernel, tc),
        grid_spec=grid_spec,
        out_shape=jax.ShapeDtypeStruct((t, D_MODEL), F32),
        compiler_params=_params("arbitrary"),
        name="moe_combine",
    )(dest, x1, route, y, g)


def _layer_params(l, w_in, mu_shift, w0, w2, a0, a2, g2, k_k, k_a, r_k, lnx_g, lnx_b,
                  w_proj_attn, w_proj_rwkv, w_out, ones_bd):
    row = lambda u: u.reshape(1, -1)
    return dict(
        w_in=w_in[l].astype(BF), mu=row(mu_shift[l]), w0=row(w0[l]), w2=w2[l].astype(BF),
        a0=row(a0[l]), a2=a2[l].astype(BF), g2=g2[l].astype(BF), k_k=row(k_k[l]), k_a=row(k_a[l]),
        r_k=row(r_k[l]), lnx_g=row(lnx_g[l]), lnx_b=row(lnx_b[l]),
        wa=w_proj_attn[l].astype(BF), wb=w_proj_rwkv[l].astype(BF), wo=w_out[l].astype(BF),
        ones_bd=ones_bd)


def kernel(x_prompt, x_sample, cache_k, cache_v, state_wkv, state_shift, norm_mix_g, w_in, attn_sinks, rel_bias, mu_shift, w0, w2, a0, a2, g2, k_k, k_a, r_k, lnx_g, lnx_b, w_proj_attn, w_proj_rwkv, w_out, norm_ffn_g, dense_w_gate, dense_w_up, dense_w_down, router_w, moe_w_gate, moe_w_up, moe_w_down, norm_final_g):
    batch, seq, _ = x_prompt.shape
    dec_batch, dec_seq, _ = x_sample.shape
    assert dec_seq == 1 and seq % WINDOW == 0

    head_id = jnp.arange(RWKV_W, dtype=jnp.int32) // RWKV_HEAD
    ones_bd = (head_id[:, None] == head_id[None, :]).astype(BF)
    bias_by_dist = _bias_lookup(rel_bias, _t5_bucket(np.arange(WINDOW + 1)))
    bias_c = jnp.transpose(bias_by_dist[WINDOW:0:-1], (1, 0))[None]
    bias_n = bias_by_dist[0][None, :, None]

    layers = [_layer_params(l, w_in, mu_shift, w0, w2, a0, a2, g2, k_k, k_a, r_k, lnx_g, lnx_b,
                            w_proj_attn, w_proj_rwkv, w_out, ones_bd) for l in range(DEPTH)]

    def router_pieces(l):
        return jnp.pad(router_w[l // 2], ((0, 0), (0, LANES - N_EXPERTS))).astype(BF)

    n_prompt = batch * seq
    bm = MOE_BLOCK_ROWS
    n_all = n_prompt + dec_batch
    n_blk = -(-(n_all * TOP_K) // bm) + N_EXPERTS

    def mix(x, is_prompt, l, state):
        p = layers[l]
        n_tok = x.shape[0]
        gm = norm_mix_g[l].reshape(1, -1)
        sinks = attn_sinks[l]
        if is_prompt:
            b, t = batch, seq
            prev_z = jnp.zeros((b, 1, RWKV_IN), F32)
            q, k, v, gate, z_last, *coef = _inproj(x, gm, p, prev_z, min(INPROJ_ROWS, t), t)
            sink_col = jnp.repeat(sinks, WINDOW).reshape(N_KV_HEADS, GQA_GROUP * WINDOW, 1)
            ya = _swa_prompt(q.reshape(b, t, ATT_W), k.reshape(b, t, KV_W), v.reshape(b, t, KV_W),
                             rel_bias, sink_col).reshape(n_tok, ATT_W)
            k4 = k.reshape(b, t, N_KV_HEADS, HEAD_DIM)
            v4 = v.reshape(b, t, N_KV_HEADS, HEAD_DIM)
            state["k"].append(k4[:, t - WINDOW:])
            state["v"].append(v4[:, t - WINDOW:])
            s0 = jnp.zeros((b, RWKV_HEADS, RWKV_HEAD, RWKV_HEAD), F32)
            yb3, s_new = _rwkv_scan([u.reshape(b, t, RWKV_W) for u in coef], s0, p, b, RWKV_CHUNK)
            yb = yb3.reshape(n_tok, RWKV_W)
            state["z"].append(z_last.reshape(b, RWKV_IN))
        else:
            b = dec_batch
            q, k, v, gate, z, *coef = _inproj(x, gm, p, state_shift[l], n_tok, None)
            ck = cache_k[l].reshape(b, WINDOW, KV_W)
            cv = cache_v[l].reshape(b, WINDOW, KV_W)
            ya3 = _swa_sample(q.reshape(b, N_Q_HEADS, HEAD_DIM), k, v, ck, cv, bias_c, bias_n,
                              sinks.reshape(1, N_Q_HEADS, 1), min(SAMPLE_ATTN_SEQS, b))
            ya = ya3.reshape(b, ATT_W)
            state["k"].append(jnp.concatenate([ck[:, 1:], k[:, None]], axis=1)
                              .reshape(b, WINDOW, N_KV_HEADS, HEAD_DIM))
            state["v"].append(jnp.concatenate([cv[:, 1:], v[:, None]], axis=1)
                              .reshape(b, WINDOW, N_KV_HEADS, HEAD_DIM))
            yb, s_new = _rwkv_step(coef, state_wkv[l], p)
            state["z"].append(z)
        state["s"].append(s_new)
        return ya, yb, gate

    groups = [dict(x=x_prompt.reshape(n_prompt, D_MODEL), prompt=True, tm=min(DENSE_ROWS, n_prompt), k=[], v=[], s=[], z=[]),
              dict(x=x_sample.reshape(dec_batch, D_MODEL), prompt=False, tm=min(DENSE_ROWS, dec_batch), k=[], v=[], s=[], z=[])]
    for l in range(DEPTH):
        p = layers[l]
        gn = norm_ffn_g[l].reshape(1, -1)
        j = l // 2
        if l % 2 == 0:
            wg, wu, wd = (_to_bf16(w[j], w.shape[1] // 4) for w in (dense_w_gate, dense_w_up, dense_w_down))
            for grp in groups:
                ya, yb, gate = mix(grp["x"], grp["prompt"], l, grp)
                grp["x"] = _merge_dense(grp["x"], ya, yb, gate, p["wa"], p["wb"], p["wo"], gn, wg, wu, wd,
                                        min(MERGE_DENSE_ROWS, grp["tm"]))
        else:
            assert l == DEPTH - 1
            wg, wu, wd = (w[j].astype(BF) for w in (moe_w_gate, moe_w_up, moe_w_down))
            router = router_pieces(l)
            cnt = jnp.zeros((1, LANES), F32)
            xs = None
            for grp in groups:
                ya, yb, gate = mix(grp["x"], grp["prompt"], l, grp)
                x1, h, route, cnt = _merge_out(grp["x"], ya, yb, gate, p["wa"], p["wb"], p["wo"], gn,
                                               min(ROUTER_ROWS, grp["tm"]), router=router, cnt_in=cnt)
                grp["x1"], grp["route"], grp["h"] = x1, route, h
            counts = cnt[0, :N_EXPERTS].astype(jnp.int32)
            first_row, blk_e, n_used, fill_blocks = _row_layout(counts, bm, n_blk)
            for grp in groups:
                route = grp["route"]
                expert = route[:, 0:TOP_K].astype(jnp.int32)
                rank = route[:, 2 * TOP_K:3 * TOP_K].astype(jnp.int32)
                start = jnp.sum(jnp.where(expert[..., None] == jnp.arange(N_EXPERTS, dtype=jnp.int32),
                                          first_row, 0), axis=-1)
                grp["dest"] = (start + rank).reshape(-1)
                xs = _dispatch(grp["dest"], fill_blocks, grp["h"], xs, min(DISPATCH_ROWS, grp["h"].shape[0]), bm,
                               n_blk * bm)
            y = _moe_ffn(xs, blk_e, n_used, wg, wu, wd, bm)
            for grp in groups:
                grp["x"] = _moe_combine(grp["dest"], grp["x1"], grp["route"], y,
                                        norm_final_g.reshape(1, -1), min(COMBINE_ROWS, grp["x1"].shape[0]))

    gp, gs = groups
    outs = []
    for grp, shape in ((gp, (batch, seq, D_MODEL)), (gs, (dec_batch, dec_seq, D_MODEL))):
        outs.append((grp["x"].reshape(shape), jnp.stack(grp["k"]), jnp.stack(grp["v"]),
                     jnp.stack(grp["s"]), jnp.stack(grp["z"])))
    (y_p, nk_p, nv_p, ns_p, nz_p), (y_s, nk_s, nv_s, ns_s, nz_s) = outs
    return (y_p, y_s, nk_p, nv_p, ns_p, nz_p, nk_s, nv_s, ns_s, nz_s)
```
